```python
import math
import jax, jax.numpy as jnp
from jax import lax
import numpy as np

D_MODEL = 1024
BATCH = 8
SEQ = 2048
DEPTH = 2
DEC_BATCH = 128
DEC_SEQ = 1
PAST_LEN = 16384
PAGE_SIZE = 128

MIX_WIDTH = D_MODEL
N_MIXERS = 4
GW = MIX_WIDTH // N_MIXERS
HEAD_DIM = 64
NH = GW // HEAD_DIM
LORA_W = 32
LORA_A = 32
LORA_G = 64
RWKV_COLS = 3 * GW + LORA_W + LORA_A + LORA_G
RWKV_SPLITS = [GW, 2 * GW, 3 * GW, 3 * GW + LORA_W, 3 * GW + LORA_W + LORA_A]
RWKV_GN_EPS = 64e-5
HGRN_COLS = 4 * GW
SSM_STATE = 128
SSM_GROUPS = 2
HEADS_PER_GROUP = NH // SSM_GROUPS
CONV_WIDTH = 4
SSM_CONV_CH = GW + 2 * SSM_GROUPS * SSM_STATE
SSM_COLS = GW + SSM_CONV_CH + NH
MLSTM_COLS = 3 * GW + 2 * NH + GW
MLSTM_SPLITS = [GW, 2 * GW, 3 * GW, 3 * GW + NH, 3 * GW + 2 * NH]
IN_COLS = RWKV_COLS + HGRN_COLS + SSM_COLS + MLSTM_COLS
GROUP_SPLITS = [RWKV_COLS, RWKV_COLS + HGRN_COLS, RWKV_COLS + HGRN_COLS + SSM_COLS]
CHUNK = 64
N_MEM = 256
XATTN_HEADS = 4
XATTN_HEAD_DIM = D_MODEL // XATTN_HEADS
D_FF = 4 * D_MODEL
NORM_EPS = 1e-6

kernel_name = 'hybrid_rwkv7_hgrn2_mamba2_mlstm_decode_step'

f32 = jnp.float32


def rmsnorm(x, g):
    xf = x.astype(f32)
    y = xf * lax.rsqrt(jnp.mean(xf * xf, axis=-1, keepdims=True) + NORM_EPS)
    return (y * g.astype(f32)).astype(x.dtype)


def head_rmsnorm(o, g):
    b, t = o.shape[:2]
    o = o * lax.rsqrt(jnp.mean(o * o, axis=-1, keepdims=True) + NORM_EPS)
    return o.reshape(b, t, GW) * g.astype(f32)


def causal_conv(u, buf, w, bias):
    t = u.shape[1]
    cat = jnp.concatenate([buf.astype(f32), u], axis=1)
    out = bias.astype(f32) + sum(cat[:, j:j + t] * w[j].astype(f32) for j in range(CONV_WIDTH))
    return out, cat[:, -(CONV_WIDTH - 1):]


def rwkv7_mix(p, shift_prev, S0, lp):
    p = p.astype(f32)
    bsz, t, _ = p.shape
    prev_rows = jnp.concatenate([shift_prev.astype(f32)[:, None], p[:, :-1]], axis=1)
    pm = p + (prev_rows - p) * lp['rwkv_mu'].astype(f32)
    r, k, v, dw, da, dg = jnp.split(pm, RWKV_SPLITS, axis=-1)
    w_log = -jax.nn.softplus(-(lp['rwkv_w0'] + jnp.tanh(dw) @ lp['rwkv_w_up'].astype(f32))) - 0.5
    decay = jnp.exp(-jnp.exp(w_log))
    a = jax.nn.sigmoid(lp['rwkv_a0'] + da @ lp['rwkv_a_up'].astype(f32))
    g = jax.nn.sigmoid(dg) @ lp['rwkv_g_up'].astype(f32)
    heads = lambda z: z.reshape(bsz, t, NH, HEAD_DIM)
    kk = heads(k * lp['rwkv_k_k'])
    kk = kk / jnp.maximum(jnp.linalg.norm(kk, axis=-1, keepdims=True), 1e-12)
    k = k * (1.0 + (a - 1.0) * lp['rwkv_k_a'])
    r_h, k_h, v_h, w_h, a_h = heads(r), heads(k), heads(v), heads(decay), heads(a)

    def step(S, inp):
        r_t, k_t, v_t, w_t, kk_t, a_t = inp
        S = (S * w_t[:, :, None, :]
             - jnp.einsum('bhvk,bhk->bhv', S, kk_t)[..., None] * (kk_t * a_t)[:, :, None, :]
             + v_t[..., None] * k_t[:, :, None, :])
        return S, jnp.einsum('bhvk,bhk->bhv', S, r_t)

    xs = tuple(jnp.moveaxis(z, 1, 0) for z in (r_h, k_h, v_h, w_h, kk, a_h))
    S_T, o = lax.scan(step, S0.astype(f32), xs)
    o = jnp.moveaxis(o, 0, 1)
    mu = jnp.mean(o, axis=-1, keepdims=True)
    var = jnp.mean(jnp.square(o - mu), axis=-1, keepdims=True)
    o = ((o - mu) * lax.rsqrt(var + RWKV_GN_EPS)).reshape(bsz, t, GW) * lp['rwkv_ln_g'] + lp['rwkv_ln_b']
    bonus = (jnp.sum(r_h * k_h * lp['rwkv_r_k'].astype(f32), axis=-1, keepdims=True) * v_h).reshape(bsz, t, GW)
    return (o + bonus) * g, S_T, p[:, -1]


def hgrn2_mix(p, S0, lb, norm_g):
    p = p.astype(f32)
    bsz, t, _ = p.shape
    lb = lb.astype(f32)
    q_pre, f_pre, i_in, g = jnp.split(p, 4, axis=-1)
    q = jax.nn.silu(q_pre)
    log_f = jnp.log(lb + (1.0 - lb) * jax.nn.sigmoid(f_pre))
    k = (1.0 - lb) * jax.nn.sigmoid(-f_pre)
    L = math.gcd(t, CHUNK)
    nc = t // L
    chunks = lambda z: z.reshape(bsz, nc, L, NH, HEAD_DIM).transpose(1, 0, 3, 2, 4)
    causal = jnp.tril(jnp.ones((L, L), dtype=bool))[:, :, None]

    def step(S, inp):
        q_c, k_c, lf_c, v_c = inp
        b = jnp.cumsum(lf_c, axis=2)
        inter = jnp.einsum('bhtk,bhkv->bhtv', q_c * jnp.exp(b), S)
        diff = b[:, :, :, None, :] - b[:, :, None, :, :]
        dec = jnp.where(causal, jnp.exp(jnp.where(causal, diff, 0.0)), 0.0)
        A = jnp.einsum('bhtk,bhtsk,bhsk->bhts', q_c, dec, k_c)
        intra = jnp.einsum('bhts,bhsv->bhtv', A, v_c)
        b_end = b[:, :, -1]
        S = jnp.exp(b_end)[..., None] * S + jnp.einsum('bhsk,bhsv->bhkv', k_c * jnp.exp(b_end[:, :, None] - b), v_c)
        return S, inter + intra

    S_T, o = lax.scan(step, S0.astype(f32), tuple(map(chunks, (q, k, log_f, i_in))))
    o = o.transpose(1, 0, 3, 2, 4).reshape(bsz, t, NH, HEAD_DIM)
    return head_rmsnorm(o, norm_g) * jax.nn.silu(g), S_T


def mamba2_mix(p, conv_buf, h0, conv_w, conv_b, dt_bias, A_log, D, norm_g):
    p = p.astype(f32)
    bsz, t, _ = p.shape
    z, xbc, dt_pre = jnp.split(p, [GW, GW + SSM_CONV_CH], axis=-1)
    xbc, new_buf = causal_conv(xbc, conv_buf, conv_w, conv_b)
    xbc = jax.nn.silu(xbc)
    xs, Bm, Cm = jnp.split(xbc, [GW, GW + SSM_GROUPS * SSM_STATE], axis=-1)
    x_h = xs.reshape(bsz, t, NH, HEAD_DIM)
    B_h = jnp.repeat(Bm.reshape(bsz, t, SSM_GROUPS, SSM_STATE), HEADS_PER_GROUP, axis=2)
    C_h = jnp.repeat(Cm.reshape(bsz, t, SSM_GROUPS, SSM_STATE), HEADS_PER_GROUP, axis=2)
    dt = jax.nn.softplus(dt_pre + dt_bias.astype(f32))
    la = dt * (-jnp.exp(A_log.astype(f32)))
    L = math.gcd(t, CHUNK)
    nc = t // L
    chunks = lambda u: u.reshape(bsz, nc, L, *u.shape[2:]).swapaxes(0, 1)
    causal = jnp.tril(jnp.ones((L, L), dtype=bool))[None, :, :, None]

    def step(h, inp):
        x_c, dt_c, la_c, B_c, C_c = inp
        cum = jnp.cumsum(la_c, axis=1)
        seg = cum[:, :, None, :] - cum[:, None, :, :]
        Lm = jnp.where(causal, jnp.exp(jnp.where(causal, seg, 0.0)), 0.0)
        scores = jnp.einsum('bthn,bshn->btsh', C_c, B_c) * Lm
        y = jnp.einsum('btsh,bsh,bshp->bthp', scores, dt_c, x_c)
        y = y + jnp.einsum('bthn,bhpn->bthp', C_c, h) * jnp.exp(cum)[..., None]
        w_end = jnp.exp(cum[:, -1:] - cum) * dt_c
        h = jnp.exp(cum[:, -1])[:, :, None, None] * h + jnp.einsum('bsh,bshp,bshn->bhpn', w_end, x_c, B_c)
        return h, y

    h_T, y = lax.scan(step, h0.astype(f32), tuple(map(chunks, (x_h, dt, la, B_h, C_h))))
    y = y.swapaxes(0, 1).reshape(bsz, t, NH, HEAD_DIM) + x_h * D.astype(f32)[:, None]
    y = (y.reshape(bsz, t, GW) * jax.nn.silu(z)).reshape(bsz, t, SSM_GROUPS, GW // SSM_GROUPS)
    y = y * lax.rsqrt(jnp.mean(y * y, axis=-1, keepdims=True) + NORM_EPS)
    return y.reshape(bsz, t, GW) * norm_g.astype(f32), h_T, new_buf


def mlstm_mix(p, C0, n0, m0, i_bias, f_bias, norm_g):
    p = p.astype(f32)
    bsz, t, _ = p.shape
    q, k, v, i_pre, f_pre, o_pre = jnp.split(p, MLSTM_SPLITS, axis=-1)
    heads = lambda z: z.reshape(bsz, t, NH, HEAD_DIM)
    q, k, v = heads(q), heads(k) * (HEAD_DIM ** -0.5), heads(v)
    log_i = i_pre + i_bias.astype(f32)
    log_f = jax.nn.log_sigmoid(f_pre + f_bias.astype(f32))
    L = math.gcd(t, CHUNK)
    nc = t // L
    chunks = lambda u: u.reshape(bsz, nc, L, *u.shape[2:]).swapaxes(0, 1)
    causal = jnp.tril(jnp.ones((L, L), dtype=bool))[None, :, :, None]

    def step(carry, inp):
        C, n, m = carry
        q_c, k_c, v_c, li_c, lf_c = inp
        b = jnp.cumsum(lf_c, axis=1)
        log_inter = b + m[:, None, :]
        log_intra = jnp.where(causal, b[:, :, None, :] - b[:, None, :, :] + li_c[:, None, :, :], -jnp.inf)
        m_t = jnp.maximum(log_inter, jnp.max(log_intra, axis=2))
        w_inter = jnp.exp(log_inter - m_t)
        w_intra = jnp.exp(log_intra - m_t[:, :, None, :])
        qk = jnp.einsum('bthd,bshd->btsh', q_c, k_c) * w_intra
        num = jnp.einsum('btsh,bshd->bthd', qk, v_c) + w_inter[..., None] * jnp.einsum('bhvk,bthk->bthv', C, q_c)
        den = jnp.sum(qk, axis=2) + w_inter * jnp.einsum('bhk,bthk->bth', n, q_c)
        h = num / jnp.maximum(jnp.abs(den), jnp.exp(-m_t))[..., None]
        m_end = m_t[:, -1]
        g_prev = jnp.exp(b[:, -1] + m - m_end)
        g_s = jnp.exp(b[:, -1:] - b + li_c - m_end[:, None])
        C = g_prev[..., None, None] * C + jnp.einsum('bsh,bshv,bshk->bhvk', g_s, v_c, k_c)
        n = g_prev[..., None] * n + jnp.einsum('bsh,bshk->bhk', g_s, k_c)
        return (C, n, m_end), h

    (C_T, n_T, m_T), h = lax.scan(step, (C0.astype(f32), n0.astype(f32), m0.astype(f32)),
                                  tuple(map(chunks, (q, k, v, log_i, log_f))))
    h = h.swapaxes(0, 1).reshape(bsz, t, NH, HEAD_DIM)
    return head_rmsnorm(h, norm_g) * jax.nn.sigmoid(o_pre), C_T, n_T, m_T


def hgrn_lower_bounds(logits):
    pr = jax.nn.softmax(logits.astype(f32), axis=0)
    return jnp.cumsum(pr, axis=0) - pr[0:1]


def memory_kv(mem, g, wk, wv):
    mn = rmsnorm(mem, g)
    b = mem.shape[0]
    return (mn @ wk).reshape(b, N_MEM, XATTN_HEADS, XATTN_HEAD_DIM), (mn @ wv).reshape(b, N_MEM, XATTN_HEADS, XATTN_HEAD_DIM)


def cross_attend(xn, mem_k, mem_v, wq, wo):
    b, t, _ = xn.shape
    q = (xn @ wq).reshape(b, t, XATTN_HEADS, XATTN_HEAD_DIM)
    s = jnp.einsum('bthd,bmhd->bhtm', q, mem_k.astype(q.dtype)).astype(f32) * (XATTN_HEAD_DIM ** -0.5)
    pr = jax.nn.softmax(s, axis=-1).astype(xn.dtype)
    o = jnp.einsum('bhtm,bmhd->bthd', pr, mem_v.astype(xn.dtype)).reshape(b, t, D_MODEL)
    return o @ wo


def decoder_layer(x, st, mem_k, mem_v, lp):
    wkv, shift, hgrn_S, ssm_h, conv_buf, mC, mn, mm = st
    proj = rmsnorm(x, lp['norm_mix_g']) @ lp['w_in']
    p_r, p_h, p_s, p_m = jnp.split(proj, GROUP_SPLITS, axis=-1)
    o_r, wkv, shift = rwkv7_mix(p_r, shift, wkv, lp)
    o_h, hgrn_S = hgrn2_mix(p_h, hgrn_S, lp['hgrn_lb'], lp['hgrn_norm_g'])
    o_s, ssm_h, conv_buf = mamba2_mix(p_s, conv_buf, ssm_h, lp['ssm_conv_w'], lp['ssm_conv_b'],
                                      lp['ssm_dt_bias'], lp['ssm_A_log'], lp['ssm_D'], lp['ssm_norm_g'])
    o_m, mC, mn, mm = mlstm_mix(p_m, mC, mn, mm, lp['mlstm_i_bias'], lp['mlstm_f_bias'], lp['mlstm_norm_g'])
    mixed = jnp.concatenate([o_r, o_h, o_s, o_m], axis=-1).astype(x.dtype)
    x = x + mixed @ lp['w_out']
    x = x + cross_attend(rmsnorm(x, lp['norm_x_g']), mem_k, mem_v, lp['xattn_wq'], lp['xattn_wo'])
    hf = rmsnorm(x, lp['norm_ff_g']) @ lp['ff_w1']
    x = x + jnp.square(jax.nn.relu(hf)) @ lp['ff_w2']
    return x, (wkv, shift, hgrn_S, ssm_h, conv_buf, mC, mn, mm)


def zero_state(bsz):
    return (jnp.zeros((bsz, NH, HEAD_DIM, HEAD_DIM), f32), jnp.zeros((bsz, RWKV_COLS), f32),
            jnp.zeros((bsz, NH, HEAD_DIM, HEAD_DIM), f32), jnp.zeros((bsz, NH, HEAD_DIM, SSM_STATE), f32),
            jnp.zeros((bsz, CONV_WIDTH - 1, SSM_CONV_CH), f32), jnp.zeros((bsz, NH, HEAD_DIM, HEAD_DIM), f32),
            jnp.zeros((bsz, NH, HEAD_DIM), f32), jnp.zeros((bsz, NH), f32))


def setup_inputs(seed: int = 0) -> dict:
    key = jax.random.key(seed)
    ks = iter(jax.random.split(key, 64))
    nrm = lambda shape, scale=1.0: jax.random.normal(next(ks), shape, f32) * scale
    unif = lambda shape, lo, hi: jax.random.uniform(next(ks), shape, f32, lo, hi)
    gain = lambda shape: 1.0 + nrm(shape, 0.02)
    dt0 = jnp.exp(unif((DEPTH, NH), math.log(1e-3), math.log(1e-1)))
    return {
        'x_prompt': nrm((BATCH, SEQ, D_MODEL)),
        'x_sample': nrm((DEC_BATCH, DEC_SEQ, D_MODEL)),
        'state_rwkv_wkv': nrm((DEPTH, DEC_BATCH, NH, HEAD_DIM, HEAD_DIM), 0.3),
        'state_rwkv_shift': nrm((DEPTH, DEC_BATCH, RWKV_COLS)),
        'state_hgrn': nrm((DEPTH, DEC_BATCH, NH, HEAD_DIM, HEAD_DIM), 0.3),
        'state_ssm': nrm((DEPTH, DEC_BATCH, NH, HEAD_DIM, SSM_STATE), 0.3),
        'state_ssm_conv': nrm((DEPTH, DEC_BATCH, CONV_WIDTH - 1, SSM_CONV_CH)),
        'state_mlstm_C': nrm((DEPTH, DEC_BATCH, NH, HEAD_DIM, HEAD_DIM), 0.3),
        'state_mlstm_n': nrm((DEPTH, DEC_BATCH, NH, HEAD_DIM), 0.3),
        'state_mlstm_m': nrm((DEPTH, DEC_BATCH, NH)),
        'cache_mem_k': nrm((DEPTH, DEC_BATCH, N_MEM, XATTN_HEADS, XATTN_HEAD_DIM)),
        'cache_mem_v': nrm((DEPTH, DEC_BATCH, N_MEM, XATTN_HEADS, XATTN_HEAD_DIM)),
        'mem_prompt': nrm((BATCH, N_MEM, D_MODEL)),
        'norm_mix_g': gain((DEPTH, D_MODEL)),
        'w_in': nrm((DEPTH, D_MODEL, IN_COLS), D_MODEL ** -0.5),
        'w_out': nrm((DEPTH, MIX_WIDTH, D_MODEL), MIX_WIDTH ** -0.5),
        'rwkv_mu': unif((DEPTH, RWKV_COLS), 0.0, 1.0),
        'rwkv_w0': unif((DEPTH, GW), -5.0, 1.0),
        'rwkv_w_up': nrm((DEPTH, LORA_W, GW), 0.3 * LORA_W ** -0.5),
        'rwkv_a0': nrm((DEPTH, GW), 0.1),
        'rwkv_a_up': nrm((DEPTH, LORA_A, GW), 0.3 * LORA_A ** -0.5),
        'rwkv_g_up': nrm((DEPTH, LORA_G, GW), LORA_G ** -0.5),
        'rwkv_k_k': 0.85 + nrm((DEPTH, GW), 0.02),
        'rwkv_k_a': gain((DEPTH, GW)),
        'rwkv_r_k': nrm((DEPTH, NH, HEAD_DIM), 0.1),
        'rwkv_ln_g': gain((DEPTH, GW)),
        'rwkv_ln_b': nrm((DEPTH, GW), 0.02),
        'hgrn_lb_logits': nrm((DEPTH, GW), 0.5),
        'hgrn_norm_g': gain((DEPTH, GW)),
        'ssm_conv_w': nrm((DEPTH, CONV_WIDTH, SSM_CONV_CH), 0.5),
        'ssm_conv_b': nrm((DEPTH, SSM_CONV_CH), 0.02),
        'ssm_dt_bias': dt0 + jnp.log(-jnp.expm1(-dt0)),
        'ssm_A_log': jnp.log(unif((DEPTH, NH), 1.0, 16.0)),
        'ssm_D': gain((DEPTH, NH)),
        'ssm_norm_g': gain((DEPTH, GW)),
        'mlstm_i_bias': nrm((DEPTH, NH), 0.1),
        'mlstm_f_bias': unif((DEPTH, NH), 3.0, 6.0),
        'mlstm_norm_g': gain((DEPTH, GW)),
        'norm_x_g': gain((DEPTH, D_MODEL)),
        'norm_mem_g': gain((DEPTH, D_MODEL)),
        'xattn_wq': nrm((DEPTH, D_MODEL, D_MODEL), D_MODEL ** -0.5),
        'xattn_wk': nrm((DEPTH, D_MODEL, D_MODEL), D_MODEL ** -0.5),
        'xattn_wv': nrm((DEPTH, D_MODEL, D_MODEL), D_MODEL ** -0.5),
        'xattn_wo': nrm((DEPTH, D_MODEL, D_MODEL), D_MODEL ** -0.5),
        'norm_ff_g': gain((DEPTH, D_MODEL)),
        'ff_w1': nrm((DEPTH, D_MODEL, D_FF), D_MODEL ** -0.5),
        'ff_w2': nrm((DEPTH, D_FF, D_MODEL), D_FF ** -0.5),
        'final_norm_g': gain((D_MODEL,)),
    }


def reference(x_prompt, x_sample, state_rwkv_wkv, state_rwkv_shift, state_hgrn, state_ssm, state_ssm_conv,
              state_mlstm_C, state_mlstm_n, state_mlstm_m, cache_mem_k, cache_mem_v, mem_prompt,
              norm_mix_g, w_in, w_out, rwkv_mu, rwkv_w0, rwkv_w_up, rwkv_a0, rwkv_a_up, rwkv_g_up,
              rwkv_k_k, rwkv_k_a, rwkv_r_k, rwkv_ln_g, rwkv_ln_b, hgrn_lb_logits, hgrn_norm_g,
              ssm_conv_w, ssm_conv_b, ssm_dt_bias, ssm_A_log, ssm_D, ssm_norm_g,
              mlstm_i_bias, mlstm_f_bias, mlstm_norm_g, norm_x_g, norm_mem_g,
              xattn_wq, xattn_wk, xattn_wv, xattn_wo, norm_ff_g, ff_w1, ff_w2, final_norm_g):
    lb_all = hgrn_lower_bounds(hgrn_lb_logits)
    hp, hs = x_prompt, x_sample
    prompt_states, sample_states, prompt_mem = [], [], []
    for l in range(DEPTH):
        lp = {
            'norm_mix_g': norm_mix_g[l], 'w_in': w_in[l], 'w_out': w_out[l],
            'rwkv_mu': rwkv_mu[l], 'rwkv_w0': rwkv_w0[l], 'rwkv_w_up': rwkv_w_up[l], 'rwkv_a0': rwkv_a0[l],
            'rwkv_a_up': rwkv_a_up[l], 'rwkv_g_up': rwkv_g_up[l], 'rwkv_k_k': rwkv_k_k[l], 'rwkv_k_a': rwkv_k_a[l],
            'rwkv_r_k': rwkv_r_k[l], 'rwkv_ln_g': rwkv_ln_g[l], 'rwkv_ln_b': rwkv_ln_b[l],
            'hgrn_lb': lb_all[l], 'hgrn_norm_g': hgrn_norm_g[l],
            'ssm_conv_w': ssm_conv_w[l], 'ssm_conv_b': ssm_conv_b[l], 'ssm_dt_bias': ssm_dt_bias[l],
            'ssm_A_log': ssm_A_log[l], 'ssm_D': ssm_D[l], 'ssm_norm_g': ssm_norm_g[l],
            'mlstm_i_bias': mlstm_i_bias[l], 'mlstm_f_bias': mlstm_f_bias[l], 'mlstm_norm_g': mlstm_norm_g[l],
            'norm_x_g': norm_x_g[l], 'xattn_wq': xattn_wq[l], 'xattn_wo': xattn_wo[l],
            'norm_ff_g': norm_ff_g[l], 'ff_w1': ff_w1[l], 'ff_w2': ff_w2[l],
        }
        mk, mv = memory_kv(mem_prompt, norm_mem_g[l], xattn_wk[l], xattn_wv[l])
        hp, st_p = decoder_layer(hp, zero_state(hp.shape[0]), mk, mv, lp)
        prompt_states.append(st_p)
        prompt_mem.append((mk, mv))
        st_in = (state_rwkv_wkv[l], state_rwkv_shift[l], state_hgrn[l], state_ssm[l], state_ssm_conv[l],
                 state_mlstm_C[l], state_mlstm_n[l], state_mlstm_m[l])
        hs, st_s = decoder_layer(hs, st_in, cache_mem_k[l], cache_mem_v[l], lp)
        sample_states.append(st_s)
    y_prompt = rmsnorm(hp, final_norm_g)
    y_sample = rmsnorm(hs, final_norm_g)
    p_wkv, p_shift, p_hgrn, p_ssm, p_conv, p_C, p_n, p_m = [jnp.stack(s) for s in zip(*prompt_states)]
    s_wkv, s_shift, s_hgrn, s_ssm, s_conv, s_C, s_n, s_m = [jnp.stack(s) for s in zip(*sample_states)]
    p_mem_k = jnp.stack([kv[0] for kv in prompt_mem])
    p_mem_v = jnp.stack([kv[1] for kv in prompt_mem])
    return (y_prompt, y_sample, p_wkv, p_shift, p_hgrn, p_ssm, p_conv, p_C, p_n, p_m, p_mem_k, p_mem_v,
            s_wkv, s_shift, s_hgrn, s_ssm, s_conv, s_C, s_n, s_m)
```

```python
import functools

import numpy as np
import jax
import jax.numpy as jnp
from jax import lax
from jax.experimental import pallas as pl
from jax.experimental.pallas import tpu as pltpu

F32 = jnp.float32
BF16 = jnp.bfloat16
HI = lax.Precision.HIGHEST

D_MODEL = 1024
DEPTH = 2
NH = 4
HEAD_DIM = 64
GW = 256
SSM_STATE = 128
N_MEM = 256
XH = 4
XHD = 256
D_FF = 4096
NORM_EPS = 1e-6
RWKV_GN_EPS = 64e-5
RWKV_COLS = 896
LANES = 128
SUBLANES = 8
VMEM_LIMIT = 48 * 1024 * 1024

RW_OFF, RW_W = 0, 1152
HG_OFF, HG_W = 1152, 1024
SS_OFF, SS_W = 2176, 1280
ML_OFF, ML_W = 3456, 1536
IN_RE = 5120


def _in_col_index():
    idx = -np.ones(IN_RE, np.int32)
    idx[0:768] = np.arange(768)
    idx[768:800] = 768 + np.arange(32)
    idx[896:928] = 800 + np.arange(32)
    idx[1024:1088] = 832 + np.arange(64)
    idx[HG_OFF:HG_OFF + 1024] = 896 + np.arange(1024)
    ss = 1920
    idx[SS_OFF:SS_OFF + 1024] = ss + np.arange(1024)
    idx[SS_OFF + 1024:SS_OFF + 1280] = ss + 1024 + np.repeat(np.arange(NH), HEAD_DIM)
    ml = 2948
    idx[ML_OFF:ML_OFF + 768] = ml + np.arange(768)
    idx[ML_OFF + 768:ML_OFF + 1024] = ml + 768 + np.repeat(np.arange(NH), HEAD_DIM)
    idx[ML_OFF + 1024:ML_OFF + 1280] = ml + 772 + np.repeat(np.arange(NH), HEAD_DIM)
    idx[ML_OFF + 1280:ML_OFF + 1536] = ml + 776 + np.arange(256)
    return idx


_IN_IDX = _in_col_index()
_RW_IDX = _IN_IDX[:RW_W]
_RW_INV = np.concatenate([np.arange(800), 896 + np.arange(32), 1024 + np.arange(64)]).astype(np.int32)


def _gather_cols(a, idx):
    g = jnp.take(a, jnp.asarray(np.maximum(idx, 0)), axis=-1)
    return jnp.where(jnp.asarray(idx >= 0), g, jnp.zeros((), a.dtype))


def _seg_matrix(width):
    i = np.arange(GW)
    return jnp.asarray((i[:, None] // width == i[None, :] // width).astype(np.float32))


def _cparams(sem):
    return pltpu.CompilerParams(dimension_semantics=sem, vmem_limit_bytes=VMEM_LIMIT)


def _softplus(x):
    return jnp.maximum(x, 0.0) + jnp.log(1.0 + jnp.exp(-jnp.abs(x)))


def _sigmoid(x):
    return 1.0 / (1.0 + jnp.exp(-x))


def _rms(x, g):
    ms = jnp.mean(x * x, axis=-1, keepdims=True)
    return x * lax.rsqrt(ms + NORM_EPS) * g


def _norm_matmul_kernel(x_ref, g_ref, w_ref, o_ref, xn_ref):
    @pl.when(pl.program_id(1) == 0)
    def _():
        xn_ref[...] = _rms(x_ref[...], g_ref[...]).astype(BF16)

    o_ref[...] = jnp.dot(xn_ref[...], w_ref[...], preferred_element_type=F32)


def _row_tile(m):
    return min(m, 1024)


def _norm_matmul(x, g, w):
    m, k = x.shape
    n = w.shape[1]
    tm = _row_tile(m)
    tn = 512 if n % 512 == 0 else n
    return pl.pallas_call(
        _norm_matmul_kernel,
        grid=(m // tm, n // tn),
        in_specs=[pl.BlockSpec((tm, k), lambda i, j: (i, 0)),
                  pl.BlockSpec((1, k), lambda i, j: (0, 0)),
                  pl.BlockSpec((k, tn), lambda i, j: (0, j))],
        out_specs=pl.BlockSpec((tm, tn), lambda i, j: (i, j)),
        out_shape=jax.ShapeDtypeStruct((m, n), F32),
        scratch_shapes=[pltpu.VMEM((tm, k), BF16)],
        compiler_params=_cparams(("parallel", "arbitrary")),
    )(x, g.reshape(1, k), w)


def _mm_res_kernel(a_ref, w_ref, res_ref, o_ref):
    o_ref[...] = res_ref[...] + jnp.dot(a_ref[...].astype(BF16), w_ref[...], preferred_element_type=F32)


def _mm_res(a, w, res):
    m, k = a.shape
    n = w.shape[1]
    tm = min(m, 512)
    return pl.pallas_call(
        _mm_res_kernel,
        grid=(m // tm,),
        in_specs=[pl.BlockSpec((tm, k), lambda i: (i, 0)),
                  pl.BlockSpec((k, n), lambda i: (0, 0)),
                  pl.BlockSpec((tm, n), lambda i: (i, 0))],
        out_specs=pl.BlockSpec((tm, n), lambda i: (i, 0)),
        out_shape=jax.ShapeDtypeStruct((m, n), F32),
        compiler_params=_cparams(("parallel",)),
    )(a, w, res)


def _ffn_kernel(x_ref, g_ref, w1_ref, w2_ref, o_ref, xn_ref, acc_ref):
    j = pl.program_id(1)

    @pl.when(j == 0)
    def _():
        xn_ref[...] = _rms(x_ref[...], g_ref[...]).astype(BF16)
        acc_ref[...] = jnp.zeros_like(acc_ref)

    h = jnp.dot(xn_ref[...], w1_ref[...], preferred_element_type=F32)
    h = jnp.square(jnp.maximum(h, 0.0))
    acc_ref[...] += jnp.dot(h.astype(BF16), w2_ref[...], preferred_element_type=F32)

    @pl.when(j == pl.num_programs(1) - 1)
    def _():
        o_ref[...] = x_ref[...] + acc_ref[...]


def _ffn(x, g, w1, w2):
    m, k = x.shape
    ff = w1.shape[1]
    tm = _row_tile(m)
    tf = 1024
    return pl.pallas_call(
        _ffn_kernel,
        grid=(m // tm, ff // tf),
        in_specs=[pl.BlockSpec((tm, k), lambda i, j: (i, 0)),
                  pl.BlockSpec((1, k), lambda i, j: (0, 0)),
                  pl.BlockSpec((k, tf), lambda i, j: (0, j)),
                  pl.BlockSpec((tf, k), lambda i, j: (j, 0))],
        out_specs=pl.BlockSpec((tm, k), lambda i, j: (i, 0)),
        out_shape=jax.ShapeDtypeStruct((m, k), F32),
        scratch_shapes=[pltpu.VMEM((tm, k), BF16), pltpu.VMEM((tm, k), F32)],
        compiler_params=_cparams(("parallel", "arbitrary")),
    )(x, g.reshape(1, k), w1, w2)


def _rmsnorm_kernel(x_ref, g_ref, o_ref):
    o_ref[...] = _rms(x_ref[...], g_ref[...])


def _rmsnorm(x, g):
    m, k = x.shape
    tm = min(m, 512)
    return pl.pallas_call(
        _rmsnorm_kernel,
        grid=(m // tm,),
        in_specs=[pl.BlockSpec((tm, k), lambda i: (i, 0)), pl.BlockSpec((1, k), lambda i: (0, 0))],
        out_specs=pl.BlockSpec((tm, k), lambda i: (i, 0)),
        out_shape=jax.ShapeDtypeStruct((m, k), F32),
        compiler_params=_cparams(("parallel",)),
    )(x, g.reshape(1, k))


PREP_OUT_WIDTHS = (1536, 512, 1024, 256, 1024, 512, 1280, 256)


def _prep_math(p_ref, prev_rw, c3, c2, c1, mu_ref, pv_ref, lora_ref, logit_ref, conv_ref, seg_ref, layer, outs):
    rw_scan, rw_post, hg_scan, hg_post, ss_scan, ss_post, ml_scan, ml_post = outs
    pv = pv_ref[...]
    row = lambda i: pv[i:i + 1, :]
    seg = seg_ref[...]
    dot_hi = lambda a, b: jnp.dot(a, b, precision=HI, preferred_element_type=F32)

    pr = p_ref[:, RW_OFF:RW_OFF + RW_W]
    pm = pr + (prev_rw - pr) * mu_ref[...]
    r, k, v = pm[:, 0:256], pm[:, 256:512], pm[:, 512:768]
    dw, da, dg = pm[:, 768:896], pm[:, 896:1024], pm[:, 1024:1152]
    w_log = -_softplus(-(row(0) + dot_hi(jnp.tanh(dw), lora_ref[0]))) - 0.5
    decay = jnp.exp(-jnp.exp(w_log))
    a = _sigmoid(row(1) + dot_hi(da, lora_ref[1]))
    g = dot_hi(_sigmoid(dg), lora_ref[2])
    kk = k * row(2)
    kk = kk / jnp.maximum(jnp.sqrt(dot_hi(kk * kk, seg)), 1e-12)
    k2 = k * (1.0 + (a - 1.0) * row(3))
    bonus = dot_hi(r * k2 * row(4), seg) * v
    rw_scan[:, 0:256] = r
    rw_scan[:, 256:512] = k2
    rw_scan[:, 512:768] = v
    rw_scan[:, 768:1024] = decay
    rw_scan[:, 1024:1280] = kk
    rw_scan[:, 1280:1536] = kk * a
    rw_post[:, 0:256] = bonus
    rw_post[:, 256:512] = g

    lg = logit_ref[...]
    e = jnp.exp(lg - jnp.max(lg, axis=0, keepdims=True))
    prb = e / jnp.sum(e, axis=0, keepdims=True)
    lb = jnp.sum(prb[0:layer + 1], axis=0, keepdims=True) - prb[0:1]
    q_pre = p_ref[:, HG_OFF:HG_OFF + 256]
    f_pre = p_ref[:, HG_OFF + 256:HG_OFF + 512]
    gg = p_ref[:, HG_OFF + 768:HG_OFF + 1024]
    hg_scan[:, 0:256] = q_pre * _sigmoid(q_pre)
    hg_scan[:, 256:512] = lb + (1.0 - lb) * _sigmoid(f_pre)
    hg_scan[:, 512:768] = (1.0 - lb) * _sigmoid(-f_pre)
    hg_scan[:, 768:1024] = p_ref[:, HG_OFF + 512:HG_OFF + 768]
    hg_post[...] = gg * _sigmoid(gg)

    z = p_ref[:, SS_OFF:SS_OFF + 256]
    u = p_ref[:, SS_OFF + 256:SS_OFF + 1024]
    cw = conv_ref[...]
    conv = cw[4:5] + c3 * cw[0:1] + c2 * cw[1:2] + c1 * cw[2:3] + u * cw[3:4]
    xbc = conv * _sigmoid(conv)
    xs = xbc[:, 0:256]
    dt = _softplus(p_ref[:, SS_OFF + 1024:SS_OFF + 1280] + row(5))
    ss_scan[:, 0:256] = xs * dt
    ss_scan[:, 256:512] = jnp.exp(dt * (-jnp.exp(row(6))))
    ss_scan[:, 512:1024] = xbc[:, 256:768]
    ss_post[:, 0:256] = xs * row(7)
    ss_post[:, 256:512] = z * _sigmoid(z)

    ml_scan[:, 0:256] = p_ref[:, ML_OFF:ML_OFF + 256]
    ml_scan[:, 256:512] = p_ref[:, ML_OFF + 256:ML_OFF + 512] * (HEAD_DIM ** -0.5)
    ml_scan[:, 512:768] = p_ref[:, ML_OFF + 512:ML_OFF + 768]
    ml_scan[:, 768:1024] = p_ref[:, ML_OFF + 768:ML_OFF + 1024] + row(8)
    ml_scan[:, 1024:1280] = -_softplus(-(p_ref[:, ML_OFF + 1024:ML_OFF + 1280] + row(9)))
    ml_post[...] = _sigmoid(p_ref[:, ML_OFF + 1280:ML_OFF + 1536])


def _shift_rows(x, tail, j):
    xr = pltpu.roll(x, j, 0)
    tr = pltpu.roll(tail, j, 0)
    rows = lax.broadcasted_iota(jnp.int32, tr.shape, 0)
    first = jnp.where(rows < j, tr, xr[0:SUBLANES])
    return jnp.concatenate([first, xr[SUBLANES:]], axis=0)


def _prep_seq_kernel(layer, p_ref, prev8_ref, tail0_ref, mu_ref, pv_ref, lora_ref, logit_ref, conv_ref, seg_ref, *outs):
    first = pl.program_id(1) == 0
    tail_rw = jnp.where(first, tail0_ref[0, :, RW_OFF:RW_OFF + RW_W], prev8_ref[:, RW_OFF:RW_OFF + RW_W])
    xo = SS_OFF + 256
    tail_u = jnp.where(first, tail0_ref[0, :, xo:xo + 768], prev8_ref[:, xo:xo + 768])
    prev_rw = _shift_rows(p_ref[:, RW_OFF:RW_OFF + RW_W], tail_rw, 1)
    u = p_ref[:, xo:xo + 768]
    c1, c2, c3 = (_shift_rows(u, tail_u, j) for j in (1, 2, 3))
    _prep_math(p_ref, prev_rw, c3, c2, c1, mu_ref, pv_ref, lora_ref, logit_ref, conv_ref, seg_ref, layer, outs)


def _prep_step_kernel(layer, p_ref, prev_ref, c3_ref, c2_ref, c1_ref, mu_ref, pv_ref, lora_ref, logit_ref, conv_ref,
                      seg_ref, *outs):
    _prep_math(p_ref, prev_ref[...], c3_ref[...], c2_ref[...], c1_ref[...], mu_ref, pv_ref, lora_ref, logit_ref,
               conv_ref, seg_ref, layer, outs)


def _prep_param_specs(nidx):
    z = (lambda *a: (0, 0)) if nidx else None
    c2 = lambda *a: (0, 0)
    c3 = lambda *a: (0, 0, 0)
    return [pl.BlockSpec((1, RW_W), c2), pl.BlockSpec((16, GW), c2), pl.BlockSpec((3, LANES, GW), c3),
            pl.BlockSpec((DEPTH, GW), c2), pl.BlockSpec((8, 768), c2), pl.BlockSpec((GW, GW), c2)]


def _prep_seq(proj, tail0, params, layer, bsz, t):
    tt = 256
    nt = t // tt
    m = bsz * t
    return pl.pallas_call(
        functools.partial(_prep_seq_kernel, layer),
        grid=(bsz, nt),
        in_specs=[pl.BlockSpec((tt, IN_RE), lambda b, i: (b * nt + i, 0)),
                  pl.BlockSpec((SUBLANES, IN_RE), lambda b, i: (jnp.maximum((b * nt + i) * (tt // SUBLANES) - 1, 0), 0)),
                  pl.BlockSpec((1, SUBLANES, IN_RE), lambda b, i: (b, 0, 0))] + _prep_param_specs(2),
        out_specs=[pl.BlockSpec((tt, w), lambda b, i: (b * nt + i, 0)) for w in PREP_OUT_WIDTHS],
        out_shape=[jax.ShapeDtypeStruct((m, w), F32) for w in PREP_OUT_WIDTHS],
        compiler_params=_cparams(("parallel", "parallel")),
    )(proj, proj, tail0, *params)


def _prep_step(proj, prev_rw, c3, c2, c1, params, layer):
    m = proj.shape[0]
    full = lambda a: pl.BlockSpec(a.shape, lambda i: (0,) * a.ndim)
    return pl.pallas_call(
        functools.partial(_prep_step_kernel, layer),
        grid=(1,),
        in_specs=[full(proj), full(prev_rw), full(c3), full(c2), full(c1)] + _prep_param_specs(1),
        out_specs=[pl.BlockSpec((m, w), lambda i: (0, 0)) for w in PREP_OUT_WIDTHS],
        out_shape=[jax.ShapeDtypeStruct((m, w), F32) for w in PREP_OUT_WIDTHS],
        compiler_params=_cparams(("arbitrary",)),
    )(proj, prev_rw, c3, c2, c1, *params)


def _scan_loop(tb, body):
    if tb == 1:
        body(0, 0)
    else:
        lax.fori_loop(0, tb, body, 0)


def _rwkv_scan_kernel(tb, vh, kk_ref, w_ref, ka_ref, k_ref, r_ref, v_ref, s0_ref, o_ref, s_ref):
    @pl.when(pl.program_id(1) == 0)
    def _():
        s_ref[...] = s0_ref[...]

    def body(t, c):
        kk, w, ka, k, r = kk_ref[t], w_ref[t], ka_ref[t], k_ref[t], r_ref[t]
        for i in range(vh):
            s = s_ref[i]
            sa = jnp.sum(s * kk, axis=0, keepdims=True)
            s = s * w - sa * ka + v_ref[t, i:i + 1, :] * k
            o_ref[t, i:i + 1, :] = jnp.sum(s * r, axis=0, keepdims=True)
            s_ref[i] = s
        return c

    _scan_loop(tb, body)


def _hgrn_scan_kernel(tb, vh, f_ref, k_ref, q_ref, v_ref, s0_ref, o_ref, s_ref):
    @pl.when(pl.program_id(1) == 0)
    def _():
        s_ref[...] = s0_ref[...]

    def body(t, c):
        f, k, q = f_ref[t], k_ref[t], q_ref[t]
        for i in range(vh):
            s = s_ref[i] * f + v_ref[t, i:i + 1, :] * k
            o_ref[t, i:i + 1, :] = jnp.sum(s * q, axis=0, keepdims=True)
            s_ref[i] = s
        return c

    _scan_loop(tb, body)


def _ssm_scan_kernel(tb, vh, b_ref, c_ref, dec_ref, x_ref, s0_ref, o_ref, s_ref):
    @pl.when(pl.program_id(1) == 0)
    def _():
        s_ref[...] = s0_ref[...]

    def body(t, c):
        bn, cn, dec = b_ref[t], c_ref[t], dec_ref[t]
        for i in range(vh):
            s = s_ref[i] * dec + x_ref[t, i:i + 1, :] * bn
            o_ref[t, i:i + 1, :] = jnp.sum(s * cn, axis=0, keepdims=True)
            s_ref[i] = s
        return c

    _scan_loop(tb, body)


def _mlstm_scan_kernel(tb, vh, q_ref, k_ref, li_ref, lf_ref, v_ref, c0_ref, n0_ref, m0_ref, o_ref, c_ref, n_ref, m_ref):
    @pl.when(pl.program_id(1) == 0)
    def _():
        c_ref[...] = c0_ref[...]
        n_ref[...] = n0_ref[...]
        m_ref[...] = m0_ref[...]

    def body(t, c):
        q, k, li, lf = q_ref[t], k_ref[t], li_ref[t], lf_ref[t]
        m_old = m_ref[...]
        m_new = jnp.maximum(lf + m_old, li)
        gp = jnp.exp(lf + m_old - m_new)
        gs = jnp.exp(li - m_new)
        n = gp * n_ref[...] + gs * k
        den = jnp.sum(n * q, axis=0, keepdims=True)
        scale = 1.0 / jnp.maximum(jnp.abs(den), jnp.exp(-m_new))
        gk = gs * k
        for i in range(vh):
            s = c_ref[i] * gp + v_ref[t, i:i + 1, :] * gk
            o_ref[t, i:i + 1, :] = jnp.sum(s * q, axis=0, keepdims=True) * scale
            c_ref[i] = s
        n_ref[...] = n
        m_ref[...] = m_new
        return c

    _scan_loop(tb, body)


def _scan_call(kernel_fn, seq_ins, state_ins, t, vh, lanes, tb):
    nl = lanes // LANES
    seq_spec = lambda a: pl.BlockSpec((tb, a.shape[1], LANES), lambda j, i: (i, 0, j))

    def state_spec(a):
        nd = a.ndim
        return pl.BlockSpec(a.shape[:-1] + (LANES,), lambda j, i: (0,) * (nd - 1) + (j,))

    o_shape = jax.ShapeDtypeStruct((t, vh, lanes), F32)
    outs = pl.pallas_call(
        functools.partial(kernel_fn, tb, vh),
        grid=(nl, t // tb),
        in_specs=[seq_spec(a) for a in seq_ins] + [state_spec(a) for a in state_ins],
        out_specs=[pl.BlockSpec((tb, vh, LANES), lambda j, i: (i, 0, j))] + [state_spec(a) for a in state_ins],
        out_shape=[o_shape] + [jax.ShapeDtypeStruct(a.shape, F32) for a in state_ins],
        compiler_params=_cparams(("parallel", "arbitrary")),
    )(*seq_ins, *state_ins)
    return outs[0], outs[1:]


class _Lay:
    def __init__(self, bsz, t):
        self.b, self.t = bsz, t
        bh = bsz * NH
        self.vl = max(LANES // bh, 1)
        self.lanes = self.vl * bh
        self.tb = 32 if t % 32 == 0 else t

    def kop(self, x, k=HEAD_DIM):
        x = x.reshape(self.b, self.t, NH, k).transpose(1, 3, 0, 2).reshape(self.t, k, self.b * NH)
        return jnp.tile(x, (1, 1, self.vl))

    def sop(self, x):
        x = x.reshape(self.b, self.t, NH, HEAD_DIM)[..., 0].transpose(1, 0, 2).reshape(self.t, 1, self.b * NH)
        return jnp.tile(x, (1, 1, self.vl))

    def vop(self, x, v=HEAD_DIM):
        x = x.reshape(self.b, self.t, NH, self.vl, v // self.vl).transpose(1, 4, 3, 0, 2)
        return x.reshape(self.t, v // self.vl, self.lanes)

    def unvop(self, o, v=HEAD_DIM):
        o = o.reshape(self.t, v // self.vl, self.vl, self.b, NH).transpose(3, 0, 4, 2, 1)
        return o.reshape(self.b * self.t, NH * v)

    def state(self, s):
        v, k = s.shape[2], s.shape[3]
        s = s.reshape(self.b, NH, self.vl, v // self.vl, k).transpose(3, 4, 2, 0, 1)
        return s.reshape(v // self.vl, k, self.lanes)

    def unstate(self, s):
        vh, k = s.shape[0], s.shape[1]
        s = s.reshape(vh, k, self.vl, self.b, NH).transpose(3, 4, 2, 0, 1)
        return s.reshape(self.b, NH, vh * self.vl, k)

    def kstate(self, n):
        k = n.shape[2]
        return jnp.tile(n.transpose(2, 0, 1).reshape(k, self.b * NH), (1, self.vl))

    def unkstate(self, n):
        return n[:, :self.b * NH].reshape(n.shape[0], self.b, NH).transpose(1, 2, 0)


def _post_kernel(x_ref, orw_ref, ohg_ref, oss_ref, oml_ref, rwp_ref, hgp_ref, ssp_ref, mlp_ref, pq_ref, seg64_ref,
                 seg128_ref, w_ref, o_ref):
    pq = pq_ref[...]
    row = lambda i: pq[i:i + 1, :]
    s64 = seg64_ref[...]
    segmean = lambda a, s, n: jnp.dot(a, s, precision=HI, preferred_element_type=F32) * (1.0 / n)

    o = orw_ref[...]
    d = o - segmean(o, s64, HEAD_DIM)
    var = segmean(d * d, s64, HEAD_DIM)
    m_r = (d * lax.rsqrt(var + RWKV_GN_EPS) * row(0) + row(1) + rwp_ref[:, 0:256]) * rwp_ref[:, 256:512]

    o = ohg_ref[...]
    m_h = o * lax.rsqrt(segmean(o * o, s64, HEAD_DIM) + NORM_EPS) * row(2) * hgp_ref[...]

    y = (oss_ref[...] + ssp_ref[:, 0:256]) * ssp_ref[:, 256:512]
    m_s = y * lax.rsqrt(segmean(y * y, seg128_ref[...], 128) + NORM_EPS) * row(3)

    o = oml_ref[...]
    m_m = o * lax.rsqrt(segmean(o * o, s64, HEAD_DIM) + NORM_EPS) * row(4) * mlp_ref[...]

    acc = x_ref[...]
    for i, part in enumerate((m_r, m_h, m_s, m_m)):
        acc = acc + jnp.dot(part.astype(BF16), w_ref[i * GW:(i + 1) * GW, :], preferred_element_type=F32)
    o_ref[...] = acc


def _post(x, o_parts, post_parts, pq, seg64, seg128, w_out):
    m = x.shape[0]
    tm = min(m, 512)
    rows = lambda w: pl.BlockSpec((tm, w), lambda i: (i, 0))
    const = lambda a: pl.BlockSpec(a.shape, lambda i: (0, 0))
    return pl.pallas_call(
        _post_kernel,
        grid=(m // tm,),
        in_specs=[rows(D_MODEL)] + [rows(GW)] * 4 + [rows(512), rows(GW), rows(512), rows(GW)]
        + [const(pq), const(seg64), const(seg128), const(w_out)],
        out_specs=rows(D_MODEL),
        out_shape=jax.ShapeDtypeStruct((m, D_MODEL), F32),
        compiler_params=_cparams(("parallel",)),
    )(x, *o_parts, *post_parts, pq, seg64, seg128, w_out)


def _xattn_seq_kernel(q_ref, k_ref, v_ref, x_ref, wo_ref, o_ref):
    acc = x_ref[...]
    for h in range(XH):
        sl = slice(h * XHD, (h + 1) * XHD)
        qh = q_ref[:, sl].astype(BF16)
        kh = k_ref[:, sl].astype(BF16)
        s = lax.dot_general(qh, kh, (((1,), (1,)), ((), ())), preferred_element_type=F32) * (XHD ** -0.5)
        e = jnp.exp(s - jnp.max(s, axis=-1, keepdims=True))
        pr = e / jnp.sum(e, axis=-1, keepdims=True)
        oh = jnp.dot(pr.astype(BF16), v_ref[:, sl].astype(BF16), preferred_element_type=F32)
        acc = acc + jnp.dot(oh.astype(BF16), wo_ref[sl, :], preferred_element_type=F32)
    o_ref[...] = acc


def _xattn_seq(q, kv, x, wo, bsz, t):
    tq = 512
    nt = t // tq
    return pl.pallas_call(
        _xattn_seq_kernel,
        grid=(bsz, nt),
        in_specs=[pl.BlockSpec((tq, D_MODEL), lambda b, i: (b * nt + i, 0)),
                  pl.BlockSpec((N_MEM, D_MODEL), lambda b, i: (b, 0)),
                  pl.BlockSpec((N_MEM, D_MODEL), lambda b, i: (b, 1)),
                  pl.BlockSpec((tq, D_MODEL), lambda b, i: (b * nt + i, 0)),
                  pl.BlockSpec((D_MODEL, D_MODEL), lambda b, i: (0, 0))],
        out_specs=pl.BlockSpec((tq, D_MODEL), lambda b, i: (b * nt + i, 0)),
        out_shape=jax.ShapeDtypeStruct((bsz * t, D_MODEL), F32),
        compiler_params=_cparams(("parallel", "parallel")),
    )(q, kv, kv, x, wo)


def _xattn_step_kernel(bb, q_ref, k_ref, v_ref, o_ref):
    for i in range(bb):
        t = k_ref[i] * q_ref[i]
        s = jnp.sum(t, axis=-1, keepdims=True) * (XHD ** -0.5)
        e = jnp.exp(s - jnp.max(s, axis=0, keepdims=True))
        pr = e / jnp.sum(e, axis=0, keepdims=True)
        o_ref[i] = jnp.sum(pr * v_ref[i], axis=0)


def _xattn_step(q, cache_k, cache_v, layer):
    bsz = q.shape[0]
    bb = 2
    cache_spec = pl.BlockSpec((None, bb, N_MEM, XH, XHD), lambda i: (layer, i, 0, 0, 0))
    return pl.pallas_call(
        functools.partial(_xattn_step_kernel, bb),
        grid=(bsz // bb,),
        in_specs=[pl.BlockSpec((bb, XH, XHD), lambda i: (i, 0, 0)), cache_spec, cache_spec],
        out_specs=pl.BlockSpec((bb, XH, XHD), lambda i: (i, 0, 0)),
        out_shape=jax.ShapeDtypeStruct((bsz, XH, XHD), F32),
        compiler_params=_cparams(("parallel",)),
    )(q.reshape(bsz, XH, XHD), cache_k, cache_v).reshape(bsz, D_MODEL)


def _mixers(lay, scan, states):
    rw, hg, ss, ml = scan
    t = lay.t
    cols = lambda a, i, w=GW: a[:, i * w:(i + 1) * w]
    vh = HEAD_DIM // lay.vl
    o_rw, (s_rw,) = _scan_call(
        _rwkv_scan_kernel,
        [lay.kop(cols(rw, 4)), lay.kop(cols(rw, 3)), lay.kop(cols(rw, 5)), lay.kop(cols(rw, 1)), lay.kop(cols(rw, 0)),
         lay.vop(cols(rw, 2))], [states["wkv"]], t, vh, lay.lanes, lay.tb)
    o_hg, (s_hg,) = _scan_call(
        _hgrn_scan_kernel,
        [lay.kop(cols(hg, 1)), lay.kop(cols(hg, 2)), lay.kop(cols(hg, 0)), lay.vop(cols(hg, 3))],
        [states["hgrn"]], t, vh, lay.lanes, lay.tb)
    bc = ss[:, 512:1024].reshape(-1, 2, 2, SSM_STATE)
    b_h = jnp.repeat(bc[:, 0], NH // 2, axis=1).reshape(-1, NH * SSM_STATE)
    c_h = jnp.repeat(bc[:, 1], NH // 2, axis=1).reshape(-1, NH * SSM_STATE)
    o_ss, (s_ss,) = _scan_call(
        _ssm_scan_kernel,
        [lay.kop(b_h, SSM_STATE), lay.kop(c_h, SSM_STATE), lay.sop(cols(ss, 1)), lay.vop(cols(ss, 0))],
        [states["ssm"]], t, vh, lay.lanes, lay.tb)
    o_ml, (s_c, s_n, s_m) = _scan_call(
        _mlstm_scan_kernel,
        [lay.kop(cols(ml, 0)), lay.kop(cols(ml, 1)), lay.sop(cols(ml, 3)), lay.sop(cols(ml, 4)), lay.vop(cols(ml, 2))],
        [states["C"], states["n"], states["m"]], t, vh, lay.lanes, lay.tb)
    outs = [lay.unvop(o) for o in (o_rw, o_hg, o_ss, o_ml)]
    new_states = {"wkv": s_rw, "hgrn": s_hg, "ssm": s_ss, "C": s_c, "n": s_n, "m": s_m}
    return outs, new_states


def _export_states(lay, st):
    bh = lay.b * NH
    return (lay.unstate(st["wkv"]),
            lay.unstate(st["hgrn"]).transpose(0, 1, 3, 2),
            lay.unstate(st["ssm"]),
            lay.unstate(st["C"]),
            lay.unkstate(st["n"]),
            st["m"][0, :bh].reshape(lay.b, NH))


def _import_states(lay, wkv, hgrn, ssm, c, n, m):
    return {"wkv": lay.state(wkv), "hgrn": lay.state(hgrn.transpose(0, 1, 3, 2)), "ssm": lay.state(ssm),
            "C": lay.state(c), "n": lay.kstate(n), "m": jnp.tile(m.reshape(1, lay.b * NH), (1, lay.vl))}


def kernel(x_prompt, x_sample, state_rwkv_wkv, state_rwkv_shift, state_hgrn, state_ssm, state_ssm_conv, state_mlstm_C, state_mlstm_n, state_mlstm_m, cache_mem_k, cache_mem_v, mem_prompt, norm_mix_g, w_in, w_out, rwkv_mu, rwkv_w0, rwkv_w_up, rwkv_a0, rwkv_a_up, rwkv_g_up, rwkv_k_k, rwkv_k_a, rwkv_r_k, rwkv_ln_g, rwkv_ln_b, hgrn_lb_logits, hgrn_norm_g, ssm_conv_w, ssm_conv_b, ssm_dt_bias, ssm_A_log, ssm_D, ssm_norm_g, mlstm_i_bias, mlstm_f_bias, mlstm_norm_g, norm_x_g, norm_mem_g, xattn_wq, xattn_wk, xattn_wv, xattn_wo, norm_ff_g, ff_w1, ff_w2, final_norm_g):
    bp, tp, _ = x_prompt.shape
    bs = x_sample.shape[0]
    lay_p, lay_s = _Lay(bp, tp), _Lay(bs, 1)
    seg64, seg128 = _seg_matrix(HEAD_DIM), _seg_matrix(128)
    rep = lambda a: jnp.repeat(a, HEAD_DIM)
    pad_rows = lambda a: jnp.pad(a, ((0, LANES - a.shape[0]), (0, 0)))

    hp = x_prompt.reshape(bp * tp, D_MODEL)
    hs = x_sample.reshape(bs, D_MODEL)
    mem = mem_prompt.reshape(bp * N_MEM, D_MODEL)
    zeros = lambda *s: jnp.zeros(s, F32)
    p_states, s_states, p_mem = [], [], []

    for l in range(DEPTH):
        w_in_re = _gather_cols(w_in[l], _IN_IDX).astype(BF16)
        pv = jnp.stack([rwkv_w0[l], rwkv_a0[l], rwkv_k_k[l], rwkv_k_a[l], rwkv_r_k[l].reshape(GW),
                        rep(ssm_dt_bias[l]), rep(ssm_A_log[l]), rep(ssm_D[l]), rep(mlstm_i_bias[l]),
                        rep(mlstm_f_bias[l])] + [zeros(GW)] * 6)
        lora = jnp.stack([pad_rows(rwkv_w_up[l]), pad_rows(rwkv_a_up[l]), pad_rows(rwkv_g_up[l])])
        conv = jnp.concatenate([ssm_conv_w[l], ssm_conv_b[l][None], zeros(3, 768)], axis=0)
        params = (_gather_cols(rwkv_mu[l], _RW_IDX).reshape(1, RW_W), pv, lora, hgrn_lb_logits, conv, seg64)
        pq = jnp.stack([rwkv_ln_g[l], rwkv_ln_b[l], hgrn_norm_g[l], ssm_norm_g[l], mlstm_norm_g[l]] + [zeros(GW)] * 3)
        w_out_b = w_out[l].astype(BF16)
        wq_b, wo_b = xattn_wq[l].astype(BF16), xattn_wo[l].astype(BF16)
        wkv_b = jnp.concatenate([xattn_wk[l], xattn_wv[l]], axis=1).astype(BF16)
        w1_b, w2_b = ff_w1[l].astype(BF16), ff_w2[l].astype(BF16)

        kv = _norm_matmul(mem, norm_mem_g[l], wkv_b)
        p_mem.append((kv[:, :D_MODEL].reshape(bp, N_MEM, XH, XHD), kv[:, D_MODEL:].reshape(bp, N_MEM, XH, XHD)))
        proj = _norm_matmul(hp, norm_mix_g[l], w_in_re)
        prep = _prep_seq(proj, zeros(bp, SUBLANES, IN_RE), params, l, bp, tp)
        st0 = _import_states(lay_p, zeros(bp, NH, 64, 64), zeros(bp, NH, 64, 64), zeros(bp, NH, 64, SSM_STATE),
                             zeros(bp, NH, 64, 64), zeros(bp, NH, 64), zeros(bp, NH))
        o_parts, st = _mixers(lay_p, prep[0::2], st0)
        proj3 = proj.reshape(bp, tp, IN_RE)
        shift_new = jnp.take(proj3[:, -1, :RW_W], jnp.asarray(_RW_INV), axis=-1)
        conv_new = proj3[:, tp - 3:, SS_OFF + 256:SS_OFF + 1024]
        wkv_n, hg_n, ssm_n, c_n, n_n, m_n = _export_states(lay_p, st)
        p_states.append((wkv_n, shift_new, hg_n, ssm_n, conv_new, c_n, n_n, m_n))
        hp = _post(hp, o_parts, prep[1::2], pq, seg64, seg128, w_out_b)
        q = _norm_matmul(hp, norm_x_g[l], wq_b)
        hp = _xattn_seq(q, kv, hp, wo_b, bp, tp)
        hp = _ffn(hp, norm_ff_g[l], w1_b, w2_b)

        proj = _norm_matmul(hs, norm_mix_g[l], w_in_re)
        prev_rw = _gather_cols(state_rwkv_shift[l], _RW_IDX)
        cb = state_ssm_conv[l]
        prep = _prep_step(proj, prev_rw, cb[:, 0], cb[:, 1], cb[:, 2], params, l)
        st0 = _import_states(lay_s, state_rwkv_wkv[l], state_hgrn[l], state_ssm[l], state_mlstm_C[l],
                             state_mlstm_n[l], state_mlstm_m[l])
        o_parts, st = _mixers(lay_s, prep[0::2], st0)
        shift_new = jnp.take(proj[:, :RW_W], jnp.asarray(_RW_INV), axis=-1)
        conv_new = jnp.concatenate([cb[:, 1:], proj[:, None, SS_OFF + 256:SS_OFF + 1024]], axis=1)
        wkv_n, hg_n, ssm_n, c_n, n_n, m_n = _export_states(lay_s, st)
        s_states.append((wkv_n, shift_new, hg_n, ssm_n, conv_new, c_n, n_n, m_n))
        hs = _post(hs, o_parts, prep[1::2], pq, seg64, seg128, w_out_b)
        q = _norm_matmul(hs, norm_x_g[l], wq_b)
        att = _xattn_step(q, cache_mem_k, cache_mem_v, l)
        hs = _mm_res(att, wo_b, hs)
        hs = _ffn(hs, norm_ff_g[l], w1_b, w2_b)

    y_prompt = _rmsnorm(hp, final_norm_g).reshape(bp, tp, D_MODEL)
    y_sample = _rmsnorm(hs, final_norm_g).reshape(bs, 1, D_MODEL)
    p_out = [jnp.stack(s) for s in zip(*p_states)]
    s_out = [jnp.stack(s) for s in zip(*s_states)]
    p_mem_k = jnp.stack([kv[0] for kv in p_mem])
    p_mem_v = jnp.stack([kv[1] for kv in p_mem])
    return (y_prompt, y_sample, *p_out, p_mem_k, p_mem_v, *s_out)
```

```python
import functools

import numpy as np
import jax
import jax.numpy as jnp
from jax import lax
from jax.experimental import pallas as pl
from jax.experimental.pallas import tpu as pltpu

F32 = jnp.float32
BF16 = jnp.bfloat16
HI = lax.Precision.HIGHEST

D_MODEL = 1024
DEPTH = 2
NH = 4
HEAD_DIM = 64
GW = 256
SSM_STATE = 128
N_MEM = 256
XH = 4
XHD = 256
D_FF = 4096
NORM_EPS = 1e-6
RWKV_GN_EPS = 64e-5
RWKV_COLS = 896
LANES = 128
SUBLANES = 8
VMEM_LIMIT = 48 * 1024 * 1024
CHUNK = 128
SUB = 16

HG_OFF, HG_W = 0, 1024
SS_OFF, SS_W = 1024, 1024
ML_OFF, ML_W = 2048, 1024
GT_OFF, GT_W = 3072, 128
RW_OFF, RW_W = 3456, 1152
IN_RE = 4608
GT_DT, GT_I, GT_F = 0, 4, 8


def _in_col_index():
    idx = -np.ones(IN_RE, np.int32)
    idx[HG_OFF:HG_OFF + 1024] = 896 + np.arange(1024)
    ss = 1920
    idx[SS_OFF:SS_OFF + 1024] = ss + np.arange(1024)
    ml = 2948
    idx[ML_OFF:ML_OFF + 768] = ml + np.arange(768)
    idx[ML_OFF + 768:ML_OFF + 1024] = ml + 776 + np.arange(256)
    idx[GT_OFF + GT_DT:GT_OFF + GT_DT + NH] = ss + 1024 + np.arange(NH)
    idx[GT_OFF + GT_I:GT_OFF + GT_I + NH] = ml + 768 + np.arange(NH)
    idx[GT_OFF + GT_F:GT_OFF + GT_F + NH] = ml + 772 + np.arange(NH)
    idx[RW_OFF:RW_OFF + 768] = np.arange(768)
    idx[RW_OFF + 768:RW_OFF + 800] = 768 + np.arange(32)
    idx[RW_OFF + 896:RW_OFF + 928] = 800 + np.arange(32)
    idx[RW_OFF + 1024:RW_OFF + 1088] = 832 + np.arange(64)
    return idx


_IN_IDX = _in_col_index()
_RW_IDX = _IN_IDX[RW_OFF:RW_OFF + RW_W]
_RW_INV = np.concatenate([np.arange(800), 896 + np.arange(32), 1024 + np.arange(64)]).astype(np.int32)


def _gather_cols(a, idx):
    g = jnp.take(a, jnp.asarray(np.maximum(idx, 0)), axis=-1)
    return jnp.where(jnp.asarray(idx >= 0), g, jnp.zeros((), a.dtype))


def _seg_matrix(width):
    i = np.arange(GW)
    return jnp.asarray((i[:, None] // width == i[None, :] // width).astype(np.float32))


def _gate_expand():
    e = np.zeros((3, LANES, GW), np.float32)
    for s, off in enumerate((GT_DT, GT_I, GT_F)):
        for h in range(NH):
            e[s, off + h, h * HEAD_DIM:(h + 1) * HEAD_DIM] = 1.0
    return jnp.asarray(e)


def _gate_row(dt_bias, i_bias, f_bias):
    z = jnp.zeros((LANES,), F32)
    return z.at[GT_DT:GT_DT + NH].set(dt_bias).at[GT_I:GT_I + NH].set(i_bias).at[GT_F:GT_F + NH].set(f_bias)


def _cparams(sem):
    return pltpu.CompilerParams(dimension_semantics=sem, vmem_limit_bytes=VMEM_LIMIT)


def _softplus(x):
    return jnp.maximum(x, 0.0) + jnp.log(1.0 + jnp.exp(-jnp.abs(x)))


def _sigmoid(x):
    return 1.0 / (1.0 + jnp.exp(-x))


def _rms(x, g):
    ms = jnp.mean(x * x, axis=-1, keepdims=True)
    return x * lax.rsqrt(ms + NORM_EPS) * g


def _dot(a, b):
    return jnp.dot(a.astype(BF16), b.astype(BF16), preferred_element_type=F32)


def _dot_nt(a, b):
    return lax.dot_general(a.astype(BF16), b.astype(BF16), (((1,), (1,)), ((), ())), preferred_element_type=F32)


def _dot_tn(a, b):
    return lax.dot_general(a.astype(BF16), b.astype(BF16), (((0,), (0,)), ((), ())), preferred_element_type=F32)


def _dot_hi(a, b):
    return jnp.dot(a, b, precision=HI, preferred_element_type=F32)


def _norm_matmul_kernel(x_ref, g_ref, w_ref, o_ref, xn_ref):
    @pl.when(pl.program_id(1) == 0)
    def _():
        xn_ref[...] = _rms(x_ref[...], g_ref[...]).astype(BF16)

    o_ref[...] = jnp.dot(xn_ref[...], w_ref[...], preferred_element_type=F32)


def _row_tile(m):
    return min(m, 1024)


def _norm_matmul(x, g, w):
    m, k = x.shape
    n = w.shape[1]
    tm = _row_tile(m)
    tn = 512 if n % 512 == 0 else n
    return pl.pallas_call(
        _norm_matmul_kernel,
        grid=(m // tm, n // tn),
        in_specs=[pl.BlockSpec((tm, k), lambda i, j: (i, 0)),
                  pl.BlockSpec((1, k), lambda i, j: (0, 0)),
                  pl.BlockSpec((k, tn), lambda i, j: (0, j))],
        out_specs=pl.BlockSpec((tm, tn), lambda i, j: (i, j)),
        out_shape=jax.ShapeDtypeStruct((m, n), F32),
        scratch_shapes=[pltpu.VMEM((tm, k), BF16)],
        compiler_params=_cparams(("parallel", "arbitrary")),
    )(x, g.reshape(1, k), w)


def _mm_res_kernel(a_ref, w_ref, res_ref, o_ref):
    o_ref[...] = res_ref[...] + jnp.dot(a_ref[...].astype(BF16), w_ref[...], preferred_element_type=F32)


def _mm_res(a, w, res):
    m, k = a.shape
    n = w.shape[1]
    tm = min(m, 512)
    return pl.pallas_call(
        _mm_res_kernel,
        grid=(m // tm,),
        in_specs=[pl.BlockSpec((tm, k), lambda i: (i, 0)),
                  pl.BlockSpec((k, n), lambda i: (0, 0)),
                  pl.BlockSpec((tm, n), lambda i: (i, 0))],
        out_specs=pl.BlockSpec((tm, n), lambda i: (i, 0)),
        out_shape=jax.ShapeDtypeStruct((m, n), F32),
        compiler_params=_cparams(("parallel",)),
    )(a, w, res)


def _ffn_kernel(x_ref, g_ref, w1_ref, w2_ref, o_ref, xn_ref, acc_ref):
    j = pl.program_id(1)

    @pl.when(j == 0)
    def _():
        xn_ref[...] = _rms(x_ref[...], g_ref[...]).astype(BF16)
        acc_ref[...] = jnp.zeros_like(acc_ref)

    h = jnp.dot(xn_ref[...], w1_ref[...], preferred_element_type=F32)
    h = jnp.square(jnp.maximum(h, 0.0))
    acc_ref[...] += jnp.dot(h.astype(BF16), w2_ref[...], preferred_element_type=F32)

    @pl.when(j == pl.num_programs(1) - 1)
    def _():
        o_ref[...] = x_ref[...] + acc_ref[...]


def _ffn(x, g, w1, w2):
    m, k = x.shape
    ff = w1.shape[1]
    tm = _row_tile(m)
    tf = 1024
    return pl.pallas_call(
        _ffn_kernel,
        grid=(m // tm, ff // tf),
        in_specs=[pl.BlockSpec((tm, k), lambda i, j: (i, 0)),
                  pl.BlockSpec((1, k), lambda i, j: (0, 0)),
                  pl.BlockSpec((k, tf), lambda i, j: (0, j)),
                  pl.BlockSpec((tf, k), lambda i, j: (j, 0))],
        out_specs=pl.BlockSpec((tm, k), lambda i, j: (i, 0)),
        out_shape=jax.ShapeDtypeStruct((m, k), F32),
        scratch_shapes=[pltpu.VMEM((tm, k), BF16), pltpu.VMEM((tm, k), F32)],
        compiler_params=_cparams(("parallel", "arbitrary")),
    )(x, g.reshape(1, k), w1, w2)


def _rmsnorm_kernel(x_ref, g_ref, o_ref):
    o_ref[...] = _rms(x_ref[...], g_ref[...])


def _rmsnorm(x, g):
    m, k = x.shape
    tm = min(m, 512)
    return pl.pallas_call(
        _rmsnorm_kernel,
        grid=(m // tm,),
        in_specs=[pl.BlockSpec((tm, k), lambda i: (i, 0)), pl.BlockSpec((1, k), lambda i: (0, 0))],
        out_specs=pl.BlockSpec((tm, k), lambda i: (i, 0)),
        out_shape=jax.ShapeDtypeStruct((m, k), F32),
        compiler_params=_cparams(("parallel",)),
    )(x, g.reshape(1, k))


def _shift_rows(x, tail, j):
    xr = pltpu.roll(x, j, 0)
    tr = pltpu.roll(tail, j, 0)
    rows = lax.broadcasted_iota(jnp.int32, tr.shape, 0)
    first = jnp.where(rows < j, tr, xr[0:SUBLANES])
    return jnp.concatenate([first, xr[SUBLANES:]], axis=0)


def _hgrn_lower_bound(logit_ref, layer):
    lg = logit_ref[...]
    e = jnp.exp(lg - jnp.max(lg, axis=0, keepdims=True))
    prb = e / jnp.sum(e, axis=0, keepdims=True)
    return jnp.sum(prb[0:layer + 1], axis=0, keepdims=True) - prb[0:1]


def _conv_silu(u, c3, c2, c1, conv_ref):
    cw = conv_ref[...]
    conv = cw[4:5] + c3 * cw[0:1] + c2 * cw[1:2] + c1 * cw[2:3] + u * cw[3:4]
    return conv * _sigmoid(conv)


def _prep_rwkv(pr, prev_rw, mu_ref, pv_ref, lora_ref, seg_ref, rw_scan, rw_post):
    pv = pv_ref[...]
    row = lambda i: pv[i:i + 1, :]
    seg = seg_ref[...]
    pm = pr + (prev_rw - pr) * mu_ref[...]
    r, k, v = pm[:, 0:256], pm[:, 256:512], pm[:, 512:768]
    dw, da, dg = pm[:, 768:896], pm[:, 896:1024], pm[:, 1024:1152]
    w_log = -_softplus(-(row(0) + _dot_hi(jnp.tanh(dw), lora_ref[0]))) - 0.5
    a = _sigmoid(row(1) + _dot_hi(da, lora_ref[1]))
    kk = k * row(2)
    kk = kk / jnp.maximum(jnp.sqrt(_dot_hi(kk * kk, seg)), 1e-12)
    k2 = k * (1.0 + (a - 1.0) * row(3))
    rw_scan[:, 0:256] = r
    rw_scan[:, 256:512] = k2
    rw_scan[:, 512:768] = v
    rw_scan[:, 768:1024] = jnp.exp(-jnp.exp(w_log))
    rw_scan[:, 1024:1280] = kk
    rw_scan[:, 1280:1536] = kk * a
    rw_post[:, 0:256] = _dot_hi(r * k2 * row(4), seg) * v
    rw_post[:, 256:512] = _dot_hi(_sigmoid(dg), lora_ref[2])


def _prep_rw_seq_kernel(p_ref, prev8_ref, tail0_ref, mu_ref, pv_ref, lora_ref, seg_ref, rw_scan, rw_post):
    tail = jnp.where(pl.program_id(1) == 0, tail0_ref[0], prev8_ref[...])
    pr = p_ref[...]
    _prep_rwkv(pr, _shift_rows(pr, tail, 1), mu_ref, pv_ref, lora_ref, seg_ref, rw_scan, rw_post)


def _prep_rw_seq(proj, tail0, mu, pv, lora, seg, bsz, t):
    tt = 512
    nt = t // tt
    m = bsz * t
    cb = RW_OFF // RW_W
    c2 = lambda b, i: (0, 0)
    return pl.pallas_call(
        _prep_rw_seq_kernel,
        grid=(bsz, nt),
        in_specs=[pl.BlockSpec((tt, RW_W), lambda b, i: (b * nt + i, cb)),
                  pl.BlockSpec((SUBLANES, RW_W),
                               lambda b, i: (jnp.maximum((b * nt + i) * (tt // SUBLANES) - 1, 0), cb)),
                  pl.BlockSpec((1, SUBLANES, RW_W), lambda b, i: (b, 0, 0)),
                  pl.BlockSpec((1, RW_W), c2), pl.BlockSpec((16, GW), c2),
                  pl.BlockSpec((3, LANES, GW), lambda b, i: (0, 0, 0)), pl.BlockSpec((GW, GW), c2)],
        out_specs=[pl.BlockSpec((tt, 1536), lambda b, i: (b * nt + i, 0)),
                   pl.BlockSpec((tt, 512), lambda b, i: (b * nt + i, 0))],
        out_shape=[jax.ShapeDtypeStruct((m, 1536), F32), jax.ShapeDtypeStruct((m, 512), F32)],
        compiler_params=_cparams(("parallel", "parallel")),
    )(proj, proj, tail0, mu, pv, lora, seg)


PREP_OUT_WIDTHS = (1536, 512, 1024, 256, 1024, 512, 1280, 256)


def _prep_step_kernel(layer, p_ref, prev_ref, c3_ref, c2_ref, c1_ref, mu_ref, pv_ref, lora_ref, seg_ref, logit_ref,
                      conv_ref, ex_ref, rw_scan, rw_post, hg_scan, hg_post, ss_scan, ss_post, ml_scan, ml_post):
    pv = pv_ref[...]
    row = lambda i: pv[i:i + 1, :]
    _prep_rwkv(p_ref[:, RW_OFF:RW_OFF + RW_W], prev_ref[...], mu_ref, pv_ref, lora_ref, seg_ref, rw_scan, rw_post)
    gates = p_ref[:, GT_OFF:GT_OFF + GT_W]
    dt_pre, i_pre, f_pre_m = (_dot_hi(gates, ex_ref[s]) for s in range(3))

    lb = _hgrn_lower_bound(logit_ref, layer)
    q_pre = p_ref[:, HG_OFF:HG_OFF + 256]
    f_pre = p_ref[:, HG_OFF + 256:HG_OFF + 512]
    gg = p_ref[:, HG_OFF + 768:HG_OFF + 1024]
    hg_scan[:, 0:256] = q_pre * _sigmoid(q_pre)
    hg_scan[:, 256:512] = lb + (1.0 - lb) * _sigmoid(f_pre)
    hg_scan[:, 512:768] = (1.0 - lb) * _sigmoid(-f_pre)
    hg_scan[:, 768:1024] = p_ref[:, HG_OFF + 512:HG_OFF + 768]
    hg_post[...] = gg * _sigmoid(gg)

    z = p_ref[:, SS_OFF:SS_OFF + 256]
    xbc = _conv_silu(p_ref[:, SS_OFF + 256:SS_OFF + 1024], c3_ref[...], c2_ref[...], c1_ref[...], conv_ref)
    xs = xbc[:, 0:256]
    dt = _softplus(dt_pre + row(5))
    ss_scan[:, 0:256] = xs * dt
    ss_scan[:, 256:512] = jnp.exp(dt * (-jnp.exp(row(6))))
    ss_scan[:, 512:1024] = xbc[:, 256:768]
    ss_post[:, 0:256] = xs * row(7)
    ss_post[:, 256:512] = z * _sigmoid(z)

    ml_scan[:, 0:256] = p_ref[:, ML_OFF:ML_OFF + 256]
    ml_scan[:, 256:512] = p_ref[:, ML_OFF + 256:ML_OFF + 512] * (HEAD_DIM ** -0.5)
    ml_scan[:, 512:768] = p_ref[:, ML_OFF + 512:ML_OFF + 768]
    ml_scan[:, 768:1024] = i_pre + row(8)
    ml_scan[:, 1024:1280] = -_softplus(-(f_pre_m + row(9)))
    ml_post[...] = _sigmoid(p_ref[:, ML_OFF + 768:ML_OFF + 1024])


def _prep_step(proj, prev_rw, c3, c2, c1, params, layer):
    m = proj.shape[0]
    full = lambda a: pl.BlockSpec(a.shape, lambda i: (0,) * a.ndim)
    ins = (proj, prev_rw, c3, c2, c1) + tuple(params)
    return pl.pallas_call(
        functools.partial(_prep_step_kernel, layer),
        grid=(1,),
        in_specs=[full(a) for a in ins],
        out_specs=[pl.BlockSpec((m, w), lambda i: (0, 0)) for w in PREP_OUT_WIDTHS],
        out_shape=[jax.ShapeDtypeStruct((m, w), F32) for w in PREP_OUT_WIDTHS],
        compiler_params=_cparams(("arbitrary",)),
    )(*ins)


def _ssm_chunk_kernel(ss_ref, gt_ref, conv_ref, gp_ref, pd_ref, tril_ref, o_ref, ht_ref, tail_ref):
    @pl.when(pl.program_id(1) == 0)
    def _():
        ht_ref[...] = jnp.zeros_like(ht_ref)
        tail_ref[...] = jnp.zeros_like(tail_ref)

    L = ss_ref.shape[0]
    u = ss_ref[:, 256:1024]
    tail = tail_ref[...]
    c1, c2, c3 = (_shift_rows(u, tail, j) for j in (1, 2, 3))
    tail_ref[...] = u[L - SUBLANES:L]
    xbc = _conv_silu(u, c3, c2, c1, conv_ref)
    z = ss_ref[:, 0:256]
    zs = z * _sigmoid(z)
    dt = _softplus(gt_ref[...] + gp_ref[0:1])
    la = dt * (-jnp.exp(gp_ref[1:2]))
    tril = tril_ref[...]
    mask = tril > 0.5
    cum = _dot_hi(tril, la)
    cum_t = cum.T
    pd = pd_ref[...]
    for g in range(2):
        bg = xbc[:, 256 + SSM_STATE * g:256 + SSM_STATE * (g + 1)]
        cg = xbc[:, 512 + SSM_STATE * g:512 + SSM_STATE * (g + 1)]
        scores = _dot_nt(cg, bg)
        bg_t = bg.T
        ys = []
        for h in (2 * g, 2 * g + 1):
            sl = slice(h * HEAD_DIM, (h + 1) * HEAD_DIM)
            cc = cum[:, GT_DT + h:GT_DT + h + 1]
            cr = cum_t[GT_DT + h:GT_DT + h + 1, :]
            lm = jnp.where(mask, jnp.exp(jnp.where(mask, cc - cr, 0.0)), 0.0)
            dtc = dt[:, GT_DT + h:GT_DT + h + 1]
            xh = xbc[:, sl]
            ht = ht_ref[0, h]
            y = _dot(scores * lm, xh * dtc) + _dot(cg, ht) * jnp.exp(cc)
            cend = cc[L - 1:L, :]
            ht_ref[0, h] = jnp.exp(cend) * ht + _dot(bg_t, xh * (jnp.exp(cend - cc) * dtc))
            ys.append((y + xh * pd[0:1, sl]) * zs[:, sl])
        ms = (jnp.sum(ys[0] * ys[0], axis=-1, keepdims=True) + jnp.sum(ys[1] * ys[1], axis=-1, keepdims=True))
        scale = lax.rsqrt(ms * (1.0 / (2 * HEAD_DIM)) + NORM_EPS)
        for i, h in enumerate((2 * g, 2 * g + 1)):
            sl = slice(h * HEAD_DIM, (h + 1) * HEAD_DIM)
            o_ref[:, sl] = ys[i] * scale * pd[1:2, sl]


def _mlstm_chunk_kernel(ml_ref, gt_ref, gp_ref, pn_ref, tril_ref, o_ref, ct_ref, n_ref, m_ref):
    @pl.when(pl.program_id(1) == 0)
    def _():
        ct_ref[...] = jnp.zeros_like(ct_ref)
        n_ref[...] = jnp.zeros_like(n_ref)
        m_ref[...] = jnp.zeros_like(m_ref)

    L = ml_ref.shape[0]
    x = gt_ref[...] + gp_ref[0:1]
    lf = -_softplus(-x)
    tril = tril_ref[...]
    mask = tril > 0.5
    b = _dot_hi(tril, lf)
    b_t = b.T
    x_t = x.T
    m_row = m_ref[0]
    m_new_row = m_row
    lane = lax.broadcasted_iota(jnp.int32, m_row.shape, 1)
    pn = pn_ref[...]
    for h in range(NH):
        sl = slice(h * HEAD_DIM, (h + 1) * HEAD_DIM)
        b_col, b_row = b[:, GT_F + h:GT_F + h + 1], b_t[GT_F + h:GT_F + h + 1, :]
        li_col, li_row = x[:, GT_I + h:GT_I + h + 1], x_t[GT_I + h:GT_I + h + 1, :]
        m_prev = m_row[:, h:h + 1]
        log_inter = b_col + m_prev
        log_intra = jnp.where(mask, b_col - b_row + li_row, -jnp.inf)
        m_t = jnp.maximum(log_inter, jnp.max(log_intra, axis=-1, keepdims=True))
        w_inter = jnp.exp(log_inter - m_t)
        qh = ml_ref[:, sl]
        kh = ml_ref[:, 256 + h * HEAD_DIM:256 + (h + 1) * HEAD_DIM] * (HEAD_DIM ** -0.5)
        vh = ml_ref[:, 512 + h * HEAD_DIM:512 + (h + 1) * HEAD_DIM]
        qk = _dot_nt(qh, kh) * jnp.exp(log_intra - m_t)
        ct = ct_ref[0, h]
        n_h = n_ref[0, h:h + 1, :]
        num = _dot(qk, vh) + w_inter * _dot(qh, ct)
        den = jnp.sum(qk, axis=-1, keepdims=True) + w_inter * jnp.sum(qh * n_h, axis=-1, keepdims=True)
        hout = num / jnp.maximum(jnp.abs(den), jnp.exp(-m_t))
        m_end = m_t[L - 1:L, :]
        b_end = b_col[L - 1:L, :]
        g_prev = jnp.exp(b_end + m_prev - m_end)
        kg = kh * jnp.exp(b_end - b_col + li_col - m_end)
        ct_ref[0, h] = g_prev * ct + _dot_tn(kg, vh)
        n_ref[0, h:h + 1, :] = g_prev * n_h + jnp.sum(kg, axis=0, keepdims=True)
        m_new_row = jnp.where(lane == h, m_end, m_new_row)
        hn = hout * lax.rsqrt(jnp.mean(hout * hout, axis=-1, keepdims=True) + NORM_EPS)
        o_ref[:, sl] = hn * pn[0:1, sl] * _sigmoid(ml_ref[:, 768 + h * HEAD_DIM:768 + (h + 1) * HEAD_DIM])
    m_ref[0] = m_new_row


def _hgrn_chunk_kernel(layer, hg_ref, logit_ref, pn_ref, tril_ref, seg_ref, bd_ref, o_ref, st_ref, q_scr, b_scr, i_scr):
    @pl.when(pl.program_id(1) == 0)
    def _():
        st_ref[...] = jnp.zeros_like(st_ref)

    lb = _hgrn_lower_bound(logit_ref, layer)
    tril = tril_ref[...]
    seg = seg_ref[...]
    bd = bd_ref[...]
    pn = pn_ref[...]
    srow = lax.broadcasted_iota(jnp.int32, (SUB, GW), 0)

    st = st_ref[0]
    for c in range(hg_ref.shape[0] // SUB):
        r0 = c * SUB
        rows = slice(r0, r0 + SUB)
        q_pre = hg_ref[rows, 0:256]
        f_pre = hg_ref[rows, 256:512]
        v = hg_ref[rows, 512:768]
        gg = hg_ref[rows, 768:1024]
        q = q_pre * _sigmoid(q_pre)
        k = (1.0 - lb) * _sigmoid(-f_pre)
        b = _dot_hi(tril, jnp.log(lb + (1.0 - lb) * _sigmoid(f_pre)))
        b_end = b[SUB - 1:SUB, :]
        q_scr[rows, :] = q
        b_scr[rows, :] = b
        tiles = []
        for t in range(SUB):
            live = srow <= t
            dec = jnp.where(live, jnp.exp(jnp.where(live, b_scr[r0 + t:r0 + t + 1, :] - b, 0.0)), 0.0)
            tiles.append(dec * q_scr[r0 + t:r0 + t + 1, :] * k)
        a_b = jnp.dot(jnp.concatenate(tiles, axis=0).astype(BF16), seg, preferred_element_type=F32)
        for t in range(SUB):
            i_scr[r0 + t:r0 + t + 1, :] = jnp.sum(a_b[t * SUB:(t + 1) * SUB, :] * v, axis=0, keepdims=True)
        o = _dot_nt(q * jnp.exp(b), st) + i_scr[rows, :]
        st = st * jnp.exp(b_end) + _dot_tn(v, k * jnp.exp(b_end - b)) * bd
        gate = gg * _sigmoid(gg)
        for h in range(NH):
            sl = slice(h * HEAD_DIM, (h + 1) * HEAD_DIM)
            oh = o[:, sl]
            o_ref[rows, sl] = oh * lax.rsqrt(jnp.mean(oh * oh, axis=-1, keepdims=True) + NORM_EPS) * pn[0:1, sl] * gate[:, sl]
    st_ref[0] = st


def _chunk_call(kernel_fn, proj, col_blocks, consts, state_shapes, scratch, bsz, t):
    nt = t // CHUNK
    m = bsz * t
    rows = lambda w, cb: pl.BlockSpec((CHUNK, w), lambda b, i: (b * nt + i, cb))
    const = lambda a: pl.BlockSpec(a.shape, lambda b, i: (0,) * a.ndim)
    state = lambda s: pl.BlockSpec((1,) + s, lambda b, i: (b,) + (0,) * len(s))
    outs = pl.pallas_call(
        kernel_fn,
        grid=(bsz, nt),
        in_specs=[rows(w, cb) for w, cb in col_blocks] + [const(a) for a in consts],
        out_specs=[rows(GW, 0)] + [state(s) for s in state_shapes],
        out_shape=[jax.ShapeDtypeStruct((m, GW), F32)] + [jax.ShapeDtypeStruct((bsz,) + s, F32) for s in state_shapes],
        scratch_shapes=scratch,
        compiler_params=_cparams(("parallel", "arbitrary")),
    )(*([proj] * len(col_blocks)), *consts)
    return outs[0], outs[1:]


def _scan_loop(tb, body):
    if tb == 1:
        body(0, 0)
    else:
        lax.fori_loop(0, tb, body, 0)


def _rwkv_scan_kernel(tb, vh, kk_ref, w_ref, ka_ref, k_ref, r_ref, v_ref, s0_ref, o_ref, s_ref):
    @pl.when(pl.program_id(1) == 0)
    def _():
        s_ref[...] = s0_ref[...]

    def body(t, c):
        kk, w, ka, k, r = kk_ref[t], w_ref[t], ka_ref[t], k_ref[t], r_ref[t]
        for i in range(vh):
            s = s_ref[i]
            sa = jnp.sum(s * kk, axis=0, keepdims=True)
            s = s * w - sa * ka + v_ref[t, i:i + 1, :] * k
            o_ref[t, i:i + 1, :] = jnp.sum(s * r, axis=0, keepdims=True)
            s_ref[i] = s
        return c

    _scan_loop(tb, body)


def _hgrn_scan_kernel(tb, vh, f_ref, k_ref, q_ref, v_ref, s0_ref, o_ref, s_ref):
    @pl.when(pl.program_id(1) == 0)
    def _():
        s_ref[...] = s0_ref[...]

    def body(t, c):
        f, k, q = f_ref[t], k_ref[t], q_ref[t]
        for i in range(vh):
            s = s_ref[i] * f + v_ref[t, i:i + 1, :] * k
            o_ref[t, i:i + 1, :] = jnp.sum(s * q, axis=0, keepdims=True)
            s_ref[i] = s
        return c

    _scan_loop(tb, body)


def _ssm_scan_kernel(tb, vh, b_ref, c_ref, dec_ref, x_ref, s0_ref, o_ref, s_ref):
    @pl.when(pl.program_id(1) == 0)
    def _():
        s_ref[...] = s0_ref[...]

    def body(t, c):
        bn, cn, dec = b_ref[t], c_ref[t], dec_ref[t]
        for i in range(vh):
            s = s_ref[i] * dec + x_ref[t, i:i + 1, :] * bn
            o_ref[t, i:i + 1, :] = jnp.sum(s * cn, axis=0, keepdims=True)
            s_ref[i] = s
        return c

    _scan_loop(tb, body)


def _mlstm_scan_kernel(tb, vh, q_ref, k_ref, li_ref, lf_ref, v_ref, c0_ref, n0_ref, m0_ref, o_ref, c_ref, n_ref, m_ref):
    @pl.when(pl.program_id(1) == 0)
    def _():
        c_ref[...] = c0_ref[...]
        n_ref[...] = n0_ref[...]
        m_ref[...] = m0_ref[...]

    def body(t, c):
        q, k, li, lf = q_ref[t], k_ref[t], li_ref[t], lf_ref[t]
        m_old = m_ref[...]
        m_new = jnp.maximum(lf + m_old, li)
        gp = jnp.exp(lf + m_old - m_new)
        gs = jnp.exp(li - m_new)
        n = gp * n_ref[...] + gs * k
        den = jnp.sum(n * q, axis=0, keepdims=True)
        scale = 1.0 / jnp.maximum(jnp.abs(den), jnp.exp(-m_new))
        gk = gs * k
        for i in range(vh):
            s = c_ref[i] * gp + v_ref[t, i:i + 1, :] * gk
            o_ref[t, i:i + 1, :] = jnp.sum(s * q, axis=0, keepdims=True) * scale
            c_ref[i] = s
        n_ref[...] = n
        m_ref[...] = m_new
        return c

    _scan_loop(tb, body)


def _scan_call(kernel_fn, seq_ins, state_ins, t, vh, lanes, tb):
    nl = lanes // LANES
    seq_spec = lambda a: pl.BlockSpec((tb, a.shape[1], LANES), lambda j, i: (i, 0, j))

    def state_spec(a):
        nd = a.ndim
        return pl.BlockSpec(a.shape[:-1] + (LANES,), lambda j, i: (0,) * (nd - 1) + (j,))

    o_shape = jax.ShapeDtypeStruct((t, vh, lanes), F32)
    outs = pl.pallas_call(
        functools.partial(kernel_fn, tb, vh),
        grid=(nl, t // tb),
        in_specs=[seq_spec(a) for a in seq_ins] + [state_spec(a) for a in state_ins],
        out_specs=[pl.BlockSpec((tb, vh, LANES), lambda j, i: (i, 0, j))] + [state_spec(a) for a in state_ins],
        out_shape=[o_shape] + [jax.ShapeDtypeStruct(a.shape, F32) for a in state_ins],
        compiler_params=_cparams(("parallel", "arbitrary")),
    )(*seq_ins, *state_ins)
    return outs[0], outs[1:]


class _Lay:
    def __init__(self, bsz, t):
        self.b, self.t = bsz, t
        bh = bsz * NH
        self.vl = max(LANES // bh, 1)
        self.lanes = self.vl * bh
        self.tb = 32 if t % 32 == 0 else t

    def kop(self, x, k=HEAD_DIM):
        x = x.reshape(self.b, self.t, NH, k).transpose(1, 3, 0, 2).reshape(self.t, k, self.b * NH)
        return jnp.tile(x, (1, 1, self.vl))

    def sop(self, x):
        x = x.reshape(self.b, self.t, NH, HEAD_DIM)[..., 0].transpose(1, 0, 2).reshape(self.t, 1, self.b * NH)
        return jnp.tile(x, (1, 1, self.vl))

    def vop(self, x, v=HEAD_DIM):
        x = x.reshape(self.b, self.t, NH, self.vl, v // self.vl).transpose(1, 4, 3, 0, 2)
        return x.reshape(self.t, v // self.vl, self.lanes)

    def unvop(self, o, v=HEAD_DIM):
        o = o.reshape(self.t, v // self.vl, self.vl, self.b, NH).transpose(3, 0, 4, 2, 1)
        return o.reshape(self.b * self.t, NH * v)

    def state(self, s):
        v, k = s.shape[2], s.shape[3]
        s = s.reshape(self.b, NH, self.vl, v // self.vl, k).transpose(3, 4, 2, 0, 1)
        return s.reshape(v // self.vl, k, self.lanes)

    def unstate(self, s):
        vh, k = s.shape[0], s.shape[1]
        s = s.reshape(vh, k, self.vl, self.b, NH).transpose(3, 4, 2, 0, 1)
        return s.reshape(self.b, NH, vh * self.vl, k)

    def kstate(self, n):
        k = n.shape[2]
        return jnp.tile(n.transpose(2, 0, 1).reshape(k, self.b * NH), (1, self.vl))

    def unkstate(self, n):
        return n[:, :self.b * NH].reshape(n.shape[0], self.b, NH).transpose(1, 2, 0)


def _rwkv_scan(lay, rw, s0):
    cols = lambda i: rw[:, i * GW:(i + 1) * GW]
    o, (s,) = _scan_call(
        _rwkv_scan_kernel,
        [lay.kop(cols(4)), lay.kop(cols(3)), lay.kop(cols(5)), lay.kop(cols(1)), lay.kop(cols(0)), lay.vop(cols(2))],
        [s0], lay.t, HEAD_DIM // lay.vl, lay.lanes, lay.tb)
    return lay.unvop(o), s


def _segmean(a, s, n):
    return _dot_hi(a, s) * (1.0 / n)


def _rwkv_post(o, rwp_ref, pq, s64):
    d = o - _segmean(o, s64, HEAD_DIM)
    var = _segmean(d * d, s64, HEAD_DIM)
    return (d * lax.rsqrt(var + RWKV_GN_EPS) * pq[0:1] + pq[1:2] + rwp_ref[:, 0:256]) * rwp_ref[:, 256:512]


def _out_proj(x, parts, w_ref):
    acc = x
    for i, part in enumerate(parts):
        acc = acc + jnp.dot(part.astype(BF16), w_ref[i * GW:(i + 1) * GW, :], preferred_element_type=F32)
    return acc


def _post_step_kernel(x_ref, orw_ref, ohg_ref, oss_ref, oml_ref, rwp_ref, hgp_ref, ssp_ref, mlp_ref, pq_ref, seg64_ref,
                      seg128_ref, w_ref, o_ref):
    pq = pq_ref[...]
    s64 = seg64_ref[...]
    m_r = _rwkv_post(orw_ref[...], rwp_ref, pq, s64)
    o = ohg_ref[...]
    m_h = o * lax.rsqrt(_segmean(o * o, s64, HEAD_DIM) + NORM_EPS) * pq[2:3] * hgp_ref[...]
    y = (oss_ref[...] + ssp_ref[:, 0:256]) * ssp_ref[:, 256:512]
    m_s = y * lax.rsqrt(_segmean(y * y, seg128_ref[...], 128) + NORM_EPS) * pq[3:4]
    o = oml_ref[...]
    m_m = o * lax.rsqrt(_segmean(o * o, s64, HEAD_DIM) + NORM_EPS) * pq[4:5] * mlp_ref[...]
    o_ref[...] = _out_proj(x_ref[...], (m_r, m_h, m_s, m_m), w_ref)


def _post_seq_kernel(x_ref, orw_ref, rwp_ref, mh_ref, ms_ref, mm_ref, pq_ref, seg64_ref, w_ref, o_ref):
    m_r = _rwkv_post(orw_ref[...], rwp_ref, pq_ref[...], seg64_ref[...])
    o_ref[...] = _out_proj(x_ref[...], (m_r, mh_ref[...], ms_ref[...], mm_ref[...]), w_ref)


def _post_call(kernel_fn, x, row_ins, consts):
    m = x.shape[0]
    tm = min(m, 512)
    rows = lambda a: pl.BlockSpec((tm, a.shape[1]), lambda i: (i, 0))
    const = lambda a: pl.BlockSpec(a.shape, lambda i: (0, 0))
    return pl.pallas_call(
        kernel_fn,
        grid=(m // tm,),
        in_specs=[rows(a) for a in (x,) + tuple(row_ins)] + [const(a) for a in consts],
        out_specs=rows(x),
        out_shape=jax.ShapeDtypeStruct((m, D_MODEL), F32),
        compiler_params=_cparams(("parallel",)),
    )(x, *row_ins, *consts)


def _xattn_seq_kernel(q_ref, k_ref, v_ref, x_ref, wo_ref, o_ref):
    acc = x_ref[...]
    for h in range(XH):
        sl = slice(h * XHD, (h + 1) * XHD)
        s = _dot_nt(q_ref[:, sl], k_ref[:, sl]) * (XHD ** -0.5)
        e = jnp.exp(s - jnp.max(s, axis=-1, keepdims=True))
        pr = e / jnp.sum(e, axis=-1, keepdims=True)
        acc = acc + _dot(_dot(pr, v_ref[:, sl]), wo_ref[sl, :])
    o_ref[...] = acc


def _xattn_seq(q, kv, x, wo, bsz, t):
    tq = 512
    nt = t // tq
    return pl.pallas_call(
        _xattn_seq_kernel,
        grid=(bsz, nt),
        in_specs=[pl.BlockSpec((tq, D_MODEL), lambda b, i: (b * nt + i, 0)),
                  pl.BlockSpec((N_MEM, D_MODEL), lambda b, i: (b, 0)),
                  pl.BlockSpec((N_MEM, D_MODEL), lambda b, i: (b, 1)),
                  pl.BlockSpec((tq, D_MODEL), lambda b, i: (b * nt + i, 0)),
                  pl.BlockSpec((D_MODEL, D_MODEL), lambda b, i: (0, 0))],
        out_specs=pl.BlockSpec((tq, D_MODEL), lambda b, i: (b * nt + i, 0)),
        out_shape=jax.ShapeDtypeStruct((bsz * t, D_MODEL), F32),
        compiler_params=_cparams(("parallel", "parallel")),
    )(q, kv, kv, x, wo)


def _xattn_step_kernel(bb, q_ref, k_ref, v_ref, o_ref):
    for i in range(bb):
        t = k_ref[i] * q_ref[i]
        s = jnp.sum(t, axis=-1, keepdims=True) * (XHD ** -0.5)
        e = jnp.exp(s - jnp.max(s, axis=0, keepdims=True))
        pr = e / jnp.sum(e, axis=0, keepdims=True)
        o_ref[i] = jnp.sum(pr * v_ref[i], axis=0)


def _xattn_step(q, cache_k, cache_v, layer):
    bsz = q.shape[0]
    bb = 2
    cache_spec = pl.BlockSpec((None, bb, N_MEM, XH, XHD), lambda i: (layer, i, 0, 0, 0))
    return pl.pallas_call(
        functools.partial(_xattn_step_kernel, bb),
        grid=(bsz // bb,),
        in_specs=[pl.BlockSpec((bb, XH, XHD), lambda i: (i, 0, 0)), cache_spec, cache_spec],
        out_specs=pl.BlockSpec((bb, XH, XHD), lambda i: (i, 0, 0)),
        out_shape=jax.ShapeDtypeStruct((bsz, XH, XHD), F32),
        compiler_params=_cparams(("parallel",)),
    )(q.reshape(bsz, XH, XHD), cache_k, cache_v).reshape(bsz, D_MODEL)


def _mixers_step(lay, scan, states):
    rw, hg, ss, ml = scan
    t = lay.t
    cols = lambda a, i, w=GW: a[:, i * w:(i + 1) * w]
    vh = HEAD_DIM // lay.vl
    o_rw, s_rw = _rwkv_scan(lay, rw, states["wkv"])
    o_hg, (s_hg,) = _scan_call(
        _hgrn_scan_kernel,
        [lay.kop(cols(hg, 1)), lay.kop(cols(hg, 2)), lay.kop(cols(hg, 0)), lay.vop(cols(hg, 3))],
        [states["hgrn"]], t, vh, lay.lanes, lay.tb)
    bc = ss[:, 512:1024].reshape(-1, 2, 2, SSM_STATE)
    b_h = jnp.repeat(bc[:, 0], NH // 2, axis=1).reshape(-1, NH * SSM_STATE)
    c_h = jnp.repeat(bc[:, 1], NH // 2, axis=1).reshape(-1, NH * SSM_STATE)
    o_ss, (s_ss,) = _scan_call(
        _ssm_scan_kernel,
        [lay.kop(b_h, SSM_STATE), lay.kop(c_h, SSM_STATE), lay.sop(cols(ss, 1)), lay.vop(cols(ss, 0))],
        [states["ssm"]], t, vh, lay.lanes, lay.tb)
    o_ml, (s_c, s_n, s_m) = _scan_call(
        _mlstm_scan_kernel,
        [lay.kop(cols(ml, 0)), lay.kop(cols(ml, 1)), lay.sop(cols(ml, 3)), lay.sop(cols(ml, 4)), lay.vop(cols(ml, 2))],
        [states["C"], states["n"], states["m"]], t, vh, lay.lanes, lay.tb)
    outs = [o_rw] + [lay.unvop(o) for o in (o_hg, o_ss, o_ml)]
    new_states = {"wkv": s_rw, "hgrn": s_hg, "ssm": s_ss, "C": s_c, "n": s_n, "m": s_m}
    return outs, new_states


def _export_states(lay, st):
    bh = lay.b * NH
    return (lay.unstate(st["wkv"]),
            lay.unstate(st["hgrn"]).transpose(0, 1, 3, 2),
            lay.unstate(st["ssm"]),
            lay.unstate(st["C"]),
            lay.unkstate(st["n"]),
            st["m"][0, :bh].reshape(lay.b, NH))


def _import_states(lay, wkv, hgrn, ssm, c, n, m):
    return {"wkv": lay.state(wkv), "hgrn": lay.state(hgrn.transpose(0, 1, 3, 2)), "ssm": lay.state(ssm),
            "C": lay.state(c), "n": lay.kstate(n), "m": jnp.tile(m.reshape(1, lay.b * NH), (1, lay.vl))}


def kernel(x_prompt, x_sample, state_rwkv_wkv, state_rwkv_shift, state_hgrn, state_ssm, state_ssm_conv, state_mlstm_C, state_mlstm_n, state_mlstm_m, cache_mem_k, cache_mem_v, mem_prompt, norm_mix_g, w_in, w_out, rwkv_mu, rwkv_w0, rwkv_w_up, rwkv_a0, rwkv_a_up, rwkv_g_up, rwkv_k_k, rwkv_k_a, rwkv_r_k, rwkv_ln_g, rwkv_ln_b, hgrn_lb_logits, hgrn_norm_g, ssm_conv_w, ssm_conv_b, ssm_dt_bias, ssm_A_log, ssm_D, ssm_norm_g, mlstm_i_bias, mlstm_f_bias, mlstm_norm_g, norm_x_g, norm_mem_g, xattn_wq, xattn_wk, xattn_wv, xattn_wo, norm_ff_g, ff_w1, ff_w2, final_norm_g):
    bp, tp, _ = x_prompt.shape
    bs = x_sample.shape[0]
    lay_p, lay_s = _Lay(bp, tp), _Lay(bs, 1)
    seg64, seg128 = _seg_matrix(HEAD_DIM), _seg_matrix(128)
    seg64_b = seg64.astype(BF16)
    gate_ex = _gate_expand()
    tril_c = jnp.asarray(np.tril(np.ones((CHUNK, CHUNK), np.float32)))
    tril_s = jnp.asarray(np.tril(np.ones((SUB, SUB), np.float32)))
    rep = lambda a: jnp.repeat(a, HEAD_DIM)
    pad_rows = lambda a, n: jnp.pad(a, ((0, n - a.shape[0]), (0, 0)))
    zeros = lambda *s: jnp.zeros(s, F32)

    hp = x_prompt.reshape(bp * tp, D_MODEL)
    hs = x_sample.reshape(bs, D_MODEL)
    mem = mem_prompt.reshape(bp * N_MEM, D_MODEL)
    p_states, s_states, p_mem = [], [], []

    for l in range(DEPTH):
        w_in_re = _gather_cols(w_in[l], _IN_IDX).astype(BF16)
        pv = pad_rows(jnp.stack([rwkv_w0[l], rwkv_a0[l], rwkv_k_k[l], rwkv_k_a[l], rwkv_r_k[l].reshape(GW),
                                 rep(ssm_dt_bias[l]), rep(ssm_A_log[l]), rep(ssm_D[l]), rep(mlstm_i_bias[l]),
                                 rep(mlstm_f_bias[l])]), 16)
        lora = jnp.stack([pad_rows(rwkv_w_up[l], LANES), pad_rows(rwkv_a_up[l], LANES), pad_rows(rwkv_g_up[l], LANES)])
        conv = pad_rows(jnp.concatenate([ssm_conv_w[l], ssm_conv_b[l][None]], axis=0), 8)
        mu = _gather_cols(rwkv_mu[l], _RW_IDX).reshape(1, RW_W)
        gp = pad_rows(jnp.stack([_gate_row(ssm_dt_bias[l], mlstm_i_bias[l], mlstm_f_bias[l]),
                                 _gate_row(ssm_A_log[l], zeros(NH), zeros(NH))]), 8)
        pd = pad_rows(jnp.stack([rep(ssm_D[l]), ssm_norm_g[l]]), 8)
        pn_h = pad_rows(hgrn_norm_g[l][None], 8)
        pn_m = pad_rows(mlstm_norm_g[l][None], 8)
        pq = pad_rows(jnp.stack([rwkv_ln_g[l], rwkv_ln_b[l], hgrn_norm_g[l], ssm_norm_g[l], mlstm_norm_g[l]]), 8)
        w_out_b = w_out[l].astype(BF16)
        wq_b, wo_b = xattn_wq[l].astype(BF16), xattn_wo[l].astype(BF16)
        wkv_b = jnp.concatenate([xattn_wk[l], xattn_wv[l]], axis=1).astype(BF16)
        w1_b, w2_b = ff_w1[l].astype(BF16), ff_w2[l].astype(BF16)

        kv = _norm_matmul(mem, norm_mem_g[l], wkv_b)
        p_mem.append((kv[:, :D_MODEL].reshape(bp, N_MEM, XH, XHD), kv[:, D_MODEL:].reshape(bp, N_MEM, XH, XHD)))
        proj = _norm_matmul(hp, norm_mix_g[l], w_in_re)
        m_h, (st_h,) = _chunk_call(
            functools.partial(_hgrn_chunk_kernel, l), proj, [(HG_W, HG_OFF // HG_W)],
            [hgrn_lb_logits, pn_h, tril_s, seg64_b, seg64], [(GW, GW)],
            [pltpu.VMEM((CHUNK, GW), F32)] * 3, bp, tp)
        m_s, (st_s,) = _chunk_call(
            _ssm_chunk_kernel, proj, [(SS_W, SS_OFF // SS_W), (GT_W, GT_OFF // GT_W)],
            [conv, gp, pd, tril_c], [(NH, SSM_STATE, HEAD_DIM)], [pltpu.VMEM((SUBLANES, 768), F32)], bp, tp)
        m_m, (st_c, st_n, st_m) = _chunk_call(
            _mlstm_chunk_kernel, proj, [(ML_W, ML_OFF // ML_W), (GT_W, GT_OFF // GT_W)],
            [gp, pn_m, tril_c], [(NH, HEAD_DIM, HEAD_DIM), (SUBLANES, HEAD_DIM), (1, LANES)], [], bp, tp)
        rw_scan, rw_post = _prep_rw_seq(proj, zeros(bp, SUBLANES, RW_W), mu, pv, lora, seg64, bp, tp)
        o_rw, st_r = _rwkv_scan(lay_p, rw_scan, lay_p.state(zeros(bp, NH, HEAD_DIM, HEAD_DIM)))
        proj3 = proj.reshape(bp, tp, IN_RE)
        shift_new = jnp.take(proj3[:, -1, RW_OFF:RW_OFF + RW_W], jnp.asarray(_RW_INV), axis=-1)
        conv_new = proj3[:, tp - 3:, SS_OFF + 256:SS_OFF + 1024]
        st_h5 = st_h.reshape(bp, NH, HEAD_DIM, NH, HEAD_DIM)
        hgrn_new = jnp.stack([st_h5[:, h, :, h, :] for h in range(NH)], axis=1).transpose(0, 1, 3, 2)
        p_states.append((lay_p.unstate(st_r), shift_new, hgrn_new, st_s.transpose(0, 1, 3, 2), conv_new,
                         st_c.transpose(0, 1, 3, 2), st_n[:, :NH], st_m[:, 0, :NH]))
        hp = _post_call(_post_seq_kernel, hp, (o_rw, rw_post, m_h, m_s, m_m), (pq, seg64, w_out_b))
        q = _norm_matmul(hp, norm_x_g[l], wq_b)
        hp = _xattn_seq(q, kv, hp, wo_b, bp, tp)
        hp = _ffn(hp, norm_ff_g[l], w1_b, w2_b)

        proj = _norm_matmul(hs, norm_mix_g[l], w_in_re)
        prev_rw = _gather_cols(state_rwkv_shift[l], _RW_IDX)
        cb = state_ssm_conv[l]
        prep = _prep_step(proj, prev_rw, cb[:, 0], cb[:, 1], cb[:, 2],
                          (mu, pv, lora, seg64, hgrn_lb_logits, conv, gate_ex), l)
        st0 = _import_states(lay_s, state_rwkv_wkv[l], state_hgrn[l], state_ssm[l], state_mlstm_C[l],
                             state_mlstm_n[l], state_mlstm_m[l])
        o_parts, st = _mixers_step(lay_s, prep[0::2], st0)
        shift_new = jnp.take(proj[:, RW_OFF:RW_OFF + RW_W], jnp.asarray(_RW_INV), axis=-1)
        conv_new = jnp.concatenate([cb[:, 1:], proj[:, None, SS_OFF + 256:SS_OFF + 1024]], axis=1)
        wkv_n, hg_n, ssm_n, c_n, n_n, m_n = _export_states(lay_s, st)
        s_states.append((wkv_n, shift_new, hg_n, ssm_n, conv_new, c_n, n_n, m_n))
        hs = _post_call(_post_step_kernel, hs, tuple(o_parts) + tuple(prep[1::2]), (pq, seg64, seg128, w_out_b))
        q = _norm_matmul(hs, norm_x_g[l], wq_b)
        att = _xattn_step(q, cache_mem_k, cache_mem_v, l)
        hs = _mm_res(att, wo_b, hs)
        hs = _ffn(hs, norm_ff_g[l], w1_b, w2_b)

    y_prompt = _rmsnorm(hp, final_norm_g).reshape(bp, tp, D_MODEL)
    y_sample = _rmsnorm(hs, final_norm_g).reshape(bs, 1, D_MODEL)
    p_out = [jnp.stack(s) for s in zip(*p_states)]
    s_out = [jnp.stack(s) for s in zip(*s_states)]
    p_mem_k = jnp.stack([kv[0] for kv in p_mem])
    p_mem_v = jnp.stack([kv[1] for kv in p_mem])
    return (y_prompt, y_sample, *p_out, p_mem_k, p_mem_v, *s_out)
```

```python
import functools

import numpy as np
import jax
import jax.numpy as jnp
from jax import lax
from jax.experimental import pallas as pl
from jax.experimental.pallas import tpu as pltpu

F32 = jnp.float32
BF16 = jnp.bfloat16
HI = lax.Precision.HIGHEST

D_MODEL = 1024
DEPTH = 2
NH = 4
HEAD_DIM = 64
GW = 256
SSM_STATE = 128
N_MEM = 256
XH = 4
XHD = 256
D_FF = 4096
NORM_EPS = 1e-6
RWKV_GN_EPS = 64e-5
RWKV_COLS = 896
LANES = 128
SUBLANES = 8
VMEM_LIMIT = 48 * 1024 * 1024
CHUNK = 128
SUB = 16

HG_OFF, HG_W = 0, 1024
SS_OFF, SS_W = 1024, 1024
ML_OFF, ML_W = 2048, 1024
GT_OFF, GT_W = 3072, 128
RW_OFF, RW_W = 3456, 1152
IN_RE = 4608
GT_DT, GT_I, GT_F = 0, 4, 8


def _in_col_index():
    idx = -np.ones(IN_RE, np.int32)
    idx[HG_OFF:HG_OFF + 1024] = 896 + np.arange(1024)
    ss = 1920
    idx[SS_OFF:SS_OFF + 1024] = ss + np.arange(1024)
    ml = 2948
    idx[ML_OFF:ML_OFF + 768] = ml + np.arange(768)
    idx[ML_OFF + 768:ML_OFF + 1024] = ml + 776 + np.arange(256)
    idx[GT_OFF + GT_DT:GT_OFF + GT_DT + NH] = ss + 1024 + np.arange(NH)
    idx[GT_OFF + GT_I:GT_OFF + GT_I + NH] = ml + 768 + np.arange(NH)
    idx[GT_OFF + GT_F:GT_OFF + GT_F + NH] = ml + 772 + np.arange(NH)
    idx[RW_OFF:RW_OFF + 768] = np.arange(768)
    idx[RW_OFF + 768:RW_OFF + 800] = 768 + np.arange(32)
    idx[RW_OFF + 896:RW_OFF + 928] = 800 + np.arange(32)
    idx[RW_OFF + 1024:RW_OFF + 1088] = 832 + np.arange(64)
    return idx


_IN_IDX = _in_col_index()
_RW_IDX = _IN_IDX[RW_OFF:RW_OFF + RW_W]
_RW_INV = np.concatenate([np.arange(800), 896 + np.arange(32), 1024 + np.arange(64)]).astype(np.int32)


def _gather_cols(a, idx):
    g = jnp.take(a, jnp.asarray(np.maximum(idx, 0)), axis=-1)
    return jnp.where(jnp.asarray(idx >= 0), g, jnp.zeros((), a.dtype))


def _seg_matrix(width):
    i = np.arange(GW)
    return jnp.asarray((i[:, None] // width == i[None, :] // width).astype(np.float32))


def _gate_expand():
    e = np.zeros((3, LANES, GW), np.float32)
    for s, off in enumerate((GT_DT, GT_I, GT_F)):
        for h in range(NH):
            e[s, off + h, h * HEAD_DIM:(h + 1) * HEAD_DIM] = 1.0
    return jnp.asarray(e)


def _gate_row(dt_bias, i_bias, f_bias):
    z = jnp.zeros((LANES,), F32)
    return z.at[GT_DT:GT_DT + NH].set(dt_bias).at[GT_I:GT_I + NH].set(i_bias).at[GT_F:GT_F + NH].set(f_bias)


def _cparams(sem):
    return pltpu.CompilerParams(dimension_semantics=sem, vmem_limit_bytes=VMEM_LIMIT)


def _softplus(x):
    return jnp.maximum(x, 0.0) + jnp.log(1.0 + jnp.exp(-jnp.abs(x)))


def _sigmoid(x):
    return 1.0 / (1.0 + jnp.exp(-x))


def _rms(x, g):
    ms = jnp.mean(x * x, axis=-1, keepdims=True)
    return x * lax.rsqrt(ms + NORM_EPS) * g


def _dot(a, b):
    return jnp.dot(a.astype(BF16), b.astype(BF16), preferred_element_type=F32)


def _dot_nt(a, b):
    return lax.dot_general(a.astype(BF16), b.astype(BF16), (((1,), (1,)), ((), ())), preferred_element_type=F32)


def _dot_tn(a, b):
    return lax.dot_general(a.astype(BF16), b.astype(BF16), (((0,), (0,)), ((), ())), preferred_element_type=F32)


def _dot_hi(a, b):
    return jnp.dot(a, b, precision=HI, preferred_element_type=F32)


def _norm_matmul_kernel(x_ref, g_ref, w_ref, o_ref, xn_ref):
    @pl.when(pl.program_id(1) == 0)
    def _():
        xn_ref[...] = _rms(x_ref[...], g_ref[...]).astype(BF16)

    o_ref[...] = jnp.dot(xn_ref[...], w_ref[...], preferred_element_type=F32)


def _row_tile(m):
    return min(m, 1024)


def _norm_matmul(x, g, w):
    m, k = x.shape
    n = w.shape[1]
    tm = _row_tile(m)
    tn = next(c for c in (1536, 1024, 512, n) if n % c == 0)
    return pl.pallas_call(
        _norm_matmul_kernel,
        grid=(m // tm, n // tn),
        in_specs=[pl.BlockSpec((tm, k), lambda i, j: (i, 0)),
                  pl.BlockSpec((1, k), lambda i, j: (0, 0)),
                  pl.BlockSpec((k, tn), lambda i, j: (0, j))],
        out_specs=pl.BlockSpec((tm, tn), lambda i, j: (i, j)),
        out_shape=jax.ShapeDtypeStruct((m, n), F32),
        scratch_shapes=[pltpu.VMEM((tm, k), BF16)],
        compiler_params=_cparams(("parallel", "arbitrary")),
    )(x, g.reshape(1, k), w)


def _mm_res_kernel(a_ref, w_ref, res_ref, o_ref):
    o_ref[...] = res_ref[...] + jnp.dot(a_ref[...].astype(BF16), w_ref[...], preferred_element_type=F32)


def _mm_res(a, w, res):
    m, k = a.shape
    n = w.shape[1]
    tm = min(m, 512)
    return pl.pallas_call(
        _mm_res_kernel,
        grid=(m // tm,),
        in_specs=[pl.BlockSpec((tm, k), lambda i: (i, 0)),
                  pl.BlockSpec((k, n), lambda i: (0, 0)),
                  pl.BlockSpec((tm, n), lambda i: (i, 0))],
        out_specs=pl.BlockSpec((tm, n), lambda i: (i, 0)),
        out_shape=jax.ShapeDtypeStruct((m, n), F32),
        compiler_params=_cparams(("parallel",)),
    )(a, w, res)


def _ffn_kernel(x_ref, g_ref, w1_ref, w2_ref, o_ref, xn_ref, acc_ref):
    j = pl.program_id(1)

    @pl.when(j == 0)
    def _():
        xn_ref[...] = _rms(x_ref[...], g_ref[...]).astype(BF16)
        acc_ref[...] = jnp.zeros_like(acc_ref)

    h = jnp.dot(xn_ref[...], w1_ref[...], preferred_element_type=F32)
    h = jnp.square(jnp.maximum(h, 0.0))
    acc_ref[...] += jnp.dot(h.astype(BF16), w2_ref[...], preferred_element_type=F32)

    @pl.when(j == pl.num_programs(1) - 1)
    def _():
        o_ref[...] = x_ref[...] + acc_ref[...]


def _ffn(x, g, w1, w2):
    m, k = x.shape
    ff = w1.shape[1]
    tm = _row_tile(m)
    tf = 1024
    return pl.pallas_call(
        _ffn_kernel,
        grid=(m // tm, ff // tf),
        in_specs=[pl.BlockSpec((tm, k), lambda i, j: (i, 0)),
                  pl.BlockSpec((1, k), lambda i, j: (0, 0)),
                  pl.BlockSpec((k, tf), lambda i, j: (0, j)),
                  pl.BlockSpec((tf, k), lambda i, j: (j, 0))],
        out_specs=pl.BlockSpec((tm, k), lambda i, j: (i, 0)),
        out_shape=jax.ShapeDtypeStruct((m, k), F32),
        scratch_shapes=[pltpu.VMEM((tm, k), BF16), pltpu.VMEM((tm, k), F32)],
        compiler_params=_cparams(("parallel", "arbitrary")),
    )(x, g.reshape(1, k), w1, w2)


def _rmsnorm_kernel(x_ref, g_ref, o_ref):
    o_ref[...] = _rms(x_ref[...], g_ref[...])


def _rmsnorm(x, g):
    m, k = x.shape
    tm = min(m, 512)
    return pl.pallas_call(
        _rmsnorm_kernel,
        grid=(m // tm,),
        in_specs=[pl.BlockSpec((tm, k), lambda i: (i, 0)), pl.BlockSpec((1, k), lambda i: (0, 0))],
        out_specs=pl.BlockSpec((tm, k), lambda i: (i, 0)),
        out_shape=jax.ShapeDtypeStruct((m, k), F32),
        compiler_params=_cparams(("parallel",)),
    )(x, g.reshape(1, k))


def _shift_rows(x, tail, j):
    xr = pltpu.roll(x, j, 0)
    tr = pltpu.roll(tail, j, 0)
    rows = lax.broadcasted_iota(jnp.int32, tr.shape, 0)
    first = jnp.where(rows < j, tr, xr[0:SUBLANES])
    return jnp.concatenate([first, xr[SUBLANES:]], axis=0)


def _hgrn_lower_bound(logit_ref, layer):
    lg = logit_ref[...]
    e = jnp.exp(lg - jnp.max(lg, axis=0, keepdims=True))
    prb = e / jnp.sum(e, axis=0, keepdims=True)
    return jnp.sum(prb[0:layer + 1], axis=0, keepdims=True) - prb[0:1]


def _conv_silu(u, c3, c2, c1, conv_ref):
    cw = conv_ref[...]
    conv = cw[4:5] + c3 * cw[0:1] + c2 * cw[1:2] + c1 * cw[2:3] + u * cw[3:4]
    return conv * _sigmoid(conv)


def _prep_rwkv(pr, prev_rw, mu_ref, pv_ref, lora_ref, seg_ref, rw_scan, rw_post, log_decay):
    pv = pv_ref[...]
    row = lambda i: pv[i:i + 1, :]
    seg = seg_ref[...]
    pm = pr + (prev_rw - pr) * mu_ref[...]
    r, k, v = pm[:, 0:256], pm[:, 256:512], pm[:, 512:768]
    dw, da, dg = pm[:, 768:896], pm[:, 896:1024], pm[:, 1024:1152]
    w_log = -_softplus(-(row(0) + _dot_hi(jnp.tanh(dw), lora_ref[0]))) - 0.5
    a = _sigmoid(row(1) + _dot_hi(da, lora_ref[1]))
    kk = k * row(2)
    kk = kk / jnp.maximum(jnp.sqrt(_dot_hi(kk * kk, seg)), 1e-12)
    k2 = k * (1.0 + (a - 1.0) * row(3))
    rw_scan[:, 0:256] = r
    rw_scan[:, 256:512] = k2
    rw_scan[:, 512:768] = v
    rw_scan[:, 768:1024] = -jnp.exp(w_log) if log_decay else jnp.exp(-jnp.exp(w_log))
    rw_scan[:, 1024:1280] = kk
    rw_scan[:, 1280:1536] = kk * a
    rw_post[:, 0:256] = _dot_hi(r * k2 * row(4), seg) * v
    rw_post[:, 256:512] = _dot_hi(_sigmoid(dg), lora_ref[2])


def _prep_rw_seq_kernel(p_ref, prev8_ref, tail0_ref, mu_ref, pv_ref, lora_ref, seg_ref, rw_scan, rw_post):
    tail = jnp.where(pl.program_id(1) == 0, tail0_ref[0], prev8_ref[...])
    pr = p_ref[...]
    _prep_rwkv(pr, _shift_rows(pr, tail, 1), mu_ref, pv_ref, lora_ref, seg_ref, rw_scan, rw_post, True)


def _prep_rw_seq(proj, tail0, mu, pv, lora, seg, bsz, t):
    tt = 512
    nt = t // tt
    m = bsz * t
    cb = RW_OFF // RW_W
    c2 = lambda b, i: (0, 0)
    return pl.pallas_call(
        _prep_rw_seq_kernel,
        grid=(bsz, nt),
        in_specs=[pl.BlockSpec((tt, RW_W), lambda b, i: (b * nt + i, cb)),
                  pl.BlockSpec((SUBLANES, RW_W),
                               lambda b, i: (jnp.maximum((b * nt + i) * (tt // SUBLANES) - 1, 0), cb)),
                  pl.BlockSpec((1, SUBLANES, RW_W), lambda b, i: (b, 0, 0)),
                  pl.BlockSpec((1, RW_W), c2), pl.BlockSpec((16, GW), c2),
                  pl.BlockSpec((3, LANES, GW), lambda b, i: (0, 0, 0)), pl.BlockSpec((GW, GW), c2)],
        out_specs=[pl.BlockSpec((tt, 1536), lambda b, i: (b * nt + i, 0)),
                   pl.BlockSpec((tt, 512), lambda b, i: (b * nt + i, 0))],
        out_shape=[jax.ShapeDtypeStruct((m, 1536), F32), jax.ShapeDtypeStruct((m, 512), F32)],
        compiler_params=_cparams(("parallel", "parallel")),
    )(proj, proj, tail0, mu, pv, lora, seg)


PREP_OUT_WIDTHS = (1536, 512, 1024, 256, 1024, 512, 1280, 256)


def _prep_step_kernel(layer, p_ref, prev_ref, c3_ref, c2_ref, c1_ref, mu_ref, pv_ref, lora_ref, seg_ref, logit_ref,
                      conv_ref, ex_ref, rw_scan, rw_post, hg_scan, hg_post, ss_scan, ss_post, ml_scan, ml_post):
    pv = pv_ref[...]
    row = lambda i: pv[i:i + 1, :]
    _prep_rwkv(p_ref[:, RW_OFF:RW_OFF + RW_W], prev_ref[...], mu_ref, pv_ref, lora_ref, seg_ref, rw_scan, rw_post,
               False)
    gates = p_ref[:, GT_OFF:GT_OFF + GT_W]
    dt_pre, i_pre, f_pre_m = (_dot_hi(gates, ex_ref[s]) for s in range(3))

    lb = _hgrn_lower_bound(logit_ref, layer)
    q_pre = p_ref[:, HG_OFF:HG_OFF + 256]
    f_pre = p_ref[:, HG_OFF + 256:HG_OFF + 512]
    gg = p_ref[:, HG_OFF + 768:HG_OFF + 1024]
    hg_scan[:, 0:256] = q_pre * _sigmoid(q_pre)
    hg_scan[:, 256:512] = lb + (1.0 - lb) * _sigmoid(f_pre)
    hg_scan[:, 512:768] = (1.0 - lb) * _sigmoid(-f_pre)
    hg_scan[:, 768:1024] = p_ref[:, HG_OFF + 512:HG_OFF + 768]
    hg_post[...] = gg * _sigmoid(gg)

    z = p_ref[:, SS_OFF:SS_OFF + 256]
    xbc = _conv_silu(p_ref[:, SS_OFF + 256:SS_OFF + 1024], c3_ref[...], c2_ref[...], c1_ref[...], conv_ref)
    xs = xbc[:, 0:256]
    dt = _softplus(dt_pre + row(5))
    ss_scan[:, 0:256] = xs * dt
    ss_scan[:, 256:512] = jnp.exp(dt * (-jnp.exp(row(6))))
    ss_scan[:, 512:1024] = xbc[:, 256:768]
    ss_post[:, 0:256] = xs * row(7)
    ss_post[:, 256:512] = z * _sigmoid(z)

    ml_scan[:, 0:256] = p_ref[:, ML_OFF:ML_OFF + 256]
    ml_scan[:, 256:512] = p_ref[:, ML_OFF + 256:ML_OFF + 512] * (HEAD_DIM ** -0.5)
    ml_scan[:, 512:768] = p_ref[:, ML_OFF + 512:ML_OFF + 768]
    ml_scan[:, 768:1024] = i_pre + row(8)
    ml_scan[:, 1024:1280] = -_softplus(-(f_pre_m + row(9)))
    ml_post[...] = _sigmoid(p_ref[:, ML_OFF + 768:ML_OFF + 1024])


def _prep_step(proj, prev_rw, c3, c2, c1, params, layer):
    m = proj.shape[0]
    full = lambda a: pl.BlockSpec(a.shape, lambda i: (0,) * a.ndim)
    ins = (proj, prev_rw, c3, c2, c1) + tuple(params)
    return pl.pallas_call(
        functools.partial(_prep_step_kernel, layer),
        grid=(1,),
        in_specs=[full(a) for a in ins],
        out_specs=[pl.BlockSpec((m, w), lambda i: (0, 0)) for w in PREP_OUT_WIDTHS],
        out_shape=[jax.ShapeDtypeStruct((m, w), F32) for w in PREP_OUT_WIDTHS],
        compiler_params=_cparams(("arbitrary",)),
    )(*ins)


def _ssm_chunk_kernel(ss_ref, gt_ref, conv_ref, gp_ref, pd_ref, tril_ref, o_ref, ht_ref, tail_ref):
    @pl.when(pl.program_id(1) == 0)
    def _():
        ht_ref[...] = jnp.zeros_like(ht_ref)
        tail_ref[...] = jnp.zeros_like(tail_ref)

    L = ss_ref.shape[0]
    u = ss_ref[:, 256:1024]
    tail = tail_ref[...]
    c1, c2, c3 = (_shift_rows(u, tail, j) for j in (1, 2, 3))
    tail_ref[...] = u[L - SUBLANES:L]
    xbc = _conv_silu(u, c3, c2, c1, conv_ref)
    z = ss_ref[:, 0:256]
    zs = z * _sigmoid(z)
    dt = _softplus(gt_ref[...] + gp_ref[0:1])
    la = dt * (-jnp.exp(gp_ref[1:2]))
    tril = tril_ref[...]
    mask = tril > 0.5
    cum = _dot_hi(tril, la)
    cum_t = cum.T
    pd = pd_ref[...]
    for g in range(2):
        bg = xbc[:, 256 + SSM_STATE * g:256 + SSM_STATE * (g + 1)]
        cg = xbc[:, 512 + SSM_STATE * g:512 + SSM_STATE * (g + 1)]
        scores = _dot_nt(cg, bg)
        bg_t = bg.T
        ys = []
        for h in (2 * g, 2 * g + 1):
            sl = slice(h * HEAD_DIM, (h + 1) * HEAD_DIM)
            cc = cum[:, GT_DT + h:GT_DT + h + 1]
            cr = cum_t[GT_DT + h:GT_DT + h + 1, :]
            lm = jnp.where(mask, jnp.exp(jnp.where(mask, cc - cr, 0.0)), 0.0)
            dtc = dt[:, GT_DT + h:GT_DT + h + 1]
            xh = xbc[:, sl]
            ht = ht_ref[0, h]
            y = _dot(scores * lm, xh * dtc) + _dot(cg, ht) * jnp.exp(cc)
            cend = cc[L - 1:L, :]
            ht_ref[0, h] = jnp.exp(cend) * ht + _dot(bg_t, xh * (jnp.exp(cend - cc) * dtc))
            ys.append((y + xh * pd[0:1, sl]) * zs[:, sl])
        ms = (jnp.sum(ys[0] * ys[0], axis=-1, keepdims=True) + jnp.sum(ys[1] * ys[1], axis=-1, keepdims=True))
        scale = lax.rsqrt(ms * (1.0 / (2 * HEAD_DIM)) + NORM_EPS)
        for i, h in enumerate((2 * g, 2 * g + 1)):
            sl = slice(h * HEAD_DIM, (h + 1) * HEAD_DIM)
            o_ref[:, sl] = ys[i] * scale * pd[1:2, sl]


def _mlstm_chunk_kernel(ml_ref, gt_ref, gp_ref, pn_ref, tril_ref, o_ref, ct_ref, n_ref, m_ref):
    @pl.when(pl.program_id(1) == 0)
    def _():
        ct_ref[...] = jnp.zeros_like(ct_ref)
        n_ref[...] = jnp.zeros_like(n_ref)
        m_ref[...] = jnp.zeros_like(m_ref)

    L = ml_ref.shape[0]
    x = gt_ref[...] + gp_ref[0:1]
    lf = -_softplus(-x)
    tril = tril_ref[...]
    mask = tril > 0.5
    b = _dot_hi(tril, lf)
    b_t = b.T
    x_t = x.T
    m_row = m_ref[0]
    m_new_row = m_row
    lane = lax.broadcasted_iota(jnp.int32, m_row.shape, 1)
    pn = pn_ref[...]
    for h in range(NH):
        sl = slice(h * HEAD_DIM, (h + 1) * HEAD_DIM)
        b_col, b_row = b[:, GT_F + h:GT_F + h + 1], b_t[GT_F + h:GT_F + h + 1, :]
        li_col, li_row = x[:, GT_I + h:GT_I + h + 1], x_t[GT_I + h:GT_I + h + 1, :]
        m_prev = m_row[:, h:h + 1]
        log_inter = b_col + m_prev
        log_intra = jnp.where(mask, b_col - b_row + li_row, -jnp.inf)
        m_t = jnp.maximum(log_inter, jnp.max(log_intra, axis=-1, keepdims=True))
        w_inter = jnp.exp(log_inter - m_t)
        qh = ml_ref[:, sl]
        kh = ml_ref[:, 256 + h * HEAD_DIM:256 + (h + 1) * HEAD_DIM] * (HEAD_DIM ** -0.5)
        vh = ml_ref[:, 512 + h * HEAD_DIM:512 + (h + 1) * HEAD_DIM]
        qk = _dot_nt(qh, kh) * jnp.exp(log_intra - m_t)
        ct = ct_ref[0, h]
        n_h = n_ref[0, h:h + 1, :]
        num = _dot(qk, vh) + w_inter * _dot(qh, ct)
        den = jnp.sum(qk, axis=-1, keepdims=True) + w_inter * jnp.sum(qh * n_h, axis=-1, keepdims=True)
        hout = num / jnp.maximum(jnp.abs(den), jnp.exp(-m_t))
        m_end = m_t[L - 1:L, :]
        b_end = b_col[L - 1:L, :]
        g_prev = jnp.exp(b_end + m_prev - m_end)
        kg = kh * jnp.exp(b_end - b_col + li_col - m_end)
        ct_ref[0, h] = g_prev * ct + _dot_tn(kg, vh)
        n_ref[0, h:h + 1, :] = g_prev * n_h + jnp.sum(kg, axis=0, keepdims=True)
        m_new_row = jnp.where(lane == h, m_end, m_new_row)
        hn = hout * lax.rsqrt(jnp.mean(hout * hout, axis=-1, keepdims=True) + NORM_EPS)
        o_ref[:, sl] = hn * pn[0:1, sl] * _sigmoid(ml_ref[:, 768 + h * HEAD_DIM:768 + (h + 1) * HEAD_DIM])
    m_ref[0] = m_new_row


def _hgrn_chunk_kernel(layer, hg_ref, logit_ref, pn_ref, tril_ref, seg_ref, bd_ref, o_ref, st_ref, q_scr, b_scr, i_scr):
    @pl.when(pl.program_id(1) == 0)
    def _():
        st_ref[...] = jnp.zeros_like(st_ref)

    lb = _hgrn_lower_bound(logit_ref, layer)
    tril = tril_ref[...]
    seg = seg_ref[...]
    bd = bd_ref[...]
    pn = pn_ref[...]
    srow = lax.broadcasted_iota(jnp.int32, (SUB, GW), 0)

    st = st_ref[0]
    for c in range(hg_ref.shape[0] // SUB):
        r0 = c * SUB
        rows = slice(r0, r0 + SUB)
        q_pre = hg_ref[rows, 0:256]
        f_pre = hg_ref[rows, 256:512]
        v = hg_ref[rows, 512:768]
        gg = hg_ref[rows, 768:1024]
        q = q_pre * _sigmoid(q_pre)
        k = (1.0 - lb) * _sigmoid(-f_pre)
        b = _dot_hi(tril, jnp.log(lb + (1.0 - lb) * _sigmoid(f_pre)))
        b_end = b[SUB - 1:SUB, :]
        q_scr[rows, :] = q
        b_scr[rows, :] = b
        tiles = []
        for t in range(SUB):
            live = srow <= t
            dec = jnp.where(live, jnp.exp(jnp.where(live, b_scr[r0 + t:r0 + t + 1, :] - b, 0.0)), 0.0)
            tiles.append(dec * q_scr[r0 + t:r0 + t + 1, :] * k)
        a_b = jnp.dot(jnp.concatenate(tiles, axis=0).astype(BF16), seg, preferred_element_type=F32)
        for t in range(SUB):
            i_scr[r0 + t:r0 + t + 1, :] = jnp.sum(a_b[t * SUB:(t + 1) * SUB, :] * v, axis=0, keepdims=True)
        o = _dot_nt(q * jnp.exp(b), st) + i_scr[rows, :]
        st = st * jnp.exp(b_end) + _dot_tn(v, k * jnp.exp(b_end - b)) * bd
        gate = gg * _sigmoid(gg)
        for h in range(NH):
            sl = slice(h * HEAD_DIM, (h + 1) * HEAD_DIM)
            oh = o[:, sl]
            o_ref[rows, sl] = oh * lax.rsqrt(jnp.mean(oh * oh, axis=-1, keepdims=True) + NORM_EPS) * pn[0:1, sl] * gate[:, sl]
    st_ref[0] = st


def _rwkv_chunk_kernel(rw_ref, tril_ref, seg_ref, bd_ref, o_ref, st_ref, c_scr, iw_scr, kx_scr, rx_scr, om_scr, oi_scr):
    @pl.when(pl.program_id(1) == 0)
    def _():
        st_ref[...] = jnp.zeros_like(st_ref)

    tril = tril_ref[...]
    seg = seg_ref[...]
    bd = bd_ref[...]
    srow = lax.broadcasted_iota(jnp.int32, (SUB, GW), 0)
    csum = lambda a: jnp.sum(a, axis=0, keepdims=True)
    st = st_ref[0]
    for c in range(rw_ref.shape[0] // SUB):
        r0 = c * SUB
        rows = slice(r0, r0 + SUB)
        one = lambda t: slice(r0 + t, r0 + t + 1)
        r, k, v = rw_ref[rows, 0:256], rw_ref[rows, 256:512], rw_ref[rows, 512:768]
        lw, kk, ka = rw_ref[rows, 768:1024], rw_ref[rows, 1024:1280], rw_ref[rows, 1280:1536]
        cs = _dot_hi(tril, lw)
        c_end = cs[SUB - 1:SUB, :]
        c_scr[rows, :] = cs
        iw_scr[rows, :] = jnp.exp(-lw)
        kx_scr[rows, :] = kk * jnp.exp(cs - lw)
        rx_scr[rows, :] = r * jnp.exp(cs)
        taa, tak, tra, trk = [], [], [], []
        for t in range(SUB):
            d2 = jnp.where(srow <= t, jnp.exp(jnp.minimum(c_scr[one(t), :] - cs, 0.0)), 0.0)
            d1 = jnp.where(srow < t, d2 * iw_scr[one(t), :], 0.0)
            a1 = d1 * rw_ref[one(t), 1024:1280]
            a2 = d2 * rw_ref[one(t), 0:256]
            taa.append(a1 * ka)
            tak.append(a1 * k)
            tra.append(a2 * ka)
            trk.append(a2 * k)
        mb = jnp.dot(jnp.concatenate(taa + tak + tra + trk, axis=0).astype(BF16), seg, preferred_element_type=F32)
        tile = lambda f, t: mb[(f * SUB + t) * SUB:(f * SUB + t + 1) * SUB, :]
        g_arr = jnp.zeros((SUB, GW), F32)
        u_arr = jnp.zeros((SUB, GW), F32)
        for t in range(SUB):
            maa, ara = tile(0, t), tile(2, t)
            g_t = kx_scr[one(t), :] - csum(maa * g_arr)
            u_t = csum(tile(1, t) * v) - csum(maa * u_arr)
            g_arr = jnp.where(srow == t, g_t, g_arr)
            u_arr = jnp.where(srow == t, u_t, u_arr)
            om_scr[one(t), :] = rx_scr[one(t), :] - csum(ara * g_arr)
            oi_scr[one(t), :] = csum(tile(3, t) * v) - csum(ara * u_arr)
        o_ref[rows, :] = _dot_nt(om_scr[rows, :], st) + oi_scr[rows, :]
        e_end = jnp.exp(c_end - cs)
        a_hat = ka * e_end
        phi = _dot_tn(g_arr, a_hat) * bd
        psi = _dot_tn(jnp.concatenate([v, -u_arr], axis=0), jnp.concatenate([k * e_end, a_hat], axis=0)) * bd
        st = st * jnp.exp(c_end) - _dot(st, phi) + psi
    st_ref[0] = st


def _chunk_call(kernel_fn, proj, col_blocks, consts, state_shapes, scratch, bsz, t):
    nt = t // CHUNK
    m = bsz * t
    rows = lambda w, cb: pl.BlockSpec((CHUNK, w), lambda b, i: (b * nt + i, cb))
    const = lambda a: pl.BlockSpec(a.shape, lambda b, i: (0,) * a.ndim)
    state = lambda s: pl.BlockSpec((1,) + s, lambda b, i: (b,) + (0,) * len(s))
    outs = pl.pallas_call(
        kernel_fn,
        grid=(bsz, nt),
        in_specs=[rows(w, cb) for w, cb in col_blocks] + [const(a) for a in consts],
        out_specs=[rows(GW, 0)] + [state(s) for s in state_shapes],
        out_shape=[jax.ShapeDtypeStruct((m, GW), F32)] + [jax.ShapeDtypeStruct((bsz,) + s, F32) for s in state_shapes],
        scratch_shapes=scratch,
        compiler_params=_cparams(("parallel", "arbitrary")),
    )(*([proj] * len(col_blocks)), *consts)
    return outs[0], outs[1:]


def _scan_loop(tb, body):
    if tb == 1:
        body(0, 0)
    else:
        lax.fori_loop(0, tb, body, 0)


def _rwkv_scan_kernel(tb, vh, kk_ref, w_ref, ka_ref, k_ref, r_ref, v_ref, s0_ref, o_ref, s_ref):
    @pl.when(pl.program_id(1) == 0)
    def _():
        s_ref[...] = s0_ref[...]

    def body(t, c):
        kk, w, ka, k, r = kk_ref[t], w_ref[t], ka_ref[t], k_ref[t], r_ref[t]
        for i in range(vh):
            s = s_ref[i]
            sa = jnp.sum(s * kk, axis=0, keepdims=True)
            s = s * w - sa * ka + v_ref[t, i:i + 1, :] * k
            o_ref[t, i:i + 1, :] = jnp.sum(s * r, axis=0, keepdims=True)
            s_ref[i] = s
        return c

    _scan_loop(tb, body)


def _hgrn_scan_kernel(tb, vh, f_ref, k_ref, q_ref, v_ref, s0_ref, o_ref, s_ref):
    @pl.when(pl.program_id(1) == 0)
    def _():
        s_ref[...] = s0_ref[...]

    def body(t, c):
        f, k, q = f_ref[t], k_ref[t], q_ref[t]
        for i in range(vh):
            s = s_ref[i] * f + v_ref[t, i:i + 1, :] * k
            o_ref[t, i:i + 1, :] = jnp.sum(s * q, axis=0, keepdims=True)
            s_ref[i] = s
        return c

    _scan_loop(tb, body)


def _ssm_scan_kernel(tb, vh, b_ref, c_ref, dec_ref, x_ref, s0_ref, o_ref, s_ref):
    @pl.when(pl.program_id(1) == 0)
    def _():
        s_ref[...] = s0_ref[...]

    def body(t, c):
        bn, cn, dec = b_ref[t], c_ref[t], dec_ref[t]
        for i in range(vh):
            s = s_ref[i] * dec + x_ref[t, i:i + 1, :] * bn
            o_ref[t, i:i + 1, :] = jnp.sum(s * cn, axis=0, keepdims=True)
            s_ref[i] = s
        return c

    _scan_loop(tb, body)


def _mlstm_scan_kernel(tb, vh, q_ref, k_ref, li_ref, lf_ref, v_ref, c0_ref, n0_ref, m0_ref, o_ref, c_ref, n_ref, m_ref):
    @pl.when(pl.program_id(1) == 0)
    def _():
        c_ref[...] = c0_ref[...]
        n_ref[...] = n0_ref[...]
        m_ref[...] = m0_ref[...]

    def body(t, c):
        q, k, li, lf = q_ref[t], k_ref[t], li_ref[t], lf_ref[t]
        m_old = m_ref[...]
        m_new = jnp.maximum(lf + m_old, li)
        gp = jnp.exp(lf + m_old - m_new)
        gs = jnp.exp(li - m_new)
        n = gp * n_ref[...] + gs * k
        den = jnp.sum(n * q, axis=0, keepdims=True)
        scale = 1.0 / jnp.maximum(jnp.abs(den), jnp.exp(-m_new))
        gk = gs * k
        for i in range(vh):
            s = c_ref[i] * gp + v_ref[t, i:i + 1, :] * gk
            o_ref[t, i:i + 1, :] = jnp.sum(s * q, axis=0, keepdims=True) * scale
            c_ref[i] = s
        n_ref[...] = n
        m_ref[...] = m_new
        return c

    _scan_loop(tb, body)


def _scan_call(kernel_fn, seq_ins, state_ins, t, vh, lanes, tb):
    nl = lanes // LANES
    seq_spec = lambda a: pl.BlockSpec((tb, a.shape[1], LANES), lambda j, i: (i, 0, j))

    def state_spec(a):
        nd = a.ndim
        return pl.BlockSpec(a.shape[:-1] + (LANES,), lambda j, i: (0,) * (nd - 1) + (j,))

    o_shape = jax.ShapeDtypeStruct((t, vh, lanes), F32)
    outs = pl.pallas_call(
        functools.partial(kernel_fn, tb, vh),
        grid=(nl, t // tb),
        in_specs=[seq_spec(a) for a in seq_ins] + [state_spec(a) for a in state_ins],
        out_specs=[pl.BlockSpec((tb, vh, LANES), lambda j, i: (i, 0, j))] + [state_spec(a) for a in state_ins],
        out_shape=[o_shape] + [jax.ShapeDtypeStruct(a.shape, F32) for a in state_ins],
        compiler_params=_cparams(("parallel", "arbitrary")),
    )(*seq_ins, *state_ins)
    return outs[0], outs[1:]


class _Lay:
    def __init__(self, bsz, t):
        self.b, self.t = bsz, t
        bh = bsz * NH
        self.vl = max(LANES // bh, 1)
        self.lanes = self.vl * bh
        self.tb = 32 if t % 32 == 0 else t

    def kop(self, x, k=HEAD_DIM):
        x = x.reshape(self.b, self.t, NH, k).transpose(1, 3, 0, 2).reshape(self.t, k, self.b * NH)
        return jnp.tile(x, (1, 1, self.vl))

    def sop(self, x):
        x = x.reshape(self.b, self.t, NH, HEAD_DIM)[..., 0].transpose(1, 0, 2).reshape(self.t, 1, self.b * NH)
        return jnp.tile(x, (1, 1, self.vl))

    def vop(self, x, v=HEAD_DIM):
        x = x.reshape(self.b, self.t, NH, self.vl, v // self.vl).transpose(1, 4, 3, 0, 2)
        return x.reshape(self.t, v // self.vl, self.lanes)

    def unvop(self, o, v=HEAD_DIM):
        o = o.reshape(self.t, v // self.vl, self.vl, self.b, NH).transpose(3, 0, 4, 2, 1)
        return o.reshape(self.b * self.t, NH * v)

    def state(self, s):
        v, k = s.shape[2], s.shape[3]
        s = s.reshape(self.b, NH, self.vl, v // self.vl, k).transpose(3, 4, 2, 0, 1)
        return s.reshape(v // self.vl, k, self.lanes)

    def unstate(self, s):
        vh, k = s.shape[0], s.shape[1]
        s = s.reshape(vh, k, self.vl, self.b, NH).transpose(3, 4, 2, 0, 1)
        return s.reshape(self.b, NH, vh * self.vl, k)

    def kstate(self, n):
        k = n.shape[2]
        return jnp.tile(n.transpose(2, 0, 1).reshape(k, self.b * NH), (1, self.vl))

    def unkstate(self, n):
        return n[:, :self.b * NH].reshape(n.shape[0], self.b, NH).transpose(1, 2, 0)


def _rwkv_scan(lay, rw, s0):
    cols = lambda i: rw[:, i * GW:(i + 1) * GW]
    o, (s,) = _scan_call(
        _rwkv_scan_kernel,
        [lay.kop(cols(4)), lay.kop(cols(3)), lay.kop(cols(5)), lay.kop(cols(1)), lay.kop(cols(0)), lay.vop(cols(2))],
        [s0], lay.t, HEAD_DIM // lay.vl, lay.lanes, lay.tb)
    return lay.unvop(o), s


def _segmean(a, s, n):
    return _dot_hi(a, s) * (1.0 / n)


def _rwkv_post(o, rwp_ref, pq, s64):
    d = o - _segmean(o, s64, HEAD_DIM)
    var = _segmean(d * d, s64, HEAD_DIM)
    return (d * lax.rsqrt(var + RWKV_GN_EPS) * pq[0:1] + pq[1:2] + rwp_ref[:, 0:256]) * rwp_ref[:, 256:512]


def _out_proj(x, parts, w_ref):
    acc = x
    for i, part in enumerate(parts):
        acc = acc + jnp.dot(part.astype(BF16), w_ref[i * GW:(i + 1) * GW, :], preferred_element_type=F32)
    return acc


def _post_step_kernel(x_ref, orw_ref, ohg_ref, oss_ref, oml_ref, rwp_ref, hgp_ref, ssp_ref, mlp_ref, pq_ref, seg64_ref,
                      seg128_ref, w_ref, o_ref):
    pq = pq_ref[...]
    s64 = seg64_ref[...]
    m_r = _rwkv_post(orw_ref[...], rwp_ref, pq, s64)
    o = ohg_ref[...]
    m_h = o * lax.rsqrt(_segmean(o * o, s64, HEAD_DIM) + NORM_EPS) * pq[2:3] * hgp_ref[...]
    y = (oss_ref[...] + ssp_ref[:, 0:256]) * ssp_ref[:, 256:512]
    m_s = y * lax.rsqrt(_segmean(y * y, seg128_ref[...], 128) + NORM_EPS) * pq[3:4]
    o = oml_ref[...]
    m_m = o * lax.rsqrt(_segmean(o * o, s64, HEAD_DIM) + NORM_EPS) * pq[4:5] * mlp_ref[...]
    o_ref[...] = _out_proj(x_ref[...], (m_r, m_h, m_s, m_m), w_ref)


def _post_seq_kernel(x_ref, orw_ref, rwp_ref, mh_ref, ms_ref, mm_ref, pq_ref, seg64_ref, w_ref, o_ref):
    m_r = _rwkv_post(orw_ref[...], rwp_ref, pq_ref[...], seg64_ref[...])
    o_ref[...] = _out_proj(x_ref[...], (m_r, mh_ref[...], ms_ref[...], mm_ref[...]), w_ref)


def _post_call(kernel_fn, x, row_ins, consts):
    m = x.shape[0]
    tm = min(m, 512)
    rows = lambda a: pl.BlockSpec((tm, a.shape[1]), lambda i: (i, 0))
    const = lambda a: pl.BlockSpec(a.shape, lambda i: (0, 0))
    return pl.pallas_call(
        kernel_fn,
        grid=(m // tm,),
        in_specs=[rows(a) for a in (x,) + tuple(row_ins)] + [const(a) for a in consts],
        out_specs=rows(x),
        out_shape=jax.ShapeDtypeStruct((m, D_MODEL), F32),
        compiler_params=_cparams(("parallel",)),
    )(x, *row_ins, *consts)


def _xattn_seq_kernel(q_ref, k_ref, v_ref, x_ref, wo_ref, o_ref):
    acc = x_ref[...]
    for h in range(XH):
        sl = slice(h * XHD, (h + 1) * XHD)
        s = _dot_nt(q_ref[:, sl], k_ref[:, sl]) * (XHD ** -0.5)
        e = jnp.exp(s - jnp.max(s, axis=-1, keepdims=True))
        pr = e / jnp.sum(e, axis=-1, keepdims=True)
        acc = acc + _dot(_dot(pr, v_ref[:, sl]), wo_ref[sl, :])
    o_ref[...] = acc


def _xattn_seq(q, kv, x, wo, bsz, t):
    tq = 512
    nt = t // tq
    return pl.pallas_call(
        _xattn_seq_kernel,
        grid=(bsz, nt),
        in_specs=[pl.BlockSpec((tq, D_MODEL), lambda b, i: (b * nt + i, 0)),
                  pl.BlockSpec((N_MEM, D_MODEL), lambda b, i: (b, 0)),
                  pl.BlockSpec((N_MEM, D_MODEL), lambda b, i: (b, 1)),
                  pl.BlockSpec((tq, D_MODEL), lambda b, i: (b * nt + i, 0)),
                  pl.BlockSpec((D_MODEL, D_MODEL), lambda b, i: (0, 0))],
        out_specs=pl.BlockSpec((tq, D_MODEL), lambda b, i: (b * nt + i, 0)),
        out_shape=jax.ShapeDtypeStruct((bsz * t, D_MODEL), F32),
        compiler_params=_cparams(("parallel", "parallel")),
    )(q, kv, kv, x, wo)


def _xattn_step_kernel(bb, q_ref, k_ref, v_ref, o_ref):
    for i in range(bb):
        t = k_ref[i] * q_ref[i]
        s = jnp.sum(t, axis=-1, keepdims=True) * (XHD ** -0.5)
        e = jnp.exp(s - jnp.max(s, axis=0, keepdims=True))
        pr = e / jnp.sum(e, axis=0, keepdims=True)
        o_ref[i] = jnp.sum(pr * v_ref[i], axis=0)


def _xattn_step(q, cache_k, cache_v, layer):
    bsz = q.shape[0]
    bb = 4
    cache_spec = pl.BlockSpec((None, bb, N_MEM, XH, XHD), lambda i: (layer, i, 0, 0, 0))
    return pl.pallas_call(
        functools.partial(_xattn_step_kernel, bb),
        grid=(bsz // bb,),
        in_specs=[pl.BlockSpec((bb, XH, XHD), lambda i: (i, 0, 0)), cache_spec, cache_spec],
        out_specs=pl.BlockSpec((bb, XH, XHD), lambda i: (i, 0, 0)),
        out_shape=jax.ShapeDtypeStruct((bsz, XH, XHD), F32),
        compiler_params=_cparams(("parallel",)),
    )(q.reshape(bsz, XH, XHD), cache_k, cache_v).reshape(bsz, D_MODEL)


def _step_kernel(bb, rw_ref, hg_ref, ss_ref, ml_ref, wkv_ref, hgs_ref, ssm_ref, c_ref, n_ref, m_ref,
                 orw_ref, ohg_ref, oss_ref, oml_ref, wkv_o, hgs_o, ssm_o, c_o, n_o, m_o):
    ri = lax.broadcasted_iota(jnp.int32, (HEAD_DIM, HEAD_DIM), 0)
    ci = lax.broadcasted_iota(jnp.int32, (HEAD_DIM, HEAD_DIM), 1)
    eye = (ri == ci).astype(F32)
    wi = lax.broadcasted_iota(jnp.int32, (HEAD_DIM, GW), 0)
    wj = lax.broadcasted_iota(jnp.int32, (HEAD_DIM, GW), 1)
    col = lambda r: jnp.sum(eye * r, axis=-1, keepdims=True)
    lsum = lambda a: jnp.sum(a, axis=-1, keepdims=True)
    lane4 = lax.broadcasted_iota(jnp.int32, (1, NH), 1)

    def place(c, h):
        return jnp.sum(jnp.where(wj == wi + h * HEAD_DIM, c, 0.0), axis=0, keepdims=True)

    def body(b, carry):
        rb = pl.ds(b, 1)
        rw, hg, ss, ml = rw_ref[rb, :], hg_ref[rb, :], ss_ref[rb, :], ml_ref[rb, :]
        m_row = m_ref[rb, :]
        m_new_row = m_row
        o_rw = o_hg = o_ss = o_ml = jnp.zeros((1, GW), F32)
        for h in range(NH):
            hs = lambda base: slice(base + h * HEAD_DIM, base + (h + 1) * HEAD_DIM)
            one = lambda base: slice(base + h * HEAD_DIM, base + h * HEAD_DIM + 1)
            r, k, v, w, kk, ka = (rw[:, hs(i * GW)] for i in range(6))
            s = wkv_ref[b, h]
            s = s * w - lsum(s * kk) * ka + col(v) * k
            o_rw = o_rw + place(lsum(s * r), h)
            wkv_o[b, h] = s
            q, f, kh, vh = (hg[:, hs(i * GW)] for i in range(4))
            s = hgs_ref[b, h] * col(f) + col(kh) * vh
            o_hg = o_hg + place(col(jnp.sum(s * col(q), axis=0, keepdims=True)), h)
            hgs_o[b, h] = s
            g = h // (NH // 2)
            bn = ss[:, 512 + SSM_STATE * g:512 + SSM_STATE * (g + 1)]
            cn = ss[:, 768 + SSM_STATE * g:768 + SSM_STATE * (g + 1)]
            s = ssm_ref[b, h] * ss[:, one(GW)] + col(ss[:, hs(0)]) * bn
            o_ss = o_ss + place(lsum(s * cn), h)
            ssm_o[b, h] = s
            q, k, v = (ml[:, hs(i * GW)] for i in range(3))
            li, lf = ml[:, one(3 * GW)], ml[:, one(4 * GW)]
            m_old = m_row[:, h:h + 1]
            m_new = jnp.maximum(lf + m_old, li)
            gp = jnp.exp(lf + m_old - m_new)
            gs = jnp.exp(li - m_new)
            n = gp * n_ref[b, h:h + 1, :] + gs * k
            c = c_ref[b, h] * gp + col(gs * v) * k
            o_ml = o_ml + place(lsum(c * q) / jnp.maximum(jnp.abs(lsum(n * q)), jnp.exp(-m_new)), h)
            c_o[b, h] = c
            n_o[b, h:h + 1, :] = n
            m_new_row = jnp.where(lane4 == h, m_new, m_new_row)
        orw_ref[rb, :] = o_rw
        ohg_ref[rb, :] = o_hg
        oss_ref[rb, :] = o_ss
        oml_ref[rb, :] = o_ml
        m_o[rb, :] = m_new_row
        return carry

    lax.fori_loop(0, bb, body, 0)


def _mixers_step(scan, states, layer):
    bsz = scan[0].shape[0]
    bb = SUBLANES
    rows = lambda a: pl.BlockSpec((bb, a.shape[1]), lambda i: (i, 0))
    st_in = lambda a: pl.BlockSpec((None, bb) + a.shape[2:], lambda i: (layer, i) + (0,) * (a.ndim - 2))
    st_out = lambda a: pl.BlockSpec((bb,) + a.shape[2:], lambda i: (i,) + (0,) * (a.ndim - 2))
    o_shape = jax.ShapeDtypeStruct((bsz, GW), F32)
    outs = pl.pallas_call(
        functools.partial(_step_kernel, bb),
        grid=(bsz // bb,),
        in_specs=[rows(a) for a in scan] + [st_in(a) for a in states],
        out_specs=[pl.BlockSpec((bb, GW), lambda i: (i, 0))] * 4 + [st_out(a) for a in states],
        out_shape=[o_shape] * 4 + [jax.ShapeDtypeStruct(a.shape[1:], F32) for a in states],
        compiler_params=_cparams(("parallel",)),
    )(*scan, *states)
    return outs[:4], outs[4:]


def kernel(x_prompt, x_sample, state_rwkv_wkv, state_rwkv_shift, state_hgrn, state_ssm, state_ssm_conv, state_mlstm_C, state_mlstm_n, state_mlstm_m, cache_mem_k, cache_mem_v, mem_prompt, norm_mix_g, w_in, w_out, rwkv_mu, rwkv_w0, rwkv_w_up, rwkv_a0, rwkv_a_up, rwkv_g_up, rwkv_k_k, rwkv_k_a, rwkv_r_k, rwkv_ln_g, rwkv_ln_b, hgrn_lb_logits, hgrn_norm_g, ssm_conv_w, ssm_conv_b, ssm_dt_bias, ssm_A_log, ssm_D, ssm_norm_g, mlstm_i_bias, mlstm_f_bias, mlstm_norm_g, norm_x_g, norm_mem_g, xattn_wq, xattn_wk, xattn_wv, xattn_wo, norm_ff_g, ff_w1, ff_w2, final_norm_g):
    bp, tp, _ = x_prompt.shape
    bs = x_sample.shape[0]
    seg64, seg128 = _seg_matrix(HEAD_DIM), _seg_matrix(128)
    seg64_b = seg64.astype(BF16)
    gate_ex = _gate_expand()
    tril_c = jnp.asarray(np.tril(np.ones((CHUNK, CHUNK), np.float32)))
    tril_s = jnp.asarray(np.tril(np.ones((SUB, SUB), np.float32)))
    rep = lambda a: jnp.repeat(a, HEAD_DIM)
    pad_rows = lambda a, n: jnp.pad(a, ((0, n - a.shape[0]), (0, 0)))
    zeros = lambda *s: jnp.zeros(s, F32)

    hp = x_prompt.reshape(bp * tp, D_MODEL)
    hs = x_sample.reshape(bs, D_MODEL)
    mem = mem_prompt.reshape(bp * N_MEM, D_MODEL)
    p_states, s_states, p_mem = [], [], []

    for l in range(DEPTH):
        w_in_re = _gather_cols(w_in[l], _IN_IDX).astype(BF16)
        pv = pad_rows(jnp.stack([rwkv_w0[l], rwkv_a0[l], rwkv_k_k[l], rwkv_k_a[l], rwkv_r_k[l].reshape(GW),
                                 rep(ssm_dt_bias[l]), rep(ssm_A_log[l]), rep(ssm_D[l]), rep(mlstm_i_bias[l]),
                                 rep(mlstm_f_bias[l])]), 16)
        lora = jnp.stack([pad_rows(rwkv_w_up[l], LANES), pad_rows(rwkv_a_up[l], LANES), pad_rows(rwkv_g_up[l], LANES)])
        conv = pad_rows(jnp.concatenate([ssm_conv_w[l], ssm_conv_b[l][None]], axis=0), 8)
        mu = _gather_cols(rwkv_mu[l], _RW_IDX).reshape(1, RW_W)
        gp = pad_rows(jnp.stack([_gate_row(ssm_dt_bias[l], mlstm_i_bias[l], mlstm_f_bias[l]),
                                 _gate_row(ssm_A_log[l], zeros(NH), zeros(NH))]), 8)
        pd = pad_rows(jnp.stack([rep(ssm_D[l]), ssm_norm_g[l]]), 8)
        pn_h = pad_rows(hgrn_norm_g[l][None], 8)
        pn_m = pad_rows(mlstm_norm_g[l][None], 8)
        pq = pad_rows(jnp.stack([rwkv_ln_g[l], rwkv_ln_b[l], hgrn_norm_g[l], ssm_norm_g[l], mlstm_norm_g[l]]), 8)
        w_out_b = w_out[l].astype(BF16)
        wq_b, wo_b = xattn_wq[l].astype(BF16), xattn_wo[l].astype(BF16)
        wkv_b = jnp.concatenate([xattn_wk[l], xattn_wv[l]], axis=1).astype(BF16)
        w1_b, w2_b = ff_w1[l].astype(BF16), ff_w2[l].astype(BF16)

        kv = _norm_matmul(mem, norm_mem_g[l], wkv_b)
        p_mem.append((kv[:, :D_MODEL].reshape(bp, N_MEM, XH, XHD), kv[:, D_MODEL:].reshape(bp, N_MEM, XH, XHD)))
        proj = _norm_matmul(hp, norm_mix_g[l], w_in_re)
        m_h, (st_h,) = _chunk_call(
            functools.partial(_hgrn_chunk_kernel, l), proj, [(HG_W, HG_OFF // HG_W)],
            [hgrn_lb_logits, pn_h, tril_s, seg64_b, seg64], [(GW, GW)],
            [pltpu.VMEM((CHUNK, GW), F32)] * 3, bp, tp)
        m_s, (st_s,) = _chunk_call(
            _ssm_chunk_kernel, proj, [(SS_W, SS_OFF // SS_W), (GT_W, GT_OFF // GT_W)],
            [conv, gp, pd, tril_c], [(NH, SSM_STATE, HEAD_DIM)], [pltpu.VMEM((SUBLANES, 768), F32)], bp, tp)
        m_m, (st_c, st_n, st_m) = _chunk_call(
            _mlstm_chunk_kernel, proj, [(ML_W, ML_OFF // ML_W), (GT_W, GT_OFF // GT_W)],
            [gp, pn_m, tril_c], [(NH, HEAD_DIM, HEAD_DIM), (SUBLANES, HEAD_DIM), (1, LANES)], [], bp, tp)
        rw_scan, rw_post = _prep_rw_seq(proj, zeros(bp, SUBLANES, RW_W), mu, pv, lora, seg64, bp, tp)
        o_rw, (st_r,) = _chunk_call(
            _rwkv_chunk_kernel, rw_scan, [(1536, 0)], [tril_s, seg64_b, seg64], [(GW, GW)],
            [pltpu.VMEM((CHUNK, GW), F32)] * 6, bp, tp)
        proj3 = proj.reshape(bp, tp, IN_RE)
        shift_new = jnp.take(proj3[:, -1, RW_OFF:RW_OFF + RW_W], jnp.asarray(_RW_INV), axis=-1)
        conv_new = proj3[:, tp - 3:, SS_OFF + 256:SS_OFF + 1024]
        diag_blocks = lambda s: jnp.stack(
            [s.reshape(bp, NH, HEAD_DIM, NH, HEAD_DIM)[:, h, :, h, :] for h in range(NH)], axis=1)
        hgrn_new = diag_blocks(st_h).transpose(0, 1, 3, 2)
        p_states.append((diag_blocks(st_r), shift_new, hgrn_new, st_s.transpose(0, 1, 3, 2), conv_new,
                         st_c.transpose(0, 1, 3, 2), st_n[:, :NH], st_m[:, 0, :NH]))
        hp = _post_call(_post_seq_kernel, hp, (o_rw, rw_post, m_h, m_s, m_m), (pq, seg64, w_out_b))
        q = _norm_matmul(hp, norm_x_g[l], wq_b)
        hp = _xattn_seq(q, kv, hp, wo_b, bp, tp)
        hp = _ffn(hp, norm_ff_g[l], w1_b, w2_b)

        proj = _norm_matmul(hs, norm_mix_g[l], w_in_re)
        prev_rw = _gather_cols(state_rwkv_shift[l], _RW_IDX)
        cb = state_ssm_conv[l]
        prep = _prep_step(proj, prev_rw, cb[:, 0], cb[:, 1], cb[:, 2],
                          (mu, pv, lora, seg64, hgrn_lb_logits, conv, gate_ex), l)
        o_parts, (wkv_n, hg_n, ssm_n, c_n, n_n, m_n) = _mixers_step(
            prep[0::2], (state_rwkv_wkv, state_hgrn, state_ssm, state_mlstm_C, state_mlstm_n, state_mlstm_m), l)
        shift_new = jnp.take(proj[:, RW_OFF:RW_OFF + RW_W], jnp.asarray(_RW_INV), axis=-1)
        conv_new = jnp.concatenate([cb[:, 1:], proj[:, None, SS_OFF + 256:SS_OFF + 1024]], axis=1)
        s_states.append((wkv_n, shift_new, hg_n, ssm_n, conv_new, c_n, n_n, m_n))
        hs = _post_call(_post_step_kernel, hs, tuple(o_parts) + tuple(prep[1::2]), (pq, seg64, seg128, w_out_b))
        q = _norm_matmul(hs, norm_x_g[l], wq_b)
        att = _xattn_step(q, cache_mem_k, cache_mem_v, l)
        hs = _mm_res(att, wo_b, hs)
        hs = _ffn(hs, norm_ff_g[l], w1_b, w2_b)

    y_prompt = _rmsnorm(hp, final_norm_g).reshape(bp, tp, D_MODEL)
    y_sample = _rmsnorm(hs, final_norm_g).reshape(bs, 1, D_MODEL)
    p_out = [jnp.stack(s) for s in zip(*p_states)]
    s_out = [jnp.stack(s) for s in zip(*s_states)]
    p_mem_k = jnp.stack([kv[0] for kv in p_mem])
    p_mem_v = jnp.stack([kv[1] for kv in p_mem])
    return (y_prompt, y_sample, *p_out, p_mem_k, p_mem_v, *s_out)
```

```python
import functools

import numpy as np
import jax
import jax.numpy as jnp
from jax import lax
from jax.experimental import pallas as pl
from jax.experimental.pallas import tpu as pltpu

F32 = jnp.float32
BF16 = jnp.bfloat16
HI = lax.Precision.HIGHEST

D_MODEL = 1024
DEPTH = 2
NH = 4
HEAD_DIM = 64
GW = 256
SSM_STATE = 128
N_MEM = 256
XH = 4
XHD = 256
D_FF = 4096
NORM_EPS = 1e-6
RWKV_GN_EPS = 64e-5
RWKV_COLS = 896
LANES = 128
SUBLANES = 8
VMEM_LIMIT = 48 * 1024 * 1024
CHUNK = 128
SUB = 16

HG_OFF, HG_W = 0, 1024
SS_OFF, SS_W = 1024, 1024
ML_OFF, ML_W = 2048, 1024
GT_OFF, GT_W = 3072, 128
RW_OFF, RW_W = 3456, 1152
IN_RE = 4608
GT_DT, GT_I, GT_F = 0, 4, 8


def _in_col_index():
    idx = -np.ones(IN_RE, np.int32)
    idx[HG_OFF:HG_OFF + 1024] = 896 + np.arange(1024)
    ss = 1920
    idx[SS_OFF:SS_OFF + 1024] = ss + np.arange(1024)
    ml = 2948
    idx[ML_OFF:ML_OFF + 768] = ml + np.arange(768)
    idx[ML_OFF + 768:ML_OFF + 1024] = ml + 776 + np.arange(256)
    idx[GT_OFF + GT_DT:GT_OFF + GT_DT + NH] = ss + 1024 + np.arange(NH)
    idx[GT_OFF + GT_I:GT_OFF + GT_I + NH] = ml + 768 + np.arange(NH)
    idx[GT_OFF + GT_F:GT_OFF + GT_F + NH] = ml + 772 + np.arange(NH)
    idx[RW_OFF:RW_OFF + 768] = np.arange(768)
    idx[RW_OFF + 768:RW_OFF + 800] = 768 + np.arange(32)
    idx[RW_OFF + 896:RW_OFF + 928] = 800 + np.arange(32)
    idx[RW_OFF + 1024:RW_OFF + 1088] = 832 + np.arange(64)
    return idx


_IN_IDX = _in_col_index()
_RW_IDX = _IN_IDX[RW_OFF:RW_OFF + RW_W]
_RW_INV = np.concatenate([np.arange(800), 896 + np.arange(32), 1024 + np.arange(64)]).astype(np.int32)


def _gather_cols(a, idx):
    g = jnp.take(a, jnp.asarray(np.maximum(idx, 0)), axis=-1)
    return jnp.where(jnp.asarray(idx >= 0), g, jnp.zeros((), a.dtype))


def _seg_matrix(width):
    i = np.arange(GW)
    return jnp.asarray((i[:, None] // width == i[None, :] // width).astype(np.float32))


def _gate_expand():
    e = np.zeros((3, LANES, GW), np.float32)
    for s, off in enumerate((GT_DT, GT_I, GT_F)):
        for h in range(NH):
            e[s, off + h, h * HEAD_DIM:(h + 1) * HEAD_DIM] = 1.0
    return jnp.asarray(e)


def _gate_row(dt_bias, i_bias, f_bias):
    z = jnp.zeros((LANES,), F32)
    return z.at[GT_DT:GT_DT + NH].set(dt_bias).at[GT_I:GT_I + NH].set(i_bias).at[GT_F:GT_F + NH].set(f_bias)


def _cparams(sem):
    return pltpu.CompilerParams(dimension_semantics=sem, vmem_limit_bytes=VMEM_LIMIT)


def _softplus(x):
    return jnp.maximum(x, 0.0) + jnp.log(1.0 + jnp.exp(-jnp.abs(x)))


def _sigmoid(x):
    return 1.0 / (1.0 + jnp.exp(-x))


def _rms(x, g):
    ms = jnp.mean(x * x, axis=-1, keepdims=True)
    return x * lax.rsqrt(ms + NORM_EPS) * g


def _dot(a, b):
    return jnp.dot(a.astype(BF16), b.astype(BF16), preferred_element_type=F32)


def _dot_nt(a, b):
    return lax.dot_general(a.astype(BF16), b.astype(BF16), (((1,), (1,)), ((), ())), preferred_element_type=F32)


def _dot_tn(a, b):
    return lax.dot_general(a.astype(BF16), b.astype(BF16), (((0,), (0,)), ((), ())), preferred_element_type=F32)


def _dot_hi(a, b):
    return jnp.dot(a, b, precision=HI, preferred_element_type=F32)


def _norm_matmul_kernel(x_ref, g_ref, w_ref, o_ref, xn_ref):
    @pl.when(pl.program_id(1) == 0)
    def _():
        xn_ref[...] = _rms(x_ref[...], g_ref[...]).astype(BF16)

    o_ref[...] = jnp.dot(xn_ref[...], w_ref[...], preferred_element_type=F32)


def _row_tile(m):
    return min(m, 1024)


def _norm_matmul(x, g, w):
    m, k = x.shape
    n = w.shape[1]
    tm = _row_tile(m)
    tn = next(c for c in (1536, 1024, 512, n) if n % c == 0)
    return pl.pallas_call(
        _norm_matmul_kernel,
        grid=(m // tm, n // tn),
        in_specs=[pl.BlockSpec((tm, k), lambda i, j: (i, 0)),
                  pl.BlockSpec((1, k), lambda i, j: (0, 0)),
                  pl.BlockSpec((k, tn), lambda i, j: (0, j))],
        out_specs=pl.BlockSpec((tm, tn), lambda i, j: (i, j)),
        out_shape=jax.ShapeDtypeStruct((m, n), F32),
        scratch_shapes=[pltpu.VMEM((tm, k), BF16)],
        compiler_params=_cparams(("parallel", "arbitrary")),
    )(x, g.reshape(1, k), w)


def _mm_res_kernel(a_ref, w_ref, res_ref, o_ref):
    o_ref[...] = res_ref[...] + jnp.dot(a_ref[...].astype(BF16), w_ref[...], preferred_element_type=F32)


def _mm_res(a, w, res):
    m, k = a.shape
    n = w.shape[1]
    tm = min(m, 512)
    return pl.pallas_call(
        _mm_res_kernel,
        grid=(m // tm,),
        in_specs=[pl.BlockSpec((tm, k), lambda i: (i, 0)),
                  pl.BlockSpec((k, n), lambda i: (0, 0)),
                  pl.BlockSpec((tm, n), lambda i: (i, 0))],
        out_specs=pl.BlockSpec((tm, n), lambda i: (i, 0)),
        out_shape=jax.ShapeDtypeStruct((m, n), F32),
        compiler_params=_cparams(("parallel",)),
    )(a, w, res)


def _ffn_kernel(x_ref, g_ref, w1_ref, w2_ref, o_ref, xn_ref, acc_ref):
    j = pl.program_id(1)

    @pl.when(j == 0)
    def _():
        xn_ref[...] = _rms(x_ref[...], g_ref[...]).astype(BF16)
        acc_ref[...] = jnp.zeros_like(acc_ref)

    h = jnp.dot(xn_ref[...], w1_ref[...], preferred_element_type=F32)
    h = jnp.square(jnp.maximum(h, 0.0))
    acc_ref[...] += jnp.dot(h.astype(BF16), w2_ref[...], preferred_element_type=F32)

    @pl.when(j == pl.num_programs(1) - 1)
    def _():
        o_ref[...] = x_ref[...] + acc_ref[...]


def _ffn(x, g, w1, w2):
    m, k = x.shape
    ff = w1.shape[1]
    tm = _row_tile(m)
    tf = 1024
    return pl.pallas_call(
        _ffn_kernel,
        grid=(m // tm, ff // tf),
        in_specs=[pl.BlockSpec((tm, k), lambda i, j: (i, 0)),
                  pl.BlockSpec((1, k), lambda i, j: (0, 0)),
                  pl.BlockSpec((k, tf), lambda i, j: (0, j)),
                  pl.BlockSpec((tf, k), lambda i, j: (j, 0))],
        out_specs=pl.BlockSpec((tm, k), lambda i, j: (i, 0)),
        out_shape=jax.ShapeDtypeStruct((m, k), F32),
        scratch_shapes=[pltpu.VMEM((tm, k), BF16), pltpu.VMEM((tm, k), F32)],
        compiler_params=_cparams(("parallel", "arbitrary")),
    )(x, g.reshape(1, k), w1, w2)


def _rmsnorm_kernel(x_ref, g_ref, o_ref):
    o_ref[...] = _rms(x_ref[...], g_ref[...])


def _rmsnorm(x, g):
    m, k = x.shape
    tm = min(m, 512)
    return pl.pallas_call(
        _rmsnorm_kernel,
        grid=(m // tm,),
        in_specs=[pl.BlockSpec((tm, k), lambda i: (i, 0)), pl.BlockSpec((1, k), lambda i: (0, 0))],
        out_specs=pl.BlockSpec((tm, k), lambda i: (i, 0)),
        out_shape=jax.ShapeDtypeStruct((m, k), F32),
        compiler_params=_cparams(("parallel",)),
    )(x, g.reshape(1, k))


def _shift_rows(x, tail, j):
    xr = pltpu.roll(x, j, 0)
    tr = pltpu.roll(tail, j, 0)
    rows = lax.broadcasted_iota(jnp.int32, tr.shape, 0)
    first = jnp.where(rows < j, tr, xr[0:SUBLANES])
    return jnp.concatenate([first, xr[SUBLANES:]], axis=0)


def _hgrn_lower_bound(logit_ref, layer):
    lg = logit_ref[...]
    e = jnp.exp(lg - jnp.max(lg, axis=0, keepdims=True))
    prb = e / jnp.sum(e, axis=0, keepdims=True)
    return jnp.sum(prb[0:layer + 1], axis=0, keepdims=True) - prb[0:1]


def _conv_silu(u, c3, c2, c1, conv_ref):
    cw = conv_ref[...]
    conv = cw[4:5] + c3 * cw[0:1] + c2 * cw[1:2] + c1 * cw[2:3] + u * cw[3:4]
    return conv * _sigmoid(conv)


def _prep_rwkv(pr, prev_rw, mu_ref, pv_ref, lora_ref, seg_ref, rw_scan, rw_post, log_decay):
    pv = pv_ref[...]
    row = lambda i: pv[i:i + 1, :]
    seg = seg_ref[...]
    pm = pr + (prev_rw - pr) * mu_ref[...]
    r, k, v = pm[:, 0:256], pm[:, 256:512], pm[:, 512:768]
    dw, da, dg = pm[:, 768:896], pm[:, 896:1024], pm[:, 1024:1152]
    w_log = -_softplus(-(row(0) + _dot_hi(jnp.tanh(dw), lora_ref[0]))) - 0.5
    a = _sigmoid(row(1) + _dot_hi(da, lora_ref[1]))
    kk = k * row(2)
    kk = kk / jnp.maximum(jnp.sqrt(_dot_hi(kk * kk, seg)), 1e-12)
    k2 = k * (1.0 + (a - 1.0) * row(3))
    rw_scan[:, 0:256] = r
    rw_scan[:, 256:512] = k2
    rw_scan[:, 512:768] = v
    rw_scan[:, 768:1024] = -jnp.exp(w_log) if log_decay else jnp.exp(-jnp.exp(w_log))
    rw_scan[:, 1024:1280] = kk
    rw_scan[:, 1280:1536] = kk * a
    rw_post[:, 0:256] = _dot_hi(r * k2 * row(4), seg) * v
    rw_post[:, 256:512] = _dot_hi(_sigmoid(dg), lora_ref[2])


def _prep_rw_seq_kernel(p_ref, prev8_ref, tail0_ref, mu_ref, pv_ref, lora_ref, seg_ref, rw_scan, rw_post):
    tail = jnp.where(pl.program_id(1) == 0, tail0_ref[0], prev8_ref[...])
    pr = p_ref[...]
    _prep_rwkv(pr, _shift_rows(pr, tail, 1), mu_ref, pv_ref, lora_ref, seg_ref, rw_scan, rw_post, True)


def _prep_rw_seq(proj, tail0, mu, pv, lora, seg, bsz, t):
    tt = 512
    nt = t // tt
    m = bsz * t
    cb = RW_OFF // RW_W
    c2 = lambda b, i: (0, 0)
    return pl.pallas_call(
        _prep_rw_seq_kernel,
        grid=(bsz, nt),
        in_specs=[pl.BlockSpec((tt, RW_W), lambda b, i: (b * nt + i, cb)),
                  pl.BlockSpec((SUBLANES, RW_W),
                               lambda b, i: (jnp.maximum((b * nt + i) * (tt // SUBLANES) - 1, 0), cb)),
                  pl.BlockSpec((1, SUBLANES, RW_W), lambda b, i: (b, 0, 0)),
                  pl.BlockSpec((1, RW_W), c2), pl.BlockSpec((16, GW), c2),
                  pl.BlockSpec((3, LANES, GW), lambda b, i: (0, 0, 0)), pl.BlockSpec((GW, GW), c2)],
        out_specs=[pl.BlockSpec((tt, 1536), lambda b, i: (b * nt + i, 0)),
                   pl.BlockSpec((tt, 512), lambda b, i: (b * nt + i, 0))],
        out_shape=[jax.ShapeDtypeStruct((m, 1536), F32), jax.ShapeDtypeStruct((m, 512), F32)],
        compiler_params=_cparams(("parallel", "parallel")),
    )(proj, proj, tail0, mu, pv, lora, seg)


PREP_OUT_WIDTHS = (1536, 512, 1024, 256, 1024, 512, 1280, 256)


def _prep_step_kernel(layer, p_ref, prev_ref, c3_ref, c2_ref, c1_ref, mu_ref, pv_ref, lora_ref, seg_ref, logit_ref,
                      conv_ref, ex_ref, rw_scan, rw_post, hg_scan, hg_post, ss_scan, ss_post, ml_scan, ml_post):
    pv = pv_ref[...]
    row = lambda i: pv[i:i + 1, :]
    _prep_rwkv(p_ref[:, RW_OFF:RW_OFF + RW_W], prev_ref[...], mu_ref, pv_ref, lora_ref, seg_ref, rw_scan, rw_post,
               False)
    gates = p_ref[:, GT_OFF:GT_OFF + GT_W]
    dt_pre, i_pre, f_pre_m = (_dot_hi(gates, ex_ref[s]) for s in range(3))

    lb = _hgrn_lower_bound(logit_ref, layer)
    q_pre = p_ref[:, HG_OFF:HG_OFF + 256]
    f_pre = p_ref[:, HG_OFF + 256:HG_OFF + 512]
    gg = p_ref[:, HG_OFF + 768:HG_OFF + 1024]
    hg_scan[:, 0:256] = q_pre * _sigmoid(q_pre)
    hg_scan[:, 256:512] = lb + (1.0 - lb) * _sigmoid(f_pre)
    hg_scan[:, 512:768] = (1.0 - lb) * _sigmoid(-f_pre)
    hg_scan[:, 768:1024] = p_ref[:, HG_OFF + 512:HG_OFF + 768]
    hg_post[...] = gg * _sigmoid(gg)

    z = p_ref[:, SS_OFF:SS_OFF + 256]
    xbc = _conv_silu(p_ref[:, SS_OFF + 256:SS_OFF + 1024], c3_ref[...], c2_ref[...], c1_ref[...], conv_ref)
    xs = xbc[:, 0:256]
    dt = _softplus(dt_pre + row(5))
    ss_scan[:, 0:256] = xs * dt
    ss_scan[:, 256:512] = jnp.exp(dt * (-jnp.exp(row(6))))
    ss_scan[:, 512:1024] = xbc[:, 256:768]
    ss_post[:, 0:256] = xs * row(7)
    ss_post[:, 256:512] = z * _sigmoid(z)

    ml_scan[:, 0:256] = p_ref[:, ML_OFF:ML_OFF + 256]
    ml_scan[:, 256:512] = p_ref[:, ML_OFF + 256:ML_OFF + 512] * (HEAD_DIM ** -0.5)
    ml_scan[:, 512:768] = p_ref[:, ML_OFF + 512:ML_OFF + 768]
    ml_scan[:, 768:1024] = i_pre + row(8)
    ml_scan[:, 1024:1280] = -_softplus(-(f_pre_m + row(9)))
    ml_post[...] = _sigmoid(p_ref[:, ML_OFF + 768:ML_OFF + 1024])


def _prep_step(proj, prev_rw, c3, c2, c1, params, layer):
    m = proj.shape[0]
    full = lambda a: pl.BlockSpec(a.shape, lambda i: (0,) * a.ndim)
    ins = (proj, prev_rw, c3, c2, c1) + tuple(params)
    return pl.pallas_call(
        functools.partial(_prep_step_kernel, layer),
        grid=(1,),
        in_specs=[full(a) for a in ins],
        out_specs=[pl.BlockSpec((m, w), lambda i: (0, 0)) for w in PREP_OUT_WIDTHS],
        out_shape=[jax.ShapeDtypeStruct((m, w), F32) for w in PREP_OUT_WIDTHS],
        compiler_params=_cparams(("arbitrary",)),
    )(*ins)


def _ssm_chunk_kernel(ss_ref, gt_ref, conv_ref, gp_ref, pd_ref, tril_ref, o_ref, ht_ref, tail_ref):
    @pl.when(pl.program_id(1) == 0)
    def _():
        ht_ref[...] = jnp.zeros_like(ht_ref)
        tail_ref[...] = jnp.zeros_like(tail_ref)

    L = ss_ref.shape[0]
    u = ss_ref[:, 256:1024]
    tail = tail_ref[...]
    c1, c2, c3 = (_shift_rows(u, tail, j) for j in (1, 2, 3))
    tail_ref[...] = u[L - SUBLANES:L]
    xbc = _conv_silu(u, c3, c2, c1, conv_ref)
    z = ss_ref[:, 0:256]
    zs = z * _sigmoid(z)
    dt = _softplus(gt_ref[...] + gp_ref[0:1])
    la = dt * (-jnp.exp(gp_ref[1:2]))
    tril = tril_ref[...]
    mask = tril > 0.5
    cum = _dot_hi(tril, la)
    cum_t = cum.T
    pd = pd_ref[...]
    hsl = lambda h: slice(h * HEAD_DIM, (h + 1) * HEAD_DIM)
    bgs = [xbc[:, 256 + SSM_STATE * g:256 + SSM_STATE * (g + 1)] for g in range(2)]
    cgs = [xbc[:, 512 + SSM_STATE * g:512 + SSM_STATE * (g + 1)] for g in range(2)]
    scores = [_dot_nt(cgs[g], bgs[g]) for g in range(2)]
    hts = [ht_ref[0, h] for h in range(NH)]
    inter = [_dot(cgs[h // 2], hts[h]) for h in range(NH)]
    bg_t = [bgs[g].T for g in range(2)]
    ccs = [cum[:, GT_DT + h:GT_DT + h + 1] for h in range(NH)]
    dtc = [dt[:, GT_DT + h:GT_DT + h + 1] for h in range(NH)]
    lms = []
    for h in range(NH):
        cr = cum_t[GT_DT + h:GT_DT + h + 1, :]
        lms.append(jnp.where(mask, jnp.exp(jnp.where(mask, ccs[h] - cr, 0.0)), 0.0))
    ys = []
    for h in range(NH):
        xh = xbc[:, hsl(h)]
        y = _dot(scores[h // 2] * lms[h], xh * dtc[h]) + inter[h] * jnp.exp(ccs[h])
        ys.append((y + xh * pd[0:1, hsl(h)]) * zs[:, hsl(h)])
    for h in range(NH):
        cend = ccs[h][L - 1:L, :]
        ht_ref[0, h] = jnp.exp(cend) * hts[h] + _dot(bg_t[h // 2], xbc[:, hsl(h)] * (jnp.exp(cend - ccs[h]) * dtc[h]))
    for g in range(2):
        y0, y1 = ys[2 * g], ys[2 * g + 1]
        ms = jnp.sum(y0 * y0, axis=-1, keepdims=True) + jnp.sum(y1 * y1, axis=-1, keepdims=True)
        scale = lax.rsqrt(ms * (1.0 / (2 * HEAD_DIM)) + NORM_EPS)
        for h in (2 * g, 2 * g + 1):
            o_ref[:, hsl(h)] = ys[h] * scale * pd[1:2, hsl(h)]


def _mlstm_chunk_kernel(ml_ref, gt_ref, gp_ref, pn_ref, tril_ref, o_ref, ct_ref, n_ref, m_ref):
    @pl.when(pl.program_id(1) == 0)
    def _():
        ct_ref[...] = jnp.zeros_like(ct_ref)
        n_ref[...] = jnp.zeros_like(n_ref)
        m_ref[...] = jnp.zeros_like(m_ref)

    L = ml_ref.shape[0]
    x = gt_ref[...] + gp_ref[0:1]
    lf = -_softplus(-x)
    tril = tril_ref[...]
    mask = tril > 0.5
    b = _dot_hi(tril, lf)
    b_t = b.T
    x_t = x.T
    m_row = m_ref[0]
    m_new_row = m_row
    lane = lax.broadcasted_iota(jnp.int32, m_row.shape, 1)
    pn = pn_ref[...]
    hsl = lambda h, base=0: slice(base + h * HEAD_DIM, base + (h + 1) * HEAD_DIM)
    for h in range(NH):
        b_col, b_row = b[:, GT_F + h:GT_F + h + 1], b_t[GT_F + h:GT_F + h + 1, :]
        li_col, li_row = x[:, GT_I + h:GT_I + h + 1], x_t[GT_I + h:GT_I + h + 1, :]
        m_prev = m_row[:, h:h + 1]
        log_inter = b_col + m_prev
        log_intra = jnp.where(mask, b_col - b_row + li_row, -jnp.inf)
        m_t = jnp.maximum(log_inter, jnp.max(log_intra, axis=-1, keepdims=True))
        w_inter = jnp.exp(log_inter - m_t)
        qh = ml_ref[:, hsl(h)]
        kh = ml_ref[:, hsl(h, 256)] * (HEAD_DIM ** -0.5)
        vh = ml_ref[:, hsl(h, 512)]
        qk = _dot_nt(qh, kh) * jnp.exp(log_intra - m_t)
        ct = ct_ref[0, h]
        n_h = n_ref[0, h:h + 1, :]
        num = _dot(qk, vh) + w_inter * _dot(qh, ct)
        den = jnp.sum(qk, axis=-1, keepdims=True) + w_inter * jnp.sum(qh * n_h, axis=-1, keepdims=True)
        hout = num / jnp.maximum(jnp.abs(den), jnp.exp(-m_t))
        m_end = m_t[L - 1:L, :]
        b_end = b_col[L - 1:L, :]
        g_prev = jnp.exp(b_end + m_prev - m_end)
        kg = kh * jnp.exp(b_end - b_col + li_col - m_end)
        ct_ref[0, h] = g_prev * ct + _dot_tn(kg, vh)
        n_ref[0, h:h + 1, :] = g_prev * n_h + jnp.sum(kg, axis=0, keepdims=True)
        m_new_row = jnp.where(lane == h, m_end, m_new_row)
        hn = hout * lax.rsqrt(jnp.mean(hout * hout, axis=-1, keepdims=True) + NORM_EPS)
        o_ref[:, hsl(h)] = hn * pn[0:1, hsl(h)] * _sigmoid(ml_ref[:, hsl(h, 768)])
    m_ref[0] = m_new_row


def _hgrn_chunk_kernel(layer, hg_ref, logit_ref, pn_ref, tril_ref, ones_ref, seg_ref, bd_ref, o_ref, st_ref, q_scr,
                       b_scr, i_scr):
    @pl.when(pl.program_id(1) == 0)
    def _():
        st_ref[...] = jnp.zeros_like(st_ref)

    lb = _hgrn_lower_bound(logit_ref, layer)
    seg = seg_ref[...]
    bd = bd_ref[...]
    pn = pn_ref[...]
    srow = lax.broadcasted_iota(jnp.int32, (SUB, GW), 0)
    L = hg_ref.shape[0]

    q_pre, f_pre, v, gg = hg_ref[:, 0:256], hg_ref[:, 256:512], hg_ref[:, 512:768], hg_ref[:, 768:1024]
    q = q_pre * _sigmoid(q_pre)
    k = (1.0 - lb) * _sigmoid(-f_pre)
    lf = jnp.log(lb + (1.0 - lb) * _sigmoid(f_pre))
    b = _dot_hi(tril_ref[...], lf)
    b_tot = _dot_hi(ones_ref[...], lf)
    q_scr[...] = q
    b_scr[...] = b
    tiles = []
    for t in range(L):
        rows = slice((t // SUB) * SUB, (t // SUB + 1) * SUB)
        live = srow <= (t % SUB)
        dec = jnp.where(live, jnp.exp(jnp.where(live, b_scr[t:t + 1, :] - b[rows], 0.0)), 0.0)
        tiles.append(dec * q_scr[t:t + 1, :] * k[rows])
    a_b = jnp.dot(jnp.concatenate(tiles, axis=0).astype(BF16), seg, preferred_element_type=F32)
    for t in range(L):
        rows = slice((t // SUB) * SUB, (t // SUB + 1) * SUB)
        i_scr[t:t + 1, :] = jnp.sum(a_b[t * SUB:(t + 1) * SUB, :] * v[rows], axis=0, keepdims=True)
    qx = q * jnp.exp(b)
    kx = k * jnp.exp(b_tot - b)
    p_end = jnp.exp(b_tot)
    st = st_ref[0]
    for c in range(L // SUB):
        rows = slice(c * SUB, (c + 1) * SUB)
        upd = _dot_tn(v[rows], kx[rows]) * bd
        i_scr[rows, :] = _dot_nt(qx[rows], st) + i_scr[rows, :]
        st = st * p_end[c * SUB:c * SUB + 1, :] + upd
    st_ref[0] = st
    o = i_scr[...]
    gate = gg * _sigmoid(gg)
    for h in range(NH):
        sl = slice(h * HEAD_DIM, (h + 1) * HEAD_DIM)
        oh = o[:, sl]
        o_ref[:, sl] = oh * lax.rsqrt(jnp.mean(oh * oh, axis=-1, keepdims=True) + NORM_EPS) * pn[0:1, sl] * gate[:, sl]


def _rwkv_chunk_kernel(rw_ref, tril_ref, seg_ref, bd_ref, o_ref, st_ref, kx_scr, mv_scr):
    @pl.when(pl.program_id(1) == 0)
    def _():
        st_ref[...] = jnp.zeros_like(st_ref)

    tril = tril_ref[...]
    seg = seg_ref[...]
    bd = bd_ref[...]
    srow = lax.broadcasted_iota(jnp.int32, (SUB, GW), 0)
    csum = lambda a: jnp.sum(a, axis=0, keepdims=True)
    xi = lax.broadcasted_iota(jnp.int32, (NH * SUB, GW), 0)
    xj = lax.broadcasted_iota(jnp.int32, (NH * SUB, GW), 1)
    head_rows = (xi // SUB == xj // HEAD_DIM).astype(F32)
    ct = lax.broadcasted_iota(jnp.int32, (SUB, NH * SUB), 0)
    cj = lax.broadcasted_iota(jnp.int32, (SUB, NH * SUB), 1) & (SUB - 1)
    expand = lambda x: jnp.concatenate([x] * NH, axis=0) * head_rows
    st = st_ref[0]
    for c in range(rw_ref.shape[0] // SUB):
        r0 = c * SUB
        rows = slice(r0, r0 + SUB)
        one = lambda t: slice(r0 + t, r0 + t + 1)
        r, k, v = rw_ref[rows, 0:256], rw_ref[rows, 256:512], rw_ref[rows, 512:768]
        lw, kk, ka = rw_ref[rows, 768:1024], rw_ref[rows, 1024:1280], rw_ref[rows, 1280:1536]
        cs = _dot_hi(tril, lw)
        c_end = cs[SUB - 1:SUB, :]
        e_inv = jnp.exp(-cs)
        kkx = kk * jnp.exp(cs - lw)
        rx = r * jnp.exp(cs)
        kae, ke = ka * e_inv, k * e_inv
        kx_scr[rows, :] = kkx
        maa = jnp.dot(jnp.concatenate([kx_scr[one(t), :] * kae for t in range(SUB)], axis=0).astype(BF16), seg,
                      preferred_element_type=F32)
        v_x = expand(v)
        mk = _dot_nt(jnp.concatenate([kkx, rx], axis=0), expand(ke))
        m_ak = jnp.where(cj < ct, mk[0:SUB], 0.0)
        a_rk = jnp.where(cj <= ct, mk[SUB:2 * SUB], 0.0)
        a_ra = jnp.where(cj <= ct, _dot_nt(rx, expand(kae)), 0.0)
        mv_scr[rows, :] = _dot(m_ak, v_x)
        g_arr = jnp.zeros((SUB, GW), F32)
        u_arr = jnp.zeros((SUB, GW), F32)
        for t in range(SUB):
            m_t = jnp.where(srow < t, maa[t * SUB:(t + 1) * SUB, :], 0.0)
            g_t = kx_scr[one(t), :] - csum(m_t * g_arr)
            u_t = mv_scr[one(t), :] - csum(m_t * u_arr)
            g_arr = jnp.where(srow == t, g_t, g_arr)
            u_arr = jnp.where(srow == t, u_t, u_arr)
        omega = rx - _dot(a_ra, expand(g_arr))
        o_ref[rows, :] = _dot_nt(omega, st) + _dot(a_rk, v_x) - _dot(a_ra, expand(u_arr))
        e_end = jnp.exp(c_end - cs)
        a_hat = ka * e_end
        phi = _dot_tn(g_arr, a_hat) * bd
        psi = _dot_tn(jnp.concatenate([v, -u_arr], axis=0), jnp.concatenate([k * e_end, a_hat], axis=0)) * bd
        st = st * jnp.exp(c_end) - _dot(st, phi) + psi
    st_ref[0] = st


def _rwkv_chunk2_kernel(rw_ref, tril_ref, ones_ref, seg_ref, bd_ref, o_ref, st_ref, kx_scr, mv_scr, g_scr, u_scr):
    @pl.when(pl.program_id(1) == 0)
    def _():
        st_ref[...] = jnp.zeros_like(st_ref)

    L = rw_ref.shape[0]
    ns = L // SUB
    xr = ns * NH * SUB
    seg = seg_ref[...]
    bd = bd_ref[...]
    srow = lax.broadcasted_iota(jnp.int32, (SUB, GW), 0)
    csum = lambda a: jnp.sum(a, axis=0, keepdims=True)
    xi = lax.broadcasted_iota(jnp.int32, (xr, GW), 0)
    xj = lax.broadcasted_iota(jnp.int32, (xr, GW), 1)
    head_rows = (((xi // SUB) & (NH - 1)) == xj // HEAD_DIM).astype(F32)
    ti = lax.broadcasted_iota(jnp.int32, (L, xr), 0)
    tj = lax.broadcasted_iota(jnp.int32, (L, xr), 1)
    same = (tj // (NH * SUB)) == (ti // SUB)
    lt = same & ((tj & (SUB - 1)) < (ti & (SUB - 1)))
    le = same & ((tj & (SUB - 1)) <= (ti & (SUB - 1)))

    def expand(x):
        parts = []
        for c in range(ns):
            parts += [x[c * SUB:(c + 1) * SUB]] * NH
        return jnp.concatenate(parts, axis=0) * head_rows

    r, k, v = rw_ref[:, 0:256], rw_ref[:, 256:512], rw_ref[:, 512:768]
    lw, kk, ka = rw_ref[:, 768:1024], rw_ref[:, 1024:1280], rw_ref[:, 1280:1536]
    cs = _dot_hi(tril_ref[...], lw)
    c_tot = _dot_hi(ones_ref[...], lw)
    e_inv = jnp.exp(-cs)
    kkx = kk * jnp.exp(cs - lw)
    rx = r * jnp.exp(cs)
    kae, ke = ka * e_inv, k * e_inv
    e_end = jnp.exp(c_tot - cs)
    a_hat, k_hat = ka * e_end, k * e_end
    p_end = jnp.exp(c_tot)
    kx_scr[...] = kkx
    tiles = [kx_scr[t:t + 1, :] * kae[(t // SUB) * SUB:(t // SUB + 1) * SUB] for t in range(L)]
    maa = jnp.dot(jnp.concatenate(tiles, axis=0).astype(BF16), seg, preferred_element_type=F32)
    v_x = expand(v)
    mk = _dot_nt(jnp.concatenate([kkx, rx], axis=0), expand(ke))
    m_ak = jnp.where(lt, mk[0:L], 0.0)
    a_rk = jnp.where(le, mk[L:2 * L], 0.0)
    a_ra = jnp.where(le, _dot_nt(rx, expand(kae)), 0.0)
    mv_scr[...] = _dot(m_ak, v_x)
    for c in range(ns):
        g_arr = jnp.zeros((SUB, GW), F32)
        u_arr = jnp.zeros((SUB, GW), F32)
        for t in range(SUB):
            i = c * SUB + t
            m_t = jnp.where(srow < t, maa[i * SUB:(i + 1) * SUB, :], 0.0)
            g_t = kx_scr[i:i + 1, :] - csum(m_t * g_arr)
            u_t = mv_scr[i:i + 1, :] - csum(m_t * u_arr)
            g_arr = jnp.where(srow == t, g_t, g_arr)
            u_arr = jnp.where(srow == t, u_t, u_arr)
        g_scr[c * SUB:(c + 1) * SUB, :] = g_arr
        u_scr[c * SUB:(c + 1) * SUB, :] = u_arr
    g_all, u_all = g_scr[...], u_scr[...]
    omega = rx - _dot(a_ra, expand(g_all))
    o_ind = _dot(a_rk, v_x) - _dot(a_ra, expand(u_all))
    st = st_ref[0]
    for c in range(ns):
        rows = slice(c * SUB, (c + 1) * SUB)
        phi = _dot_tn(g_all[rows], a_hat[rows]) * bd
        psi = _dot_tn(jnp.concatenate([v[rows], -u_all[rows]], axis=0),
                      jnp.concatenate([k_hat[rows], a_hat[rows]], axis=0)) * bd
        o_ref[rows, :] = _dot_nt(omega[rows], st) + o_ind[rows]
        st = st * p_end[c * SUB:c * SUB + 1, :] - _dot(st, phi) + psi
    st_ref[0] = st


def _chunk_call(kernel_fn, proj, col_blocks, consts, state_shapes, scratch, bsz, t):
    nt = t // CHUNK
    m = bsz * t
    rows = lambda w, cb: pl.BlockSpec((CHUNK, w), lambda b, i: (b * nt + i, cb))
    const = lambda a: pl.BlockSpec(a.shape, lambda b, i: (0,) * a.ndim)
    state = lambda s: pl.BlockSpec((1,) + s, lambda b, i: (b,) + (0,) * len(s))
    outs = pl.pallas_call(
        kernel_fn,
        grid=(bsz, nt),
        in_specs=[rows(w, cb) for w, cb in col_blocks] + [const(a) for a in consts],
        out_specs=[rows(GW, 0)] + [state(s) for s in state_shapes],
        out_shape=[jax.ShapeDtypeStruct((m, GW), F32)] + [jax.ShapeDtypeStruct((bsz,) + s, F32) for s in state_shapes],
        scratch_shapes=scratch,
        compiler_params=_cparams(("parallel", "arbitrary")),
    )(*([proj] * len(col_blocks)), *consts)
    return outs[0], outs[1:]


def _scan_loop(tb, body):
    if tb == 1:
        body(0, 0)
    else:
        lax.fori_loop(0, tb, body, 0)


def _rwkv_scan_kernel(tb, vh, kk_ref, w_ref, ka_ref, k_ref, r_ref, v_ref, s0_ref, o_ref, s_ref):
    @pl.when(pl.program_id(1) == 0)
    def _():
        s_ref[...] = s0_ref[...]

    def body(t, c):
        kk, w, ka, k, r = kk_ref[t], w_ref[t], ka_ref[t], k_ref[t], r_ref[t]
        for i in range(vh):
            s = s_ref[i]
            sa = jnp.sum(s * kk, axis=0, keepdims=True)
            s = s * w - sa * ka + v_ref[t, i:i + 1, :] * k
            o_ref[t, i:i + 1, :] = jnp.sum(s * r, axis=0, keepdims=True)
            s_ref[i] = s
        return c

    _scan_loop(tb, body)


def _hgrn_scan_kernel(tb, vh, f_ref, k_ref, q_ref, v_ref, s0_ref, o_ref, s_ref):
    @pl.when(pl.program_id(1) == 0)
    def _():
        s_ref[...] = s0_ref[...]

    def body(t, c):
        f, k, q = f_ref[t], k_ref[t], q_ref[t]
        for i in range(vh):
            s = s_ref[i] * f + v_ref[t, i:i + 1, :] * k
            o_ref[t, i:i + 1, :] = jnp.sum(s * q, axis=0, keepdims=True)
            s_ref[i] = s
        return c

    _scan_loop(tb, body)


def _ssm_scan_kernel(tb, vh, b_ref, c_ref, dec_ref, x_ref, s0_ref, o_ref, s_ref):
    @pl.when(pl.program_id(1) == 0)
    def _():
        s_ref[...] = s0_ref[...]

    def body(t, c):
        bn, cn, dec = b_ref[t], c_ref[t], dec_ref[t]
        for i in range(vh):
            s = s_ref[i] * dec + x_ref[t, i:i + 1, :] * bn
            o_ref[t, i:i + 1, :] = jnp.sum(s * cn, axis=0, keepdims=True)
            s_ref[i] = s
        return c

    _scan_loop(tb, body)


def _mlstm_scan_kernel(tb, vh, q_ref, k_ref, li_ref, lf_ref, v_ref, c0_ref, n0_ref, m0_ref, o_ref, c_ref, n_ref, m_ref):
    @pl.when(pl.program_id(1) == 0)
    def _():
        c_ref[...] = c0_ref[...]
        n_ref[...] = n0_ref[...]
        m_ref[...] = m0_ref[...]

    def body(t, c):
        q, k, li, lf = q_ref[t], k_ref[t], li_ref[t], lf_ref[t]
        m_old = m_ref[...]
        m_new = jnp.maximum(lf + m_old, li)
        gp = jnp.exp(lf + m_old - m_new)
        gs = jnp.exp(li - m_new)
        n = gp * n_ref[...] + gs * k
        den = jnp.sum(n * q, axis=0, keepdims=True)
        scale = 1.0 / jnp.maximum(jnp.abs(den), jnp.exp(-m_new))
        gk = gs * k
        for i in range(vh):
            s = c_ref[i] * gp + v_ref[t, i:i + 1, :] * gk
            o_ref[t, i:i + 1, :] = jnp.sum(s * q, axis=0, keepdims=True) * scale
            c_ref[i] = s
        n_ref[...] = n
        m_ref[...] = m_new
        return c

    _scan_loop(tb, body)


def _scan_call(kernel_fn, seq_ins, state_ins, t, vh, lanes, tb):
    nl = lanes // LANES
    seq_spec = lambda a: pl.BlockSpec((tb, a.shape[1], LANES), lambda j, i: (i, 0, j))

    def state_spec(a):
        nd = a.ndim
        return pl.BlockSpec(a.shape[:-1] + (LANES,), lambda j, i: (0,) * (nd - 1) + (j,))

    o_shape = jax.ShapeDtypeStruct((t, vh, lanes), F32)
    outs = pl.pallas_call(
        functools.partial(kernel_fn, tb, vh),
        grid=(nl, t // tb),
        in_specs=[seq_spec(a) for a in seq_ins] + [state_spec(a) for a in state_ins],
        out_specs=[pl.BlockSpec((tb, vh, LANES), lambda j, i: (i, 0, j))] + [state_spec(a) for a in state_ins],
        out_shape=[o_shape] + [jax.ShapeDtypeStruct(a.shape, F32) for a in state_ins],
        compiler_params=_cparams(("parallel", "arbitrary")),
    )(*seq_ins, *state_ins)
    return outs[0], outs[1:]


class _Lay:
    def __init__(self, bsz, t):
        self.b, self.t = bsz, t
        bh = bsz * NH
        self.vl = max(LANES // bh, 1)
        self.lanes = self.vl * bh
        self.tb = 32 if t % 32 == 0 else t

    def kop(self, x, k=HEAD_DIM):
        x = x.reshape(self.b, self.t, NH, k).transpose(1, 3, 0, 2).reshape(self.t, k, self.b * NH)
        return jnp.tile(x, (1, 1, self.vl))

    def sop(self, x):
        x = x.reshape(self.b, self.t, NH, HEAD_DIM)[..., 0].transpose(1, 0, 2).reshape(self.t, 1, self.b * NH)
        return jnp.tile(x, (1, 1, self.vl))

    def vop(self, x, v=HEAD_DIM):
        x = x.reshape(self.b, self.t, NH, self.vl, v // self.vl).transpose(1, 4, 3, 0, 2)
        return x.reshape(self.t, v // self.vl, self.lanes)

    def unvop(self, o, v=HEAD_DIM):
        o = o.reshape(self.t, v // self.vl, self.vl, self.b, NH).transpose(3, 0, 4, 2, 1)
        return o.reshape(self.b * self.t, NH * v)

    def state(self, s):
        v, k = s.shape[2], s.shape[3]
        s = s.reshape(self.b, NH, self.vl, v // self.vl, k).transpose(3, 4, 2, 0, 1)
        return s.reshape(v // self.vl, k, self.lanes)

    def unstate(self, s):
        vh, k = s.shape[0], s.shape[1]
        s = s.reshape(vh, k, self.vl, self.b, NH).transpose(3, 4, 2, 0, 1)
        return s.reshape(self.b, NH, vh * self.vl, k)

    def kstate(self, n):
        k = n.shape[2]
        return jnp.tile(n.transpose(2, 0, 1).reshape(k, self.b * NH), (1, self.vl))

    def unkstate(self, n):
        return n[:, :self.b * NH].reshape(n.shape[0], self.b, NH).transpose(1, 2, 0)


def _rwkv_scan(lay, rw, s0):
    cols = lambda i: rw[:, i * GW:(i + 1) * GW]
    o, (s,) = _scan_call(
        _rwkv_scan_kernel,
        [lay.kop(cols(4)), lay.kop(cols(3)), lay.kop(cols(5)), lay.kop(cols(1)), lay.kop(cols(0)), lay.vop(cols(2))],
        [s0], lay.t, HEAD_DIM // lay.vl, lay.lanes, lay.tb)
    return lay.unvop(o), s


def _segmean(a, s, n):
    return _dot_hi(a, s) * (1.0 / n)


def _rwkv_post(o, rwp_ref, pq, s64):
    d = o - _segmean(o, s64, HEAD_DIM)
    var = _segmean(d * d, s64, HEAD_DIM)
    return (d * lax.rsqrt(var + RWKV_GN_EPS) * pq[0:1] + pq[1:2] + rwp_ref[:, 0:256]) * rwp_ref[:, 256:512]


def _out_proj(x, parts, w_ref):
    acc = x
    for i, part in enumerate(parts):
        acc = acc + jnp.dot(part.astype(BF16), w_ref[i * GW:(i + 1) * GW, :], preferred_element_type=F32)
    return acc


def _post_step_kernel(x_ref, orw_ref, ohg_ref, oss_ref, oml_ref, rwp_ref, hgp_ref, ssp_ref, mlp_ref, pq_ref, seg64_ref,
                      seg128_ref, w_ref, o_ref):
    pq = pq_ref[...]
    s64 = seg64_ref[...]
    m_r = _rwkv_post(orw_ref[...], rwp_ref, pq, s64)
    o = ohg_ref[...]
    m_h = o * lax.rsqrt(_segmean(o * o, s64, HEAD_DIM) + NORM_EPS) * pq[2:3] * hgp_ref[...]
    y = (oss_ref[...] + ssp_ref[:, 0:256]) * ssp_ref[:, 256:512]
    m_s = y * lax.rsqrt(_segmean(y * y, seg128_ref[...], 128) + NORM_EPS) * pq[3:4]
    o = oml_ref[...]
    m_m = o * lax.rsqrt(_segmean(o * o, s64, HEAD_DIM) + NORM_EPS) * pq[4:5] * mlp_ref[...]
    o_ref[...] = _out_proj(x_ref[...], (m_r, m_h, m_s, m_m), w_ref)


def _post_seq_kernel(x_ref, orw_ref, rwp_ref, mh_ref, ms_ref, mm_ref, pq_ref, seg64_ref, w_ref, o_ref):
    m_r = _rwkv_post(orw_ref[...], rwp_ref, pq_ref[...], seg64_ref[...])
    o_ref[...] = _out_proj(x_ref[...], (m_r, mh_ref[...], ms_ref[...], mm_ref[...]), w_ref)


def _post_call(kernel_fn, x, row_ins, consts):
    m = x.shape[0]
    tm = min(m, 512)
    rows = lambda a: pl.BlockSpec((tm, a.shape[1]), lambda i: (i, 0))
    const = lambda a: pl.BlockSpec(a.shape, lambda i: (0, 0))
    return pl.pallas_call(
        kernel_fn,
        grid=(m // tm,),
        in_specs=[rows(a) for a in (x,) + tuple(row_ins)] + [const(a) for a in consts],
        out_specs=rows(x),
        out_shape=jax.ShapeDtypeStruct((m, D_MODEL), F32),
        compiler_params=_cparams(("parallel",)),
    )(x, *row_ins, *consts)


def _xattn_seq_kernel(q_ref, k_ref, v_ref, x_ref, wo_ref, o_ref):
    acc = x_ref[...]
    for h in range(XH):
        sl = slice(h * XHD, (h + 1) * XHD)
        s = _dot_nt(q_ref[:, sl], k_ref[:, sl]) * (XHD ** -0.5)
        e = jnp.exp(s - jnp.max(s, axis=-1, keepdims=True))
        pr = e / jnp.sum(e, axis=-1, keepdims=True)
        acc = acc + _dot(_dot(pr, v_ref[:, sl]), wo_ref[sl, :])
    o_ref[...] = acc


def _xattn_seq(q, kv, x, wo, bsz, t):
    tq = 512
    nt = t // tq
    return pl.pallas_call(
        _xattn_seq_kernel,
        grid=(bsz, nt),
        in_specs=[pl.BlockSpec((tq, D_MODEL), lambda b, i: (b * nt + i, 0)),
                  pl.BlockSpec((N_MEM, D_MODEL), lambda b, i: (b, 0)),
                  pl.BlockSpec((N_MEM, D_MODEL), lambda b, i: (b, 1)),
                  pl.BlockSpec((tq, D_MODEL), lambda b, i: (b * nt + i, 0)),
                  pl.BlockSpec((D_MODEL, D_MODEL), lambda b, i: (0, 0))],
        out_specs=pl.BlockSpec((tq, D_MODEL), lambda b, i: (b * nt + i, 0)),
        out_shape=jax.ShapeDtypeStruct((bsz * t, D_MODEL), F32),
        compiler_params=_cparams(("parallel", "parallel")),
    )(q, kv, kv, x, wo)


def _xattn_step_kernel(bb, q_ref, k_ref, v_ref, o_ref):
    for i in range(bb):
        t = k_ref[i] * q_ref[i]
        s = jnp.sum(t, axis=-1, keepdims=True) * (XHD ** -0.5)
        e = jnp.exp(s - jnp.max(s, axis=0, keepdims=True))
        pr = e / jnp.sum(e, axis=0, keepdims=True)
        o_ref[i] = jnp.sum(pr * v_ref[i], axis=0)


def _xattn_step(q, cache_k, cache_v, layer):
    bsz = q.shape[0]
    bb = 4
    cache_spec = pl.BlockSpec((None, bb, N_MEM, XH, XHD), lambda i: (layer, i, 0, 0, 0))
    return pl.pallas_call(
        functools.partial(_xattn_step_kernel, bb),
        grid=(bsz // bb,),
        in_specs=[pl.BlockSpec((bb, XH, XHD), lambda i: (i, 0, 0)), cache_spec, cache_spec],
        out_specs=pl.BlockSpec((bb, XH, XHD), lambda i: (i, 0, 0)),
        out_shape=jax.ShapeDtypeStruct((bsz, XH, XHD), F32),
        compiler_params=_cparams(("parallel",)),
    )(q.reshape(bsz, XH, XHD), cache_k, cache_v).reshape(bsz, D_MODEL)


def _step_kernel(bb, rw_ref, hg_ref, ss_ref, ml_ref, wkv_ref, hgs_ref, ssm_ref, c_ref, n_ref, m_ref,
                 orw_ref, ohg_ref, oss_ref, oml_ref, wkv_o, hgs_o, ssm_o, c_o, n_o, m_o):
    ri = lax.broadcasted_iota(jnp.int32, (HEAD_DIM, HEAD_DIM), 0)
    ci = lax.broadcasted_iota(jnp.int32, (HEAD_DIM, HEAD_DIM), 1)
    eye = (ri == ci).astype(F32)
    wi = lax.broadcasted_iota(jnp.int32, (HEAD_DIM, GW), 0)
    wj = lax.broadcasted_iota(jnp.int32, (HEAD_DIM, GW), 1)
    col = lambda r: jnp.sum(eye * r, axis=-1, keepdims=True)
    lsum = lambda a: jnp.sum(a, axis=-1, keepdims=True)
    lane4 = lax.broadcasted_iota(jnp.int32, (1, NH), 1)

    def place(c, h):
        return jnp.sum(jnp.where(wj == wi + h * HEAD_DIM, c, 0.0), axis=0, keepdims=True)

    def body(b, carry):
        rb = slice(b, b + 1)
        rw, hg, ss, ml = rw_ref[rb, :], hg_ref[rb, :], ss_ref[rb, :], ml_ref[rb, :]
        m_row = m_ref[rb, :]
        m_new_row = m_row
        o_rw = o_hg = o_ss = o_ml = jnp.zeros((1, GW), F32)
        for h in range(NH):
            hs = lambda base: slice(base + h * HEAD_DIM, base + (h + 1) * HEAD_DIM)
            one = lambda base: slice(base + h * HEAD_DIM, base + h * HEAD_DIM + 1)
            r, k, v, w, kk, ka = (rw[:, hs(i * GW)] for i in range(6))
            s = wkv_ref[b, h]
            s = s * w - lsum(s * kk) * ka + col(v) * k
            o_rw = o_rw + place(lsum(s * r), h)
            wkv_o[b, h] = s
            q, f, kh, vh = (hg[:, hs(i * GW)] for i in range(4))
            s = hgs_ref[b, h] * col(f) + col(kh) * vh
            o_hg = o_hg + place(col(jnp.sum(s * col(q), axis=0, keepdims=True)), h)
            hgs_o[b, h] = s
            g = h // (NH // 2)
            bn = ss[:, 512 + SSM_STATE * g:512 + SSM_STATE * (g + 1)]
            cn = ss[:, 768 + SSM_STATE * g:768 + SSM_STATE * (g + 1)]
            s = ssm_ref[b, h] * ss[:, one(GW)] + col(ss[:, hs(0)]) * bn
            o_ss = o_ss + place(lsum(s * cn), h)
            ssm_o[b, h] = s
            q, k, v = (ml[:, hs(i * GW)] for i in range(3))
            li, lf = ml[:, one(3 * GW)], ml[:, one(4 * GW)]
            m_old = m_row[:, h:h + 1]
            m_new = jnp.maximum(lf + m_old, li)
            gp = jnp.exp(lf + m_old - m_new)
            gs = jnp.exp(li - m_new)
            n = gp * n_ref[b, h:h + 1, :] + gs * k
            c = c_ref[b, h] * gp + col(gs * v) * k
            o_ml = o_ml + place(lsum(c * q) / jnp.maximum(jnp.abs(lsum(n * q)), jnp.exp(-m_new)), h)
            c_o[b, h] = c
            n_o[b, h:h + 1, :] = n
            m_new_row = jnp.where(lane4 == h, m_new, m_new_row)
        orw_ref[rb, :] = o_rw
        ohg_ref[rb, :] = o_hg
        oss_ref[rb, :] = o_ss
        oml_ref[rb, :] = o_ml
        m_o[rb, :] = m_new_row
        return carry

    for b in range(bb):
        body(b, 0)


def _stepv_kernel(rw_ref, hg_ref, xdt_ref, bn_ref, cn_ref, sc_ref, ml_ref, wkv_ref, hgs_ref, ssm_ref, c_ref, n_ref, m_ref,
                  orw_ref, ohg_ref, oss_ref, oml_ref, wkv_o, hgs_o, ssm_o, c_o, n_o, m_o):
    ri = lax.broadcasted_iota(jnp.int32, (HEAD_DIM, HEAD_DIM), 0)
    ci = lax.broadcasted_iota(jnp.int32, (HEAD_DIM, HEAD_DIM), 1)
    eye = (ri == ci).astype(F32)
    col = lambda r: jnp.sum(eye * r, axis=-1, keepdims=True)
    row = lambda c: jnp.sum(eye * c, axis=-2, keepdims=True)
    lsum = lambda a: jnp.sum(a, axis=-1, keepdims=True)
    r, k, v, w, kk, ka = (rw_ref[:, i] for i in range(6))
    s = wkv_ref[...]
    s = s * w - lsum(s * kk) * ka + col(v) * k
    orw_ref[...] = row(lsum(s * r))
    wkv_o[...] = s
    q, f, kh, vh = (hg_ref[:, i] for i in range(4))
    s = hgs_ref[...] * col(f) + col(kh) * vh
    ohg_ref[...] = jnp.sum(s * col(q), axis=-2, keepdims=True)
    hgs_o[...] = s
    dec, li, lf = (sc_ref[:, i] for i in range(3))
    s = ssm_ref[...] * dec + col(xdt_ref[...]) * bn_ref[...]
    oss_ref[...] = row(lsum(s * cn_ref[...]))
    ssm_o[...] = s
    q, k, v = (ml_ref[:, i] for i in range(3))
    m_old = m_ref[...]
    m_new = jnp.maximum(lf + m_old, li)
    gp = jnp.exp(lf + m_old - m_new)
    gs = jnp.exp(li - m_new)
    n = gp * n_ref[...] + gs * k
    c = c_ref[...] * gp + col(gs * v) * k
    oml_ref[...] = row(lsum(c * q) / jnp.maximum(jnp.abs(lsum(n * q)), jnp.exp(-m_new)))
    c_o[...] = c
    n_o[...] = n
    m_o[...] = m_new


def _mixers_stepv(scan, states, layer):
    rw, hg, ss, ml = scan
    wkv, hgs, ssm, cst, nst, mst = states
    bsz = rw.shape[0]
    bb = SUBLANES
    heads = lambda a, nf: a.reshape(bsz, nf, NH, 1, HEAD_DIM)
    per_head = lambda a: a.reshape(bsz, NH, HEAD_DIM)[:, :, 0].reshape(bsz, NH, 1, 1)
    groups = lambda a: jnp.repeat(a.reshape(bsz, 2, 1, SSM_STATE), NH // 2, axis=1)
    seq_ins = (heads(rw, 6), heads(hg, 4), heads(ss[:, 0:GW], 1)[:, 0], groups(ss[:, 512:768]), groups(ss[:, 768:1024]),
               jnp.stack([per_head(ss[:, GW:2 * GW]), per_head(ml[:, 3 * GW:4 * GW]), per_head(ml[:, 4 * GW:5 * GW])],
                         axis=1),
               heads(ml[:, 0:3 * GW], 3))
    st_ins = (wkv, hgs, ssm, cst, nst.reshape(DEPTH, bsz, NH, 1, HEAD_DIM), mst.reshape(DEPTH, bsz, NH, 1, 1))
    blk = lambda a: pl.BlockSpec((bb,) + a.shape[1:], lambda i: (i,) + (0,) * (a.ndim - 1))
    st_in = lambda a: pl.BlockSpec((None, bb) + a.shape[2:], lambda i: (layer, i) + (0,) * (a.ndim - 2))
    st_out = lambda a: pl.BlockSpec((bb,) + a.shape[2:], lambda i: (i,) + (0,) * (a.ndim - 2))
    o_shape = jax.ShapeDtypeStruct((bsz, NH, 1, HEAD_DIM), F32)
    o_spec = pl.BlockSpec((bb, NH, 1, HEAD_DIM), lambda i: (i, 0, 0, 0))
    outs = pl.pallas_call(
        _stepv_kernel,
        grid=(bsz // bb,),
        in_specs=[blk(a) for a in seq_ins] + [st_in(a) for a in st_ins],
        out_specs=[o_spec] * 4 + [st_out(a) for a in st_ins],
        out_shape=[o_shape] * 4 + [jax.ShapeDtypeStruct(a.shape[1:], F32) for a in st_ins],
        compiler_params=_cparams(("parallel",)),
    )(*seq_ins, *st_ins)
    o_parts = [o.reshape(bsz, GW) for o in outs[:4]]
    wkv_n, hgs_n, ssm_n, c_n, n_n, m_n = outs[4:]
    return o_parts, (wkv_n, hgs_n, ssm_n, c_n, n_n.reshape(bsz, NH, HEAD_DIM), m_n.reshape(bsz, NH))


def _mixers_step(scan, states, layer):
    bsz = scan[0].shape[0]
    bb = SUBLANES
    rows = lambda a: pl.BlockSpec((bb, a.shape[1]), lambda i: (i, 0))
    st_in = lambda a: pl.BlockSpec((None, bb) + a.shape[2:], lambda i: (layer, i) + (0,) * (a.ndim - 2))
    st_out = lambda a: pl.BlockSpec((bb,) + a.shape[2:], lambda i: (i,) + (0,) * (a.ndim - 2))
    o_shape = jax.ShapeDtypeStruct((bsz, GW), F32)
    outs = pl.pallas_call(
        functools.partial(_step_kernel, bb),
        grid=(bsz // bb,),
        in_specs=[rows(a) for a in scan] + [st_in(a) for a in states],
        out_specs=[pl.BlockSpec((bb, GW), lambda i: (i, 0))] * 4 + [st_out(a) for a in states],
        out_shape=[o_shape] * 4 + [jax.ShapeDtypeStruct(a.shape[1:], F32) for a in states],
        compiler_params=_cparams(("parallel",)),
    )(*scan, *states)
    return outs[:4], outs[4:]


def kernel(x_prompt, x_sample, state_rwkv_wkv, state_rwkv_shift, state_hgrn, state_ssm, state_ssm_conv, state_mlstm_C, state_mlstm_n, state_mlstm_m, cache_mem_k, cache_mem_v, mem_prompt, norm_mix_g, w_in, w_out, rwkv_mu, rwkv_w0, rwkv_w_up, rwkv_a0, rwkv_a_up, rwkv_g_up, rwkv_k_k, rwkv_k_a, rwkv_r_k, rwkv_ln_g, rwkv_ln_b, hgrn_lb_logits, hgrn_norm_g, ssm_conv_w, ssm_conv_b, ssm_dt_bias, ssm_A_log, ssm_D, ssm_norm_g, mlstm_i_bias, mlstm_f_bias, mlstm_norm_g, norm_x_g, norm_mem_g, xattn_wq, xattn_wk, xattn_wv, xattn_wo, norm_ff_g, ff_w1, ff_w2, final_norm_g):
    bp, tp, _ = x_prompt.shape
    bs = x_sample.shape[0]
    seg64, seg128 = _seg_matrix(HEAD_DIM), _seg_matrix(128)
    seg64_b = seg64.astype(BF16)
    gate_ex = _gate_expand()
    tril_c = jnp.asarray(np.tril(np.ones((CHUNK, CHUNK), np.float32)))
    tril_s = jnp.asarray(np.tril(np.ones((SUB, SUB), np.float32)))
    blk = np.arange(CHUNK) // SUB
    ones_blk = jnp.asarray((blk[:, None] == blk[None, :]).astype(np.float32))
    tril_blk = ones_blk * tril_c
    rep = lambda a: jnp.repeat(a, HEAD_DIM)
    pad_rows = lambda a, n: jnp.pad(a, ((0, n - a.shape[0]), (0, 0)))
    zeros = lambda *s: jnp.zeros(s, F32)

    hp = x_prompt.reshape(bp * tp, D_MODEL)
    hs = x_sample.reshape(bs, D_MODEL)
    mem = mem_prompt.reshape(bp * N_MEM, D_MODEL)
    p_states, s_states, p_mem = [], [], []

    for l in range(DEPTH):
        w_in_re = _gather_cols(w_in[l], _IN_IDX).astype(BF16)
        pv = pad_rows(jnp.stack([rwkv_w0[l], rwkv_a0[l], rwkv_k_k[l], rwkv_k_a[l], rwkv_r_k[l].reshape(GW),
                                 rep(ssm_dt_bias[l]), rep(ssm_A_log[l]), rep(ssm_D[l]), rep(mlstm_i_bias[l]),
                                 rep(mlstm_f_bias[l])]), 16)
        lora = jnp.stack([pad_rows(rwkv_w_up[l], LANES), pad_rows(rwkv_a_up[l], LANES), pad_rows(rwkv_g_up[l], LANES)])
        conv = pad_rows(jnp.concatenate([ssm_conv_w[l], ssm_conv_b[l][None]], axis=0), 8)
        mu = _gather_cols(rwkv_mu[l], _RW_IDX).reshape(1, RW_W)
        gp = pad_rows(jnp.stack([_gate_row(ssm_dt_bias[l], mlstm_i_bias[l], mlstm_f_bias[l]),
                                 _gate_row(ssm_A_log[l], zeros(NH), zeros(NH))]), 8)
        pd = pad_rows(jnp.stack([rep(ssm_D[l]), ssm_norm_g[l]]), 8)
        pn_h = pad_rows(hgrn_norm_g[l][None], 8)
        pn_m = pad_rows(mlstm_norm_g[l][None], 8)
        pq = pad_rows(jnp.stack([rwkv_ln_g[l], rwkv_ln_b[l], hgrn_norm_g[l], ssm_norm_g[l], mlstm_norm_g[l]]), 8)
        w_out_b = w_out[l].astype(BF16)
        wq_b, wo_b = xattn_wq[l].astype(BF16), xattn_wo[l].astype(BF16)
        wkv_b = jnp.concatenate([xattn_wk[l], xattn_wv[l]], axis=1).astype(BF16)
        w1_b, w2_b = ff_w1[l].astype(BF16), ff_w2[l].astype(BF16)

        kv = _norm_matmul(mem, norm_mem_g[l], wkv_b)
        p_mem.append((kv[:, :D_MODEL].reshape(bp, N_MEM, XH, XHD), kv[:, D_MODEL:].reshape(bp, N_MEM, XH, XHD)))
        proj = _norm_matmul(hp, norm_mix_g[l], w_in_re)
        m_h, (st_h,) = _chunk_call(
            functools.partial(_hgrn_chunk_kernel, l), proj, [(HG_W, HG_OFF // HG_W)],
            [hgrn_lb_logits, pn_h, tril_blk, ones_blk, seg64_b, seg64], [(GW, GW)],
            [pltpu.VMEM((CHUNK, GW), F32)] * 3, bp, tp)
        m_s, (st_s,) = _chunk_call(
            _ssm_chunk_kernel, proj, [(SS_W, SS_OFF // SS_W), (GT_W, GT_OFF // GT_W)],
            [conv, gp, pd, tril_c], [(NH, SSM_STATE, HEAD_DIM)], [pltpu.VMEM((SUBLANES, 768), F32)], bp, tp)
        m_m, (st_c, st_n, st_m) = _chunk_call(
            _mlstm_chunk_kernel, proj, [(ML_W, ML_OFF // ML_W), (GT_W, GT_OFF // GT_W)],
            [gp, pn_m, tril_c], [(NH, HEAD_DIM, HEAD_DIM), (SUBLANES, HEAD_DIM), (1, LANES)], [], bp, tp)
        rw_scan, rw_post = _prep_rw_seq(proj, zeros(bp, SUBLANES, RW_W), mu, pv, lora, seg64, bp, tp)
        o_rw, (st_r,) = _chunk_call(
            _rwkv_chunk2_kernel, rw_scan, [(1536, 0)], [tril_blk, ones_blk, seg64_b, seg64], [(GW, GW)],
            [pltpu.VMEM((CHUNK, GW), F32)] * 4, bp, tp)
        proj3 = proj.reshape(bp, tp, IN_RE)
        shift_new = jnp.take(proj3[:, -1, RW_OFF:RW_OFF + RW_W], jnp.asarray(_RW_INV), axis=-1)
        conv_new = proj3[:, tp - 3:, SS_OFF + 256:SS_OFF + 1024]
        diag_blocks = lambda s: jnp.stack(
            [s.reshape(bp, NH, HEAD_DIM, NH, HEAD_DIM)[:, h, :, h, :] for h in range(NH)], axis=1)
        hgrn_new = diag_blocks(st_h).transpose(0, 1, 3, 2)
        p_states.append((diag_blocks(st_r), shift_new, hgrn_new, st_s.transpose(0, 1, 3, 2), conv_new,
                         st_c.transpose(0, 1, 3, 2), st_n[:, :NH], st_m[:, 0, :NH]))
        hp = _post_call(_post_seq_kernel, hp, (o_rw, rw_post, m_h, m_s, m_m), (pq, seg64, w_out_b))
        q = _norm_matmul(hp, norm_x_g[l], wq_b)
        hp = _xattn_seq(q, kv, hp, wo_b, bp, tp)
        hp = _ffn(hp, norm_ff_g[l], w1_b, w2_b)

        proj = _norm_matmul(hs, norm_mix_g[l], w_in_re)
        prev_rw = _gather_cols(state_rwkv_shift[l], _RW_IDX)
        cb = state_ssm_conv[l]
        prep = _prep_step(proj, prev_rw, cb[:, 0], cb[:, 1], cb[:, 2],
                          (mu, pv, lora, seg64, hgrn_lb_logits, conv, gate_ex), l)
        o_parts, (wkv_n, hg_n, ssm_n, c_n, n_n, m_n) = _mixers_stepv(
            prep[0::2], (state_rwkv_wkv, state_hgrn, state_ssm, state_mlstm_C, state_mlstm_n, state_mlstm_m), l)
        shift_new = jnp.take(proj[:, RW_OFF:RW_OFF + RW_W], jnp.asarray(_RW_INV), axis=-1)
        conv_new = jnp.concatenate([cb[:, 1:], proj[:, None, SS_OFF + 256:SS_OFF + 1024]], axis=1)
        s_states.append((wkv_n, shift_new, hg_n, ssm_n, conv_new, c_n, n_n, m_n))
        hs = _post_call(_post_step_kernel, hs, tuple(o_parts) + tuple(prep[1::2]), (pq, seg64, seg128, w_out_b))
        q = _norm_matmul(hs, norm_x_g[l], wq_b)
        att = _xattn_step(q, cache_mem_k, cache_mem_v, l)
        hs = _mm_res(att, wo_b, hs)
        hs = _ffn(hs, norm_ff_g[l], w1_b, w2_b)

    y_prompt = _rmsnorm(hp, final_norm_g).reshape(bp, tp, D_MODEL)
    y_sample = _rmsnorm(hs, final_norm_g).reshape(bs, 1, D_MODEL)
    p_out = [jnp.stack(s) for s in zip(*p_states)]
    s_out = [jnp.stack(s) for s in zip(*s_states)]
    p_mem_k = jnp.stack([kv[0] for kv in p_mem])
    p_mem_v = jnp.stack([kv[1] for kv in p_mem])
    return (y_prompt, y_sample, *p_out, p_mem_k, p_mem_v, *s_out)
```

```python
import functools

import numpy as np
import jax
import jax.numpy as jnp
from jax import lax
from jax.experimental import pallas as pl
from jax.experimental.pallas import tpu as pltpu

F32 = jnp.float32
BF16 = jnp.bfloat16
HI = lax.Precision.HIGHEST

D_MODEL = 1024
DEPTH = 2
NH = 4
HEAD_DIM = 64
GW = 256
SSM_STATE = 128
N_MEM = 256
XH = 4
XHD = 256
D_FF = 4096
NORM_EPS = 1e-6
RWKV_GN_EPS = 64e-5
RWKV_COLS = 896
LANES = 128
SUBLANES = 8
VMEM_LIMIT = 48 * 1024 * 1024
CHUNK = 128
SUB = 16

HG_OFF, HG_W = 0, 1024
SS_OFF, SS_W = 1024, 1024
ML_OFF, ML_W = 2048, 1024
GT_OFF, GT_W = 3072, 128
RW_OFF, RW_W = 3456, 1152
IN_RE = 4608
GT_DT, GT_I, GT_F = 0, 4, 8


def _in_col_index():
    idx = -np.ones(IN_RE, np.int32)
    idx[HG_OFF:HG_OFF + 1024] = 896 + np.arange(1024)
    ss = 1920
    idx[SS_OFF:SS_OFF + 1024] = ss + np.arange(1024)
    ml = 2948
    idx[ML_OFF:ML_OFF + 768] = ml + np.arange(768)
    idx[ML_OFF + 768:ML_OFF + 1024] = ml + 776 + np.arange(256)
    idx[GT_OFF + GT_DT:GT_OFF + GT_DT + NH] = ss + 1024 + np.arange(NH)
    idx[GT_OFF + GT_I:GT_OFF + GT_I + NH] = ml + 768 + np.arange(NH)
    idx[GT_OFF + GT_F:GT_OFF + GT_F + NH] = ml + 772 + np.arange(NH)
    idx[RW_OFF:RW_OFF + 768] = np.arange(768)
    idx[RW_OFF + 768:RW_OFF + 800] = 768 + np.arange(32)
    idx[RW_OFF + 896:RW_OFF + 928] = 800 + np.arange(32)
    idx[RW_OFF + 1024:RW_OFF + 1088] = 832 + np.arange(64)
    return idx


_IN_IDX = _in_col_index()
_RW_IDX = _IN_IDX[RW_OFF:RW_OFF + RW_W]
_RW_INV = np.concatenate([np.arange(800), 896 + np.arange(32), 1024 + np.arange(64)]).astype(np.int32)


def _gather_cols(a, idx):
    g = jnp.take(a, jnp.asarray(np.maximum(idx, 0)), axis=-1)
    return jnp.where(jnp.asarray(idx >= 0), g, jnp.zeros((), a.dtype))


def _seg_matrix(width):
    i = np.arange(GW)
    return jnp.asarray((i[:, None] // width == i[None, :] // width).astype(np.float32))


def _gate_expand():
    e = np.zeros((3, LANES, GW), np.float32)
    for s, off in enumerate((GT_DT, GT_I, GT_F)):
        for h in range(NH):
            e[s, off + h, h * HEAD_DIM:(h + 1) * HEAD_DIM] = 1.0
    return jnp.asarray(e)


def _gate_row(dt_bias, i_bias, f_bias):
    z = jnp.zeros((LANES,), F32)
    return z.at[GT_DT:GT_DT + NH].set(dt_bias).at[GT_I:GT_I + NH].set(i_bias).at[GT_F:GT_F + NH].set(f_bias)


def _cparams(sem):
    return pltpu.CompilerParams(dimension_semantics=sem, vmem_limit_bytes=VMEM_LIMIT)


def _softplus(x):
    return jnp.maximum(x, 0.0) + jnp.log(1.0 + jnp.exp(-jnp.abs(x)))


def _sigmoid(x):
    return 1.0 / (1.0 + jnp.exp(-x))


def _rms(x, g):
    ms = jnp.mean(x * x, axis=-1, keepdims=True)
    return x * lax.rsqrt(ms + NORM_EPS) * g


def _dot(a, b):
    return jnp.dot(a.astype(BF16), b.astype(BF16), preferred_element_type=F32)


def _dot_nt(a, b):
    return lax.dot_general(a.astype(BF16), b.astype(BF16), (((1,), (1,)), ((), ())), preferred_element_type=F32)


def _dot_tn(a, b):
    return lax.dot_general(a.astype(BF16), b.astype(BF16), (((0,), (0,)), ((), ())), preferred_element_type=F32)


def _dot_hi(a, b):
    return jnp.dot(a, b, precision=HI, preferred_element_type=F32)


def _norm_matmul_kernel(x_ref, g_ref, w_ref, o_ref, xn_ref):
    @pl.when(pl.program_id(1) == 0)
    def _():
        xn_ref[...] = _rms(x_ref[...], g_ref[...]).astype(BF16)

    o_ref[...] = jnp.dot(xn_ref[...], w_ref[...], preferred_element_type=F32)


def _row_tile(m):
    return min(m, 1024)


def _norm_matmul(x, g, w):
    m, k = x.shape
    n = w.shape[1]
    tm = _row_tile(m)
    tn = next(c for c in (1536, 1024, 512, n) if n % c == 0)
    return pl.pallas_call(
        _norm_matmul_kernel,
        grid=(m // tm, n // tn),
        in_specs=[pl.BlockSpec((tm, k), lambda i, j: (i, 0)),
                  pl.BlockSpec((1, k), lambda i, j: (0, 0)),
                  pl.BlockSpec((k, tn), lambda i, j: (0, j))],
        out_specs=pl.BlockSpec((tm, tn), lambda i, j: (i, j)),
        out_shape=jax.ShapeDtypeStruct((m, n), F32),
        scratch_shapes=[pltpu.VMEM((tm, k), BF16)],
        compiler_params=_cparams(("parallel", "arbitrary")),
    )(x, g.reshape(1, k), w)


def _mm_res_kernel(a_ref, w_ref, res_ref, o_ref):
    o_ref[...] = res_ref[...] + jnp.dot(a_ref[...].astype(BF16), w_ref[...], preferred_element_type=F32)


def _mm_res(a, w, res):
    m, k = a.shape
    n = w.shape[1]
    tm = min(m, 512)
    return pl.pallas_call(
        _mm_res_kernel,
        grid=(m // tm,),
        in_specs=[pl.BlockSpec((tm, k), lambda i: (i, 0)),
                  pl.BlockSpec((k, n), lambda i: (0, 0)),
                  pl.BlockSpec((tm, n), lambda i: (i, 0))],
        out_specs=pl.BlockSpec((tm, n), lambda i: (i, 0)),
        out_shape=jax.ShapeDtypeStruct((m, n), F32),
        compiler_params=_cparams(("parallel",)),
    )(a, w, res)


def _ffn_kernel(x_ref, g_ref, w1_ref, w2_ref, o_ref, xn_ref, acc_ref):
    j = pl.program_id(1)

    @pl.when(j == 0)
    def _():
        xn_ref[...] = _rms(x_ref[...], g_ref[...]).astype(BF16)
        acc_ref[...] = jnp.zeros_like(acc_ref)

    h = jnp.dot(xn_ref[...], w1_ref[...], preferred_element_type=F32)
    h = jnp.square(jnp.maximum(h, 0.0))
    acc_ref[...] += jnp.dot(h.astype(BF16), w2_ref[...], preferred_element_type=F32)

    @pl.when(j == pl.num_programs(1) - 1)
    def _():
        o_ref[...] = x_ref[...] + acc_ref[...]


def _ffn(x, g, w1, w2):
    m, k = x.shape
    ff = w1.shape[1]
    tm = _row_tile(m)
    tf = 1024
    return pl.pallas_call(
        _ffn_kernel,
        grid=(m // tm, ff // tf),
        in_specs=[pl.BlockSpec((tm, k), lambda i, j: (i, 0)),
                  pl.BlockSpec((1, k), lambda i, j: (0, 0)),
                  pl.BlockSpec((k, tf), lambda i, j: (0, j)),
                  pl.BlockSpec((tf, k), lambda i, j: (j, 0))],
        out_specs=pl.BlockSpec((tm, k), lambda i, j: (i, 0)),
        out_shape=jax.ShapeDtypeStruct((m, k), F32),
        scratch_shapes=[pltpu.VMEM((tm, k), BF16), pltpu.VMEM((tm, k), F32)],
        compiler_params=_cparams(("parallel", "arbitrary")),
    )(x, g.reshape(1, k), w1, w2)


def _rmsnorm_kernel(x_ref, g_ref, o_ref):
    o_ref[...] = _rms(x_ref[...], g_ref[...])


def _rmsnorm(x, g):
    m, k = x.shape
    tm = min(m, 512)
    return pl.pallas_call(
        _rmsnorm_kernel,
        grid=(m // tm,),
        in_specs=[pl.BlockSpec((tm, k), lambda i: (i, 0)), pl.BlockSpec((1, k), lambda i: (0, 0))],
        out_specs=pl.BlockSpec((tm, k), lambda i: (i, 0)),
        out_shape=jax.ShapeDtypeStruct((m, k), F32),
        compiler_params=_cparams(("parallel",)),
    )(x, g.reshape(1, k))


def _shift_rows(x, tail, j):
    xr = pltpu.roll(x, j, 0)
    tr = pltpu.roll(tail, j, 0)
    rows = lax.broadcasted_iota(jnp.int32, tr.shape, 0)
    first = jnp.where(rows < j, tr, xr[0:SUBLANES])
    return jnp.concatenate([first, xr[SUBLANES:]], axis=0)


def _hgrn_lower_bound(logit_ref, layer):
    lg = logit_ref[...]
    e = jnp.exp(lg - jnp.max(lg, axis=0, keepdims=True))
    prb = e / jnp.sum(e, axis=0, keepdims=True)
    return jnp.sum(prb[0:layer + 1], axis=0, keepdims=True) - prb[0:1]


def _conv_silu(u, c3, c2, c1, conv_ref):
    cw = conv_ref[...]
    conv = cw[4:5] + c3 * cw[0:1] + c2 * cw[1:2] + c1 * cw[2:3] + u * cw[3:4]
    return conv * _sigmoid(conv)


def _prep_rwkv(pr, prev_rw, mu_ref, pv_ref, lora_ref, seg_ref, rw_scan, rw_post, log_decay):
    pv = pv_ref[...]
    row = lambda i: pv[i:i + 1, :]
    seg = seg_ref[...]
    pm = pr + (prev_rw - pr) * mu_ref[...]
    r, k, v = pm[:, 0:256], pm[:, 256:512], pm[:, 512:768]
    dw, da, dg = pm[:, 768:896], pm[:, 896:1024], pm[:, 1024:1152]
    w_log = -_softplus(-(row(0) + _dot_hi(jnp.tanh(dw), lora_ref[0]))) - 0.5
    a = _sigmoid(row(1) + _dot_hi(da, lora_ref[1]))
    kk = k * row(2)
    kk = kk / jnp.maximum(jnp.sqrt(_dot_hi(kk * kk, seg)), 1e-12)
    k2 = k * (1.0 + (a - 1.0) * row(3))
    rw_scan[:, 0:256] = r
    rw_scan[:, 256:512] = k2
    rw_scan[:, 512:768] = v
    rw_scan[:, 768:1024] = -jnp.exp(w_log) if log_decay else jnp.exp(-jnp.exp(w_log))
    rw_scan[:, 1024:1280] = kk
    rw_scan[:, 1280:1536] = kk * a
    rw_post[:, 0:256] = _dot_hi(r * k2 * row(4), seg) * v
    rw_post[:, 256:512] = _dot_hi(_sigmoid(dg), lora_ref[2])


def _prep_rw_seq_kernel(p_ref, prev8_ref, tail0_ref, mu_ref, pv_ref, lora_ref, seg_ref, rw_scan, rw_post):
    tail = jnp.where(pl.program_id(1) == 0, tail0_ref[0], prev8_ref[...])
    pr = p_ref[...]
    _prep_rwkv(pr, _shift_rows(pr, tail, 1), mu_ref, pv_ref, lora_ref, seg_ref, rw_scan, rw_post, True)


def _prep_rw_seq(proj, tail0, mu, pv, lora, seg, bsz, t):
    tt = 512
    nt = t // tt
    m = bsz * t
    cb = RW_OFF // RW_W
    c2 = lambda b, i: (0, 0)
    return pl.pallas_call(
        _prep_rw_seq_kernel,
        grid=(bsz, nt),
        in_specs=[pl.BlockSpec((tt, RW_W), lambda b, i: (b * nt + i, cb)),
                  pl.BlockSpec((SUBLANES, RW_W),
                               lambda b, i: (jnp.maximum((b * nt + i) * (tt // SUBLANES) - 1, 0), cb)),
                  pl.BlockSpec((1, SUBLANES, RW_W), lambda b, i: (b, 0, 0)),
                  pl.BlockSpec((1, RW_W), c2), pl.BlockSpec((16, GW), c2),
                  pl.BlockSpec((3, LANES, GW), lambda b, i: (0, 0, 0)), pl.BlockSpec((GW, GW), c2)],
        out_specs=[pl.BlockSpec((tt, 1536), lambda b, i: (b * nt + i, 0)),
                   pl.BlockSpec((tt, 512), lambda b, i: (b * nt + i, 0))],
        out_shape=[jax.ShapeDtypeStruct((m, 1536), F32), jax.ShapeDtypeStruct((m, 512), F32)],
        compiler_params=_cparams(("parallel", "parallel")),
    )(proj, proj, tail0, mu, pv, lora, seg)


PREP_OUT_WIDTHS = (1536, 512, 1024, 256, 1024, 512, 1280, 256)


def _prep_step_kernel(layer, p_ref, prev_ref, c3_ref, c2_ref, c1_ref, mu_ref, pv_ref, lora_ref, seg_ref, logit_ref,
                      conv_ref, ex_ref, rw_scan, rw_post, hg_scan, hg_post, ss_scan, ss_post, ml_scan, ml_post):
    pv = pv_ref[...]
    row = lambda i: pv[i:i + 1, :]
    _prep_rwkv(p_ref[:, RW_OFF:RW_OFF + RW_W], prev_ref[...], mu_ref, pv_ref, lora_ref, seg_ref, rw_scan, rw_post,
               False)
    gates = p_ref[:, GT_OFF:GT_OFF + GT_W]
    dt_pre, i_pre, f_pre_m = (_dot_hi(gates, ex_ref[s]) for s in range(3))

    lb = _hgrn_lower_bound(logit_ref, layer)
    q_pre = p_ref[:, HG_OFF:HG_OFF + 256]
    f_pre = p_ref[:, HG_OFF + 256:HG_OFF + 512]
    gg = p_ref[:, HG_OFF + 768:HG_OFF + 1024]
    hg_scan[:, 0:256] = q_pre * _sigmoid(q_pre)
    hg_scan[:, 256:512] = lb + (1.0 - lb) * _sigmoid(f_pre)
    hg_scan[:, 512:768] = (1.0 - lb) * _sigmoid(-f_pre)
    hg_scan[:, 768:1024] = p_ref[:, HG_OFF + 512:HG_OFF + 768]
    hg_post[...] = gg * _sigmoid(gg)

    z = p_ref[:, SS_OFF:SS_OFF + 256]
    xbc = _conv_silu(p_ref[:, SS_OFF + 256:SS_OFF + 1024], c3_ref[...], c2_ref[...], c1_ref[...], conv_ref)
    xs = xbc[:, 0:256]
    dt = _softplus(dt_pre + row(5))
    ss_scan[:, 0:256] = xs * dt
    ss_scan[:, 256:512] = jnp.exp(dt * (-jnp.exp(row(6))))
    ss_scan[:, 512:1024] = xbc[:, 256:768]
    ss_post[:, 0:256] = xs * row(7)
    ss_post[:, 256:512] = z * _sigmoid(z)

    ml_scan[:, 0:256] = p_ref[:, ML_OFF:ML_OFF + 256]
    ml_scan[:, 256:512] = p_ref[:, ML_OFF + 256:ML_OFF + 512] * (HEAD_DIM ** -0.5)
    ml_scan[:, 512:768] = p_ref[:, ML_OFF + 512:ML_OFF + 768]
    ml_scan[:, 768:1024] = i_pre + row(8)
    ml_scan[:, 1024:1280] = -_softplus(-(f_pre_m + row(9)))
    ml_post[...] = _sigmoid(p_ref[:, ML_OFF + 768:ML_OFF + 1024])


def _prep_step(proj, prev_rw, c3, c2, c1, params, layer):
    m = proj.shape[0]
    full = lambda a: pl.BlockSpec(a.shape, lambda i: (0,) * a.ndim)
    ins = (proj, prev_rw, c3, c2, c1) + tuple(params)
    return pl.pallas_call(
        functools.partial(_prep_step_kernel, layer),
        grid=(1,),
        in_specs=[full(a) for a in ins],
        out_specs=[pl.BlockSpec((m, w), lambda i: (0, 0)) for w in PREP_OUT_WIDTHS],
        out_shape=[jax.ShapeDtypeStruct((m, w), F32) for w in PREP_OUT_WIDTHS],
        compiler_params=_cparams(("arbitrary",)),
    )(*ins)


def _ssm_chunk_body(ss_ref, gt_ref, conv_ref, gp_ref, pd_ref, tril_ref, o_ref, ht_ref, tail_ref):
    @pl.when(pl.program_id(1) == 0)
    def _():
        ht_ref[...] = jnp.zeros_like(ht_ref)
        tail_ref[...] = jnp.zeros_like(tail_ref)

    yield
    L = ss_ref.shape[0]
    u = ss_ref[:, 256:1024]
    tail = tail_ref[...]
    c1, c2, c3 = (_shift_rows(u, tail, j) for j in (1, 2, 3))
    tail_ref[...] = u[L - SUBLANES:L]
    xbc = _conv_silu(u, c3, c2, c1, conv_ref)
    z = ss_ref[:, 0:256]
    zs = z * _sigmoid(z)
    dt = _softplus(gt_ref[...] + gp_ref[0:1])
    la = dt * (-jnp.exp(gp_ref[1:2]))
    tril = tril_ref[...]
    mask = tril > 0.5
    cum = _dot_hi(tril, la)
    yield
    cum_t = cum.T
    pd = pd_ref[...]
    hsl = lambda h: slice(h * HEAD_DIM, (h + 1) * HEAD_DIM)
    bgs = [xbc[:, 256 + SSM_STATE * g:256 + SSM_STATE * (g + 1)] for g in range(2)]
    cgs = [xbc[:, 512 + SSM_STATE * g:512 + SSM_STATE * (g + 1)] for g in range(2)]
    scores = [_dot_nt(cgs[g], bgs[g]) for g in range(2)]
    hts = [ht_ref[0, h] for h in range(NH)]
    inter = [_dot(cgs[h // 2], hts[h]) for h in range(NH)]
    bg_t = [bgs[g].T for g in range(2)]
    ccs = [cum[:, GT_DT + h:GT_DT + h + 1] for h in range(NH)]
    dtc = [dt[:, GT_DT + h:GT_DT + h + 1] for h in range(NH)]
    lms = []
    for h in range(NH):
        cr = cum_t[GT_DT + h:GT_DT + h + 1, :]
        lms.append(jnp.where(mask, jnp.exp(jnp.where(mask, ccs[h] - cr, 0.0)), 0.0))
    yield
    ydots = [_dot(scores[h // 2] * lms[h], xbc[:, hsl(h)] * dtc[h]) for h in range(NH)]
    cends = [ccs[h][L - 1:L, :] for h in range(NH)]
    hdots = [_dot(bg_t[h // 2], xbc[:, hsl(h)] * (jnp.exp(cends[h] - ccs[h]) * dtc[h])) for h in range(NH)]
    yield
    ys = []
    for h in range(NH):
        y = ydots[h] + inter[h] * jnp.exp(ccs[h])
        ys.append((y + xbc[:, hsl(h)] * pd[0:1, hsl(h)]) * zs[:, hsl(h)])
        ht_ref[0, h] = jnp.exp(cends[h]) * hts[h] + hdots[h]
    for g in range(2):
        y0, y1 = ys[2 * g], ys[2 * g + 1]
        ms = jnp.sum(y0 * y0, axis=-1, keepdims=True) + jnp.sum(y1 * y1, axis=-1, keepdims=True)
        scale = lax.rsqrt(ms * (1.0 / (2 * HEAD_DIM)) + NORM_EPS)
        for h in (2 * g, 2 * g + 1):
            o_ref[:, hsl(h)] = ys[h] * scale * pd[1:2, hsl(h)]


def _mlstm_chunk_body(ml_ref, gt_ref, gp_ref, pn_ref, tril_ref, o_ref, ct_ref, n_ref, m_ref):
    @pl.when(pl.program_id(1) == 0)
    def _():
        ct_ref[...] = jnp.zeros_like(ct_ref)
        n_ref[...] = jnp.zeros_like(n_ref)
        m_ref[...] = jnp.zeros_like(m_ref)

    yield
    L = ml_ref.shape[0]
    x = gt_ref[...] + gp_ref[0:1]
    lf = -_softplus(-x)
    tril = tril_ref[...]
    mask = tril > 0.5
    b = _dot_hi(tril, lf)
    yield
    b_t = b.T
    x_t = x.T
    m_row = m_ref[0]
    m_new_row = m_row
    lane = lax.broadcasted_iota(jnp.int32, m_row.shape, 1)
    pn = pn_ref[...]
    hsl = lambda h, base=0: slice(base + h * HEAD_DIM, base + (h + 1) * HEAD_DIM)
    for h in range(NH):
        b_col, b_row = b[:, GT_F + h:GT_F + h + 1], b_t[GT_F + h:GT_F + h + 1, :]
        li_col, li_row = x[:, GT_I + h:GT_I + h + 1], x_t[GT_I + h:GT_I + h + 1, :]
        m_prev = m_row[:, h:h + 1]
        log_inter = b_col + m_prev
        log_intra = jnp.where(mask, b_col - b_row + li_row, -jnp.inf)
        m_t = jnp.maximum(log_inter, jnp.max(log_intra, axis=-1, keepdims=True))
        w_inter = jnp.exp(log_inter - m_t)
        qh = ml_ref[:, hsl(h)]
        kh = ml_ref[:, hsl(h, 256)] * (HEAD_DIM ** -0.5)
        vh = ml_ref[:, hsl(h, 512)]
        ct = ct_ref[0, h]
        qk_raw = _dot_nt(qh, kh)
        q_ct = _dot(qh, ct)
        yield
        qk = qk_raw * jnp.exp(log_intra - m_t)
        n_h = n_ref[0, h:h + 1, :]
        m_end = m_t[L - 1:L, :]
        b_end = b_col[L - 1:L, :]
        g_prev = jnp.exp(b_end + m_prev - m_end)
        kg = kh * jnp.exp(b_end - b_col + li_col - m_end)
        qk_v = _dot(qk, vh)
        kg_v = _dot_tn(kg, vh)
        yield
        num = qk_v + w_inter * q_ct
        den = jnp.sum(qk, axis=-1, keepdims=True) + w_inter * jnp.sum(qh * n_h, axis=-1, keepdims=True)
        hout = num / jnp.maximum(jnp.abs(den), jnp.exp(-m_t))
        ct_ref[0, h] = g_prev * ct + kg_v
        n_ref[0, h:h + 1, :] = g_prev * n_h + jnp.sum(kg, axis=0, keepdims=True)
        m_new_row = jnp.where(lane == h, m_end, m_new_row)
        hn = hout * lax.rsqrt(jnp.mean(hout * hout, axis=-1, keepdims=True) + NORM_EPS)
        o_ref[:, hsl(h)] = hn * pn[0:1, hsl(h)] * _sigmoid(ml_ref[:, hsl(h, 768)])
    m_ref[0] = m_new_row


def _hgrn_chunk_body(layer, hg_ref, logit_ref, pn_ref, tril_ref, ones_ref, seg_ref, bd_ref, o_ref, st_ref, q_scr,
                     b_scr, i_scr):
    @pl.when(pl.program_id(1) == 0)
    def _():
        st_ref[...] = jnp.zeros_like(st_ref)

    yield
    lb = _hgrn_lower_bound(logit_ref, layer)
    seg = seg_ref[...]
    bd = bd_ref[...]
    pn = pn_ref[...]
    srow = lax.broadcasted_iota(jnp.int32, (SUB, GW), 0)
    L = hg_ref.shape[0]

    q_pre, f_pre, v, gg = hg_ref[:, 0:256], hg_ref[:, 256:512], hg_ref[:, 512:768], hg_ref[:, 768:1024]
    q = q_pre * _sigmoid(q_pre)
    k = (1.0 - lb) * _sigmoid(-f_pre)
    lf = jnp.log(lb + (1.0 - lb) * _sigmoid(f_pre))
    b = _dot_hi(tril_ref[...], lf)
    b_tot = _dot_hi(ones_ref[...], lf)
    yield
    q_scr[...] = q
    b_scr[...] = b
    quarter = L // 4
    for part in range(4):
        tiles = []
        for t in range(part * quarter, (part + 1) * quarter):
            rows = slice((t // SUB) * SUB, (t // SUB + 1) * SUB)
            live = srow <= (t % SUB)
            dec = jnp.where(live, jnp.exp(jnp.where(live, b_scr[t:t + 1, :] - b[rows], 0.0)), 0.0)
            tiles.append(dec * q_scr[t:t + 1, :] * k[rows])
        a_b = jnp.dot(jnp.concatenate(tiles, axis=0).astype(BF16), seg, preferred_element_type=F32)
        yield
        for j in range(quarter):
            t = part * quarter + j
            rows = slice((t // SUB) * SUB, (t // SUB + 1) * SUB)
            i_scr[t:t + 1, :] = jnp.sum(a_b[j * SUB:(j + 1) * SUB, :] * v[rows], axis=0, keepdims=True)
    qx = q * jnp.exp(b)
    kx = k * jnp.exp(b_tot - b)
    p_end = jnp.exp(b_tot)
    st = st_ref[0]
    for c in range(L // SUB):
        rows = slice(c * SUB, (c + 1) * SUB)
        upd = _dot_tn(v[rows], kx[rows]) * bd
        inter = _dot_nt(qx[rows], st)
        yield
        i_scr[rows, :] = inter + i_scr[rows, :]
        st = st * p_end[c * SUB:c * SUB + 1, :] + upd
    st_ref[0] = st
    o = i_scr[...]
    gate = gg * _sigmoid(gg)
    for h in range(NH):
        sl = slice(h * HEAD_DIM, (h + 1) * HEAD_DIM)
        oh = o[:, sl]
        o_ref[:, sl] = oh * lax.rsqrt(jnp.mean(oh * oh, axis=-1, keepdims=True) + NORM_EPS) * pn[0:1, sl] * gate[:, sl]


def _rwkv_chunk_kernel(rw_ref, tril_ref, seg_ref, bd_ref, o_ref, st_ref, kx_scr, mv_scr):
    @pl.when(pl.program_id(1) == 0)
    def _():
        st_ref[...] = jnp.zeros_like(st_ref)

    tril = tril_ref[...]
    seg = seg_ref[...]
    bd = bd_ref[...]
    srow = lax.broadcasted_iota(jnp.int32, (SUB, GW), 0)
    csum = lambda a: jnp.sum(a, axis=0, keepdims=True)
    xi = lax.broadcasted_iota(jnp.int32, (NH * SUB, GW), 0)
    xj = lax.broadcasted_iota(jnp.int32, (NH * SUB, GW), 1)
    head_rows = (xi // SUB == xj // HEAD_DIM).astype(F32)
    ct = lax.broadcasted_iota(jnp.int32, (SUB, NH * SUB), 0)
    cj = lax.broadcasted_iota(jnp.int32, (SUB, NH * SUB), 1) & (SUB - 1)
    expand = lambda x: jnp.concatenate([x] * NH, axis=0) * head_rows
    st = st_ref[0]
    for c in range(rw_ref.shape[0] // SUB):
        r0 = c * SUB
        rows = slice(r0, r0 + SUB)
        one = lambda t: slice(r0 + t, r0 + t + 1)
        r, k, v = rw_ref[rows, 0:256], rw_ref[rows, 256:512], rw_ref[rows, 512:768]
        lw, kk, ka = rw_ref[rows, 768:1024], rw_ref[rows, 1024:1280], rw_ref[rows, 1280:1536]
        cs = _dot_hi(tril, lw)
        c_end = cs[SUB - 1:SUB, :]
        e_inv = jnp.exp(-cs)
        kkx = kk * jnp.exp(cs - lw)
        rx = r * jnp.exp(cs)
        kae, ke = ka * e_inv, k * e_inv
        kx_scr[rows, :] = kkx
        maa = jnp.dot(jnp.concatenate([kx_scr[one(t), :] * kae for t in range(SUB)], axis=0).astype(BF16), seg,
                      preferred_element_type=F32)
        v_x = expand(v)
        mk = _dot_nt(jnp.concatenate([kkx, rx], axis=0), expand(ke))
        m_ak = jnp.where(cj < ct, mk[0:SUB], 0.0)
        a_rk = jnp.where(cj <= ct, mk[SUB:2 * SUB], 0.0)
        a_ra = jnp.where(cj <= ct, _dot_nt(rx, expand(kae)), 0.0)
        mv_scr[rows, :] = _dot(m_ak, v_x)
        g_arr = jnp.zeros((SUB, GW), F32)
        u_arr = jnp.zeros((SUB, GW), F32)
        for t in range(SUB):
            m_t = jnp.where(srow < t, maa[t * SUB:(t + 1) * SUB, :], 0.0)
            g_t = kx_scr[one(t), :] - csum(m_t * g_arr)
            u_t = mv_scr[one(t), :] - csum(m_t * u_arr)
            g_arr = jnp.where(srow == t, g_t, g_arr)
            u_arr = jnp.where(srow == t, u_t, u_arr)
        omega = rx - _dot(a_ra, expand(g_arr))
        o_ref[rows, :] = _dot_nt(omega, st) + _dot(a_rk, v_x) - _dot(a_ra, expand(u_arr))
        e_end = jnp.exp(c_end - cs)
        a_hat = ka * e_end
        phi = _dot_tn(g_arr, a_hat) * bd
        psi = _dot_tn(jnp.concatenate([v, -u_arr], axis=0), jnp.concatenate([k * e_end, a_hat], axis=0)) * bd
        st = st * jnp.exp(c_end) - _dot(st, phi) + psi
    st_ref[0] = st


def _rwkv_chunk_body(rw_ref, tril_ref, ones_ref, seg_ref, bd_ref, o_ref, st_ref, kx_scr, mv_scr, g_scr, u_scr):
    @pl.when(pl.program_id(1) == 0)
    def _():
        st_ref[...] = jnp.zeros_like(st_ref)

    yield
    L = rw_ref.shape[0]
    ns = L // SUB
    xr = ns * NH * SUB
    seg = seg_ref[...]
    bd = bd_ref[...]
    srow = lax.broadcasted_iota(jnp.int32, (SUB, GW), 0)
    csum = lambda a: jnp.sum(a, axis=0, keepdims=True)
    xi = lax.broadcasted_iota(jnp.int32, (xr, GW), 0)
    xj = lax.broadcasted_iota(jnp.int32, (xr, GW), 1)
    head_rows = (((xi // SUB) & (NH - 1)) == xj // HEAD_DIM).astype(F32)
    ti = lax.broadcasted_iota(jnp.int32, (L, xr), 0)
    tj = lax.broadcasted_iota(jnp.int32, (L, xr), 1)
    same = (tj // (NH * SUB)) == (ti // SUB)
    lt = same & ((tj & (SUB - 1)) < (ti & (SUB - 1)))
    le = same & ((tj & (SUB - 1)) <= (ti & (SUB - 1)))

    def expand(x):
        parts = []
        for c in range(ns):
            parts += [x[c * SUB:(c + 1) * SUB]] * NH
        return jnp.concatenate(parts, axis=0) * head_rows

    r, k, v = rw_ref[:, 0:256], rw_ref[:, 256:512], rw_ref[:, 512:768]
    lw, kk, ka = rw_ref[:, 768:1024], rw_ref[:, 1024:1280], rw_ref[:, 1280:1536]
    cs = _dot_hi(tril_ref[...], lw)
    c_tot = _dot_hi(ones_ref[...], lw)
    yield
    e_inv = jnp.exp(-cs)
    kkx = kk * jnp.exp(cs - lw)
    rx = r * jnp.exp(cs)
    kae, ke = ka * e_inv, k * e_inv
    e_end = jnp.exp(c_tot - cs)
    a_hat, k_hat = ka * e_end, k * e_end
    p_end = jnp.exp(c_tot)
    kx_scr[...] = kkx
    tiles = [kx_scr[t:t + 1, :] * kae[(t // SUB) * SUB:(t // SUB + 1) * SUB] for t in range(L)]
    maa = jnp.dot(jnp.concatenate(tiles, axis=0).astype(BF16), seg, preferred_element_type=F32)
    v_x = expand(v)
    mk = _dot_nt(jnp.concatenate([kkx, rx], axis=0), expand(ke))
    ra = _dot_nt(rx, expand(kae))
    yield
    m_ak = jnp.where(lt, mk[0:L], 0.0)
    a_rk = jnp.where(le, mk[L:2 * L], 0.0)
    a_ra = jnp.where(le, ra, 0.0)
    mv_scr[...] = _dot(m_ak, v_x)
    rk_v = _dot(a_rk, v_x)
    yield
    for c in range(ns):
        g_arr = jnp.zeros((SUB, GW), F32)
        u_arr = jnp.zeros((SUB, GW), F32)
        for t in range(SUB):
            i = c * SUB + t
            m_t = maa[i * SUB:(i + 1) * SUB, :]
            g_t = kx_scr[i:i + 1, :] - csum(m_t * g_arr)
            u_t = mv_scr[i:i + 1, :] - csum(m_t * u_arr)
            g_arr = jnp.where(srow == t, g_t, g_arr)
            u_arr = jnp.where(srow == t, u_t, u_arr)
        g_scr[c * SUB:(c + 1) * SUB, :] = g_arr
        u_scr[c * SUB:(c + 1) * SUB, :] = u_arr
        if c % 2 == 1:
            yield
    g_all, u_all = g_scr[...], u_scr[...]
    ra_g = _dot(a_ra, expand(g_all))
    ra_u = _dot(a_ra, expand(u_all))
    phis = [_dot_tn(g_all[c * SUB:(c + 1) * SUB], a_hat[c * SUB:(c + 1) * SUB]) for c in range(ns)]
    psis = [_dot_tn(jnp.concatenate([v[c * SUB:(c + 1) * SUB], -u_all[c * SUB:(c + 1) * SUB]], axis=0),
                    jnp.concatenate([k_hat[c * SUB:(c + 1) * SUB], a_hat[c * SUB:(c + 1) * SUB]], axis=0))
            for c in range(ns)]
    yield
    omega = rx - ra_g
    o_ind = rk_v - ra_u
    st = st_ref[0]
    for c in range(ns):
        rows = slice(c * SUB, (c + 1) * SUB)
        o_dot = _dot_nt(omega[rows], st)
        st_phi = _dot(st, phis[c] * bd)
        yield
        o_ref[rows, :] = o_dot + o_ind[rows]
        st = st * p_end[c * SUB:c * SUB + 1, :] - st_phi + psis[c] * bd
    st_ref[0] = st


def _mixers_chunk_kernel(layer, hg_ref, ss_ref, ml_ref, gt_ref, rw_ref, logit_ref, pnh_ref, trilb_ref, onesb_ref,
                         segb_ref, bd_ref, conv_ref, gp_ref, pd_ref, pnm_ref, trilc_ref, ohg_ref, oss_ref, oml_ref,
                         orw_ref, sth_ref, ht_ref, ct_ref, n_ref, m_ref, str_ref, q_scr, b_scr, i_scr, tail_scr, kx_scr,
                         mv_scr, g_scr, u_scr):
    bodies = [
        _rwkv_chunk_body(rw_ref, trilb_ref, onesb_ref, segb_ref, bd_ref, orw_ref, str_ref, kx_scr, mv_scr, g_scr, u_scr),
        _hgrn_chunk_body(layer, hg_ref, logit_ref, pnh_ref, trilb_ref, onesb_ref, segb_ref, bd_ref, ohg_ref, sth_ref,
                         q_scr, b_scr, i_scr),
        _mlstm_chunk_body(ml_ref, gt_ref, gp_ref, pnm_ref, trilc_ref, oml_ref, ct_ref, n_ref, m_ref),
        _ssm_chunk_body(ss_ref, gt_ref, conv_ref, gp_ref, pd_ref, trilc_ref, oss_ref, ht_ref, tail_scr),
    ]
    while bodies:
        for body in list(bodies):
            try:
                next(body)
            except StopIteration:
                bodies.remove(body)


MIXER_STATE_SHAPES = ((GW, GW), (NH, SSM_STATE, HEAD_DIM), (NH, HEAD_DIM, HEAD_DIM), (SUBLANES, HEAD_DIM), (1, LANES),
                      (GW, GW))


def _mixers_chunk(proj, rw_scan, consts, layer, bsz, t):
    nt = t // CHUNK
    m = bsz * t
    rows = lambda w, cb: pl.BlockSpec((CHUNK, w), lambda b, i: (b * nt + i, cb))
    const = lambda a: pl.BlockSpec(a.shape, lambda b, i: (0,) * a.ndim)
    state = lambda s: pl.BlockSpec((1,) + s, lambda b, i: (b,) + (0,) * len(s))
    big = pltpu.VMEM((CHUNK, GW), F32)
    outs = pl.pallas_call(
        functools.partial(_mixers_chunk_kernel, layer),
        grid=(bsz, nt),
        in_specs=[rows(HG_W, HG_OFF // HG_W), rows(SS_W, SS_OFF // SS_W), rows(ML_W, ML_OFF // ML_W),
                  rows(GT_W, GT_OFF // GT_W), rows(1536, 0)] + [const(a) for a in consts],
        out_specs=[rows(GW, 0)] * 4 + [state(s) for s in MIXER_STATE_SHAPES],
        out_shape=[jax.ShapeDtypeStruct((m, GW), F32)] * 4
        + [jax.ShapeDtypeStruct((bsz,) + s, F32) for s in MIXER_STATE_SHAPES],
        scratch_shapes=[big, big, big, pltpu.VMEM((SUBLANES, 768), F32), big, big, big, big],
        compiler_params=_cparams(("parallel", "arbitrary")),
    )(proj, proj, proj, proj, rw_scan, *consts)
    return outs[:4], outs[4:]


def _chunk_call(kernel_fn, proj, col_blocks, consts, state_shapes, scratch, bsz, t):
    nt = t // CHUNK
    m = bsz * t
    rows = lambda w, cb: pl.BlockSpec((CHUNK, w), lambda b, i: (b * nt + i, cb))
    const = lambda a: pl.BlockSpec(a.shape, lambda b, i: (0,) * a.ndim)
    state = lambda s: pl.BlockSpec((1,) + s, lambda b, i: (b,) + (0,) * len(s))
    outs = pl.pallas_call(
        kernel_fn,
        grid=(bsz, nt),
        in_specs=[rows(w, cb) for w, cb in col_blocks] + [const(a) for a in consts],
        out_specs=[rows(GW, 0)] + [state(s) for s in state_shapes],
        out_shape=[jax.ShapeDtypeStruct((m, GW), F32)] + [jax.ShapeDtypeStruct((bsz,) + s, F32) for s in state_shapes],
        scratch_shapes=scratch,
        compiler_params=_cparams(("parallel", "arbitrary")),
    )(*([proj] * len(col_blocks)), *consts)
    return outs[0], outs[1:]


def _scan_loop(tb, body):
    if tb == 1:
        body(0, 0)
    else:
        lax.fori_loop(0, tb, body, 0)


def _rwkv_scan_kernel(tb, vh, kk_ref, w_ref, ka_ref, k_ref, r_ref, v_ref, s0_ref, o_ref, s_ref):
    @pl.when(pl.program_id(1) == 0)
    def _():
        s_ref[...] = s0_ref[...]

    def body(t, c):
        kk, w, ka, k, r = kk_ref[t], w_ref[t], ka_ref[t], k_ref[t], r_ref[t]
        for i in range(vh):
            s = s_ref[i]
            sa = jnp.sum(s * kk, axis=0, keepdims=True)
            s = s * w - sa * ka + v_ref[t, i:i + 1, :] * k
            o_ref[t, i:i + 1, :] = jnp.sum(s * r, axis=0, keepdims=True)
            s_ref[i] = s
        return c

    _scan_loop(tb, body)


def _hgrn_scan_kernel(tb, vh, f_ref, k_ref, q_ref, v_ref, s0_ref, o_ref, s_ref):
    @pl.when(pl.program_id(1) == 0)
    def _():
        s_ref[...] = s0_ref[...]

    def body(t, c):
        f, k, q = f_ref[t], k_ref[t], q_ref[t]
        for i in range(vh):
            s = s_ref[i] * f + v_ref[t, i:i + 1, :] * k
            o_ref[t, i:i + 1, :] = jnp.sum(s * q, axis=0, keepdims=True)
            s_ref[i] = s
        return c

    _scan_loop(tb, body)


def _ssm_scan_kernel(tb, vh, b_ref, c_ref, dec_ref, x_ref, s0_ref, o_ref, s_ref):
    @pl.when(pl.program_id(1) == 0)
    def _():
        s_ref[...] = s0_ref[...]

    def body(t, c):
        bn, cn, dec = b_ref[t], c_ref[t], dec_ref[t]
        for i in range(vh):
            s = s_ref[i] * dec + x_ref[t, i:i + 1, :] * bn
            o_ref[t, i:i + 1, :] = jnp.sum(s * cn, axis=0, keepdims=True)
            s_ref[i] = s
        return c

    _scan_loop(tb, body)


def _mlstm_scan_kernel(tb, vh, q_ref, k_ref, li_ref, lf_ref, v_ref, c0_ref, n0_ref, m0_ref, o_ref, c_ref, n_ref, m_ref):
    @pl.when(pl.program_id(1) == 0)
    def _():
        c_ref[...] = c0_ref[...]
        n_ref[...] = n0_ref[...]
        m_ref[...] = m0_ref[...]

    def body(t, c):
        q, k, li, lf = q_ref[t], k_ref[t], li_ref[t], lf_ref[t]
        m_old = m_ref[...]
        m_new = jnp.maximum(lf + m_old, li)
        gp = jnp.exp(lf + m_old - m_new)
        gs = jnp.exp(li - m_new)
        n = gp * n_ref[...] + gs * k
        den = jnp.sum(n * q, axis=0, keepdims=True)
        scale = 1.0 / jnp.maximum(jnp.abs(den), jnp.exp(-m_new))
        gk = gs * k
        for i in range(vh):
            s = c_ref[i] * gp + v_ref[t, i:i + 1, :] * gk
            o_ref[t, i:i + 1, :] = jnp.sum(s * q, axis=0, keepdims=True) * scale
            c_ref[i] = s
        n_ref[...] = n
        m_ref[...] = m_new
        return c

    _scan_loop(tb, body)


def _scan_call(kernel_fn, seq_ins, state_ins, t, vh, lanes, tb):
    nl = lanes // LANES
    seq_spec = lambda a: pl.BlockSpec((tb, a.shape[1], LANES), lambda j, i: (i, 0, j))

    def state_spec(a):
        nd = a.ndim
        return pl.BlockSpec(a.shape[:-1] + (LANES,), lambda j, i: (0,) * (nd - 1) + (j,))

    o_shape = jax.ShapeDtypeStruct((t, vh, lanes), F32)
    outs = pl.pallas_call(
        functools.partial(kernel_fn, tb, vh),
        grid=(nl, t // tb),
        in_specs=[seq_spec(a) for a in seq_ins] + [state_spec(a) for a in state_ins],
        out_specs=[pl.BlockSpec((tb, vh, LANES), lambda j, i: (i, 0, j))] + [state_spec(a) for a in state_ins],
        out_shape=[o_shape] + [jax.ShapeDtypeStruct(a.shape, F32) for a in state_ins],
        compiler_params=_cparams(("parallel", "arbitrary")),
    )(*seq_ins, *state_ins)
    return outs[0], outs[1:]


class _Lay:
    def __init__(self, bsz, t):
        self.b, self.t = bsz, t
        bh = bsz * NH
        self.vl = max(LANES // bh, 1)
        self.lanes = self.vl * bh
        self.tb = 32 if t % 32 == 0 else t

    def kop(self, x, k=HEAD_DIM):
        x = x.reshape(self.b, self.t, NH, k).transpose(1, 3, 0, 2).reshape(self.t, k, self.b * NH)
        return jnp.tile(x, (1, 1, self.vl))

    def sop(self, x):
        x = x.reshape(self.b, self.t, NH, HEAD_DIM)[..., 0].transpose(1, 0, 2).reshape(self.t, 1, self.b * NH)
        return jnp.tile(x, (1, 1, self.vl))

    def vop(self, x, v=HEAD_DIM):
        x = x.reshape(self.b, self.t, NH, self.vl, v // self.vl).transpose(1, 4, 3, 0, 2)
        return x.reshape(self.t, v // self.vl, self.lanes)

    def unvop(self, o, v=HEAD_DIM):
        o = o.reshape(self.t, v // self.vl, self.vl, self.b, NH).transpose(3, 0, 4, 2, 1)
        return o.reshape(self.b * self.t, NH * v)

    def state(self, s):
        v, k = s.shape[2], s.shape[3]
        s = s.reshape(self.b, NH, self.vl, v // self.vl, k).transpose(3, 4, 2, 0, 1)
        return s.reshape(v // self.vl, k, self.lanes)

    def unstate(self, s):
        vh, k = s.shape[0], s.shape[1]
        s = s.reshape(vh, k, self.vl, self.b, NH).transpose(3, 4, 2, 0, 1)
        return s.reshape(self.b, NH, vh * self.vl, k)

    def kstate(self, n):
        k = n.shape[2]
        return jnp.tile(n.transpose(2, 0, 1).reshape(k, self.b * NH), (1, self.vl))

    def unkstate(self, n):
        return n[:, :self.b * NH].reshape(n.shape[0], self.b, NH).transpose(1, 2, 0)


def _rwkv_scan(lay, rw, s0):
    cols = lambda i: rw[:, i * GW:(i + 1) * GW]
    o, (s,) = _scan_call(
        _rwkv_scan_kernel,
        [lay.kop(cols(4)), lay.kop(cols(3)), lay.kop(cols(5)), lay.kop(cols(1)), lay.kop(cols(0)), lay.vop(cols(2))],
        [s0], lay.t, HEAD_DIM // lay.vl, lay.lanes, lay.tb)
    return lay.unvop(o), s


def _segmean(a, s, n):
    return _dot_hi(a, s) * (1.0 / n)


def _rwkv_post(o, rwp_ref, pq, s64):
    d = o - _segmean(o, s64, HEAD_DIM)
    var = _segmean(d * d, s64, HEAD_DIM)
    return (d * lax.rsqrt(var + RWKV_GN_EPS) * pq[0:1] + pq[1:2] + rwp_ref[:, 0:256]) * rwp_ref[:, 256:512]


def _out_proj(x, parts, w_ref):
    acc = x
    for i, part in enumerate(parts):
        acc = acc + jnp.dot(part.astype(BF16), w_ref[i * GW:(i + 1) * GW, :], preferred_element_type=F32)
    return acc


def _post_step_kernel(x_ref, orw_ref, ohg_ref, oss_ref, oml_ref, rwp_ref, hgp_ref, ssp_ref, mlp_ref, pq_ref, seg64_ref,
                      seg128_ref, w_ref, o_ref):
    pq = pq_ref[...]
    s64 = seg64_ref[...]
    m_r = _rwkv_post(orw_ref[...], rwp_ref, pq, s64)
    o = ohg_ref[...]
    m_h = o * lax.rsqrt(_segmean(o * o, s64, HEAD_DIM) + NORM_EPS) * pq[2:3] * hgp_ref[...]
    y = (oss_ref[...] + ssp_ref[:, 0:256]) * ssp_ref[:, 256:512]
    m_s = y * lax.rsqrt(_segmean(y * y, seg128_ref[...], 128) + NORM_EPS) * pq[3:4]
    o = oml_ref[...]
    m_m = o * lax.rsqrt(_segmean(o * o, s64, HEAD_DIM) + NORM_EPS) * pq[4:5] * mlp_ref[...]
    o_ref[...] = _out_proj(x_ref[...], (m_r, m_h, m_s, m_m), w_ref)


def _post_seq_kernel(x_ref, orw_ref, rwp_ref, mh_ref, ms_ref, mm_ref, pq_ref, seg64_ref, w_ref, o_ref):
    m_r = _rwkv_post(orw_ref[...], rwp_ref, pq_ref[...], seg64_ref[...])
    o_ref[...] = _out_proj(x_ref[...], (m_r, mh_ref[...], ms_ref[...], mm_ref[...]), w_ref)


def _post_call(kernel_fn, x, row_ins, consts):
    m = x.shape[0]
    tm = min(m, 512)
    rows = lambda a: pl.BlockSpec((tm, a.shape[1]), lambda i: (i, 0))
    const = lambda a: pl.BlockSpec(a.shape, lambda i: (0, 0))
    return pl.pallas_call(
        kernel_fn,
        grid=(m // tm,),
        in_specs=[rows(a) for a in (x,) + tuple(row_ins)] + [const(a) for a in consts],
        out_specs=rows(x),
        out_shape=jax.ShapeDtypeStruct((m, D_MODEL), F32),
        compiler_params=_cparams(("parallel",)),
    )(x, *row_ins, *consts)


def _xattn_seq_kernel(q_ref, k_ref, v_ref, x_ref, wo_ref, o_ref):
    acc = x_ref[...]
    for h in range(XH):
        sl = slice(h * XHD, (h + 1) * XHD)
        s = _dot_nt(q_ref[:, sl], k_ref[:, sl]) * (XHD ** -0.5)
        e = jnp.exp(s - jnp.max(s, axis=-1, keepdims=True))
        pr = e / jnp.sum(e, axis=-1, keepdims=True)
        acc = acc + _dot(_dot(pr, v_ref[:, sl]), wo_ref[sl, :])
    o_ref[...] = acc


def _xattn_seq(q, kv, x, wo, bsz, t):
    tq = 512
    nt = t // tq
    return pl.pallas_call(
        _xattn_seq_kernel,
        grid=(bsz, nt),
        in_specs=[pl.BlockSpec((tq, D_MODEL), lambda b, i: (b * nt + i, 0)),
                  pl.BlockSpec((N_MEM, D_MODEL), lambda b, i: (b, 0)),
                  pl.BlockSpec((N_MEM, D_MODEL), lambda b, i: (b, 1)),
                  pl.BlockSpec((tq, D_MODEL), lambda b, i: (b * nt + i, 0)),
                  pl.BlockSpec((D_MODEL, D_MODEL), lambda b, i: (0, 0))],
        out_specs=pl.BlockSpec((tq, D_MODEL), lambda b, i: (b * nt + i, 0)),
        out_shape=jax.ShapeDtypeStruct((bsz * t, D_MODEL), F32),
        compiler_params=_cparams(("parallel", "parallel")),
    )(q, kv, kv, x, wo)


def _xattn_step_kernel(bb, q_ref, k_ref, v_ref, o_ref):
    for i in range(bb):
        t = k_ref[i] * q_ref[i]
        s = jnp.sum(t, axis=-1, keepdims=True) * (XHD ** -0.5)
        e = jnp.exp(s - jnp.max(s, axis=0, keepdims=True))
        pr = e / jnp.sum(e, axis=0, keepdims=True)
        o_ref[i] = jnp.sum(pr * v_ref[i], axis=0)


def _xattn_step(q, cache_k, cache_v, layer):
    bsz = q.shape[0]
    bb = 4
    cache_spec = pl.BlockSpec((None, bb, N_MEM, XH, XHD), lambda i: (layer, i, 0, 0, 0))
    return pl.pallas_call(
        functools.partial(_xattn_step_kernel, bb),
        grid=(bsz // bb,),
        in_specs=[pl.BlockSpec((bb, XH, XHD), lambda i: (i, 0, 0)), cache_spec, cache_spec],
        out_specs=pl.BlockSpec((bb, XH, XHD), lambda i: (i, 0, 0)),
        out_shape=jax.ShapeDtypeStruct((bsz, XH, XHD), F32),
        compiler_params=_cparams(("parallel",)),
    )(q.reshape(bsz, XH, XHD), cache_k, cache_v).reshape(bsz, D_MODEL)


def _step_kernel(bb, rw_ref, hg_ref, ss_ref, ml_ref, wkv_ref, hgs_ref, ssm_ref, c_ref, n_ref, m_ref,
                 orw_ref, ohg_ref, oss_ref, oml_ref, wkv_o, hgs_o, ssm_o, c_o, n_o, m_o):
    ri = lax.broadcasted_iota(jnp.int32, (HEAD_DIM, HEAD_DIM), 0)
    ci = lax.broadcasted_iota(jnp.int32, (HEAD_DIM, HEAD_DIM), 1)
    eye = (ri == ci).astype(F32)
    wi = lax.broadcasted_iota(jnp.int32, (HEAD_DIM, GW), 0)
    wj = lax.broadcasted_iota(jnp.int32, (HEAD_DIM, GW), 1)
    col = lambda r: jnp.sum(eye * r, axis=-1, keepdims=True)
    lsum = lambda a: jnp.sum(a, axis=-1, keepdims=True)
    lane4 = lax.broadcasted_iota(jnp.int32, (1, NH), 1)

    def place(c, h):
        return jnp.sum(jnp.where(wj == wi + h * HEAD_DIM, c, 0.0), axis=0, keepdims=True)

    def body(b, carry):
        rb = slice(b, b + 1)
        rw, hg, ss, ml = rw_ref[rb, :], hg_ref[rb, :], ss_ref[rb, :], ml_ref[rb, :]
        m_row = m_ref[rb, :]
        m_new_row = m_row
        o_rw = o_hg = o_ss = o_ml = jnp.zeros((1, GW), F32)
        for h in range(NH):
            hs = lambda base: slice(base + h * HEAD_DIM, base + (h + 1) * HEAD_DIM)
            one = lambda base: slice(base + h * HEAD_DIM, base + h * HEAD_DIM + 1)
            r, k, v, w, kk, ka = (rw[:, hs(i * GW)] for i in range(6))
            s = wkv_ref[b, h]
            s = s * w - lsum(s * kk) * ka + col(v) * k
            o_rw = o_rw + place(lsum(s * r), h)
            wkv_o[b, h] = s
            q, f, kh, vh = (hg[:, hs(i * GW)] for i in range(4))
            s = hgs_ref[b, h] * col(f) + col(kh) * vh
            o_hg = o_hg + place(col(jnp.sum(s * col(q), axis=0, keepdims=True)), h)
            hgs_o[b, h] = s
            g = h // (NH // 2)
            bn = ss[:, 512 + SSM_STATE * g:512 + SSM_STATE * (g + 1)]
            cn = ss[:, 768 + SSM_STATE * g:768 + SSM_STATE * (g + 1)]
            s = ssm_ref[b, h] * ss[:, one(GW)] + col(ss[:, hs(0)]) * bn
            o_ss = o_ss + place(lsum(s * cn), h)
            ssm_o[b, h] = s
            q, k, v = (ml[:, hs(i * GW)] for i in range(3))
            li, lf = ml[:, one(3 * GW)], ml[:, one(4 * GW)]
            m_old = m_row[:, h:h + 1]
            m_new = jnp.maximum(lf + m_old, li)
            gp = jnp.exp(lf + m_old - m_new)
            gs = jnp.exp(li - m_new)
            n = gp * n_ref[b, h:h + 1, :] + gs * k
            c = c_ref[b, h] * gp + col(gs * v) * k
            o_ml = o_ml + place(lsum(c * q) / jnp.maximum(jnp.abs(lsum(n * q)), jnp.exp(-m_new)), h)
            c_o[b, h] = c
            n_o[b, h:h + 1, :] = n
            m_new_row = jnp.where(lane4 == h, m_new, m_new_row)
        orw_ref[rb, :] = o_rw
        ohg_ref[rb, :] = o_hg
        oss_ref[rb, :] = o_ss
        oml_ref[rb, :] = o_ml
        m_o[rb, :] = m_new_row
        return carry

    for b in range(bb):
        body(b, 0)


def _stepv_kernel(rw_ref, hg_ref, xdt_ref, bn_ref, cn_ref, sc_ref, ml_ref, wkv_ref, hgs_ref, ssm_ref, c_ref, n_ref, m_ref,
                  orw_ref, ohg_ref, oss_ref, oml_ref, wkv_o, hgs_o, ssm_o, c_o, n_o, m_o):
    ri = lax.broadcasted_iota(jnp.int32, (HEAD_DIM, HEAD_DIM), 0)
    ci = lax.broadcasted_iota(jnp.int32, (HEAD_DIM, HEAD_DIM), 1)
    eye = (ri == ci).astype(F32)
    col = lambda r: jnp.sum(eye * r, axis=-1, keepdims=True)
    row = lambda c: jnp.sum(eye * c, axis=-2, keepdims=True)
    lsum = lambda a: jnp.sum(a, axis=-1, keepdims=True)
    r, k, v, w, kk, ka = (rw_ref[:, i] for i in range(6))
    s = wkv_ref[...]
    s = s * w - lsum(s * kk) * ka + col(v) * k
    orw_ref[...] = row(lsum(s * r))
    wkv_o[...] = s
    q, f, kh, vh = (hg_ref[:, i] for i in range(4))
    s = hgs_ref[...] * col(f) + col(kh) * vh
    ohg_ref[...] = jnp.sum(s * col(q), axis=-2, keepdims=True)
    hgs_o[...] = s
    dec, li, lf = (sc_ref[:, i] for i in range(3))
    s = ssm_ref[...] * dec + col(xdt_ref[...]) * bn_ref[...]
    oss_ref[...] = row(lsum(s * cn_ref[...]))
    ssm_o[...] = s
    q, k, v = (ml_ref[:, i] for i in range(3))
    m_old = m_ref[...]
    m_new = jnp.maximum(lf + m_old, li)
    gp = jnp.exp(lf + m_old - m_new)
    gs = jnp.exp(li - m_new)
    n = gp * n_ref[...] + gs * k
    c = c_ref[...] * gp + col(gs * v) * k
    oml_ref[...] = row(lsum(c * q) / jnp.maximum(jnp.abs(lsum(n * q)), jnp.exp(-m_new)))
    c_o[...] = c
    n_o[...] = n
    m_o[...] = m_new


def _mixers_stepv(scan, states, layer):
    rw, hg, ss, ml = scan
    wkv, hgs, ssm, cst, nst, mst = states
    bsz = rw.shape[0]
    bb = SUBLANES
    heads = lambda a, nf: a.reshape(bsz, nf, NH, 1, HEAD_DIM)
    per_head = lambda a: a.reshape(bsz, NH, HEAD_DIM)[:, :, 0].reshape(bsz, NH, 1, 1)
    groups = lambda a: jnp.repeat(a.reshape(bsz, 2, 1, SSM_STATE), NH // 2, axis=1)
    seq_ins = (heads(rw, 6), heads(hg, 4), heads(ss[:, 0:GW], 1)[:, 0], groups(ss[:, 512:768]), groups(ss[:, 768:1024]),
               jnp.stack([per_head(ss[:, GW:2 * GW]), per_head(ml[:, 3 * GW:4 * GW]), per_head(ml[:, 4 * GW:5 * GW])],
                         axis=1),
               heads(ml[:, 0:3 * GW], 3))
    st_ins = (wkv, hgs, ssm, cst, nst.reshape(DEPTH, bsz, NH, 1, HEAD_DIM), mst.reshape(DEPTH, bsz, NH, 1, 1))
    blk = lambda a: pl.BlockSpec((bb,) + a.shape[1:], lambda i: (i,) + (0,) * (a.ndim - 1))
    st_in = lambda a: pl.BlockSpec((None, bb) + a.shape[2:], lambda i: (layer, i) + (0,) * (a.ndim - 2))
    st_out = lambda a: pl.BlockSpec((bb,) + a.shape[2:], lambda i: (i,) + (0,) * (a.ndim - 2))
    o_shape = jax.ShapeDtypeStruct((bsz, NH, 1, HEAD_DIM), F32)
    o_spec = pl.BlockSpec((bb, NH, 1, HEAD_DIM), lambda i: (i, 0, 0, 0))
    outs = pl.pallas_call(
        _stepv_kernel,
        grid=(bsz // bb,),
        in_specs=[blk(a) for a in seq_ins] + [st_in(a) for a in st_ins],
        out_specs=[o_spec] * 4 + [st_out(a) for a in st_ins],
        out_shape=[o_shape] * 4 + [jax.ShapeDtypeStruct(a.shape[1:], F32) for a in st_ins],
        compiler_params=_cparams(("parallel",)),
    )(*seq_ins, *st_ins)
    o_parts = [o.reshape(bsz, GW) for o in outs[:4]]
    wkv_n, hgs_n, ssm_n, c_n, n_n, m_n = outs[4:]
    return o_parts, (wkv_n, hgs_n, ssm_n, c_n, n_n.reshape(bsz, NH, HEAD_DIM), m_n.reshape(bsz, NH))


def _mixers_step(scan, states, layer):
    bsz = scan[0].shape[0]
    bb = SUBLANES
    rows = lambda a: pl.BlockSpec((bb, a.shape[1]), lambda i: (i, 0))
    st_in = lambda a: pl.BlockSpec((None, bb) + a.shape[2:], lambda i: (layer, i) + (0,) * (a.ndim - 2))
    st_out = lambda a: pl.BlockSpec((bb,) + a.shape[2:], lambda i: (i,) + (0,) * (a.ndim - 2))
    o_shape = jax.ShapeDtypeStruct((bsz, GW), F32)
    outs = pl.pallas_call(
        functools.partial(_step_kernel, bb),
        grid=(bsz // bb,),
        in_specs=[rows(a) for a in scan] + [st_in(a) for a in states],
        out_specs=[pl.BlockSpec((bb, GW), lambda i: (i, 0))] * 4 + [st_out(a) for a in states],
        out_shape=[o_shape] * 4 + [jax.ShapeDtypeStruct(a.shape[1:], F32) for a in states],
        compiler_params=_cparams(("parallel",)),
    )(*scan, *states)
    return outs[:4], outs[4:]


def kernel(x_prompt, x_sample, state_rwkv_wkv, state_rwkv_shift, state_hgrn, state_ssm, state_ssm_conv, state_mlstm_C, state_mlstm_n, state_mlstm_m, cache_mem_k, cache_mem_v, mem_prompt, norm_mix_g, w_in, w_out, rwkv_mu, rwkv_w0, rwkv_w_up, rwkv_a0, rwkv_a_up, rwkv_g_up, rwkv_k_k, rwkv_k_a, rwkv_r_k, rwkv_ln_g, rwkv_ln_b, hgrn_lb_logits, hgrn_norm_g, ssm_conv_w, ssm_conv_b, ssm_dt_bias, ssm_A_log, ssm_D, ssm_norm_g, mlstm_i_bias, mlstm_f_bias, mlstm_norm_g, norm_x_g, norm_mem_g, xattn_wq, xattn_wk, xattn_wv, xattn_wo, norm_ff_g, ff_w1, ff_w2, final_norm_g):
    bp, tp, _ = x_prompt.shape
    bs = x_sample.shape[0]
    seg64, seg128 = _seg_matrix(HEAD_DIM), _seg_matrix(128)
    seg64_b = seg64.astype(BF16)
    gate_ex = _gate_expand()
    tril_c = jnp.asarray(np.tril(np.ones((CHUNK, CHUNK), np.float32)))
    tril_s = jnp.asarray(np.tril(np.ones((SUB, SUB), np.float32)))
    blk = np.arange(CHUNK) // SUB
    ones_blk = jnp.asarray((blk[:, None] == blk[None, :]).astype(np.float32))
    tril_blk = ones_blk * tril_c
    rep = lambda a: jnp.repeat(a, HEAD_DIM)
    pad_rows = lambda a, n: jnp.pad(a, ((0, n - a.shape[0]), (0, 0)))
    zeros = lambda *s: jnp.zeros(s, F32)

    hp = x_prompt.reshape(bp * tp, D_MODEL)
    hs = x_sample.reshape(bs, D_MODEL)
    mem = mem_prompt.reshape(bp * N_MEM, D_MODEL)
    p_states, s_states, p_mem = [], [], []

    for l in range(DEPTH):
        w_in_re = _gather_cols(w_in[l], _IN_IDX).astype(BF16)
        pv = pad_rows(jnp.stack([rwkv_w0[l], rwkv_a0[l], rwkv_k_k[l], rwkv_k_a[l], rwkv_r_k[l].reshape(GW),
                                 rep(ssm_dt_bias[l]), rep(ssm_A_log[l]), rep(ssm_D[l]), rep(mlstm_i_bias[l]),
                                 rep(mlstm_f_bias[l])]), 16)
        lora = jnp.stack([pad_rows(rwkv_w_up[l], LANES), pad_rows(rwkv_a_up[l], LANES), pad_rows(rwkv_g_up[l], LANES)])
        conv = pad_rows(jnp.concatenate([ssm_conv_w[l], ssm_conv_b[l][None]], axis=0), 8)
        mu = _gather_cols(rwkv_mu[l], _RW_IDX).reshape(1, RW_W)
        gp = pad_rows(jnp.stack([_gate_row(ssm_dt_bias[l], mlstm_i_bias[l], mlstm_f_bias[l]),
                                 _gate_row(ssm_A_log[l], zeros(NH), zeros(NH))]), 8)
        pd = pad_rows(jnp.stack([rep(ssm_D[l]), ssm_norm_g[l]]), 8)
        pn_h = pad_rows(hgrn_norm_g[l][None], 8)
        pn_m = pad_rows(mlstm_norm_g[l][None], 8)
        pq = pad_rows(jnp.stack([rwkv_ln_g[l], rwkv_ln_b[l], hgrn_norm_g[l], ssm_norm_g[l], mlstm_norm_g[l]]), 8)
        w_out_b = w_out[l].astype(BF16)
        wq_b, wo_b = xattn_wq[l].astype(BF16), xattn_wo[l].astype(BF16)
        wkv_b = jnp.concatenate([xattn_wk[l], xattn_wv[l]], axis=1).astype(BF16)
        w1_b, w2_b = ff_w1[l].astype(BF16), ff_w2[l].astype(BF16)

        kv = _norm_matmul(mem, norm_mem_g[l], wkv_b)
        p_mem.append((kv[:, :D_MODEL].reshape(bp, N_MEM, XH, XHD), kv[:, D_MODEL:].reshape(bp, N_MEM, XH, XHD)))
        proj = _norm_matmul(hp, norm_mix_g[l], w_in_re)
        rw_scan, rw_post = _prep_rw_seq(proj, zeros(bp, SUBLANES, RW_W), mu, pv, lora, seg64, bp, tp)
        (m_h, m_s, m_m, o_rw), (st_h, st_s, st_c, st_n, st_m, st_r) = _mixers_chunk(
            proj, rw_scan, (hgrn_lb_logits, pn_h, tril_blk, ones_blk, seg64_b, seg64, conv, gp, pd, pn_m, tril_c),
            l, bp, tp)
        proj3 = proj.reshape(bp, tp, IN_RE)
        shift_new = jnp.take(proj3[:, -1, RW_OFF:RW_OFF + RW_W], jnp.asarray(_RW_INV), axis=-1)
        conv_new = proj3[:, tp - 3:, SS_OFF + 256:SS_OFF + 1024]
        diag_blocks = lambda s: jnp.stack(
            [s.reshape(bp, NH, HEAD_DIM, NH, HEAD_DIM)[:, h, :, h, :] for h in range(NH)], axis=1)
        hgrn_new = diag_blocks(st_h).transpose(0, 1, 3, 2)
        p_states.append((diag_blocks(st_r), shift_new, hgrn_new, st_s.transpose(0, 1, 3, 2), conv_new,
                         st_c.transpose(0, 1, 3, 2), st_n[:, :NH], st_m[:, 0, :NH]))
        hp = _post_call(_post_seq_kernel, hp, (o_rw, rw_post, m_h, m_s, m_m), (pq, seg64, w_out_b))
        q = _norm_matmul(hp, norm_x_g[l], wq_b)
        hp = _xattn_seq(q, kv, hp, wo_b, bp, tp)
        hp = _ffn(hp, norm_ff_g[l], w1_b, w2_b)

        proj = _norm_matmul(hs, norm_mix_g[l], w_in_re)
        prev_rw = _gather_cols(state_rwkv_shift[l], _RW_IDX)
        cb = state_ssm_conv[l]
        prep = _prep_step(proj, prev_rw, cb[:, 0], cb[:, 1], cb[:, 2],
                          (mu, pv, lora, seg64, hgrn_lb_logits, conv, gate_ex), l)
        o_parts, (wkv_n, hg_n, ssm_n, c_n, n_n, m_n) = _mixers_stepv(
            prep[0::2], (state_rwkv_wkv, state_hgrn, state_ssm, state_mlstm_C, state_mlstm_n, state_mlstm_m), l)
        shift_new = jnp.take(proj[:, RW_OFF:RW_OFF + RW_W], jnp.asarray(_RW_INV), axis=-1)
        conv_new = jnp.concatenate([cb[:, 1:], proj[:, None, SS_OFF + 256:SS_OFF + 1024]], axis=1)
        s_states.append((wkv_n, shift_new, hg_n, ssm_n, conv_new, c_n, n_n, m_n))
        hs = _post_call(_post_step_kernel, hs, tuple(o_parts) + tuple(prep[1::2]), (pq, seg64, seg128, w_out_b))
        q = _norm_matmul(hs, norm_x_g[l], wq_b)
        att = _xattn_step(q, cache_mem_k, cache_mem_v, l)
        hs = _mm_res(att, wo_b, hs)
        hs = _ffn(hs, norm_ff_g[l], w1_b, w2_b)

    y_prompt = _rmsnorm(hp, final_norm_g).reshape(bp, tp, D_MODEL)
    y_sample = _rmsnorm(hs, final_norm_g).reshape(bs, 1, D_MODEL)
    p_out = [jnp.stack(s) for s in zip(*p_states)]
    s_out = [jnp.stack(s) for s in zip(*s_states)]
    p_mem_k = jnp.stack([kv[0] for kv in p_mem])
    p_mem_v = jnp.stack([kv[1] for kv in p_mem])
    return (y_prompt, y_sample, *p_out, p_mem_k, p_mem_v, *s_out)
```

```python
import functools

import numpy as np
import jax
import jax.numpy as jnp
from jax import lax
from jax.experimental import pallas as pl
from jax.experimental.pallas import tpu as pltpu

F32 = jnp.float32
BF16 = jnp.bfloat16
HI = lax.Precision.HIGHEST

D_MODEL = 1024
DEPTH = 2
NH = 4
HEAD_DIM = 64
GW = 256
SSM_STATE = 128
N_MEM = 256
XH = 4
XHD = 256
NORM_EPS = 1e-6
RWKV_GN_EPS = 64e-5
LANES = 128
SUBLANES = 8
VMEM_LIMIT = 48 * 1024 * 1024
CHUNK = 128
SUB = 16
HGRN_SUB = 16

RW_OFF, RW_W = 0, 896
HG_OFF = 896
SS_OFF = 1920
G1_OFF = 2944
ML_OFF = 3072
G2_OFF = 3840
MO_OFF = 3968
IN_RE = 4224
GT_DT, GT_I, GT_F = 0, 0, 4
LORA_ROWS = (0, 32, 64)


def _in_weight(w):
    z = lambda n: jnp.zeros((w.shape[0], n), w.dtype)
    return jnp.concatenate([w[:, :2948], z(124), w[:, 2948:3724], z(120), w[:, 3724:]], axis=1).astype(BF16)


def _seg_matrix(width):
    i = np.arange(GW)
    return jnp.asarray((i[:, None] // width == i[None, :] // width).astype(np.float32))


def _gate_expand():
    e = np.zeros((3, LANES, GW), np.float32)
    for s, off in enumerate((GT_DT, GT_I, GT_F)):
        for h in range(NH):
            e[s, off + h, h * HEAD_DIM:(h + 1) * HEAD_DIM] = 1.0
    return jnp.asarray(e)


def _lane_row(*pieces):
    z = jnp.zeros((LANES,), F32)
    for off, val in pieces:
        z = z.at[off:off + val.shape[0]].set(val)
    return z


def _pad_rows(a, n):
    return jnp.pad(a, ((0, n - a.shape[0]), (0, 0)))


def _cparams(sem):
    return pltpu.CompilerParams(dimension_semantics=sem, vmem_limit_bytes=VMEM_LIMIT)


def _softplus(x):
    return jnp.maximum(x, 0.0) + jnp.log(1.0 + jnp.exp(-jnp.abs(x)))


def _sigmoid(x):
    return 1.0 / (1.0 + jnp.exp(-x))


def _rms(x, g):
    ms = jnp.mean(x * x, axis=-1, keepdims=True)
    return x * lax.rsqrt(ms + NORM_EPS) * g


def _dot(a, b):
    return jnp.dot(a.astype(BF16), b.astype(BF16), preferred_element_type=F32)


def _dot_nt(a, b):
    return lax.dot_general(a.astype(BF16), b.astype(BF16), (((1,), (1,)), ((), ())), preferred_element_type=F32)


def _dot_tn(a, b):
    return lax.dot_general(a.astype(BF16), b.astype(BF16), (((0,), (0,)), ((), ())), preferred_element_type=F32)


def _dot_hi(a, b):
    return jnp.dot(a, b, precision=HI, preferred_element_type=F32)


def _norm_matmul_kernel(x_ref, g_ref, w_ref, o_ref, xn_ref):
    @pl.when(pl.program_id(1) == 0)
    def _():
        xn_ref[...] = _rms(x_ref[...], g_ref[...]).astype(BF16)

    o_ref[...] = jnp.dot(xn_ref[...], w_ref[...], preferred_element_type=F32)


def _row_tile(m):
    return min(m, 1024)


def _norm_matmul(x, g, w):
    m, k = x.shape
    n = w.shape[1]
    tm = _row_tile(m)
    tn = next(c for c in (1536, 1408, 1024, 512, n) if n % c == 0)
    return pl.pallas_call(
        _norm_matmul_kernel,
        grid=(m // tm, n // tn),
        in_specs=[pl.BlockSpec((tm, k), lambda i, j: (i, 0)),
                  pl.BlockSpec((1, k), lambda i, j: (0, 0)),
                  pl.BlockSpec((k, tn), lambda i, j: (0, j))],
        out_specs=pl.BlockSpec((tm, tn), lambda i, j: (i, j)),
        out_shape=jax.ShapeDtypeStruct((m, n), F32),
        scratch_shapes=[pltpu.VMEM((tm, k), BF16)],
        compiler_params=_cparams(("parallel", "arbitrary")),
    )(x, g.reshape(1, k), w)


def _mm_res_kernel(a_ref, w_ref, res_ref, o_ref):
    o_ref[...] = res_ref[...] + jnp.dot(a_ref[...].astype(BF16), w_ref[...], preferred_element_type=F32)


def _mm_res(a, w, res):
    m, k = a.shape
    n = w.shape[1]
    tm = min(m, 512)
    return pl.pallas_call(
        _mm_res_kernel,
        grid=(m // tm,),
        in_specs=[pl.BlockSpec((tm, k), lambda i: (i, 0)),
                  pl.BlockSpec((k, n), lambda i: (0, 0)),
                  pl.BlockSpec((tm, n), lambda i: (i, 0))],
        out_specs=pl.BlockSpec((tm, n), lambda i: (i, 0)),
        out_shape=jax.ShapeDtypeStruct((m, n), F32),
        compiler_params=_cparams(("parallel",)),
    )(a, w, res)


def _ffn_kernel(final, x_ref, g_ref, w1_ref, w2_ref, fg_ref, o_ref, xn_ref, acc_ref):
    j = pl.program_id(1)

    @pl.when(j == 0)
    def _():
        xn_ref[...] = _rms(x_ref[...], g_ref[...]).astype(BF16)
        acc_ref[...] = jnp.zeros_like(acc_ref)

    h = jnp.dot(xn_ref[...], w1_ref[...], preferred_element_type=F32)
    h = jnp.square(jnp.maximum(h, 0.0))
    acc_ref[...] += jnp.dot(h.astype(BF16), w2_ref[...], preferred_element_type=F32)

    @pl.when(j == pl.num_programs(1) - 1)
    def _():
        y = x_ref[...] + acc_ref[...]
        o_ref[...] = _rms(y, fg_ref[...]) if final else y


def _ffn(x, g, w1, w2, final_g, final):
    m, k = x.shape
    ff = w1.shape[1]
    tm = _row_tile(m)
    tf = 1024
    return pl.pallas_call(
        functools.partial(_ffn_kernel, final),
        grid=(m // tm, ff // tf),
        in_specs=[pl.BlockSpec((tm, k), lambda i, j: (i, 0)),
                  pl.BlockSpec((1, k), lambda i, j: (0, 0)),
                  pl.BlockSpec((k, tf), lambda i, j: (0, j)),
                  pl.BlockSpec((tf, k), lambda i, j: (j, 0)),
                  pl.BlockSpec((1, k), lambda i, j: (0, 0))],
        out_specs=pl.BlockSpec((tm, k), lambda i, j: (i, 0)),
        out_shape=jax.ShapeDtypeStruct((m, k), F32),
        scratch_shapes=[pltpu.VMEM((tm, k), BF16), pltpu.VMEM((tm, k), F32)],
        compiler_params=_cparams(("parallel", "arbitrary")),
    )(x, g.reshape(1, k), w1, w2, final_g.reshape(1, k))


def _shift_rows(x, tail, j):
    xr = pltpu.roll(x, j, 0)
    tr = pltpu.roll(tail, j, 0)
    rows = lax.broadcasted_iota(jnp.int32, tr.shape, 0)
    first = jnp.where(rows < j, tr, xr[0:SUBLANES])
    return jnp.concatenate([first, xr[SUBLANES:]], axis=0)


def _hgrn_lower_bound(logit_ref, layer):
    lg = logit_ref[...]
    e = jnp.exp(lg - jnp.max(lg, axis=0, keepdims=True))
    prb = e / jnp.sum(e, axis=0, keepdims=True)
    return jnp.sum(prb[0:layer + 1], axis=0, keepdims=True) - prb[0:1]


def _conv_silu(u, c3, c2, c1, conv_ref):
    cw = conv_ref[...]
    conv = cw[4:5] + c3 * cw[0:1] + c2 * cw[1:2] + c1 * cw[2:3] + u * cw[3:4]
    return conv * _sigmoid(conv)


def _prep_rwkv(pr, prev_rw, mu_ref, pv_ref, lora_ref, seg_ref, rw_scan, rw_post, log_decay):
    pv = pv_ref[...]
    row = lambda i: pv[i:i + 1, :]
    seg = seg_ref[...]
    pm = pr + (prev_rw - pr) * mu_ref[...]
    r, k, v, low = pm[:, 0:256], pm[:, 256:512], pm[:, 512:768], pm[:, 768:896]
    w_log = -_softplus(-(row(0) + _dot_hi(jnp.tanh(low), lora_ref[0]))) - 0.5
    a = _sigmoid(row(1) + _dot_hi(low, lora_ref[1]))
    kk = k * row(2)
    kk = kk / jnp.maximum(jnp.sqrt(_dot_hi(kk * kk, seg)), 1e-12)
    k2 = k * (1.0 + (a - 1.0) * row(3))
    rw_scan[:, 0:256] = r
    rw_scan[:, 256:512] = k2
    rw_scan[:, 512:768] = v
    rw_scan[:, 768:1024] = -jnp.exp(w_log) if log_decay else jnp.exp(-jnp.exp(w_log))
    rw_scan[:, 1024:1280] = kk
    rw_scan[:, 1280:1536] = kk * a
    rw_post[:, 0:256] = _dot_hi(r * k2 * row(4), seg) * v
    rw_post[:, 256:512] = _dot_hi(_sigmoid(low), lora_ref[2])


def _prep_rw_seq_kernel(p_ref, prev8_ref, tail0_ref, mu_ref, pv_ref, lora_ref, seg_ref, rw_scan, rw_post):
    tail = jnp.where(pl.program_id(1) == 0, tail0_ref[0], prev8_ref[...])
    pr = p_ref[...]
    _prep_rwkv(pr, _shift_rows(pr, tail, 1), mu_ref, pv_ref, lora_ref, seg_ref, rw_scan, rw_post, True)


def _prep_rw_seq(proj, tail0, mu, pv, lora, seg, bsz, t):
    tt = 512
    nt = t // tt
    m = bsz * t
    c2 = lambda b, i: (0, 0)
    return pl.pallas_call(
        _prep_rw_seq_kernel,
        grid=(bsz, nt),
        in_specs=[pl.BlockSpec((tt, RW_W), lambda b, i: (b * nt + i, 0)),
                  pl.BlockSpec((SUBLANES, RW_W), lambda b, i: (jnp.maximum((b * nt + i) * (tt // SUBLANES) - 1, 0), 0)),
                  pl.BlockSpec((1, SUBLANES, RW_W), lambda b, i: (b, 0, 0)),
                  pl.BlockSpec((1, RW_W), c2), pl.BlockSpec((16, GW), c2),
                  pl.BlockSpec((3, LANES, GW), lambda b, i: (0, 0, 0)), pl.BlockSpec((GW, GW), c2)],
        out_specs=[pl.BlockSpec((tt, 1536), lambda b, i: (b * nt + i, 0)),
                   pl.BlockSpec((tt, 512), lambda b, i: (b * nt + i, 0))],
        out_shape=[jax.ShapeDtypeStruct((m, 1536), F32), jax.ShapeDtypeStruct((m, 512), F32)],
        compiler_params=_cparams(("parallel", "parallel")),
    )(proj, proj, tail0, mu, pv, lora, seg)


PREP_OUT_WIDTHS = (1536, 512, 1024, 256, 1024, 512, 1280, 256)


def _prep_step_kernel(layer, p_ref, prev_ref, c3_ref, c2_ref, c1_ref, mu_ref, pv_ref, lora_ref, seg_ref, logit_ref,
                      conv_ref, ex_ref, rw_scan, rw_post, hg_scan, hg_post, ss_scan, ss_post, ml_scan, ml_post):
    pv = pv_ref[...]
    row = lambda i: pv[i:i + 1, :]
    _prep_rwkv(p_ref[:, RW_OFF:RW_OFF + RW_W], prev_ref[...], mu_ref, pv_ref, lora_ref, seg_ref, rw_scan, rw_post,
               False)
    g1 = p_ref[:, G1_OFF:G1_OFF + LANES]
    g2 = p_ref[:, G2_OFF:G2_OFF + LANES]
    dt_pre, i_pre, f_pre_m = _dot_hi(g1, ex_ref[0]), _dot_hi(g2, ex_ref[1]), _dot_hi(g2, ex_ref[2])

    lb = _hgrn_lower_bound(logit_ref, layer)
    q_pre = p_ref[:, HG_OFF:HG_OFF + 256]
    f_pre = p_ref[:, HG_OFF + 256:HG_OFF + 512]
    gg = p_ref[:, HG_OFF + 768:HG_OFF + 1024]
    hg_scan[:, 0:256] = q_pre * _sigmoid(q_pre)
    hg_scan[:, 256:512] = lb + (1.0 - lb) * _sigmoid(f_pre)
    hg_scan[:, 512:768] = (1.0 - lb) * _sigmoid(-f_pre)
    hg_scan[:, 768:1024] = p_ref[:, HG_OFF + 512:HG_OFF + 768]
    hg_post[...] = gg * _sigmoid(gg)

    z = p_ref[:, SS_OFF:SS_OFF + 256]
    xbc = _conv_silu(p_ref[:, SS_OFF + 256:SS_OFF + 1024], c3_ref[...], c2_ref[...], c1_ref[...], conv_ref)
    xs = xbc[:, 0:256]
    dt = _softplus(dt_pre + row(5))
    ss_scan[:, 0:256] = xs * dt
    ss_scan[:, 256:512] = jnp.exp(dt * (-jnp.exp(row(6))))
    ss_scan[:, 512:1024] = xbc[:, 256:768]
    ss_post[:, 0:256] = xs * row(7)
    ss_post[:, 256:512] = z * _sigmoid(z)

    ml_scan[:, 0:256] = p_ref[:, ML_OFF:ML_OFF + 256]
    ml_scan[:, 256:512] = p_ref[:, ML_OFF + 256:ML_OFF + 512] * (HEAD_DIM ** -0.5)
    ml_scan[:, 512:768] = p_ref[:, ML_OFF + 512:ML_OFF + 768]
    ml_scan[:, 768:1024] = i_pre + row(8)
    ml_scan[:, 1024:1280] = -_softplus(-(f_pre_m + row(9)))
    ml_post[...] = _sigmoid(p_ref[:, MO_OFF:MO_OFF + 256])


def _prep_step(proj, prev_rw, c3, c2, c1, params, layer):
    m = proj.shape[0]
    full = lambda a: pl.BlockSpec(a.shape, lambda i: (0,) * a.ndim)
    ins = (proj, prev_rw, c3, c2, c1) + tuple(params)
    return pl.pallas_call(
        functools.partial(_prep_step_kernel, layer),
        grid=(1,),
        in_specs=[full(a) for a in ins],
        out_specs=[pl.BlockSpec((m, w), lambda i: (0, 0)) for w in PREP_OUT_WIDTHS],
        out_shape=[jax.ShapeDtypeStruct((m, w), F32) for w in PREP_OUT_WIDTHS],
        compiler_params=_cparams(("arbitrary",)),
    )(*ins)


def _ssm_chunk_body(ss_ref, gt_ref, conv_ref, gp_ref, pd_ref, tril_ref, o_ref, ht_ref, tail_ref):
    @pl.when(pl.program_id(1) == 0)
    def _():
        ht_ref[...] = jnp.zeros_like(ht_ref)
        tail_ref[...] = jnp.zeros_like(tail_ref)

    yield
    L = ss_ref.shape[0]
    u = ss_ref[:, 256:1024]
    tail = tail_ref[...]
    c1, c2, c3 = (_shift_rows(u, tail, j) for j in (1, 2, 3))
    tail_ref[...] = u[L - SUBLANES:L]
    xbc = _conv_silu(u, c3, c2, c1, conv_ref)
    z = ss_ref[:, 0:256]
    zs = z * _sigmoid(z)
    dt = _softplus(gt_ref[...] + gp_ref[0:1])
    la = dt * (-jnp.exp(gp_ref[1:2]))
    tril = tril_ref[...]
    mask = tril > 0.5
    cum = _dot_hi(tril, la)
    yield
    cum_t = cum.T
    pd = pd_ref[...]
    hsl = lambda h: slice(h * HEAD_DIM, (h + 1) * HEAD_DIM)
    bgs = [xbc[:, 256 + SSM_STATE * g:256 + SSM_STATE * (g + 1)] for g in range(2)]
    cgs = [xbc[:, 512 + SSM_STATE * g:512 + SSM_STATE * (g + 1)] for g in range(2)]
    scores = [_dot_nt(cgs[g], bgs[g]) for g in range(2)]
    hts = [ht_ref[0, h] for h in range(NH)]
    inter = [_dot(cgs[h // 2], hts[h]) for h in range(NH)]
    bg_t = [bgs[g].T for g in range(2)]
    ccs = [cum[:, GT_DT + h:GT_DT + h + 1] for h in range(NH)]
    dtc = [dt[:, GT_DT + h:GT_DT + h + 1] for h in range(NH)]
    lms = []
    for h in range(NH):
        cr = cum_t[GT_DT + h:GT_DT + h + 1, :]
        lms.append(jnp.where(mask, jnp.exp(jnp.where(mask, ccs[h] - cr, 0.0)), 0.0))
    yield
    ydots = [_dot(scores[h // 2] * lms[h], xbc[:, hsl(h)] * dtc[h]) for h in range(NH)]
    cends = [ccs[h][L - 1:L, :] for h in range(NH)]
    hdots = [_dot(bg_t[h // 2], xbc[:, hsl(h)] * (jnp.exp(cends[h] - ccs[h]) * dtc[h])) for h in range(NH)]
    yield
    ys = []
    for h in range(NH):
        y = ydots[h] + inter[h] * jnp.exp(ccs[h])
        ys.append((y + xbc[:, hsl(h)] * pd[0:1, hsl(h)]) * zs[:, hsl(h)])
        ht_ref[0, h] = jnp.exp(cends[h]) * hts[h] + hdots[h]
    for g in range(2):
        y0, y1 = ys[2 * g], ys[2 * g + 1]
        ms = jnp.sum(y0 * y0, axis=-1, keepdims=True) + jnp.sum(y1 * y1, axis=-1, keepdims=True)
        scale = lax.rsqrt(ms * (1.0 / (2 * HEAD_DIM)) + NORM_EPS)
        for h in (2 * g, 2 * g + 1):
            o_ref[:, hsl(h)] = ys[h] * scale * pd[1:2, hsl(h)]


def _mlstm_chunk_body(qkv_ref, og_ref, gt_ref, gp_ref, pn_ref, tril_ref, o_ref, ct_ref, n_ref, m_ref):
    @pl.when(pl.program_id(1) == 0)
    def _():
        ct_ref[...] = jnp.zeros_like(ct_ref)
        n_ref[...] = jnp.zeros_like(n_ref)
        m_ref[...] = jnp.zeros_like(m_ref)

    yield
    L = qkv_ref.shape[0]
    x = gt_ref[...] + gp_ref[0:1]
    lf = -_softplus(-x)
    tril = tril_ref[...]
    mask = tril > 0.5
    b = _dot_hi(tril, lf)
    yield
    b_t = b.T
    x_t = x.T
    m_row = m_ref[0]
    m_new_row = m_row
    lane = lax.broadcasted_iota(jnp.int32, m_row.shape, 1)
    pn = pn_ref[...]
    hsl = lambda h, base=0: slice(base + h * HEAD_DIM, base + (h + 1) * HEAD_DIM)
    for h in range(NH):
        b_col, b_row = b[:, GT_F + h:GT_F + h + 1], b_t[GT_F + h:GT_F + h + 1, :]
        li_col, li_row = x[:, GT_I + h:GT_I + h + 1], x_t[GT_I + h:GT_I + h + 1, :]
        m_prev = m_row[:, h:h + 1]
        log_inter = b_col + m_prev
        log_intra = jnp.where(mask, b_col - b_row + li_row, -jnp.inf)
        m_t = jnp.maximum(log_inter, jnp.max(log_intra, axis=-1, keepdims=True))
        w_inter = jnp.exp(log_inter - m_t)
        qh = qkv_ref[:, hsl(h)]
        kh = qkv_ref[:, hsl(h, 256)] * (HEAD_DIM ** -0.5)
        vh = qkv_ref[:, hsl(h, 512)]
        ct = ct_ref[0, h]
        qk_raw = _dot_nt(qh, kh)
        q_ct = _dot(qh, ct)
        yield
        qk = qk_raw * jnp.exp(log_intra - m_t)
        n_h = n_ref[0, h:h + 1, :]
        m_end = m_t[L - 1:L, :]
        b_end = b_col[L - 1:L, :]
        g_prev = jnp.exp(b_end + m_prev - m_end)
        kg = kh * jnp.exp(b_end - b_col + li_col - m_end)
        qk_v = _dot(qk, vh)
        kg_v = _dot_tn(kg, vh)
        yield
        num = qk_v + w_inter * q_ct
        den = jnp.sum(qk, axis=-1, keepdims=True) + w_inter * jnp.sum(qh * n_h, axis=-1, keepdims=True)
        hout = num / jnp.maximum(jnp.abs(den), jnp.exp(-m_t))
        ct_ref[0, h] = g_prev * ct + kg_v
        n_ref[0, h:h + 1, :] = g_prev * n_h + jnp.sum(kg, axis=0, keepdims=True)
        m_new_row = jnp.where(lane == h, m_end, m_new_row)
        hn = hout * lax.rsqrt(jnp.mean(hout * hout, axis=-1, keepdims=True) + NORM_EPS)
        o_ref[:, hsl(h)] = hn * pn[0:1, hsl(h)] * _sigmoid(og_ref[:, hsl(h)])
    m_ref[0] = m_new_row


def _hgrn_chunk_body(layer, hg_ref, logit_ref, pn_ref, tril_ref, ones_ref, seg_ref, bd_ref, o_ref, st_ref, q_scr,
                     b_scr, i_scr):
    @pl.when(pl.program_id(1) == 0)
    def _():
        st_ref[...] = jnp.zeros_like(st_ref)

    yield
    lb = _hgrn_lower_bound(logit_ref, layer)
    seg = seg_ref[...]
    bd = bd_ref[...]
    pn = pn_ref[...]
    sub = HGRN_SUB
    srow = lax.broadcasted_iota(jnp.int32, (sub, GW), 0)
    L = hg_ref.shape[0]

    q_pre, f_pre, v, gg = hg_ref[:, 0:256], hg_ref[:, 256:512], hg_ref[:, 512:768], hg_ref[:, 768:1024]
    q = q_pre * _sigmoid(q_pre)
    k = (1.0 - lb) * _sigmoid(-f_pre)
    lf = jnp.log(lb + (1.0 - lb) * _sigmoid(f_pre))
    b = _dot_hi(tril_ref[...], lf)
    b_tot = _dot_hi(ones_ref[...], lf)
    yield
    q_scr[...] = q
    b_scr[...] = b
    quarter = L // 4
    for part in range(4):
        tiles = []
        for t in range(part * quarter, (part + 1) * quarter):
            rows = slice((t // sub) * sub, (t // sub + 1) * sub)
            live = srow <= (t % sub)
            dec = jnp.where(live, jnp.exp(jnp.where(live, b_scr[t:t + 1, :] - b[rows], 0.0)), 0.0)
            tiles.append(dec * q_scr[t:t + 1, :] * k[rows])
        a_b = jnp.dot(jnp.concatenate(tiles, axis=0).astype(BF16), seg, preferred_element_type=F32)
        yield
        for j in range(quarter):
            t = part * quarter + j
            rows = slice((t // sub) * sub, (t // sub + 1) * sub)
            i_scr[t:t + 1, :] = jnp.sum(a_b[j * sub:(j + 1) * sub, :] * v[rows], axis=0, keepdims=True)
    qx = q * jnp.exp(b)
    kx = k * jnp.exp(b_tot - b)
    p_end = jnp.exp(b_tot)
    st = st_ref[0]
    for c in range(L // sub):
        rows = slice(c * sub, (c + 1) * sub)
        upd = _dot_tn(v[rows], kx[rows]) * bd
        inter = _dot_nt(qx[rows], st)
        yield
        i_scr[rows, :] = inter + i_scr[rows, :]
        st = st * p_end[c * sub:c * sub + 1, :] + upd
    st_ref[0] = st
    o = i_scr[...]
    gate = gg * _sigmoid(gg)
    for h in range(NH):
        sl = slice(h * HEAD_DIM, (h + 1) * HEAD_DIM)
        oh = o[:, sl]
        o_ref[:, sl] = oh * lax.rsqrt(jnp.mean(oh * oh, axis=-1, keepdims=True) + NORM_EPS) * pn[0:1, sl] * gate[:, sl]


def _rwkv_chunk_body(rw_ref, tril_ref, ones_ref, sel_ref, bd_ref, o_ref, st_ref, kx_scr):
    @pl.when(pl.program_id(1) == 0)
    def _():
        st_ref[...] = jnp.zeros_like(st_ref)

    yield
    L = rw_ref.shape[0]
    ns = L // SUB
    xr = ns * NH * SUB
    bd = bd_ref[...]
    csum = lambda a: jnp.sum(a, axis=0, keepdims=True)
    xi = lax.broadcasted_iota(jnp.int32, (xr, GW), 0)
    xj = lax.broadcasted_iota(jnp.int32, (xr, GW), 1)
    head_rows = (((xi // SUB) & (NH - 1)) == xj // HEAD_DIM).astype(F32)
    ti = lax.broadcasted_iota(jnp.int32, (L, xr), 0)
    tj = lax.broadcasted_iota(jnp.int32, (L, xr), 1)
    same = (tj // (NH * SUB)) == (ti // SUB)
    lt = same & ((tj & (SUB - 1)) < (ti & (SUB - 1)))
    le = same & ((tj & (SUB - 1)) <= (ti & (SUB - 1)))

    def expand(x):
        parts = []
        for c in range(ns):
            parts += [x[c * SUB:(c + 1) * SUB]] * NH
        return jnp.concatenate(parts, axis=0) * head_rows

    r, k, v = rw_ref[:, 0:256], rw_ref[:, 256:512], rw_ref[:, 512:768]
    lw, kk, ka = rw_ref[:, 768:1024], rw_ref[:, 1024:1280], rw_ref[:, 1280:1536]
    cs = _dot_hi(tril_ref[...], lw)
    c_tot = _dot_hi(ones_ref[...], lw)
    yield
    e_inv = jnp.exp(-cs)
    kkx = kk * jnp.exp(cs - lw)
    rx = r * jnp.exp(cs)
    kae, ke = ka * e_inv, k * e_inv
    e_end = jnp.exp(c_tot - cs)
    a_hat, k_hat = ka * e_end, k * e_end
    p_end = jnp.exp(c_tot)
    kx_scr[...] = kkx
    tiles = [kx_scr[t:t + 1, :] * kae[(t // SUB) * SUB:(t // SUB + 1) * SUB] for t in range(L)]
    maa = jnp.dot(jnp.concatenate(tiles, axis=0).astype(BF16), sel_ref[...], preferred_element_type=F32)
    v_x = expand(v)
    kkx_x = expand(kkx)
    mk = _dot_nt(jnp.concatenate([kkx, rx], axis=0), expand(ke))
    ra = _dot_nt(rx, expand(kae))
    yield
    m_ak = jnp.where(lt, mk[0:L], 0.0)
    a_rk = jnp.where(le, mk[L:2 * L], 0.0)
    a_ra = jnp.where(le, ra, 0.0)
    mv_x = expand(_dot(m_ak, v_x))
    rk_v = _dot(a_rk, v_x)
    yield
    crow = lax.broadcasted_iota(jnp.int32, (SUB, NH * SUB), 0)
    ccol = lax.broadcasted_iota(jnp.int32, (SUB, NH * SUB), 1) & (SUB - 1)
    eye_c = (crow == ccol).astype(F32)
    gs, us = [], []
    for c in range(ns):
        t_arr = jnp.zeros((SUB, NH * SUB), F32)
        for t in range(SUB):
            i = c * SUB + t
            row_t = eye_c[t:t + 1, :] - csum(maa[i * SUB:(i + 1) * SUB, :] * t_arr)
            t_arr = jnp.where(crow == t, row_t, t_arr)
        xs = slice(c * NH * SUB, (c + 1) * NH * SUB)
        gs.append(_dot(t_arr, kkx_x[xs]))
        us.append(_dot(t_arr, mv_x[xs]))
        if c % 2 == 1:
            yield
    g_all, u_all = jnp.concatenate(gs, axis=0), jnp.concatenate(us, axis=0)
    ra_g = _dot(a_ra, expand(g_all))
    ra_u = _dot(a_ra, expand(u_all))
    phis = [_dot_tn(g_all[c * SUB:(c + 1) * SUB], a_hat[c * SUB:(c + 1) * SUB]) for c in range(ns)]
    psis = [_dot_tn(jnp.concatenate([v[c * SUB:(c + 1) * SUB], -u_all[c * SUB:(c + 1) * SUB]], axis=0),
                    jnp.concatenate([k_hat[c * SUB:(c + 1) * SUB], a_hat[c * SUB:(c + 1) * SUB]], axis=0))
            for c in range(ns)]
    yield
    omega = rx - ra_g
    o_ind = rk_v - ra_u
    st = st_ref[0]
    for c in range(ns):
        rows = slice(c * SUB, (c + 1) * SUB)
        o_dot = _dot_nt(omega[rows], st)
        st_phi = _dot(st, phis[c])
        yield
        o_ref[rows, :] = o_dot + o_ind[rows]
        st = st * p_end[c * SUB:c * SUB + 1, :] + (psis[c] - st_phi) * bd
    st_ref[0] = st


def _mixers_chunk_kernel(layer, p_ref, rw_ref, logit_ref, pnh_ref, trilb_ref, onesb_ref, segb_ref, bd_ref, conv_ref,
                         gp1_ref, gp2_ref, pd_ref, pnm_ref, trilc_ref, sel_ref, trilh_ref, onesh_ref, ohg_ref, oss_ref,
                         oml_ref, orw_ref, sth_ref, ht_ref, ct_ref, n_ref, m_ref, str_ref, q_scr, b_scr, i_scr, tail_scr,
                         kx_scr):
    cols = lambda off, w: p_ref.at[:, off:off + w]
    bodies = [
        _rwkv_chunk_body(rw_ref, trilb_ref, onesb_ref, sel_ref, bd_ref, orw_ref, str_ref, kx_scr),
        _hgrn_chunk_body(layer, cols(HG_OFF, 1024), logit_ref, pnh_ref, trilh_ref, onesh_ref, segb_ref, bd_ref, ohg_ref,
                         sth_ref, q_scr, b_scr, i_scr),
        _mlstm_chunk_body(cols(ML_OFF, 768), cols(MO_OFF, GW), cols(G2_OFF, LANES), gp2_ref, pnm_ref, trilc_ref,
                          oml_ref, ct_ref, n_ref, m_ref),
        _ssm_chunk_body(cols(SS_OFF, 1024), cols(G1_OFF, LANES), conv_ref, gp1_ref, pd_ref, trilc_ref, oss_ref, ht_ref,
                        tail_scr),
    ]
    while bodies:
        for body in list(bodies):
            try:
                next(body)
            except StopIteration:
                bodies.remove(body)


MIXER_STATE_SHAPES = ((GW, GW), (NH, SSM_STATE, HEAD_DIM), (NH, HEAD_DIM, HEAD_DIM), (SUBLANES, HEAD_DIM), (1, LANES),
                      (GW, GW))


def _mixers_chunk(proj, rw_scan, consts, layer, bsz, t):
    nt = t // CHUNK
    m = bsz * t
    rows = lambda w: pl.BlockSpec((CHUNK, w), lambda b, i: (b * nt + i, 0))
    const = lambda a: pl.BlockSpec(a.shape, lambda b, i: (0,) * a.ndim)
    state = lambda s: pl.BlockSpec((1,) + s, lambda b, i: (b,) + (0,) * len(s))
    big = pltpu.VMEM((CHUNK, GW), F32)
    outs = pl.pallas_call(
        functools.partial(_mixers_chunk_kernel, layer),
        grid=(bsz, nt),
        in_specs=[rows(IN_RE), rows(1536)] + [const(a) for a in consts],
        out_specs=[rows(GW)] * 4 + [state(s) for s in MIXER_STATE_SHAPES],
        out_shape=[jax.ShapeDtypeStruct((m, GW), F32)] * 4
        + [jax.ShapeDtypeStruct((bsz,) + s, F32) for s in MIXER_STATE_SHAPES],
        scratch_shapes=[big, big, big, pltpu.VMEM((SUBLANES, 768), F32), big],
        compiler_params=_cparams(("parallel", "arbitrary")),
    )(proj, rw_scan, *consts)
    return outs[:4], outs[4:]


def _segmean(a, s, n):
    return _dot_hi(a, s) * (1.0 / n)


def _rwkv_post(o, rwp_ref, pq, s64):
    d = o - _segmean(o, s64, HEAD_DIM)
    var = _segmean(d * d, s64, HEAD_DIM)
    return (d * lax.rsqrt(var + RWKV_GN_EPS) * pq[0:1] + pq[1:2] + rwp_ref[:, 0:256]) * rwp_ref[:, 256:512]


def _out_proj(x, parts, w_ref):
    acc = x
    for i, part in enumerate(parts):
        acc = acc + jnp.dot(part.astype(BF16), w_ref[i * GW:(i + 1) * GW, :], preferred_element_type=F32)
    return acc


def _post_step_kernel(x_ref, orw_ref, ohg_ref, oss_ref, oml_ref, rwp_ref, hgp_ref, ssp_ref, mlp_ref, pq_ref, seg64_ref,
                      seg128_ref, w_ref, o_ref):
    pq = pq_ref[...]
    s64 = seg64_ref[...]
    m_r = _rwkv_post(orw_ref[...], rwp_ref, pq, s64)
    o = ohg_ref[...]
    m_h = o * lax.rsqrt(_segmean(o * o, s64, HEAD_DIM) + NORM_EPS) * pq[2:3] * hgp_ref[...]
    y = (oss_ref[...] + ssp_ref[:, 0:256]) * ssp_ref[:, 256:512]
    m_s = y * lax.rsqrt(_segmean(y * y, seg128_ref[...], 128) + NORM_EPS) * pq[3:4]
    o = oml_ref[...]
    m_m = o * lax.rsqrt(_segmean(o * o, s64, HEAD_DIM) + NORM_EPS) * pq[4:5] * mlp_ref[...]
    o_ref[...] = _out_proj(x_ref[...], (m_r, m_h, m_s, m_m), w_ref)


def _post_seq_kernel(x_ref, orw_ref, rwp_ref, mh_ref, ms_ref, mm_ref, pq_ref, seg64_ref, w_ref, o_ref):
    m_r = _rwkv_post(orw_ref[...], rwp_ref, pq_ref[...], seg64_ref[...])
    o_ref[...] = _out_proj(x_ref[...], (m_r, mh_ref[...], ms_ref[...], mm_ref[...]), w_ref)


def _post_call(kernel_fn, x, row_ins, consts):
    m = x.shape[0]
    tm = min(m, 512)
    rows = lambda a: pl.BlockSpec((tm, a.shape[1]), lambda i: (i, 0))
    const = lambda a: pl.BlockSpec(a.shape, lambda i: (0, 0))
    return pl.pallas_call(
        kernel_fn,
        grid=(m // tm,),
        in_specs=[rows(a) for a in (x,) + tuple(row_ins)] + [const(a) for a in consts],
        out_specs=rows(x),
        out_shape=jax.ShapeDtypeStruct((m, D_MODEL), F32),
        compiler_params=_cparams(("parallel",)),
    )(x, *row_ins, *consts)


def _xattn_seq_kernel(x_ref, g_ref, wq_ref, k_ref, v_ref, wo_ref, o_ref):
    x = x_ref[...]
    q = jnp.dot(_rms(x, g_ref[...]).astype(BF16), wq_ref[...], preferred_element_type=F32)
    acc = x
    for h in range(XH):
        sl = slice(h * XHD, (h + 1) * XHD)
        s = _dot_nt(q[:, sl], k_ref[:, sl]) * (XHD ** -0.5)
        e = jnp.exp(s - jnp.max(s, axis=-1, keepdims=True))
        pr = e / jnp.sum(e, axis=-1, keepdims=True)
        acc = acc + _dot(_dot(pr, v_ref[:, sl]), wo_ref[sl, :])
    o_ref[...] = acc


def _xattn_seq(x, g, wq, kv, wo, bsz, t):
    tq = 512
    nt = t // tq
    const = lambda a: pl.BlockSpec(a.shape, lambda b, i: (0, 0))
    g2 = g.reshape(1, D_MODEL)
    return pl.pallas_call(
        _xattn_seq_kernel,
        grid=(bsz, nt),
        in_specs=[pl.BlockSpec((tq, D_MODEL), lambda b, i: (b * nt + i, 0)), const(g2), const(wq),
                  pl.BlockSpec((N_MEM, D_MODEL), lambda b, i: (b, 0)),
                  pl.BlockSpec((N_MEM, D_MODEL), lambda b, i: (b, 1)),
                  const(wo)],
        out_specs=pl.BlockSpec((tq, D_MODEL), lambda b, i: (b * nt + i, 0)),
        out_shape=jax.ShapeDtypeStruct((bsz * t, D_MODEL), F32),
        compiler_params=_cparams(("parallel", "parallel")),
    )(x, g2, wq, kv, kv, wo)


def _xattn_step_kernel(bb, q_ref, k_ref, v_ref, o_ref):
    for i in range(bb):
        t = k_ref[i] * q_ref[i]
        s = jnp.sum(t, axis=-1, keepdims=True) * (XHD ** -0.5)
        e = jnp.exp(s - jnp.max(s, axis=0, keepdims=True))
        pr = e / jnp.sum(e, axis=0, keepdims=True)
        o_ref[i] = jnp.sum(pr * v_ref[i], axis=0)


def _xattn_step(q, cache_k, cache_v, layer):
    bsz = q.shape[0]
    bb = 4
    cache_spec = pl.BlockSpec((None, bb, N_MEM, XH, XHD), lambda i: (layer, i, 0, 0, 0))
    return pl.pallas_call(
        functools.partial(_xattn_step_kernel, bb),
        grid=(bsz // bb,),
        in_specs=[pl.BlockSpec((bb, XH, XHD), lambda i: (i, 0, 0)), cache_spec, cache_spec],
        out_specs=pl.BlockSpec((bb, XH, XHD), lambda i: (i, 0, 0)),
        out_shape=jax.ShapeDtypeStruct((bsz, XH, XHD), F32),
        compiler_params=_cparams(("parallel",)),
    )(q.reshape(bsz, XH, XHD), cache_k, cache_v).reshape(bsz, D_MODEL)


N_BIG_STATES = 4


def _stepv_kernel(rw_ref, hg_ref, xdt_ref, bn_ref, cn_ref, sc_ref, ml_ref, wkv_ref, hgs_ref, ssm_ref, c_ref, n_ref, m_ref,
                  a0_ref, a1_ref, a2_ref, a3_ref, orw_ref, ohg_ref, oss_ref, oml_ref, wkv_o, hgs_o, ssm_o, c_o, n_o, m_o):
    del a0_ref, a1_ref, a2_ref, a3_ref
    ri = lax.broadcasted_iota(jnp.int32, (HEAD_DIM, HEAD_DIM), 0)
    ci = lax.broadcasted_iota(jnp.int32, (HEAD_DIM, HEAD_DIM), 1)
    eye = (ri == ci).astype(F32)
    col = lambda r: jnp.sum(eye * r, axis=-1, keepdims=True)
    row = lambda c: jnp.sum(eye * c, axis=-2, keepdims=True)
    lsum = lambda a: jnp.sum(a, axis=-1, keepdims=True)
    r, k, v, w, kk, ka = (rw_ref[:, i] for i in range(6))
    s = wkv_ref[...]
    s = s * w - lsum(s * kk) * ka + col(v) * k
    orw_ref[...] = row(lsum(s * r))
    wkv_o[...] = s
    q, f, kh, vh = (hg_ref[:, i] for i in range(4))
    s = hgs_ref[...] * col(f) + col(kh) * vh
    ohg_ref[...] = jnp.sum(s * col(q), axis=-2, keepdims=True)
    hgs_o[...] = s
    dec, li, lf = (sc_ref[:, i] for i in range(3))
    s = ssm_ref[...] * dec + col(xdt_ref[...]) * bn_ref[...]
    oss_ref[...] = row(lsum(s * cn_ref[...]))
    ssm_o[...] = s
    q, k, v = (ml_ref[:, i] for i in range(3))
    m_old = m_ref[...]
    m_new = jnp.maximum(lf + m_old, li)
    gp = jnp.exp(lf + m_old - m_new)
    gs = jnp.exp(li - m_new)
    n = gp * n_ref[...] + gs * k
    c = c_ref[...] * gp + col(gs * v) * k
    oml_ref[...] = row(lsum(c * q) / jnp.maximum(jnp.abs(lsum(n * q)), jnp.exp(-m_new)))
    c_o[...] = c
    n_o[...] = n
    m_o[...] = m_new


def _mixers_stepv(scan, states, acc, layer):
    rw, hg, ss, ml = scan
    wkv, hgs, ssm, cst, nst, mst = states
    bsz = rw.shape[0]
    bb = SUBLANES
    heads = lambda a, nf: a.reshape(bsz, nf, NH, 1, HEAD_DIM)
    per_head = lambda a: a.reshape(bsz, NH, HEAD_DIM)[:, :, 0].reshape(bsz, NH, 1, 1)
    groups = lambda a: jnp.repeat(a.reshape(bsz, 2, 1, SSM_STATE), NH // 2, axis=1)
    seq_ins = (heads(rw, 6), heads(hg, 4), heads(ss[:, 0:GW], 1)[:, 0], groups(ss[:, 512:768]), groups(ss[:, 768:1024]),
               jnp.stack([per_head(ss[:, GW:2 * GW]), per_head(ml[:, 3 * GW:4 * GW]), per_head(ml[:, 4 * GW:5 * GW])],
                         axis=1),
               heads(ml[:, 0:3 * GW], 3))
    st_ins = (wkv, hgs, ssm, cst, nst.reshape(DEPTH, bsz, NH, 1, HEAD_DIM), mst.reshape(DEPTH, bsz, NH, 1, 1))
    blk = lambda a: pl.BlockSpec((bb,) + a.shape[1:], lambda i: (i,) + (0,) * (a.ndim - 1))
    st_layer = lambda a: pl.BlockSpec((None, bb) + a.shape[2:], lambda i: (layer, i) + (0,) * (a.ndim - 2))
    st_out = lambda a: pl.BlockSpec((bb,) + a.shape[2:], lambda i: (i,) + (0,) * (a.ndim - 2))
    o_shape = jax.ShapeDtypeStruct((bsz, NH, 1, HEAD_DIM), F32)
    o_spec = pl.BlockSpec((bb, NH, 1, HEAD_DIM), lambda i: (i, 0, 0, 0))
    n_in = len(seq_ins) + len(st_ins)
    outs = pl.pallas_call(
        _stepv_kernel,
        grid=(bsz // bb,),
        in_specs=[blk(a) for a in seq_ins] + [st_layer(a) for a in st_ins]
        + [pl.BlockSpec(memory_space=pl.ANY)] * N_BIG_STATES,
        out_specs=[o_spec] * 4 + [st_layer(a) for a in st_ins[:N_BIG_STATES]]
        + [st_out(a) for a in st_ins[N_BIG_STATES:]],
        out_shape=[o_shape] * 4 + [jax.ShapeDtypeStruct(a.shape, F32) for a in st_ins[:N_BIG_STATES]]
        + [jax.ShapeDtypeStruct(a.shape[1:], F32) for a in st_ins[N_BIG_STATES:]],
        input_output_aliases={n_in + j: 4 + j for j in range(N_BIG_STATES)},
        compiler_params=_cparams(("parallel",)),
    )(*seq_ins, *st_ins, *acc)
    o_parts = [o.reshape(bsz, GW) for o in outs[:4]]
    n_n, m_n = outs[4 + N_BIG_STATES:]
    return o_parts, outs[4:4 + N_BIG_STATES], n_n.reshape(bsz, NH, HEAD_DIM), m_n.reshape(bsz, NH)


def kernel(x_prompt, x_sample, state_rwkv_wkv, state_rwkv_shift, state_hgrn, state_ssm, state_ssm_conv, state_mlstm_C, state_mlstm_n, state_mlstm_m, cache_mem_k, cache_mem_v, mem_prompt, norm_mix_g, w_in, w_out, rwkv_mu, rwkv_w0, rwkv_w_up, rwkv_a0, rwkv_a_up, rwkv_g_up, rwkv_k_k, rwkv_k_a, rwkv_r_k, rwkv_ln_g, rwkv_ln_b, hgrn_lb_logits, hgrn_norm_g, ssm_conv_w, ssm_conv_b, ssm_dt_bias, ssm_A_log, ssm_D, ssm_norm_g, mlstm_i_bias, mlstm_f_bias, mlstm_norm_g, norm_x_g, norm_mem_g, xattn_wq, xattn_wk, xattn_wv, xattn_wo, norm_ff_g, ff_w1, ff_w2, final_norm_g):
    bp, tp, _ = x_prompt.shape
    bs = x_sample.shape[0]
    seg64, seg128 = _seg_matrix(HEAD_DIM), _seg_matrix(128)
    seg64_b = seg64.astype(BF16)
    gate_ex = _gate_expand()
    tril_c = jnp.asarray(np.tril(np.ones((CHUNK, CHUNK), np.float32)))

    def block_ones(sub):
        blk = np.arange(CHUNK) // sub
        return jnp.asarray((blk[:, None] == blk[None, :]).astype(np.float32))

    ones_blk, ones_blk_h = block_ones(SUB), block_ones(HGRN_SUB)
    tril_blk, tril_blk_h = ones_blk * tril_c, ones_blk_h * tril_c
    sel_c = jnp.asarray((np.arange(GW)[:, None] // HEAD_DIM == np.arange(NH * SUB)[None, :] // SUB), BF16)
    rep = lambda a: jnp.repeat(a, HEAD_DIM)
    zeros = lambda *s: jnp.zeros(s, F32)

    hp = x_prompt.reshape(bp * tp, D_MODEL)
    hs = x_sample.reshape(bs, D_MODEL)
    mem = mem_prompt.reshape(bp * N_MEM, D_MODEL)
    p_states, s_small, p_mem = [], [], []
    s_big = tuple(jnp.zeros(a.shape, F32) for a in (state_rwkv_wkv, state_hgrn, state_ssm, state_mlstm_C))

    for l in range(DEPTH):
        last = l == DEPTH - 1
        w_in_re = _in_weight(w_in[l])
        pv = _pad_rows(jnp.stack([rwkv_w0[l], rwkv_a0[l], rwkv_k_k[l], rwkv_k_a[l], rwkv_r_k[l].reshape(GW),
                                  rep(ssm_dt_bias[l]), rep(ssm_A_log[l]), rep(ssm_D[l]), rep(mlstm_i_bias[l]),
                                  rep(mlstm_f_bias[l])]), 16)
        lora = jnp.stack([jnp.pad(w, ((r0, LANES - r0 - w.shape[0]), (0, 0)))
                          for r0, w in zip(LORA_ROWS, (rwkv_w_up[l], rwkv_a_up[l], rwkv_g_up[l]))])
        conv = _pad_rows(jnp.concatenate([ssm_conv_w[l], ssm_conv_b[l][None]], axis=0), 8)
        mu = rwkv_mu[l].reshape(1, RW_W)
        gp1 = _pad_rows(jnp.stack([_lane_row((GT_DT, ssm_dt_bias[l])), _lane_row((GT_DT, ssm_A_log[l]))]), 8)
        gp2 = _pad_rows(_lane_row((GT_I, mlstm_i_bias[l]), (GT_F, mlstm_f_bias[l]))[None], 8)
        pd = _pad_rows(jnp.stack([rep(ssm_D[l]), ssm_norm_g[l]]), 8)
        pn_h = _pad_rows(hgrn_norm_g[l][None], 8)
        pn_m = _pad_rows(mlstm_norm_g[l][None], 8)
        pq = _pad_rows(jnp.stack([rwkv_ln_g[l], rwkv_ln_b[l], hgrn_norm_g[l], ssm_norm_g[l], mlstm_norm_g[l]]), 8)
        w_out_b = w_out[l].astype(BF16)
        wq_b, wo_b = xattn_wq[l].astype(BF16), xattn_wo[l].astype(BF16)
        wkv_b = jnp.concatenate([xattn_wk[l], xattn_wv[l]], axis=1).astype(BF16)
        w1_b, w2_b = ff_w1[l].astype(BF16), ff_w2[l].astype(BF16)

        kv = _norm_matmul(mem, norm_mem_g[l], wkv_b)
        p_mem.append((kv[:, :D_MODEL].reshape(bp, N_MEM, XH, XHD), kv[:, D_MODEL:].reshape(bp, N_MEM, XH, XHD)))
        proj = _norm_matmul(hp, norm_mix_g[l], w_in_re)
        rw_scan, rw_post = _prep_rw_seq(proj, zeros(bp, SUBLANES, RW_W), mu, pv, lora, seg64, bp, tp)
        (m_h, m_s, m_m, o_rw), (st_h, st_s, st_c, st_n, st_m, st_r) = _mixers_chunk(
            proj, rw_scan, (hgrn_lb_logits, pn_h, tril_blk, ones_blk, seg64_b, seg64, conv, gp1, gp2, pd, pn_m, tril_c,
                            sel_c, tril_blk_h, ones_blk_h), l, bp, tp)
        proj3 = proj.reshape(bp, tp, IN_RE)
        diag_blocks = lambda s: jnp.stack(
            [s.reshape(bp, NH, HEAD_DIM, NH, HEAD_DIM)[:, h, :, h, :] for h in range(NH)], axis=1)
        p_states.append((diag_blocks(st_r), proj3[:, -1, RW_OFF:RW_OFF + RW_W], diag_blocks(st_h).transpose(0, 1, 3, 2),
                         st_s.transpose(0, 1, 3, 2), proj3[:, tp - 3:, SS_OFF + 256:SS_OFF + 1024],
                         st_c.transpose(0, 1, 3, 2), st_n[:, :NH], st_m[:, 0, :NH]))
        hp = _post_call(_post_seq_kernel, hp, (o_rw, rw_post, m_h, m_s, m_m), (pq, seg64, w_out_b))
        hp = _xattn_seq(hp, norm_x_g[l], wq_b, kv, wo_b, bp, tp)
        hp = _ffn(hp, norm_ff_g[l], w1_b, w2_b, final_norm_g, last)

        proj = _norm_matmul(hs, norm_mix_g[l], w_in_re)
        cb = state_ssm_conv[l]
        prep = _prep_step(proj, state_rwkv_shift[l], cb[:, 0], cb[:, 1], cb[:, 2],
                          (mu, pv, lora, seg64, hgrn_lb_logits, conv, gate_ex), l)
        o_parts, s_big, n_n, m_n = _mixers_stepv(
            prep[0::2], (state_rwkv_wkv, state_hgrn, state_ssm, state_mlstm_C, state_mlstm_n, state_mlstm_m), s_big, l)
        conv_new = jnp.concatenate([cb[:, 1:], proj[:, None, SS_OFF + 256:SS_OFF + 1024]], axis=1)
        s_small.append((proj[:, RW_OFF:RW_OFF + RW_W], conv_new, n_n, m_n))
        hs = _post_call(_post_step_kernel, hs, tuple(o_parts) + tuple(prep[1::2]), (pq, seg64, seg128, w_out_b))
        q = _norm_matmul(hs, norm_x_g[l], wq_b)
        att = _xattn_step(q, cache_mem_k, cache_mem_v, l)
        hs = _mm_res(att, wo_b, hs)
        hs = _ffn(hs, norm_ff_g[l], w1_b, w2_b, final_norm_g, last)

    y_prompt = hp.reshape(bp, tp, D_MODEL)
    y_sample = hs.reshape(bs, 1, D_MODEL)
    p_out = [jnp.stack(s) for s in zip(*p_states)]
    s_shift, s_conv, s_n, s_m = [jnp.stack(s) for s in zip(*s_small)]
    s_wkv, s_hgrn, s_ssm, s_c = s_big
    p_mem_k = jnp.stack([kv[0] for kv in p_mem])
    p_mem_v = jnp.stack([kv[1] for kv in p_mem])
    return (y_prompt, y_sample, *p_out, p_mem_k, p_mem_v, s_wkv, s_shift, s_hgrn, s_ssm, s_conv, s_c, s_n, s_m)
```

```python
import functools

import numpy as np
import jax
import jax.numpy as jnp
from jax import lax
from jax.experimental import pallas as pl
from jax.experimental.pallas import tpu as pltpu

F32 = jnp.float32
BF16 = jnp.bfloat16
HI = lax.Precision.HIGHEST

D_MODEL = 1024
DEPTH = 2
NH = 4
HEAD_DIM = 64
GW = 256
SSM_STATE = 128
N_MEM = 256
XH = 4
XHD = 256
NORM_EPS = 1e-6
RWKV_GN_EPS = 64e-5
LANES = 128
SUBLANES = 8
VMEM_LIMIT = 48 * 1024 * 1024
CHUNK = 128
SUB = 16
HGRN_SUB = 16

RW_OFF, RW_W = 0, 896
HG_OFF = 896
SS_OFF = 1920
G1_OFF = 2944
ML_OFF = 3072
G2_OFF = 3840
MO_OFF = 3968
IN_RE = 4224
GT_DT, GT_I, GT_F = 0, 0, 4
LORA_ROWS = (0, 32, 64)


def _in_weight_kernel(w_ref, o_ref):
    w = w_ref[...]
    lane = lax.broadcasted_iota(jnp.int32, (w.shape[0], LANES), 1)
    gate_tile = lambda c0, n: jnp.where(lane < n, w[:, c0:c0 + LANES], 0.0)
    o_ref[:, 0:G1_OFF] = w[:, 0:G1_OFF].astype(BF16)
    o_ref[:, G1_OFF:ML_OFF] = gate_tile(2944, NH).astype(BF16)
    o_ref[:, ML_OFF:G2_OFF] = w[:, 2948:3716].astype(BF16)
    o_ref[:, G2_OFF:MO_OFF] = gate_tile(3716, 2 * NH).astype(BF16)
    o_ref[:, MO_OFF:IN_RE] = w[:, 3724:3980].astype(BF16)


def _in_weight(w_in, layer):
    k, n = w_in.shape[1:]
    tr = 256
    return pl.pallas_call(
        _in_weight_kernel,
        grid=(k // tr,),
        in_specs=[pl.BlockSpec((None, tr, n), lambda i: (layer, i, 0))],
        out_specs=pl.BlockSpec((tr, IN_RE), lambda i: (i, 0)),
        out_shape=jax.ShapeDtypeStruct((k, IN_RE), BF16),
        compiler_params=_cparams(("parallel",)),
    )(w_in)


def _seg_matrix(width):
    i = np.arange(GW)
    return jnp.asarray((i[:, None] // width == i[None, :] // width).astype(np.float32))


def _gate_expand():
    e = np.zeros((3, LANES, GW), np.float32)
    for s, off in enumerate((GT_DT, GT_I, GT_F)):
        for h in range(NH):
            e[s, off + h, h * HEAD_DIM:(h + 1) * HEAD_DIM] = 1.0
    return jnp.asarray(e)


def _lane_row(*pieces):
    z = jnp.zeros((LANES,), F32)
    for off, val in pieces:
        z = z.at[off:off + val.shape[0]].set(val)
    return z


def _pad_rows(a, n):
    return jnp.pad(a, ((0, n - a.shape[0]), (0, 0)))


def _cparams(sem):
    return pltpu.CompilerParams(dimension_semantics=sem, vmem_limit_bytes=VMEM_LIMIT)


def _softplus(x):
    return jnp.maximum(x, 0.0) + jnp.log(1.0 + jnp.exp(-jnp.abs(x)))


def _sigmoid(x):
    return 1.0 / (1.0 + jnp.exp(-x))


def _rms(x, g):
    ms = jnp.mean(x * x, axis=-1, keepdims=True)
    return x * lax.rsqrt(ms + NORM_EPS) * g


def _dot(a, b):
    return jnp.dot(a.astype(BF16), b.astype(BF16), preferred_element_type=F32)


def _dot_nt(a, b):
    return lax.dot_general(a.astype(BF16), b.astype(BF16), (((1,), (1,)), ((), ())), preferred_element_type=F32)


def _dot_tn(a, b):
    return lax.dot_general(a.astype(BF16), b.astype(BF16), (((0,), (0,)), ((), ())), preferred_element_type=F32)


def _dot_hi(a, b):
    return jnp.dot(a, b, precision=HI, preferred_element_type=F32)


def _norm_matmul_kernel(x_ref, g_ref, w_ref, o_ref, xn_ref):
    @pl.when(pl.program_id(1) == 0)
    def _():
        xn_ref[...] = _rms(x_ref[...], g_ref[...]).astype(BF16)

    o_ref[...] = jnp.dot(xn_ref[...], w_ref[...], preferred_element_type=F32)


def _row_tile(m):
    return min(m, 1024)


def _norm_matmul(x, g, w):
    m, k = x.shape
    n = w.shape[1]
    tm = _row_tile(m)
    tn = next(c for c in (1536, 1408, 1024, 512, n) if n % c == 0)
    return pl.pallas_call(
        _norm_matmul_kernel,
        grid=(m // tm, n // tn),
        in_specs=[pl.BlockSpec((tm, k), lambda i, j: (i, 0)),
                  pl.BlockSpec((1, k), lambda i, j: (0, 0)),
                  pl.BlockSpec((k, tn), lambda i, j: (0, j))],
        out_specs=pl.BlockSpec((tm, tn), lambda i, j: (i, j)),
        out_shape=jax.ShapeDtypeStruct((m, n), F32),
        scratch_shapes=[pltpu.VMEM((tm, k), BF16)],
        compiler_params=_cparams(("parallel", "arbitrary")),
    )(x, g.reshape(1, k), w)


def _mm_res_kernel(a_ref, w_ref, res_ref, o_ref):
    o_ref[...] = res_ref[...] + jnp.dot(a_ref[...].astype(BF16), w_ref[...], preferred_element_type=F32)


def _mm_res(a, w, res):
    m, k = a.shape
    n = w.shape[1]
    tm = min(m, 512)
    return pl.pallas_call(
        _mm_res_kernel,
        grid=(m // tm,),
        in_specs=[pl.BlockSpec((tm, k), lambda i: (i, 0)),
                  pl.BlockSpec((k, n), lambda i: (0, 0)),
                  pl.BlockSpec((tm, n), lambda i: (i, 0))],
        out_specs=pl.BlockSpec((tm, n), lambda i: (i, 0)),
        out_shape=jax.ShapeDtypeStruct((m, n), F32),
        compiler_params=_cparams(("parallel",)),
    )(a, w, res)


def _ffn_kernel(final, x_ref, g_ref, w1_ref, w2_ref, fg_ref, o_ref, xn_ref, acc_ref):
    j = pl.program_id(1)

    @pl.when(j == 0)
    def _():
        xn_ref[...] = _rms(x_ref[...], g_ref[...]).astype(BF16)
        acc_ref[...] = jnp.zeros_like(acc_ref)

    h = jnp.dot(xn_ref[...], w1_ref[...], preferred_element_type=F32)
    h = jnp.square(jnp.maximum(h, 0.0))
    acc_ref[...] += jnp.dot(h.astype(BF16), w2_ref[...], preferred_element_type=F32)

    @pl.when(j == pl.num_programs(1) - 1)
    def _():
        y = x_ref[...] + acc_ref[...]
        o_ref[...] = _rms(y, fg_ref[...]) if final else y


def _ffn(x, g, w1, w2, final_g, final):
    m, k = x.shape
    ff = w1.shape[1]
    tm = _row_tile(m)
    tf = 1024
    return pl.pallas_call(
        functools.partial(_ffn_kernel, final),
        grid=(m // tm, ff // tf),
        in_specs=[pl.BlockSpec((tm, k), lambda i, j: (i, 0)),
                  pl.BlockSpec((1, k), lambda i, j: (0, 0)),
                  pl.BlockSpec((k, tf), lambda i, j: (0, j)),
                  pl.BlockSpec((tf, k), lambda i, j: (j, 0)),
                  pl.BlockSpec((1, k), lambda i, j: (0, 0))],
        out_specs=pl.BlockSpec((tm, k), lambda i, j: (i, 0)),
        out_shape=jax.ShapeDtypeStruct((m, k), F32),
        scratch_shapes=[pltpu.VMEM((tm, k), BF16), pltpu.VMEM((tm, k), F32)],
        compiler_params=_cparams(("parallel", "arbitrary")),
    )(x, g.reshape(1, k), w1, w2, final_g.reshape(1, k))


def _shift_rows(x, tail, j):
    xr = pltpu.roll(x, j, 0)
    tr = pltpu.roll(tail, j, 0)
    rows = lax.broadcasted_iota(jnp.int32, tr.shape, 0)
    first = jnp.where(rows < j, tr, xr[0:SUBLANES])
    return jnp.concatenate([first, xr[SUBLANES:]], axis=0)


def _hgrn_lower_bound(logit_ref, layer):
    lg = logit_ref[...]
    e = jnp.exp(lg - jnp.max(lg, axis=0, keepdims=True))
    prb = e / jnp.sum(e, axis=0, keepdims=True)
    return jnp.sum(prb[0:layer + 1], axis=0, keepdims=True) - prb[0:1]


def _conv_silu(u, c3, c2, c1, conv_ref):
    cw = conv_ref[...]
    conv = cw[4:5] + c3 * cw[0:1] + c2 * cw[1:2] + c1 * cw[2:3] + u * cw[3:4]
    return conv * _sigmoid(conv)


def _prep_rwkv(pr, prev_rw, mu_ref, pv_ref, lora_ref, seg_ref, rw_scan, rw_post, log_decay):
    pv = pv_ref[...]
    row = lambda i: pv[i:i + 1, :]
    seg = seg_ref[...]
    pm = pr + (prev_rw - pr) * mu_ref[...]
    r, k, v, low = pm[:, 0:256], pm[:, 256:512], pm[:, 512:768], pm[:, 768:896]
    w_log = -_softplus(-(row(0) + _dot_hi(jnp.tanh(low), lora_ref[0]))) - 0.5
    a = _sigmoid(row(1) + _dot_hi(low, lora_ref[1]))
    kk = k * row(2)
    kk = kk / jnp.maximum(jnp.sqrt(_dot_hi(kk * kk, seg)), 1e-12)
    k2 = k * (1.0 + (a - 1.0) * row(3))
    rw_scan[:, 0:256] = r
    rw_scan[:, 256:512] = k2
    rw_scan[:, 512:768] = v
    rw_scan[:, 768:1024] = -jnp.exp(w_log) if log_decay else jnp.exp(-jnp.exp(w_log))
    rw_scan[:, 1024:1280] = kk
    rw_scan[:, 1280:1536] = kk * a
    rw_post[:, 0:256] = _dot_hi(r * k2 * row(4), seg) * v
    rw_post[:, 256:512] = _dot_hi(_sigmoid(low), lora_ref[2])


def _prep_rw_seq_kernel(p_ref, prev8_ref, tail0_ref, mu_ref, pv_ref, lora_ref, seg_ref, rw_scan, rw_post):
    tail = jnp.where(pl.program_id(1) == 0, tail0_ref[0], prev8_ref[...])
    pr = p_ref[...]
    _prep_rwkv(pr, _shift_rows(pr, tail, 1), mu_ref, pv_ref, lora_ref, seg_ref, rw_scan, rw_post, True)


def _prep_rw_seq(proj, tail0, mu, pv, lora, seg, bsz, t):
    tt = 512
    nt = t // tt
    m = bsz * t
    c2 = lambda b, i: (0, 0)
    return pl.pallas_call(
        _prep_rw_seq_kernel,
        grid=(bsz, nt),
        in_specs=[pl.BlockSpec((tt, RW_W), lambda b, i: (b * nt + i, 0)),
                  pl.BlockSpec((SUBLANES, RW_W), lambda b, i: (jnp.maximum((b * nt + i) * (tt // SUBLANES) - 1, 0), 0)),
                  pl.BlockSpec((1, SUBLANES, RW_W), lambda b, i: (b, 0, 0)),
                  pl.BlockSpec((1, RW_W), c2), pl.BlockSpec((16, GW), c2),
                  pl.BlockSpec((3, LANES, GW), lambda b, i: (0, 0, 0)), pl.BlockSpec((GW, GW), c2)],
        out_specs=[pl.BlockSpec((tt, 1536), lambda b, i: (b * nt + i, 0)),
                   pl.BlockSpec((tt, 512), lambda b, i: (b * nt + i, 0))],
        out_shape=[jax.ShapeDtypeStruct((m, 1536), F32), jax.ShapeDtypeStruct((m, 512), F32)],
        compiler_params=_cparams(("parallel", "parallel")),
    )(proj, proj, tail0, mu, pv, lora, seg)


PREP_OUT_WIDTHS = (1536, 512, 1024, 256, 1024, 512, 1280, 256)


def _prep_step_kernel(layer, p_ref, prev_ref, c3_ref, c2_ref, c1_ref, mu_ref, pv_ref, lora_ref, seg_ref, logit_ref,
                      conv_ref, ex_ref, rw_scan, rw_post, hg_scan, hg_post, ss_scan, ss_post, ml_scan, ml_post):
    pv = pv_ref[...]
    row = lambda i: pv[i:i + 1, :]
    _prep_rwkv(p_ref[:, RW_OFF:RW_OFF + RW_W], prev_ref[...], mu_ref, pv_ref, lora_ref, seg_ref, rw_scan, rw_post,
               False)
    g1 = p_ref[:, G1_OFF:G1_OFF + LANES]
    g2 = p_ref[:, G2_OFF:G2_OFF + LANES]
    dt_pre, i_pre, f_pre_m = _dot_hi(g1, ex_ref[0]), _dot_hi(g2, ex_ref[1]), _dot_hi(g2, ex_ref[2])

    lb = _hgrn_lower_bound(logit_ref, layer)
    q_pre = p_ref[:, HG_OFF:HG_OFF + 256]
    f_pre = p_ref[:, HG_OFF + 256:HG_OFF + 512]
    gg = p_ref[:, HG_OFF + 768:HG_OFF + 1024]
    hg_scan[:, 0:256] = q_pre * _sigmoid(q_pre)
    hg_scan[:, 256:512] = lb + (1.0 - lb) * _sigmoid(f_pre)
    hg_scan[:, 512:768] = (1.0 - lb) * _sigmoid(-f_pre)
    hg_scan[:, 768:1024] = p_ref[:, HG_OFF + 512:HG_OFF + 768]
    hg_post[...] = gg * _sigmoid(gg)

    z = p_ref[:, SS_OFF:SS_OFF + 256]
    xbc = _conv_silu(p_ref[:, SS_OFF + 256:SS_OFF + 1024], c3_ref[...], c2_ref[...], c1_ref[...], conv_ref)
    xs = xbc[:, 0:256]
    dt = _softplus(dt_pre + row(5))
    ss_scan[:, 0:256] = xs * dt
    ss_scan[:, 256:512] = jnp.exp(dt * (-jnp.exp(row(6))))
    ss_scan[:, 512:1024] = xbc[:, 256:768]
    ss_post[:, 0:256] = xs * row(7)
    ss_post[:, 256:512] = z * _sigmoid(z)

    ml_scan[:, 0:256] = p_ref[:, ML_OFF:ML_OFF + 256]
    ml_scan[:, 256:512] = p_ref[:, ML_OFF + 256:ML_OFF + 512] * (HEAD_DIM ** -0.5)
    ml_scan[:, 512:768] = p_ref[:, ML_OFF + 512:ML_OFF + 768]
    ml_scan[:, 768:1024] = i_pre + row(8)
    ml_scan[:, 1024:1280] = -_softplus(-(f_pre_m + row(9)))
    ml_post[...] = _sigmoid(p_ref[:, MO_OFF:MO_OFF + 256])


def _prep_step(proj, prev_rw, c3, c2, c1, params, layer):
    m = proj.shape[0]
    full = lambda a: pl.BlockSpec(a.shape, lambda i: (0,) * a.ndim)
    ins = (proj, prev_rw, c3, c2, c1) + tuple(params)
    return pl.pallas_call(
        functools.partial(_prep_step_kernel, layer),
        grid=(1,),
        in_specs=[full(a) for a in ins],
        out_specs=[pl.BlockSpec((m, w), lambda i: (0, 0)) for w in PREP_OUT_WIDTHS],
        out_shape=[jax.ShapeDtypeStruct((m, w), F32) for w in PREP_OUT_WIDTHS],
        compiler_params=_cparams(("arbitrary",)),
    )(*ins)


def _ssm_chunk_body(ss_ref, gt_ref, conv_ref, gp_ref, pd_ref, tril_ref, o_ref, ht_ref, tail_ref):
    @pl.when(pl.program_id(1) == 0)
    def _():
        ht_ref[...] = jnp.zeros_like(ht_ref)
        tail_ref[...] = jnp.zeros_like(tail_ref)

    yield
    L = ss_ref.shape[0]
    u = ss_ref[:, 256:1024]
    tail = tail_ref[...]
    c1, c2, c3 = (_shift_rows(u, tail, j) for j in (1, 2, 3))
    tail_ref[...] = u[L - SUBLANES:L]
    xbc = _conv_silu(u, c3, c2, c1, conv_ref)
    z = ss_ref[:, 0:256]
    zs = z * _sigmoid(z)
    dt = _softplus(gt_ref[...] + gp_ref[0:1])
    la = dt * (-jnp.exp(gp_ref[1:2]))
    tril = tril_ref[...]
    mask = tril > 0.5
    cum = _dot_hi(tril, la)
    yield
    cum_t = cum.T
    pd = pd_ref[...]
    hsl = lambda h: slice(h * HEAD_DIM, (h + 1) * HEAD_DIM)
    bgs = [xbc[:, 256 + SSM_STATE * g:256 + SSM_STATE * (g + 1)] for g in range(2)]
    cgs = [xbc[:, 512 + SSM_STATE * g:512 + SSM_STATE * (g + 1)] for g in range(2)]
    scores = [_dot_nt(cgs[g], bgs[g]) for g in range(2)]
    hts = [ht_ref[0, h] for h in range(NH)]
    inter = [_dot(cgs[h // 2], hts[h]) for h in range(NH)]
    bg_t = [bgs[g].T for g in range(2)]
    ccs = [cum[:, GT_DT + h:GT_DT + h + 1] for h in range(NH)]
    dtc = [dt[:, GT_DT + h:GT_DT + h + 1] for h in range(NH)]
    lms = []
    for h in range(NH):
        cr = cum_t[GT_DT + h:GT_DT + h + 1, :]
        lms.append(jnp.where(mask, jnp.exp(jnp.where(mask, ccs[h] - cr, 0.0)), 0.0))
    yield
    ydots = [_dot(scores[h // 2] * lms[h], xbc[:, hsl(h)] * dtc[h]) for h in range(NH)]
    cends = [ccs[h][L - 1:L, :] for h in range(NH)]
    hdots = [_dot(bg_t[h // 2], xbc[:, hsl(h)] * (jnp.exp(cends[h] - ccs[h]) * dtc[h])) for h in range(NH)]
    yield
    ys = []
    for h in range(NH):
        y = ydots[h] + inter[h] * jnp.exp(ccs[h])
        ys.append((y + xbc[:, hsl(h)] * pd[0:1, hsl(h)]) * zs[:, hsl(h)])
        ht_ref[0, h] = jnp.exp(cends[h]) * hts[h] + hdots[h]
    for g in range(2):
        y0, y1 = ys[2 * g], ys[2 * g + 1]
        ms = jnp.sum(y0 * y0, axis=-1, keepdims=True) + jnp.sum(y1 * y1, axis=-1, keepdims=True)
        scale = lax.rsqrt(ms * (1.0 / (2 * HEAD_DIM)) + NORM_EPS)
        for h in (2 * g, 2 * g + 1):
            o_ref[:, hsl(h)] = ys[h] * scale * pd[1:2, hsl(h)]


def _mlstm_chunk_body(qkv_ref, og_ref, gt_ref, gp_ref, pn_ref, tril_ref, o_ref, ct_ref, n_ref, m_ref):
    @pl.when(pl.program_id(1) == 0)
    def _():
        ct_ref[...] = jnp.zeros_like(ct_ref)
        n_ref[...] = jnp.zeros_like(n_ref)
        m_ref[...] = jnp.zeros_like(m_ref)

    yield
    L = qkv_ref.shape[0]
    x = gt_ref[...] + gp_ref[0:1]
    lf = -_softplus(-x)
    tril = tril_ref[...]
    mask = tril > 0.5
    b = _dot_hi(tril, lf)
    yield
    b_t = b.T
    x_t = x.T
    m_row = m_ref[0]
    m_new_row = m_row
    lane = lax.broadcasted_iota(jnp.int32, m_row.shape, 1)
    pn = pn_ref[...]
    hsl = lambda h, base=0: slice(base + h * HEAD_DIM, base + (h + 1) * HEAD_DIM)
    for h in range(NH):
        b_col, b_row = b[:, GT_F + h:GT_F + h + 1], b_t[GT_F + h:GT_F + h + 1, :]
        li_col, li_row = x[:, GT_I + h:GT_I + h + 1], x_t[GT_I + h:GT_I + h + 1, :]
        m_prev = m_row[:, h:h + 1]
        log_inter = b_col + m_prev
        log_intra = jnp.where(mask, b_col - b_row + li_row, -jnp.inf)
        m_t = jnp.maximum(log_inter, jnp.max(log_intra, axis=-1, keepdims=True))
        w_inter = jnp.exp(log_inter - m_t)
        qh = qkv_ref[:, hsl(h)]
        kh = qkv_ref[:, hsl(h, 256)] * (HEAD_DIM ** -0.5)
        vh = qkv_ref[:, hsl(h, 512)]
        ct = ct_ref[0, h]
        qk_raw = _dot_nt(qh, kh)
        q_ct = _dot(qh, ct)
        yield
        qk = qk_raw * jnp.exp(log_intra - m_t)
        n_h = n_ref[0, h:h + 1, :]
        m_end = m_t[L - 1:L, :]
        b_end = b_col[L - 1:L, :]
        g_prev = jnp.exp(b_end + m_prev - m_end)
        kg = kh * jnp.exp(b_end - b_col + li_col - m_end)
        qk_v = _dot(qk, vh)
        kg_v = _dot_tn(kg, vh)
        yield
        num = qk_v + w_inter * q_ct
        den = jnp.sum(qk, axis=-1, keepdims=True) + w_inter * jnp.sum(qh * n_h, axis=-1, keepdims=True)
        hout = num / jnp.maximum(jnp.abs(den), jnp.exp(-m_t))
        ct_ref[0, h] = g_prev * ct + kg_v
        n_ref[0, h:h + 1, :] = g_prev * n_h + jnp.sum(kg, axis=0, keepdims=True)
        m_new_row = jnp.where(lane == h, m_end, m_new_row)
        hn = hout * lax.rsqrt(jnp.mean(hout * hout, axis=-1, keepdims=True) + NORM_EPS)
        o_ref[:, hsl(h)] = hn * pn[0:1, hsl(h)] * _sigmoid(og_ref[:, hsl(h)])
    m_ref[0] = m_new_row


def _hgrn_chunk_body(layer, hg_ref, logit_ref, pn_ref, tril_ref, ones_ref, seg_ref, bd_ref, o_ref, st_ref, q_scr,
                     b_scr, i_scr):
    @pl.when(pl.program_id(1) == 0)
    def _():
        st_ref[...] = jnp.zeros_like(st_ref)

    yield
    lb = _hgrn_lower_bound(logit_ref, layer)
    seg = seg_ref[...]
    bd = bd_ref[...]
    pn = pn_ref[...]
    sub = HGRN_SUB
    srow = lax.broadcasted_iota(jnp.int32, (sub, GW), 0)
    L = hg_ref.shape[0]

    q_pre, f_pre, v, gg = hg_ref[:, 0:256], hg_ref[:, 256:512], hg_ref[:, 512:768], hg_ref[:, 768:1024]
    q = q_pre * _sigmoid(q_pre)
    k = (1.0 - lb) * _sigmoid(-f_pre)
    lf = jnp.log(lb + (1.0 - lb) * _sigmoid(f_pre))
    b = _dot_hi(tril_ref[...], lf)
    b_tot = _dot_hi(ones_ref[...], lf)
    yield
    q_scr[...] = q
    b_scr[...] = b
    quarter = L // 4
    for part in range(4):
        tiles = []
        for t in range(part * quarter, (part + 1) * quarter):
            rows = slice((t // sub) * sub, (t // sub + 1) * sub)
            live = srow <= (t % sub)
            dec = jnp.where(live, jnp.exp(jnp.where(live, b_scr[t:t + 1, :] - b[rows], 0.0)), 0.0)
            tiles.append(dec * q_scr[t:t + 1, :] * k[rows])
        a_b = jnp.dot(jnp.concatenate(tiles, axis=0).astype(BF16), seg, preferred_element_type=F32)
        yield
        for j in range(quarter):
            t = part * quarter + j
            rows = slice((t // sub) * sub, (t // sub + 1) * sub)
            i_scr[t:t + 1, :] = jnp.sum(a_b[j * sub:(j + 1) * sub, :] * v[rows], axis=0, keepdims=True)
    qx = q * jnp.exp(b)
    kx = k * jnp.exp(b_tot - b)
    p_end = jnp.exp(b_tot)
    st = st_ref[0]
    for c in range(L // sub):
        rows = slice(c * sub, (c + 1) * sub)
        upd = _dot_tn(v[rows], kx[rows]) * bd
        inter = _dot_nt(qx[rows], st)
        yield
        i_scr[rows, :] = inter + i_scr[rows, :]
        st = st * p_end[c * sub:c * sub + 1, :] + upd
    st_ref[0] = st
    o = i_scr[...]
    gate = gg * _sigmoid(gg)
    for h in range(NH):
        sl = slice(h * HEAD_DIM, (h + 1) * HEAD_DIM)
        oh = o[:, sl]
        o_ref[:, sl] = oh * lax.rsqrt(jnp.mean(oh * oh, axis=-1, keepdims=True) + NORM_EPS) * pn[0:1, sl] * gate[:, sl]


def _rwkv_chunk_body(rw_ref, tril_ref, ones_ref, sel_ref, bd_ref, o_ref, st_ref, kx_scr):
    @pl.when(pl.program_id(1) == 0)
    def _():
        st_ref[...] = jnp.zeros_like(st_ref)

    yield
    L = rw_ref.shape[0]
    ns = L // SUB
    xr = ns * NH * SUB
    bd = bd_ref[...]
    csum = lambda a: jnp.sum(a, axis=0, keepdims=True)
    xi = lax.broadcasted_iota(jnp.int32, (xr, GW), 0)
    xj = lax.broadcasted_iota(jnp.int32, (xr, GW), 1)
    head_rows = (((xi // SUB) & (NH - 1)) == xj // HEAD_DIM).astype(F32)
    ti = lax.broadcasted_iota(jnp.int32, (L, xr), 0)
    tj = lax.broadcasted_iota(jnp.int32, (L, xr), 1)
    same = (tj // (NH * SUB)) == (ti // SUB)
    lt = same & ((tj & (SUB - 1)) < (ti & (SUB - 1)))
    le = same & ((tj & (SUB - 1)) <= (ti & (SUB - 1)))

    def expand(x):
        parts = []
        for c in range(ns):
            parts += [x[c * SUB:(c + 1) * SUB]] * NH
        return jnp.concatenate(parts, axis=0) * head_rows

    r, k, v = rw_ref[:, 0:256], rw_ref[:, 256:512], rw_ref[:, 512:768]
    lw, kk, ka = rw_ref[:, 768:1024], rw_ref[:, 1024:1280], rw_ref[:, 1280:1536]
    cs = _dot_hi(tril_ref[...], lw)
    c_tot = _dot_hi(ones_ref[...], lw)
    yield
    e_inv = jnp.exp(-cs)
    kkx = kk * jnp.exp(cs - lw)
    rx = r * jnp.exp(cs)
    kae, ke = ka * e_inv, k * e_inv
    e_end = jnp.exp(c_tot - cs)
    a_hat, k_hat = ka * e_end, k * e_end
    p_end = jnp.exp(c_tot)
    kx_scr[...] = kkx
    tiles = [kx_scr[t:t + 1, :] * kae[(t // SUB) * SUB:(t // SUB + 1) * SUB] for t in range(L)]
    maa = jnp.dot(jnp.concatenate(tiles, axis=0).astype(BF16), sel_ref[...], preferred_element_type=F32)
    v_x = expand(v)
    kkx_x = expand(kkx)
    mk = _dot_nt(jnp.concatenate([kkx, rx], axis=0), expand(ke))
    ra = _dot_nt(rx, expand(kae))
    yield
    m_ak = jnp.where(lt, mk[0:L], 0.0)
    a_rk = jnp.where(le, mk[L:2 * L], 0.0)
    a_ra = jnp.where(le, ra, 0.0)
    mv_x = expand(_dot(m_ak, v_x))
    rk_v = _dot(a_rk, v_x)
    yield
    crow = lax.broadcasted_iota(jnp.int32, (SUB, NH * SUB), 0)
    ccol = lax.broadcasted_iota(jnp.int32, (SUB, NH * SUB), 1) & (SUB - 1)
    eye_c = (crow == ccol).astype(F32)
    gs, us = [], []
    for c in range(ns):
        t_arr = jnp.zeros((SUB, NH * SUB), F32)
        for t in range(SUB):
            i = c * SUB + t
            row_t = eye_c[t:t + 1, :] - csum(maa[i * SUB:(i + 1) * SUB, :] * t_arr)
            t_arr = jnp.where(crow == t, row_t, t_arr)
        xs = slice(c * NH * SUB, (c + 1) * NH * SUB)
        gs.append(_dot(t_arr, kkx_x[xs]))
        us.append(_dot(t_arr, mv_x[xs]))
        if c % 2 == 1:
            yield
    g_all, u_all = jnp.concatenate(gs, axis=0), jnp.concatenate(us, axis=0)
    ra_g = _dot(a_ra, expand(g_all))
    ra_u = _dot(a_ra, expand(u_all))
    phis = [_dot_tn(g_all[c * SUB:(c + 1) * SUB], a_hat[c * SUB:(c + 1) * SUB]) for c in range(ns)]
    psis = [_dot_tn(jnp.concatenate([v[c * SUB:(c + 1) * SUB], -u_all[c * SUB:(c + 1) * SUB]], axis=0),
                    jnp.concatenate([k_hat[c * SUB:(c + 1) * SUB], a_hat[c * SUB:(c + 1) * SUB]], axis=0))
            for c in range(ns)]
    yield
    omega = rx - ra_g
    o_ind = rk_v - ra_u
    st = st_ref[0]
    for c in range(ns):
        rows = slice(c * SUB, (c + 1) * SUB)
        o_dot = _dot_nt(omega[rows], st)
        st_phi = _dot(st, phis[c])
        yield
        o_ref[rows, :] = o_dot + o_ind[rows]
        st = st * p_end[c * SUB:c * SUB + 1, :] + (psis[c] - st_phi) * bd
    st_ref[0] = st


def _mixers_chunk_kernel(layer, p_ref, rw_ref, logit_ref, pnh_ref, trilb_ref, onesb_ref, segb_ref, bd_ref, conv_ref,
                         gp1_ref, gp2_ref, pd_ref, pnm_ref, trilc_ref, sel_ref, trilh_ref, onesh_ref, ohg_ref, oss_ref,
                         oml_ref, orw_ref, sth_ref, ht_ref, ct_ref, n_ref, m_ref, str_ref, q_scr, b_scr, i_scr, tail_scr,
                         kx_scr):
    cols = lambda off, w: p_ref.at[:, off:off + w]
    bodies = [
        _rwkv_chunk_body(rw_ref, trilb_ref, onesb_ref, sel_ref, bd_ref, orw_ref, str_ref, kx_scr),
        _hgrn_chunk_body(layer, cols(HG_OFF, 1024), logit_ref, pnh_ref, trilh_ref, onesh_ref, segb_ref, bd_ref, ohg_ref,
                         sth_ref, q_scr, b_scr, i_scr),
        _mlstm_chunk_body(cols(ML_OFF, 768), cols(MO_OFF, GW), cols(G2_OFF, LANES), gp2_ref, pnm_ref, trilc_ref,
                          oml_ref, ct_ref, n_ref, m_ref),
        _ssm_chunk_body(cols(SS_OFF, 1024), cols(G1_OFF, LANES), conv_ref, gp1_ref, pd_ref, trilc_ref, oss_ref, ht_ref,
                        tail_scr),
    ]
    while bodies:
        for body in list(bodies):
            try:
                next(body)
            except StopIteration:
                bodies.remove(body)


MIXER_STATE_SHAPES = ((GW, GW), (NH, SSM_STATE, HEAD_DIM), (NH, HEAD_DIM, HEAD_DIM), (SUBLANES, HEAD_DIM), (1, LANES),
                      (GW, GW))


def _mixers_chunk(proj, rw_scan, consts, layer, bsz, t):
    nt = t // CHUNK
    m = bsz * t
    rows = lambda w: pl.BlockSpec((CHUNK, w), lambda b, i: (b * nt + i, 0))
    const = lambda a: pl.BlockSpec(a.shape, lambda b, i: (0,) * a.ndim)
    state = lambda s: pl.BlockSpec((1,) + s, lambda b, i: (b,) + (0,) * len(s))
    big = pltpu.VMEM((CHUNK, GW), F32)
    outs = pl.pallas_call(
        functools.partial(_mixers_chunk_kernel, layer),
        grid=(bsz, nt),
        in_specs=[rows(IN_RE), rows(1536)] + [const(a) for a in consts],
        out_specs=[rows(GW)] * 4 + [state(s) for s in MIXER_STATE_SHAPES],
        out_shape=[jax.ShapeDtypeStruct((m, GW), F32)] * 4
        + [jax.ShapeDtypeStruct((bsz,) + s, F32) for s in MIXER_STATE_SHAPES],
        scratch_shapes=[big, big, big, pltpu.VMEM((SUBLANES, 768), F32), big],
        compiler_params=_cparams(("parallel", "arbitrary")),
    )(proj, rw_scan, *consts)
    return outs[:4], outs[4:]


def _segmean(a, s, n):
    return _dot_hi(a, s) * (1.0 / n)


def _rwkv_post(o, rwp_ref, pq, s64):
    d = o - _segmean(o, s64, HEAD_DIM)
    var = _segmean(d * d, s64, HEAD_DIM)
    return (d * lax.rsqrt(var + RWKV_GN_EPS) * pq[0:1] + pq[1:2] + rwp_ref[:, 0:256]) * rwp_ref[:, 256:512]


def _out_proj(x, parts, w_ref):
    acc = x
    for i, part in enumerate(parts):
        acc = acc + jnp.dot(part.astype(BF16), w_ref[i * GW:(i + 1) * GW, :], preferred_element_type=F32)
    return acc


def _post_step_kernel(x_ref, orw_ref, ohg_ref, oss_ref, oml_ref, rwp_ref, hgp_ref, ssp_ref, mlp_ref, pq_ref, seg64_ref,
                      seg128_ref, w_ref, o_ref):
    pq = pq_ref[...]
    s64 = seg64_ref[...]
    m_r = _rwkv_post(orw_ref[...], rwp_ref, pq, s64)
    o = ohg_ref[...]
    m_h = o * lax.rsqrt(_segmean(o * o, s64, HEAD_DIM) + NORM_EPS) * pq[2:3] * hgp_ref[...]
    y = (oss_ref[...] + ssp_ref[:, 0:256]) * ssp_ref[:, 256:512]
    m_s = y * lax.rsqrt(_segmean(y * y, seg128_ref[...], 128) + NORM_EPS) * pq[3:4]
    o = oml_ref[...]
    m_m = o * lax.rsqrt(_segmean(o * o, s64, HEAD_DIM) + NORM_EPS) * pq[4:5] * mlp_ref[...]
    o_ref[...] = _out_proj(x_ref[...], (m_r, m_h, m_s, m_m), w_ref)


def _post_seq_kernel(x_ref, orw_ref, rwp_ref, mh_ref, ms_ref, mm_ref, pq_ref, seg64_ref, w_ref, o_ref):
    m_r = _rwkv_post(orw_ref[...], rwp_ref, pq_ref[...], seg64_ref[...])
    o_ref[...] = _out_proj(x_ref[...], (m_r, mh_ref[...], ms_ref[...], mm_ref[...]), w_ref)


def _post_call(kernel_fn, x, row_ins, consts):
    m = x.shape[0]
    tm = min(m, 512)
    rows = lambda a: pl.BlockSpec((tm, a.shape[1]), lambda i: (i, 0))
    const = lambda a: pl.BlockSpec(a.shape, lambda i: (0, 0))
    return pl.pallas_call(
        kernel_fn,
        grid=(m // tm,),
        in_specs=[rows(a) for a in (x,) + tuple(row_ins)] + [const(a) for a in consts],
        out_specs=rows(x),
        out_shape=jax.ShapeDtypeStruct((m, D_MODEL), F32),
        compiler_params=_cparams(("parallel",)),
    )(x, *row_ins, *consts)


def _xattn_seq_kernel(x_ref, g_ref, wq_ref, k_ref, v_ref, wo_ref, o_ref):
    x = x_ref[...]
    q = jnp.dot(_rms(x, g_ref[...]).astype(BF16), wq_ref[...], preferred_element_type=F32)
    acc = x
    for h in range(XH):
        sl = slice(h * XHD, (h + 1) * XHD)
        s = _dot_nt(q[:, sl], k_ref[:, sl]) * (XHD ** -0.5)
        e = jnp.exp(s - jnp.max(s, axis=-1, keepdims=True))
        pr = e / jnp.sum(e, axis=-1, keepdims=True)
        acc = acc + _dot(_dot(pr, v_ref[:, sl]), wo_ref[sl, :])
    o_ref[...] = acc


def _xattn_seq(x, g, wq, kv, wo, bsz, t):
    tq = 512
    nt = t // tq
    const = lambda a: pl.BlockSpec(a.shape, lambda b, i: (0, 0))
    g2 = g.reshape(1, D_MODEL)
    return pl.pallas_call(
        _xattn_seq_kernel,
        grid=(bsz, nt),
        in_specs=[pl.BlockSpec((tq, D_MODEL), lambda b, i: (b * nt + i, 0)), const(g2), const(wq),
                  pl.BlockSpec((N_MEM, D_MODEL), lambda b, i: (b, 0)),
                  pl.BlockSpec((N_MEM, D_MODEL), lambda b, i: (b, 1)),
                  const(wo)],
        out_specs=pl.BlockSpec((tq, D_MODEL), lambda b, i: (b * nt + i, 0)),
        out_shape=jax.ShapeDtypeStruct((bsz * t, D_MODEL), F32),
        compiler_params=_cparams(("parallel", "parallel")),
    )(x, g2, wq, kv, kv, wo)


def _xattn_step_kernel(bb, q_ref, k_ref, v_ref, o_ref):
    for i in range(bb):
        t = k_ref[i] * q_ref[i]
        s = jnp.sum(t, axis=-1, keepdims=True) * (XHD ** -0.5)
        e = jnp.exp(s - jnp.max(s, axis=0, keepdims=True))
        pr = e / jnp.sum(e, axis=0, keepdims=True)
        o_ref[i] = jnp.sum(pr * v_ref[i], axis=0)


def _xattn_step(q, cache_k, cache_v, layer):
    bsz = q.shape[0]
    bb = 4
    cache_spec = pl.BlockSpec((None, bb, N_MEM, XH, XHD), lambda i: (layer, i, 0, 0, 0))
    return pl.pallas_call(
        functools.partial(_xattn_step_kernel, bb),
        grid=(bsz // bb,),
        in_specs=[pl.BlockSpec((bb, XH, XHD), lambda i: (i, 0, 0)), cache_spec, cache_spec],
        out_specs=pl.BlockSpec((bb, XH, XHD), lambda i: (i, 0, 0)),
        out_shape=jax.ShapeDtypeStruct((bsz, XH, XHD), F32),
        compiler_params=_cparams(("parallel",)),
    )(q.reshape(bsz, XH, XHD), cache_k, cache_v).reshape(bsz, D_MODEL)


N_BIG_STATES = 4


def _stepv_kernel(rw_ref, hg_ref, xdt_ref, bn_ref, cn_ref, sc_ref, ml_ref, wkv_ref, hgs_ref, ssm_ref, c_ref, n_ref, m_ref,
                  a0_ref, a1_ref, a2_ref, a3_ref, orw_ref, ohg_ref, oss_ref, oml_ref, wkv_o, hgs_o, ssm_o, c_o, n_o, m_o):
    del a0_ref, a1_ref, a2_ref, a3_ref
    ri = lax.broadcasted_iota(jnp.int32, (HEAD_DIM, HEAD_DIM), 0)
    ci = lax.broadcasted_iota(jnp.int32, (HEAD_DIM, HEAD_DIM), 1)
    eye = (ri == ci).astype(F32)
    col = lambda r: jnp.sum(eye * r, axis=-1, keepdims=True)
    row = lambda c: jnp.sum(eye * c, axis=-2, keepdims=True)
    lsum = lambda a: jnp.sum(a, axis=-1, keepdims=True)
    r, k, v, w, kk, ka = (rw_ref[:, i] for i in range(6))
    s = wkv_ref[...]
    s = s * w - lsum(s * kk) * ka + col(v) * k
    orw_ref[...] = row(lsum(s * r))
    wkv_o[...] = s
    q, f, kh, vh = (hg_ref[:, i] for i in range(4))
    s = hgs_ref[...] * col(f) + col(kh) * vh
    ohg_ref[...] = jnp.sum(s * col(q), axis=-2, keepdims=True)
    hgs_o[...] = s
    dec, li, lf = (sc_ref[:, i] for i in range(3))
    s = ssm_ref[...] * dec + col(xdt_ref[...]) * bn_ref[...]
    oss_ref[...] = row(lsum(s * cn_ref[...]))
    ssm_o[...] = s
    q, k, v = (ml_ref[:, i] for i in range(3))
    m_old = m_ref[...]
    m_new = jnp.maximum(lf + m_old, li)
    gp = jnp.exp(lf + m_old - m_new)
    gs = jnp.exp(li - m_new)
    n = gp * n_ref[...] + gs * k
    c = c_ref[...] * gp + col(gs * v) * k
    oml_ref[...] = row(lsum(c * q) / jnp.maximum(jnp.abs(lsum(n * q)), jnp.exp(-m_new)))
    c_o[...] = c
    n_o[...] = n
    m_o[...] = m_new


def _mixers_stepv(scan, states, acc, layer):
    rw, hg, ss, ml = scan
    wkv, hgs, ssm, cst, nst, mst = states
    bsz = rw.shape[0]
    bb = SUBLANES
    heads = lambda a, nf: a.reshape(bsz, nf, NH, 1, HEAD_DIM)
    per_head = lambda a: a.reshape(bsz, NH, HEAD_DIM)[:, :, 0].reshape(bsz, NH, 1, 1)
    groups = lambda a: jnp.repeat(a.reshape(bsz, 2, 1, SSM_STATE), NH // 2, axis=1)
    seq_ins = (heads(rw, 6), heads(hg, 4), heads(ss[:, 0:GW], 1)[:, 0], groups(ss[:, 512:768]), groups(ss[:, 768:1024]),
               jnp.stack([per_head(ss[:, GW:2 * GW]), per_head(ml[:, 3 * GW:4 * GW]), per_head(ml[:, 4 * GW:5 * GW])],
                         axis=1),
               heads(ml[:, 0:3 * GW], 3))
    st_ins = (wkv, hgs, ssm, cst, nst.reshape(DEPTH, bsz, NH, 1, HEAD_DIM), mst.reshape(DEPTH, bsz, NH, 1, 1))
    blk = lambda a: pl.BlockSpec((bb,) + a.shape[1:], lambda i: (i,) + (0,) * (a.ndim - 1))
    st_layer = lambda a: pl.BlockSpec((None, bb) + a.shape[2:], lambda i: (layer, i) + (0,) * (a.ndim - 2))
    st_out = lambda a: pl.BlockSpec((bb,) + a.shape[2:], lambda i: (i,) + (0,) * (a.ndim - 2))
    o_shape = jax.ShapeDtypeStruct((bsz, NH, 1, HEAD_DIM), F32)
    o_spec = pl.BlockSpec((bb, NH, 1, HEAD_DIM), lambda i: (i, 0, 0, 0))
    n_in = len(seq_ins) + len(st_ins)
    outs = pl.pallas_call(
        _stepv_kernel,
        grid=(bsz // bb,),
        in_specs=[blk(a) for a in seq_ins] + [st_layer(a) for a in st_ins]
        + [pl.BlockSpec(memory_space=pl.ANY)] * N_BIG_STATES,
        out_specs=[o_spec] * 4 + [st_layer(a) for a in st_ins[:N_BIG_STATES]]
        + [st_out(a) for a in st_ins[N_BIG_STATES:]],
        out_shape=[o_shape] * 4 + [jax.ShapeDtypeStruct(a.shape, F32) for a in st_ins[:N_BIG_STATES]]
        + [jax.ShapeDtypeStruct(a.shape[1:], F32) for a in st_ins[N_BIG_STATES:]],
        input_output_aliases={n_in + j: 4 + j for j in range(N_BIG_STATES)},
        compiler_params=_cparams(("parallel",)),
    )(*seq_ins, *st_ins, *acc)
    o_parts = [o.reshape(bsz, GW) for o in outs[:4]]
    n_n, m_n = outs[4 + N_BIG_STATES:]
    return o_parts, outs[4:4 + N_BIG_STATES], n_n.reshape(bsz, NH, HEAD_DIM), m_n.reshape(bsz, NH)


def kernel(x_prompt, x_sample, state_rwkv_wkv, state_rwkv_shift, state_hgrn, state_ssm, state_ssm_conv, state_mlstm_C, state_mlstm_n, state_mlstm_m, cache_mem_k, cache_mem_v, mem_prompt, norm_mix_g, w_in, w_out, rwkv_mu, rwkv_w0, rwkv_w_up, rwkv_a0, rwkv_a_up, rwkv_g_up, rwkv_k_k, rwkv_k_a, rwkv_r_k, rwkv_ln_g, rwkv_ln_b, hgrn_lb_logits, hgrn_norm_g, ssm_conv_w, ssm_conv_b, ssm_dt_bias, ssm_A_log, ssm_D, ssm_norm_g, mlstm_i_bias, mlstm_f_bias, mlstm_norm_g, norm_x_g, norm_mem_g, xattn_wq, xattn_wk, xattn_wv, xattn_wo, norm_ff_g, ff_w1, ff_w2, final_norm_g):
    bp, tp, _ = x_prompt.shape
    bs = x_sample.shape[0]
    seg64, seg128 = _seg_matrix(HEAD_DIM), _seg_matrix(128)
    seg64_b = seg64.astype(BF16)
    gate_ex = _gate_expand()
    tril_c = jnp.asarray(np.tril(np.ones((CHUNK, CHUNK), np.float32)))

    def block_ones(sub):
        blk = np.arange(CHUNK) // sub
        return jnp.asarray((blk[:, None] == blk[None, :]).astype(np.float32))

    ones_blk, ones_blk_h = block_ones(SUB), block_ones(HGRN_SUB)
    tril_blk, tril_blk_h = ones_blk * tril_c, ones_blk_h * tril_c
    sel_c = jnp.asarray((np.arange(GW)[:, None] // HEAD_DIM == np.arange(NH * SUB)[None, :] // SUB), BF16)
    rep = lambda a: jnp.repeat(a, HEAD_DIM)
    zeros = lambda *s: jnp.zeros(s, F32)

    hp = x_prompt.reshape(bp * tp, D_MODEL)
    hs = x_sample.reshape(bs, D_MODEL)
    mem = mem_prompt.reshape(bp * N_MEM, D_MODEL)
    p_states, s_small, p_mem = [], [], []
    s_big = tuple(jnp.zeros(a.shape, F32) for a in (state_rwkv_wkv, state_hgrn, state_ssm, state_mlstm_C))

    for l in range(DEPTH):
        last = l == DEPTH - 1
        w_in_re = _in_weight(w_in, l)
        pv = _pad_rows(jnp.stack([rwkv_w0[l], rwkv_a0[l], rwkv_k_k[l], rwkv_k_a[l], rwkv_r_k[l].reshape(GW),
                                  rep(ssm_dt_bias[l]), rep(ssm_A_log[l]), rep(ssm_D[l]), rep(mlstm_i_bias[l]),
                                  rep(mlstm_f_bias[l])]), 16)
        lora = jnp.stack([jnp.pad(w, ((r0, LANES - r0 - w.shape[0]), (0, 0)))
                          for r0, w in zip(LORA_ROWS, (rwkv_w_up[l], rwkv_a_up[l], rwkv_g_up[l]))])
        conv = _pad_rows(jnp.concatenate([ssm_conv_w[l], ssm_conv_b[l][None]], axis=0), 8)
        mu = rwkv_mu[l].reshape(1, RW_W)
        gp1 = _pad_rows(jnp.stack([_lane_row((GT_DT, ssm_dt_bias[l])), _lane_row((GT_DT, ssm_A_log[l]))]), 8)
        gp2 = _pad_rows(_lane_row((GT_I, mlstm_i_bias[l]), (GT_F, mlstm_f_bias[l]))[None], 8)
        pd = _pad_rows(jnp.stack([rep(ssm_D[l]), ssm_norm_g[l]]), 8)
        pn_h = _pad_rows(hgrn_norm_g[l][None], 8)
        pn_m = _pad_rows(mlstm_norm_g[l][None], 8)
        pq = _pad_rows(jnp.stack([rwkv_ln_g[l], rwkv_ln_b[l], hgrn_norm_g[l], ssm_norm_g[l], mlstm_norm_g[l]]), 8)
        w_out_b = w_out[l].astype(BF16)
        wq_b, wo_b = xattn_wq[l].astype(BF16), xattn_wo[l].astype(BF16)
        wkv_b = jnp.concatenate([xattn_wk[l], xattn_wv[l]], axis=1).astype(BF16)
        w1_b, w2_b = ff_w1[l].astype(BF16), ff_w2[l].astype(BF16)

        kv = _norm_matmul(mem, norm_mem_g[l], wkv_b)
        p_mem.append((kv[:, :D_MODEL].reshape(bp, N_MEM, XH, XHD), kv[:, D_MODEL:].reshape(bp, N_MEM, XH, XHD)))
        proj = _norm_matmul(hp, norm_mix_g[l], w_in_re)
        rw_scan, rw_post = _prep_rw_seq(proj, zeros(bp, SUBLANES, RW_W), mu, pv, lora, seg64, bp, tp)
        (m_h, m_s, m_m, o_rw), (st_h, st_s, st_c, st_n, st_m, st_r) = _mixers_chunk(
            proj, rw_scan, (hgrn_lb_logits, pn_h, tril_blk, ones_blk, seg64_b, seg64, conv, gp1, gp2, pd, pn_m, tril_c,
                            sel_c, tril_blk_h, ones_blk_h), l, bp, tp)
        proj3 = proj.reshape(bp, tp, IN_RE)
        diag_blocks = lambda s: jnp.stack(
            [s.reshape(bp, NH, HEAD_DIM, NH, HEAD_DIM)[:, h, :, h, :] for h in range(NH)], axis=1)
        p_states.append((diag_blocks(st_r), proj3[:, -1, RW_OFF:RW_OFF + RW_W], diag_blocks(st_h).transpose(0, 1, 3, 2),
                         st_s.transpose(0, 1, 3, 2), proj3[:, tp - 3:, SS_OFF + 256:SS_OFF + 1024],
                         st_c.transpose(0, 1, 3, 2), st_n[:, :NH], st_m[:, 0, :NH]))
        hp = _post_call(_post_seq_kernel, hp, (o_rw, rw_post, m_h, m_s, m_m), (pq, seg64, w_out_b))
        hp = _xattn_seq(hp, norm_x_g[l], wq_b, kv, wo_b, bp, tp)
        hp = _ffn(hp, norm_ff_g[l], w1_b, w2_b, final_norm_g, last)

        proj = _norm_matmul(hs, norm_mix_g[l], w_in_re)
        cb = state_ssm_conv[l]
        prep = _prep_step(proj, state_rwkv_shift[l], cb[:, 0], cb[:, 1], cb[:, 2],
                          (mu, pv, lora, seg64, hgrn_lb_logits, conv, gate_ex), l)
        o_parts, s_big, n_n, m_n = _mixers_stepv(
            prep[0::2], (state_rwkv_wkv, state_hgrn, state_ssm, state_mlstm_C, state_mlstm_n, state_mlstm_m), s_big, l)
        conv_new = jnp.concatenate([cb[:, 1:], proj[:, None, SS_OFF + 256:SS_OFF + 1024]], axis=1)
        s_small.append((proj[:, RW_OFF:RW_OFF + RW_W], conv_new, n_n, m_n))
        hs = _post_call(_post_step_kernel, hs, tuple(o_parts) + tuple(prep[1::2]), (pq, seg64, seg128, w_out_b))
        q = _norm_matmul(hs, norm_x_g[l], wq_b)
        att = _xattn_step(q, cache_mem_k, cache_mem_v, l)
        hs = _mm_res(att, wo_b, hs)
        hs = _ffn(hs, norm_ff_g[l], w1_b, w2_b, final_norm_g, last)

    y_prompt = hp.reshape(bp, tp, D_MODEL)
    y_sample = hs.reshape(bs, 1, D_MODEL)
    p_out = [jnp.stack(s) for s in zip(*p_states)]
    s_shift, s_conv, s_n, s_m = [jnp.stack(s) for s in zip(*s_small)]
    s_wkv, s_hgrn, s_ssm, s_c = s_big
    p_mem_k = jnp.stack([kv[0] for kv in p_mem])
    p_mem_v = jnp.stack([kv[1] for kv in p_mem])
    return (y_prompt, y_sample, *p_out, p_mem_k, p_mem_v, s_wkv, s_shift, s_hgrn, s_ssm, s_conv, s_c, s_n, s_m)
```

```python
import functools

import numpy as np
import jax
import jax.numpy as jnp
from jax import lax
from jax.experimental import pallas as pl
from jax.experimental.pallas import tpu as pltpu

F32 = jnp.float32
BF16 = jnp.bfloat16
HI = lax.Precision.HIGHEST

D_MODEL = 1024
DEPTH = 2
NH = 4
HEAD_DIM = 64
GW = 256
SSM_STATE = 128
N_MEM = 256
XH = 4
XHD = 256
NORM_EPS = 1e-6
RWKV_GN_EPS = 64e-5
LANES = 128
SUBLANES = 8
VMEM_LIMIT = 48 * 1024 * 1024
CHUNK = 128
SUB = 16
HGRN_SUB = 16

RW_OFF, RW_W = 0, 896
HG_OFF = 896
SS_OFF = 1920
G1_OFF = 2944
ML_OFF = 3072
G2_OFF = 3840
MO_OFF = 3968
IN_RE = 4224
GT_DT, GT_I, GT_F = 0, 0, 4
LORA_ROWS = (0, 32, 64)


def _in_weight_kernel(w_ref, o_ref):
    w = w_ref[...]
    lane = lax.broadcasted_iota(jnp.int32, (w.shape[0], LANES), 1)
    gate_tile = lambda c0, n: jnp.where(lane < n, w[:, c0:c0 + LANES], 0.0)
    o_ref[:, 0:G1_OFF] = w[:, 0:G1_OFF].astype(BF16)
    o_ref[:, G1_OFF:ML_OFF] = gate_tile(2944, NH).astype(BF16)
    o_ref[:, ML_OFF:G2_OFF] = w[:, 2948:3716].astype(BF16)
    o_ref[:, G2_OFF:MO_OFF] = gate_tile(3716, 2 * NH).astype(BF16)
    o_ref[:, MO_OFF:IN_RE] = w[:, 3724:3980].astype(BF16)


def _in_weight(w_in, layer):
    k, n = w_in.shape[1:]
    tr = 256
    return pl.pallas_call(
        _in_weight_kernel,
        grid=(k // tr,),
        in_specs=[pl.BlockSpec((None, tr, n), lambda i: (layer, i, 0))],
        out_specs=pl.BlockSpec((tr, IN_RE), lambda i: (i, 0)),
        out_shape=jax.ShapeDtypeStruct((k, IN_RE), BF16),
        compiler_params=_cparams(("parallel",)),
    )(w_in)


def _seg_matrix(width):
    i = np.arange(GW)
    return jnp.asarray((i[:, None] // width == i[None, :] // width).astype(np.float32))


def _gate_expand():
    e = np.zeros((3, LANES, GW), np.float32)
    for s, off in enumerate((GT_DT, GT_I, GT_F)):
        for h in range(NH):
            e[s, off + h, h * HEAD_DIM:(h + 1) * HEAD_DIM] = 1.0
    return jnp.asarray(e)


def _lane_row(*pieces):
    z = jnp.zeros((LANES,), F32)
    for off, val in pieces:
        z = z.at[off:off + val.shape[0]].set(val)
    return z


def _pad_rows(a, n):
    return jnp.pad(a, ((0, n - a.shape[0]), (0, 0)))


def _cparams(sem):
    return pltpu.CompilerParams(dimension_semantics=sem, vmem_limit_bytes=VMEM_LIMIT)


def _softplus(x):
    return jnp.maximum(x, 0.0) + jnp.log(1.0 + jnp.exp(-jnp.abs(x)))


def _sigmoid(x):
    return jax.nn.sigmoid(x)


def _rms(x, g):
    ms = jnp.mean(x * x, axis=-1, keepdims=True)
    return x * lax.rsqrt(ms + NORM_EPS) * g


def _dot(a, b):
    return jnp.dot(a.astype(BF16), b.astype(BF16), preferred_element_type=F32)


def _dot_nt(a, b):
    return lax.dot_general(a.astype(BF16), b.astype(BF16), (((1,), (1,)), ((), ())), preferred_element_type=F32)


def _dot_tn(a, b):
    return lax.dot_general(a.astype(BF16), b.astype(BF16), (((0,), (0,)), ((), ())), preferred_element_type=F32)


def _dot_hi(a, b):
    return jnp.dot(a, b, precision=HI, preferred_element_type=F32)


def _norm_matmul_kernel(x_ref, g_ref, w_ref, o_ref, xn_ref):
    @pl.when(pl.program_id(1) == 0)
    def _():
        xn_ref[...] = _rms(x_ref[...], g_ref[...]).astype(BF16)

    o_ref[...] = jnp.dot(xn_ref[...], w_ref[...], preferred_element_type=F32)


def _row_tile(m):
    return min(m, 1024)


def _norm_matmul(x, g, w):
    m, k = x.shape
    n = w.shape[1]
    tm = _row_tile(m)
    tn = next(c for c in (1536, 1408, 1024, 512, n) if n % c == 0)
    return pl.pallas_call(
        _norm_matmul_kernel,
        grid=(m // tm, n // tn),
        in_specs=[pl.BlockSpec((tm, k), lambda i, j: (i, 0)),
                  pl.BlockSpec((1, k), lambda i, j: (0, 0)),
                  pl.BlockSpec((k, tn), lambda i, j: (0, j))],
        out_specs=pl.BlockSpec((tm, tn), lambda i, j: (i, j)),
        out_shape=jax.ShapeDtypeStruct((m, n), F32),
        scratch_shapes=[pltpu.VMEM((tm, k), BF16)],
        compiler_params=_cparams(("parallel", "arbitrary")),
    )(x, g.reshape(1, k), w)


def _mm_res_kernel(a_ref, w_ref, res_ref, o_ref):
    o_ref[...] = res_ref[...] + jnp.dot(a_ref[...].astype(BF16), w_ref[...], preferred_element_type=F32)


def _mm_res(a, w, res):
    m, k = a.shape
    n = w.shape[1]
    tm = min(m, 512)
    return pl.pallas_call(
        _mm_res_kernel,
        grid=(m // tm,),
        in_specs=[pl.BlockSpec((tm, k), lambda i: (i, 0)),
                  pl.BlockSpec((k, n), lambda i: (0, 0)),
                  pl.BlockSpec((tm, n), lambda i: (i, 0))],
        out_specs=pl.BlockSpec((tm, n), lambda i: (i, 0)),
        out_shape=jax.ShapeDtypeStruct((m, n), F32),
        compiler_params=_cparams(("parallel",)),
    )(a, w, res)


def _ffn_kernel(final, x_ref, g_ref, w1_ref, w2_ref, fg_ref, o_ref, xn_ref, acc_ref):
    j = pl.program_id(1)

    @pl.when(j == 0)
    def _():
        xn_ref[...] = _rms(x_ref[...], g_ref[...]).astype(BF16)
        acc_ref[...] = jnp.zeros_like(acc_ref)

    h = jnp.dot(xn_ref[...], w1_ref[...], preferred_element_type=F32)
    h = jnp.square(jnp.maximum(h, 0.0))
    acc_ref[...] += jnp.dot(h.astype(BF16), w2_ref[...], preferred_element_type=F32)

    @pl.when(j == pl.num_programs(1) - 1)
    def _():
        y = x_ref[...] + acc_ref[...]
        o_ref[...] = _rms(y, fg_ref[...]) if final else y


def _ffn(x, g, w1, w2, final_g, final):
    m, k = x.shape
    ff = w1.shape[1]
    tm = _row_tile(m)
    tf = 1024
    return pl.pallas_call(
        functools.partial(_ffn_kernel, final),
        grid=(m // tm, ff // tf),
        in_specs=[pl.BlockSpec((tm, k), lambda i, j: (i, 0)),
                  pl.BlockSpec((1, k), lambda i, j: (0, 0)),
                  pl.BlockSpec((k, tf), lambda i, j: (0, j)),
                  pl.BlockSpec((tf, k), lambda i, j: (j, 0)),
                  pl.BlockSpec((1, k), lambda i, j: (0, 0))],
        out_specs=pl.BlockSpec((tm, k), lambda i, j: (i, 0)),
        out_shape=jax.ShapeDtypeStruct((m, k), F32),
        scratch_shapes=[pltpu.VMEM((tm, k), BF16), pltpu.VMEM((tm, k), F32)],
        compiler_params=_cparams(("parallel", "arbitrary")),
    )(x, g.reshape(1, k), w1, w2, final_g.reshape(1, k))


def _shift_rows(x, tail, j):
    xr = pltpu.roll(x, j, 0)
    tr = pltpu.roll(tail, j, 0)
    rows = lax.broadcasted_iota(jnp.int32, tr.shape, 0)
    first = jnp.where(rows < j, tr, xr[0:SUBLANES])
    return jnp.concatenate([first, xr[SUBLANES:]], axis=0)


def _hgrn_lower_bound(logit_ref, layer):
    lg = logit_ref[...]
    e = jnp.exp(lg - jnp.max(lg, axis=0, keepdims=True))
    prb = e / jnp.sum(e, axis=0, keepdims=True)
    return jnp.sum(prb[0:layer + 1], axis=0, keepdims=True) - prb[0:1]


def _conv_silu(u, c3, c2, c1, conv_ref):
    cw = conv_ref[...]
    conv = cw[4:5] + c3 * cw[0:1] + c2 * cw[1:2] + c1 * cw[2:3] + u * cw[3:4]
    return conv * _sigmoid(conv)


def _prep_rwkv(pr, prev_rw, mu_ref, pv_ref, lora_ref, seg_ref, rw_scan, rw_post, log_decay):
    pv = pv_ref[...]
    row = lambda i: pv[i:i + 1, :]
    seg = seg_ref[...]
    pm = pr + (prev_rw - pr) * mu_ref[...]
    r, k, v, low = pm[:, 0:256], pm[:, 256:512], pm[:, 512:768], pm[:, 768:896]
    w_log = -_softplus(-(row(0) + _dot_hi(jnp.tanh(low), lora_ref[0]))) - 0.5
    a = _sigmoid(row(1) + _dot_hi(low, lora_ref[1]))
    kk = k * row(2)
    kk = kk / jnp.maximum(jnp.sqrt(_dot_hi(kk * kk, seg)), 1e-12)
    k2 = k * (1.0 + (a - 1.0) * row(3))
    rw_scan[:, 0:256] = r
    rw_scan[:, 256:512] = k2
    rw_scan[:, 512:768] = v
    rw_scan[:, 768:1024] = -jnp.exp(w_log) if log_decay else jnp.exp(-jnp.exp(w_log))
    rw_scan[:, 1024:1280] = kk
    rw_scan[:, 1280:1536] = kk * a
    rw_post[:, 0:256] = _dot_hi(r * k2 * row(4), seg) * v
    rw_post[:, 256:512] = _dot_hi(_sigmoid(low), lora_ref[2])


def _prep_rw_seq_kernel(p_ref, prev8_ref, tail0_ref, mu_ref, pv_ref, lora_ref, seg_ref, rw_scan, rw_post):
    tail = jnp.where(pl.program_id(1) == 0, tail0_ref[0], prev8_ref[...])
    pr = p_ref[...]
    _prep_rwkv(pr, _shift_rows(pr, tail, 1), mu_ref, pv_ref, lora_ref, seg_ref, rw_scan, rw_post, True)


def _prep_rw_seq(proj, tail0, mu, pv, lora, seg, bsz, t):
    tt = 512
    nt = t // tt
    m = bsz * t
    c2 = lambda b, i: (0, 0)
    return pl.pallas_call(
        _prep_rw_seq_kernel,
        grid=(bsz, nt),
        in_specs=[pl.BlockSpec((tt, RW_W), lambda b, i: (b * nt + i, 0)),
                  pl.BlockSpec((SUBLANES, RW_W), lambda b, i: (jnp.maximum((b * nt + i) * (tt // SUBLANES) - 1, 0), 0)),
                  pl.BlockSpec((1, SUBLANES, RW_W), lambda b, i: (b, 0, 0)),
                  pl.BlockSpec((1, RW_W), c2), pl.BlockSpec((16, GW), c2),
                  pl.BlockSpec((3, LANES, GW), lambda b, i: (0, 0, 0)), pl.BlockSpec((GW, GW), c2)],
        out_specs=[pl.BlockSpec((tt, 1536), lambda b, i: (b * nt + i, 0)),
                   pl.BlockSpec((tt, 512), lambda b, i: (b * nt + i, 0))],
        out_shape=[jax.ShapeDtypeStruct((m, 1536), F32), jax.ShapeDtypeStruct((m, 512), F32)],
        compiler_params=_cparams(("parallel", "parallel")),
    )(proj, proj, tail0, mu, pv, lora, seg)


PREP_SCAN_WIDTHS = (1536, 1024, 1280)
PREP_ROW_WIDTHS = (512, 256, 1024, 512, 256)


def _prep_step_kernel(layer, p_ref, prev_ref, c3_ref, c2_ref, c1_ref, mu_ref, pv_ref, lora_ref, seg_ref, logit_ref,
                      conv_ref, ex_ref, rw_t, hg_t, ml_t, rw_post, hg_post, ss_scan, ss_post, ml_post, rw_scan,
                      hg_scan, ml_scan):
    pv = pv_ref[...]
    row = lambda i: pv[i:i + 1, :]
    _prep_rwkv(p_ref[:, RW_OFF:RW_OFF + RW_W], prev_ref[...], mu_ref, pv_ref, lora_ref, seg_ref, rw_scan, rw_post,
               False)
    g1 = p_ref[:, G1_OFF:G1_OFF + LANES]
    g2 = p_ref[:, G2_OFF:G2_OFF + LANES]
    dt_pre, i_pre, f_pre_m = _dot_hi(g1, ex_ref[0]), _dot_hi(g2, ex_ref[1]), _dot_hi(g2, ex_ref[2])

    lb = _hgrn_lower_bound(logit_ref, layer)
    q_pre = p_ref[:, HG_OFF:HG_OFF + 256]
    f_pre = p_ref[:, HG_OFF + 256:HG_OFF + 512]
    gg = p_ref[:, HG_OFF + 768:HG_OFF + 1024]
    hg_scan[:, 0:256] = q_pre * _sigmoid(q_pre)
    hg_scan[:, 256:512] = lb + (1.0 - lb) * _sigmoid(f_pre)
    hg_scan[:, 512:768] = (1.0 - lb) * _sigmoid(-f_pre)
    hg_scan[:, 768:1024] = p_ref[:, HG_OFF + 512:HG_OFF + 768]
    hg_post[...] = gg * _sigmoid(gg)

    z = p_ref[:, SS_OFF:SS_OFF + 256]
    xbc = _conv_silu(p_ref[:, SS_OFF + 256:SS_OFF + 1024], c3_ref[...], c2_ref[...], c1_ref[...], conv_ref)
    xs = xbc[:, 0:256]
    dt = _softplus(dt_pre + row(5))
    ss_scan[:, 0:256] = xs * dt
    ss_scan[:, 256:512] = jnp.exp(dt * (-jnp.exp(row(6))))
    ss_scan[:, 512:1024] = xbc[:, 256:768]
    ss_post[:, 0:256] = xs * row(7)
    ss_post[:, 256:512] = z * _sigmoid(z)

    ml_scan[:, 0:256] = p_ref[:, ML_OFF:ML_OFF + 256]
    ml_scan[:, 256:512] = p_ref[:, ML_OFF + 256:ML_OFF + 512] * (HEAD_DIM ** -0.5)
    ml_scan[:, 512:768] = p_ref[:, ML_OFF + 512:ML_OFF + 768]
    ml_scan[:, 768:1024] = i_pre + row(8)
    ml_scan[:, 1024:1280] = -_softplus(-(f_pre_m + row(9)))
    ml_post[...] = _sigmoid(p_ref[:, MO_OFF:MO_OFF + 256])
    rw_t[...] = rw_scan[...].T
    hg_t[...] = hg_scan[...].T
    ml_t[...] = ml_scan[...].T


def _prep_step(proj, prev_rw, c3, c2, c1, params, layer):
    m = proj.shape[0]
    full = lambda a: pl.BlockSpec(a.shape, lambda i: (0,) * a.ndim)
    ins = (proj, prev_rw, c3, c2, c1) + tuple(params)
    shapes = [(w, m) for w in PREP_SCAN_WIDTHS] + [(m, w) for w in PREP_ROW_WIDTHS]
    return pl.pallas_call(
        functools.partial(_prep_step_kernel, layer),
        grid=(1,),
        in_specs=[full(a) for a in ins],
        out_specs=[pl.BlockSpec(s, lambda i: (0, 0)) for s in shapes],
        out_shape=[jax.ShapeDtypeStruct(s, F32) for s in shapes],
        scratch_shapes=[pltpu.VMEM((m, w), F32) for w in PREP_SCAN_WIDTHS],
        compiler_params=_cparams(("arbitrary",)),
    )(*ins)


def _ssm_chunk_body(ss_ref, gt_ref, conv_ref, gp_ref, pd_ref, tril_ref, o_ref, ht_ref, tail_ref):
    @pl.when(pl.program_id(1) == 0)
    def _():
        ht_ref[...] = jnp.zeros_like(ht_ref)
        tail_ref[...] = jnp.zeros_like(tail_ref)

    yield
    L = ss_ref.shape[0]
    u = ss_ref[:, 256:1024]
    tail = tail_ref[...]
    c1, c2, c3 = (_shift_rows(u, tail, j) for j in (1, 2, 3))
    tail_ref[...] = u[L - SUBLANES:L]
    xbc = _conv_silu(u, c3, c2, c1, conv_ref)
    z = ss_ref[:, 0:256]
    zs = z * _sigmoid(z)
    dt = _softplus(gt_ref[...] + gp_ref[0:1])
    la = dt * (-jnp.exp(gp_ref[1:2]))
    tril = tril_ref[...]
    mask = tril > 0.5
    cum = _dot_hi(tril, la)
    yield
    cum_t = cum.T
    pd = pd_ref[...]
    hsl = lambda h: slice(h * HEAD_DIM, (h + 1) * HEAD_DIM)
    bgs = [xbc[:, 256 + SSM_STATE * g:256 + SSM_STATE * (g + 1)] for g in range(2)]
    cgs = [xbc[:, 512 + SSM_STATE * g:512 + SSM_STATE * (g + 1)] for g in range(2)]
    scores = [_dot_nt(cgs[g], bgs[g]) for g in range(2)]
    hts = [ht_ref[0, h] for h in range(NH)]
    inter = [_dot(cgs[h // 2], hts[h]) for h in range(NH)]
    bg_t = [bgs[g].T for g in range(2)]
    ccs = [cum[:, GT_DT + h:GT_DT + h + 1] for h in range(NH)]
    dtc = [dt[:, GT_DT + h:GT_DT + h + 1] for h in range(NH)]
    lms = []
    for h in range(NH):
        cr = cum_t[GT_DT + h:GT_DT + h + 1, :]
        lms.append(jnp.where(mask, jnp.exp(jnp.where(mask, ccs[h] - cr, 0.0)), 0.0))
    yield
    ydots = [_dot(scores[h // 2] * lms[h], xbc[:, hsl(h)] * dtc[h]) for h in range(NH)]
    cends = [ccs[h][L - 1:L, :] for h in range(NH)]
    hdots = [_dot(bg_t[h // 2], xbc[:, hsl(h)] * (jnp.exp(cends[h] - ccs[h]) * dtc[h])) for h in range(NH)]
    yield
    ys = []
    for h in range(NH):
        y = ydots[h] + inter[h] * jnp.exp(ccs[h])
        ys.append((y + xbc[:, hsl(h)] * pd[0:1, hsl(h)]) * zs[:, hsl(h)])
        ht_ref[0, h] = jnp.exp(cends[h]) * hts[h] + hdots[h]
    for g in range(2):
        y0, y1 = ys[2 * g], ys[2 * g + 1]
        ms = jnp.sum(y0 * y0, axis=-1, keepdims=True) + jnp.sum(y1 * y1, axis=-1, keepdims=True)
        scale = lax.rsqrt(ms * (1.0 / (2 * HEAD_DIM)) + NORM_EPS)
        for h in (2 * g, 2 * g + 1):
            o_ref[:, hsl(h)] = ys[h] * scale * pd[1:2, hsl(h)]


def _mlstm_chunk_body(qkv_ref, og_ref, gt_ref, gp_ref, pn_ref, tril_ref, o_ref, ct_ref, n_ref, m_ref):
    @pl.when(pl.program_id(1) == 0)
    def _():
        ct_ref[...] = jnp.zeros_like(ct_ref)
        n_ref[...] = jnp.zeros_like(n_ref)
        m_ref[...] = jnp.zeros_like(m_ref)

    yield
    L = qkv_ref.shape[0]
    x = gt_ref[...] + gp_ref[0:1]
    lf = -_softplus(-x)
    tril = tril_ref[...]
    mask = tril > 0.5
    b = _dot_hi(tril, lf)
    yield
    b_t = b.T
    x_t = x.T
    m_row = m_ref[0]
    m_new_row = m_row
    lane = lax.broadcasted_iota(jnp.int32, m_row.shape, 1)
    pn = pn_ref[...]
    hsl = lambda h, base=0: slice(base + h * HEAD_DIM, base + (h + 1) * HEAD_DIM)
    for h in range(NH):
        b_col, b_row = b[:, GT_F + h:GT_F + h + 1], b_t[GT_F + h:GT_F + h + 1, :]
        li_col, li_row = x[:, GT_I + h:GT_I + h + 1], x_t[GT_I + h:GT_I + h + 1, :]
        m_prev = m_row[:, h:h + 1]
        log_inter = b_col + m_prev
        log_intra = jnp.where(mask, b_col - b_row + li_row, -jnp.inf)
        m_t = jnp.maximum(log_inter, jnp.max(log_intra, axis=-1, keepdims=True))
        w_inter = jnp.exp(log_inter - m_t)
        qh = qkv_ref[:, hsl(h)]
        kh = qkv_ref[:, hsl(h, 256)] * (HEAD_DIM ** -0.5)
        vh = qkv_ref[:, hsl(h, 512)]
        ct = ct_ref[0, h]
        qk_raw = _dot_nt(qh, kh)
        q_ct = _dot(qh, ct)
        yield
        qk = qk_raw * jnp.exp(log_intra - m_t)
        n_h = n_ref[0, h:h + 1, :]
        m_end = m_t[L - 1:L, :]
        b_end = b_col[L - 1:L, :]
        g_prev = jnp.exp(b_end + m_prev - m_end)
        kg = kh * jnp.exp(b_end - b_col + li_col - m_end)
        qk_v = _dot(qk, vh)
        kg_v = _dot_tn(kg, vh)
        yield
        num = qk_v + w_inter * q_ct
        den = jnp.sum(qk, axis=-1, keepdims=True) + w_inter * jnp.sum(qh * n_h, axis=-1, keepdims=True)
        hout = num / jnp.maximum(jnp.abs(den), jnp.exp(-m_t))
        ct_ref[0, h] = g_prev * ct + kg_v
        n_ref[0, h:h + 1, :] = g_prev * n_h + jnp.sum(kg, axis=0, keepdims=True)
        m_new_row = jnp.where(lane == h, m_end, m_new_row)
        hn = hout * lax.rsqrt(jnp.mean(hout * hout, axis=-1, keepdims=True) + NORM_EPS)
        o_ref[:, hsl(h)] = hn * pn[0:1, hsl(h)] * _sigmoid(og_ref[:, hsl(h)])
    m_ref[0] = m_new_row


def _hgrn_chunk_body(layer, hg_ref, logit_ref, pn_ref, tril_ref, ones_ref, seg_ref, bd_ref, o_ref, st_ref, q_scr,
                     b_scr, i_scr):
    @pl.when(pl.program_id(1) == 0)
    def _():
        st_ref[...] = jnp.zeros_like(st_ref)

    yield
    lb = _hgrn_lower_bound(logit_ref, layer)
    seg = seg_ref[...]
    bd = bd_ref[...]
    pn = pn_ref[...]
    sub = HGRN_SUB
    srow = lax.broadcasted_iota(jnp.int32, (sub, GW), 0)
    L = hg_ref.shape[0]

    q_pre, f_pre, v, gg = hg_ref[:, 0:256], hg_ref[:, 256:512], hg_ref[:, 512:768], hg_ref[:, 768:1024]
    q = q_pre * _sigmoid(q_pre)
    k = (1.0 - lb) * _sigmoid(-f_pre)
    lf = jnp.log(lb + (1.0 - lb) * _sigmoid(f_pre))
    b = _dot_hi(tril_ref[...], lf)
    b_tot = _dot_hi(ones_ref[...], lf)
    yield
    q_scr[...] = q
    b_scr[...] = b
    quarter = L // 4
    for part in range(4):
        tiles = []
        for t in range(part * quarter, (part + 1) * quarter):
            rows = slice((t // sub) * sub, (t // sub + 1) * sub)
            live = srow <= (t % sub)
            dec = jnp.where(live, jnp.exp(jnp.where(live, b_scr[t:t + 1, :] - b[rows], 0.0)), 0.0)
            tiles.append(dec * q_scr[t:t + 1, :] * k[rows])
        a_b = jnp.dot(jnp.concatenate(tiles, axis=0).astype(BF16), seg, preferred_element_type=F32)
        yield
        for j in range(quarter):
            t = part * quarter + j
            rows = slice((t // sub) * sub, (t // sub + 1) * sub)
            i_scr[t:t + 1, :] = jnp.sum(a_b[j * sub:(j + 1) * sub, :] * v[rows], axis=0, keepdims=True)
    qx = q * jnp.exp(b)
    kx = k * jnp.exp(b_tot - b)
    p_end = jnp.exp(b_tot)
    hsl = lambda h: slice(h * HEAD_DIM, (h + 1) * HEAD_DIM)
    sts = [st_ref[0, h] for h in range(NH)]
    v_t = v.T
    trow = lax.broadcasted_iota(jnp.int32, (L, GW), 0)
    for c in range(L // sub):
        rows = slice(c * sub, (c + 1) * sub)
        km = jnp.where((trow >= c * sub) & (trow < (c + 1) * sub), kx, 0.0)
        upds = [_dot(v_t[hsl(h), :], km[:, hsl(h)]) for h in range(NH)]
        inters = [_dot_nt(qx[rows, hsl(h)], sts[h]) for h in range(NH)]
        yield
        for h in range(NH):
            i_scr[rows, hsl(h)] = inters[h] + i_scr[rows, hsl(h)]
            sts[h] = sts[h] * p_end[c * sub:c * sub + 1, hsl(h)] + upds[h]
    for h in range(NH):
        st_ref[0, h] = sts[h]
    o = i_scr[...]
    gate = gg * _sigmoid(gg)
    for h in range(NH):
        sl = slice(h * HEAD_DIM, (h + 1) * HEAD_DIM)
        oh = o[:, sl]
        o_ref[:, sl] = oh * lax.rsqrt(jnp.mean(oh * oh, axis=-1, keepdims=True) + NORM_EPS) * pn[0:1, sl] * gate[:, sl]


def _rwkv_chunk_body(rw_ref, tril_ref, ones_ref, sel_ref, bd_ref, o_ref, st_ref, kx_scr):
    @pl.when(pl.program_id(1) == 0)
    def _():
        st_ref[...] = jnp.zeros_like(st_ref)

    yield
    L = rw_ref.shape[0]
    ns = L // SUB
    xr = ns * NH * SUB
    bd = bd_ref[...]
    csum = lambda a: jnp.sum(a, axis=0, keepdims=True)
    xi = lax.broadcasted_iota(jnp.int32, (xr, GW), 0)
    xj = lax.broadcasted_iota(jnp.int32, (xr, GW), 1)
    head_rows = (((xi // SUB) & (NH - 1)) == xj // HEAD_DIM).astype(F32)
    ti = lax.broadcasted_iota(jnp.int32, (L, xr), 0)
    tj = lax.broadcasted_iota(jnp.int32, (L, xr), 1)
    same = (tj // (NH * SUB)) == (ti // SUB)
    lt = same & ((tj & (SUB - 1)) < (ti & (SUB - 1)))
    le = same & ((tj & (SUB - 1)) <= (ti & (SUB - 1)))

    def expand(x):
        parts = []
        for c in range(ns):
            parts += [x[c * SUB:(c + 1) * SUB]] * NH
        return jnp.concatenate(parts, axis=0) * head_rows

    r, k, v = rw_ref[:, 0:256], rw_ref[:, 256:512], rw_ref[:, 512:768]
    lw, kk, ka = rw_ref[:, 768:1024], rw_ref[:, 1024:1280], rw_ref[:, 1280:1536]
    cs = _dot_hi(tril_ref[...], lw)
    c_tot = _dot_hi(ones_ref[...], lw)
    yield
    e_inv = jnp.exp(-cs)
    kkx = kk * jnp.exp(cs - lw)
    rx = r * jnp.exp(cs)
    kae, ke = ka * e_inv, k * e_inv
    e_end = jnp.exp(c_tot - cs)
    a_hat, k_hat = ka * e_end, k * e_end
    p_end = jnp.exp(c_tot)
    kx_scr[...] = kkx
    tiles = [kx_scr[t:t + 1, :] * kae[(t // SUB) * SUB:(t // SUB + 1) * SUB] for t in range(L)]
    maa = jnp.dot(jnp.concatenate(tiles, axis=0).astype(BF16), sel_ref[...], preferred_element_type=F32)
    v_x = expand(v)
    kkx_x = expand(kkx)
    mk = _dot_nt(jnp.concatenate([kkx, rx], axis=0), expand(ke))
    ra = _dot_nt(rx, expand(kae))
    yield
    m_ak = jnp.where(lt, mk[0:L], 0.0)
    a_rk = jnp.where(le, mk[L:2 * L], 0.0)
    a_ra = jnp.where(le, ra, 0.0)
    mv_x = expand(_dot(m_ak, v_x))
    rk_v = _dot(a_rk, v_x)
    yield
    crow = lax.broadcasted_iota(jnp.int32, (SUB, NH * SUB), 0)
    ccol = lax.broadcasted_iota(jnp.int32, (SUB, NH * SUB), 1) & (SUB - 1)
    eye_c = (crow == ccol).astype(F32)
    gs, us = [], []
    for c in range(ns):
        t_arr = jnp.zeros((SUB, NH * SUB), F32)
        for t in range(SUB):
            i = c * SUB + t
            row_t = eye_c[t:t + 1, :] - csum(maa[i * SUB:(i + 1) * SUB, :] * t_arr)
            t_arr = jnp.where(crow == t, row_t, t_arr)
        xs = slice(c * NH * SUB, (c + 1) * NH * SUB)
        gs.append(_dot(t_arr, kkx_x[xs]))
        us.append(_dot(t_arr, mv_x[xs]))
        if c % 2 == 1:
            yield
    g_all, u_all = jnp.concatenate(gs, axis=0), jnp.concatenate(us, axis=0)
    ra_g = _dot(a_ra, expand(g_all))
    ra_u = _dot(a_ra, expand(u_all))
    phis = [_dot_tn(g_all[c * SUB:(c + 1) * SUB], a_hat[c * SUB:(c + 1) * SUB]) for c in range(ns)]
    psis = [_dot_tn(jnp.concatenate([v[c * SUB:(c + 1) * SUB], -u_all[c * SUB:(c + 1) * SUB]], axis=0),
                    jnp.concatenate([k_hat[c * SUB:(c + 1) * SUB], a_hat[c * SUB:(c + 1) * SUB]], axis=0))
            for c in range(ns)]
    yield
    omega = rx - ra_g
    o_ind = rk_v - ra_u
    st = st_ref[0]
    for c in range(ns):
        rows = slice(c * SUB, (c + 1) * SUB)
        o_dot = _dot_nt(omega[rows], st)
        st_phi = _dot(st, phis[c])
        yield
        o_ref[rows, :] = o_dot + o_ind[rows]
        st = st * p_end[c * SUB:c * SUB + 1, :] + (psis[c] - st_phi) * bd
    st_ref[0] = st


def _mixers_chunk_kernel(layer, p_ref, rw_ref, logit_ref, pnh_ref, trilb_ref, onesb_ref, segb_ref, bd_ref, conv_ref,
                         gp1_ref, gp2_ref, pd_ref, pnm_ref, trilc_ref, sel_ref, trilh_ref, onesh_ref, ohg_ref, oss_ref,
                         oml_ref, orw_ref, sth_ref, ht_ref, ct_ref, n_ref, m_ref, str_ref, q_scr, b_scr, i_scr, tail_scr,
                         kx_scr):
    cols = lambda off, w: p_ref.at[:, off:off + w]
    bodies = [
        _rwkv_chunk_body(rw_ref, trilb_ref, onesb_ref, sel_ref, bd_ref, orw_ref, str_ref, kx_scr),
        _hgrn_chunk_body(layer, cols(HG_OFF, 1024), logit_ref, pnh_ref, trilh_ref, onesh_ref, segb_ref, bd_ref, ohg_ref,
                         sth_ref, q_scr, b_scr, i_scr),
        _mlstm_chunk_body(cols(ML_OFF, 768), cols(MO_OFF, GW), cols(G2_OFF, LANES), gp2_ref, pnm_ref, trilc_ref,
                          oml_ref, ct_ref, n_ref, m_ref),
        _ssm_chunk_body(cols(SS_OFF, 1024), cols(G1_OFF, LANES), conv_ref, gp1_ref, pd_ref, trilc_ref, oss_ref, ht_ref,
                        tail_scr),
    ]
    while bodies:
        for body in list(bodies):
            try:
                next(body)
            except StopIteration:
                bodies.remove(body)


MIXER_STATE_SHAPES = ((NH, HEAD_DIM, HEAD_DIM), (NH, SSM_STATE, HEAD_DIM), (NH, HEAD_DIM, HEAD_DIM), (SUBLANES, HEAD_DIM),
                      (1, LANES), (GW, GW))


def _mixers_chunk(proj, rw_scan, consts, layer, bsz, t):
    nt = t // CHUNK
    m = bsz * t
    rows = lambda w: pl.BlockSpec((CHUNK, w), lambda b, i: (b * nt + i, 0))
    const = lambda a: pl.BlockSpec(a.shape, lambda b, i: (0,) * a.ndim)
    state = lambda s: pl.BlockSpec((1,) + s, lambda b, i: (b,) + (0,) * len(s))
    big = pltpu.VMEM((CHUNK, GW), F32)
    outs = pl.pallas_call(
        functools.partial(_mixers_chunk_kernel, layer),
        grid=(bsz, nt),
        in_specs=[rows(IN_RE), rows(1536)] + [const(a) for a in consts],
        out_specs=[rows(GW)] * 4 + [state(s) for s in MIXER_STATE_SHAPES],
        out_shape=[jax.ShapeDtypeStruct((m, GW), F32)] * 4
        + [jax.ShapeDtypeStruct((bsz,) + s, F32) for s in MIXER_STATE_SHAPES],
        scratch_shapes=[big, big, big, pltpu.VMEM((SUBLANES, 768), F32), big],
        compiler_params=_cparams(("parallel", "arbitrary")),
    )(proj, rw_scan, *consts)
    return outs[:4], outs[4:]


def _segmean(a, s, n):
    return _dot_hi(a, s) * (1.0 / n)


def _rwkv_post(o, rwp_ref, pq, s64):
    d = o - _segmean(o, s64, HEAD_DIM)
    var = _segmean(d * d, s64, HEAD_DIM)
    return (d * lax.rsqrt(var + RWKV_GN_EPS) * pq[0:1] + pq[1:2] + rwp_ref[:, 0:256]) * rwp_ref[:, 256:512]


def _out_proj(x, parts, w_ref):
    acc = x
    for i, part in enumerate(parts):
        acc = acc + jnp.dot(part.astype(BF16), w_ref[i * GW:(i + 1) * GW, :], preferred_element_type=F32)
    return acc


def _post_step_kernel(x_ref, orw_ref, ohg_ref, oss_ref, oml_ref, rwp_ref, hgp_ref, ssp_ref, mlp_ref, pq_ref, seg64_ref,
                      seg128_ref, w_ref, o_ref):
    pq = pq_ref[...]
    s64 = seg64_ref[...]
    m_r = _rwkv_post(orw_ref[...].T, rwp_ref, pq, s64)
    o = ohg_ref[...].T
    m_h = o * lax.rsqrt(_segmean(o * o, s64, HEAD_DIM) + NORM_EPS) * pq[2:3] * hgp_ref[...]
    y = (oss_ref[...] + ssp_ref[:, 0:256]) * ssp_ref[:, 256:512]
    m_s = y * lax.rsqrt(_segmean(y * y, seg128_ref[...], 128) + NORM_EPS) * pq[3:4]
    o = oml_ref[...].T
    m_m = o * lax.rsqrt(_segmean(o * o, s64, HEAD_DIM) + NORM_EPS) * pq[4:5] * mlp_ref[...]
    o_ref[...] = _out_proj(x_ref[...], (m_r, m_h, m_s, m_m), w_ref)


def _post_seq_kernel(x_ref, orw_ref, rwp_ref, mh_ref, ms_ref, mm_ref, pq_ref, seg64_ref, w_ref, o_ref):
    m_r = _rwkv_post(orw_ref[...], rwp_ref, pq_ref[...], seg64_ref[...])
    o_ref[...] = _out_proj(x_ref[...], (m_r, mh_ref[...], ms_ref[...], mm_ref[...]), w_ref)


def _post_step(x, ins):
    full = lambda a: pl.BlockSpec(a.shape, lambda i: (0,) * a.ndim)
    return pl.pallas_call(
        _post_step_kernel,
        grid=(1,),
        in_specs=[full(a) for a in (x,) + tuple(ins)],
        out_specs=full(x),
        out_shape=jax.ShapeDtypeStruct(x.shape, F32),
        compiler_params=_cparams(("arbitrary",)),
    )(x, *ins)


def _post_call(kernel_fn, x, row_ins, consts):
    m = x.shape[0]
    tm = min(m, 512)
    rows = lambda a: pl.BlockSpec((tm, a.shape[1]), lambda i: (i, 0))
    const = lambda a: pl.BlockSpec(a.shape, lambda i: (0, 0))
    return pl.pallas_call(
        kernel_fn,
        grid=(m // tm,),
        in_specs=[rows(a) for a in (x,) + tuple(row_ins)] + [const(a) for a in consts],
        out_specs=rows(x),
        out_shape=jax.ShapeDtypeStruct((m, D_MODEL), F32),
        compiler_params=_cparams(("parallel",)),
    )(x, *row_ins, *consts)


def _xattn_seq_kernel(x_ref, g_ref, wq_ref, k_ref, v_ref, wo_ref, o_ref):
    x = x_ref[...]
    q = jnp.dot(_rms(x, g_ref[...]).astype(BF16), wq_ref[...], preferred_element_type=F32)
    acc = x
    for h in range(XH):
        sl = slice(h * XHD, (h + 1) * XHD)
        s = _dot_nt(q[:, sl], k_ref[:, sl]) * (XHD ** -0.5)
        e = jnp.exp(s - jnp.max(s, axis=-1, keepdims=True))
        pr = e / jnp.sum(e, axis=-1, keepdims=True)
        acc = acc + _dot(_dot(pr, v_ref[:, sl]), wo_ref[sl, :])
    o_ref[...] = acc


def _xattn_seq(x, g, wq, kv, wo, bsz, t):
    tq = 512
    nt = t // tq
    const = lambda a: pl.BlockSpec(a.shape, lambda b, i: (0, 0))
    g2 = g.reshape(1, D_MODEL)
    return pl.pallas_call(
        _xattn_seq_kernel,
        grid=(bsz, nt),
        in_specs=[pl.BlockSpec((tq, D_MODEL), lambda b, i: (b * nt + i, 0)), const(g2), const(wq),
                  pl.BlockSpec((N_MEM, D_MODEL), lambda b, i: (b, 0)),
                  pl.BlockSpec((N_MEM, D_MODEL), lambda b, i: (b, 1)),
                  const(wo)],
        out_specs=pl.BlockSpec((tq, D_MODEL), lambda b, i: (b * nt + i, 0)),
        out_shape=jax.ShapeDtypeStruct((bsz * t, D_MODEL), F32),
        compiler_params=_cparams(("parallel", "parallel")),
    )(x, g2, wq, kv, kv, wo)


def _xattn_step_kernel(bb, q_ref, k_ref, v_ref, o_ref):
    for i in range(bb):
        t = k_ref[i] * q_ref[i]
        s = jnp.sum(t, axis=-1, keepdims=True) * (XHD ** -0.5)
        e = jnp.exp(s - jnp.max(s, axis=0, keepdims=True))
        pr = e / jnp.sum(e, axis=0, keepdims=True)
        o_ref[i] = jnp.sum(pr * v_ref[i], axis=0)


def _xattn_step(q, cache_k, cache_v, layer):
    bsz = q.shape[0]
    bb = 4
    cache_spec = pl.BlockSpec((None, bb, N_MEM, XH, XHD), lambda i: (layer, i, 0, 0, 0))
    return pl.pallas_call(
        functools.partial(_xattn_step_kernel, bb),
        grid=(bsz // bb,),
        in_specs=[pl.BlockSpec((bb, XH, XHD), lambda i: (i, 0, 0)), cache_spec, cache_spec],
        out_specs=pl.BlockSpec((bb, XH, XHD), lambda i: (i, 0, 0)),
        out_shape=jax.ShapeDtypeStruct((bsz, XH, XHD), F32),
        compiler_params=_cparams(("parallel",)),
    )(q.reshape(bsz, XH, XHD), cache_k, cache_v).reshape(bsz, D_MODEL)


def _step_lanes_kernel(rw_ref, hg_ref, ml_ref, wkv_ref, hgs_ref, c_ref, n_ref, m_ref, a0_ref, a1_ref, a2_ref,
                       orw_ref, ohg_ref, oml_ref, wkv_o, hgs_o, c_o, n_o, m_o):
    del a0_ref, a1_ref, a2_ref
    csum = lambda a: jnp.sum(a, axis=0, keepdims=True)
    r, k, w, kk, ka = rw_ref[0], rw_ref[1], rw_ref[3], rw_ref[4], rw_ref[5]
    for i in range(HEAD_DIM):
        s = wkv_ref[i]
        s = s * w - csum(s * kk) * ka + rw_ref[2, i:i + 1, :] * k
        orw_ref[i:i + 1, :] = csum(s * r)
        wkv_o[i] = s
    vh = hg_ref[3]
    acc = jnp.zeros_like(vh)
    for i in range(HEAD_DIM):
        s = hgs_ref[i] * hg_ref[1, i:i + 1, :] + hg_ref[2, i:i + 1, :] * vh
        acc = acc + hg_ref[0, i:i + 1, :] * s
        hgs_o[i] = s
    ohg_ref[...] = acc
    q, k = ml_ref[0], ml_ref[1]
    li, lf = ml_ref[3, 0:1, :], ml_ref[4, 0:1, :]
    m_old = m_ref[...]
    m_new = jnp.maximum(lf + m_old, li)
    gp = jnp.exp(lf + m_old - m_new)
    gs = jnp.exp(li - m_new)
    n = gp * n_ref[...] + gs * k
    scale = 1.0 / jnp.maximum(jnp.abs(csum(n * q)), jnp.exp(-m_new))
    gk = gs * k
    for i in range(HEAD_DIM):
        s = c_ref[i] * gp + ml_ref[2, i:i + 1, :] * gk
        oml_ref[i:i + 1, :] = csum(s * q) * scale
        c_o[i] = s
    n_o[...] = n
    m_o[...] = m_new


def _ssm_step_kernel(xdt_ref, bn_ref, cn_ref, dec_ref, ssm_ref, a_ref, o_ref, ssm_o):
    del a_ref
    ri = lax.broadcasted_iota(jnp.int32, (HEAD_DIM, HEAD_DIM), 0)
    ci = lax.broadcasted_iota(jnp.int32, (HEAD_DIM, HEAD_DIM), 1)
    eye = (ri == ci).astype(F32)
    col = lambda r: jnp.sum(eye * r, axis=-1, keepdims=True)
    row = lambda c: jnp.sum(eye * c, axis=-2, keepdims=True)
    s = ssm_ref[...] * dec_ref[...] + col(xdt_ref[...]) * bn_ref[...]
    o_ref[...] = row(jnp.sum(s * cn_ref[...], axis=-1, keepdims=True))
    ssm_o[...] = s


def _step_lanes(rw_t, hg_t, ml_t, states, acc, layer):
    bsz = rw_t.shape[1]
    fields = lambda a: a.reshape(-1, NH, HEAD_DIM, bsz)
    seq_ins = (fields(rw_t), fields(hg_t), fields(ml_t))
    seq_spec = lambda a: pl.BlockSpec((a.shape[0], None, HEAD_DIM, bsz), lambda h: (0, h, 0, 0))
    st_layer = lambda a: pl.BlockSpec((None, None) + a.shape[2:], lambda h: (layer, h) + (0,) * (a.ndim - 2))
    st_head = lambda a: pl.BlockSpec((None,) + a.shape[2:], lambda h: (h,) + (0,) * (a.ndim - 2))
    o_shape = jax.ShapeDtypeStruct((NH, HEAD_DIM, bsz), F32)
    o_spec = pl.BlockSpec((None, HEAD_DIM, bsz), lambda h: (h, 0, 0))
    big, small = states[:3], states[3:]
    n_in = len(seq_ins) + len(states)
    outs = pl.pallas_call(
        _step_lanes_kernel,
        grid=(NH,),
        in_specs=[seq_spec(a) for a in seq_ins] + [st_layer(a) for a in states]
        + [pl.BlockSpec(memory_space=pl.ANY)] * len(big),
        out_specs=[o_spec] * 3 + [st_layer(a) for a in big] + [st_head(a) for a in small],
        out_shape=[o_shape] * 3 + [jax.ShapeDtypeStruct(a.shape, F32) for a in big]
        + [jax.ShapeDtypeStruct(a.shape[1:], F32) for a in small],
        input_output_aliases={n_in + j: 3 + j for j in range(len(big))},
        compiler_params=_cparams(("parallel",)),
    )(*seq_ins, *states, *acc)
    o_parts = [o.reshape(GW, bsz) for o in outs[:3]]
    return o_parts, outs[3:6], outs[6], outs[7]


def _ssm_step(ss, state_ssm, acc, layer):
    bsz = ss.shape[0]
    bb = SUBLANES
    per_head = lambda a: a.reshape(bsz, NH, HEAD_DIM)[:, :, 0].reshape(bsz, NH, 1, 1)
    groups = lambda a: jnp.repeat(a.reshape(bsz, 2, 1, SSM_STATE), NH // 2, axis=1)
    seq_ins = (ss[:, 0:GW].reshape(bsz, NH, 1, HEAD_DIM), groups(ss[:, 512:768]), groups(ss[:, 768:1024]),
               per_head(ss[:, GW:2 * GW]))
    blk = lambda a: pl.BlockSpec((bb,) + a.shape[1:], lambda i: (i,) + (0,) * (a.ndim - 1))
    st_layer = pl.BlockSpec((None, bb) + state_ssm.shape[2:], lambda i: (layer, i, 0, 0, 0))
    o, ssm_new = pl.pallas_call(
        _ssm_step_kernel,
        grid=(bsz // bb,),
        in_specs=[blk(a) for a in seq_ins] + [st_layer, pl.BlockSpec(memory_space=pl.ANY)],
        out_specs=[pl.BlockSpec((bb, NH, 1, HEAD_DIM), lambda i: (i, 0, 0, 0)), st_layer],
        out_shape=[jax.ShapeDtypeStruct((bsz, NH, 1, HEAD_DIM), F32), jax.ShapeDtypeStruct(state_ssm.shape, F32)],
        input_output_aliases={len(seq_ins) + 1: 1},
        compiler_params=_cparams(("parallel",)),
    )(*seq_ins, state_ssm, acc)
    return o.reshape(bsz, GW), ssm_new


def kernel(x_prompt, x_sample, state_rwkv_wkv, state_rwkv_shift, state_hgrn, state_ssm, state_ssm_conv, state_mlstm_C, state_mlstm_n, state_mlstm_m, cache_mem_k, cache_mem_v, mem_prompt, norm_mix_g, w_in, w_out, rwkv_mu, rwkv_w0, rwkv_w_up, rwkv_a0, rwkv_a_up, rwkv_g_up, rwkv_k_k, rwkv_k_a, rwkv_r_k, rwkv_ln_g, rwkv_ln_b, hgrn_lb_logits, hgrn_norm_g, ssm_conv_w, ssm_conv_b, ssm_dt_bias, ssm_A_log, ssm_D, ssm_norm_g, mlstm_i_bias, mlstm_f_bias, mlstm_norm_g, norm_x_g, norm_mem_g, xattn_wq, xattn_wk, xattn_wv, xattn_wo, norm_ff_g, ff_w1, ff_w2, final_norm_g):
    bp, tp, _ = x_prompt.shape
    bs = x_sample.shape[0]
    seg64, seg128 = _seg_matrix(HEAD_DIM), _seg_matrix(128)
    seg64_b = seg64.astype(BF16)
    gate_ex = _gate_expand()
    tril_c = jnp.asarray(np.tril(np.ones((CHUNK, CHUNK), np.float32)))

    def block_ones(sub):
        blk = np.arange(CHUNK) // sub
        return jnp.asarray((blk[:, None] == blk[None, :]).astype(np.float32))

    ones_blk, ones_blk_h = block_ones(SUB), block_ones(HGRN_SUB)
    tril_blk, tril_blk_h = ones_blk * tril_c, ones_blk_h * tril_c
    sel_c = jnp.asarray((np.arange(GW)[:, None] // HEAD_DIM == np.arange(NH * SUB)[None, :] // SUB), BF16)
    rep = lambda a: jnp.repeat(a, HEAD_DIM)
    zeros = lambda *s: jnp.zeros(s, F32)

    hp = x_prompt.reshape(bp * tp, D_MODEL)
    hs = x_sample.reshape(bs, D_MODEL)
    mem = mem_prompt.reshape(bp * N_MEM, D_MODEL)
    p_states, s_small, p_mem = [], [], []
    to_lanes = lambda a: jnp.moveaxis(a, 1, -1)
    from_lanes = lambda a: jnp.moveaxis(a, -1, 1)
    lane_states = (to_lanes(state_rwkv_wkv), to_lanes(state_hgrn), to_lanes(state_mlstm_C), to_lanes(state_mlstm_n),
                   to_lanes(state_mlstm_m)[:, :, None, :])
    s_big = tuple(jnp.zeros(a.shape, F32) for a in lane_states[:3])
    s_ssm = jnp.zeros(state_ssm.shape, F32)

    for l in range(DEPTH):
        last = l == DEPTH - 1
        w_in_re = _in_weight(w_in, l)
        pv = _pad_rows(jnp.stack([rwkv_w0[l], rwkv_a0[l], rwkv_k_k[l], rwkv_k_a[l], rwkv_r_k[l].reshape(GW),
                                  rep(ssm_dt_bias[l]), rep(ssm_A_log[l]), rep(ssm_D[l]), rep(mlstm_i_bias[l]),
                                  rep(mlstm_f_bias[l])]), 16)
        lora = jnp.stack([jnp.pad(w, ((r0, LANES - r0 - w.shape[0]), (0, 0)))
                          for r0, w in zip(LORA_ROWS, (rwkv_w_up[l], rwkv_a_up[l], rwkv_g_up[l]))])
        conv = _pad_rows(jnp.concatenate([ssm_conv_w[l], ssm_conv_b[l][None]], axis=0), 8)
        mu = rwkv_mu[l].reshape(1, RW_W)
        gp1 = _pad_rows(jnp.stack([_lane_row((GT_DT, ssm_dt_bias[l])), _lane_row((GT_DT, ssm_A_log[l]))]), 8)
        gp2 = _pad_rows(_lane_row((GT_I, mlstm_i_bias[l]), (GT_F, mlstm_f_bias[l]))[None], 8)
        pd = _pad_rows(jnp.stack([rep(ssm_D[l]), ssm_norm_g[l]]), 8)
        pn_h = _pad_rows(hgrn_norm_g[l][None], 8)
        pn_m = _pad_rows(mlstm_norm_g[l][None], 8)
        pq = _pad_rows(jnp.stack([rwkv_ln_g[l], rwkv_ln_b[l], hgrn_norm_g[l], ssm_norm_g[l], mlstm_norm_g[l]]), 8)
        w_out_b = w_out[l].astype(BF16)
        wq_b, wo_b = xattn_wq[l].astype(BF16), xattn_wo[l].astype(BF16)
        wkv_b = jnp.concatenate([xattn_wk[l], xattn_wv[l]], axis=1).astype(BF16)
        w1_b, w2_b = ff_w1[l].astype(BF16), ff_w2[l].astype(BF16)

        kv = _norm_matmul(mem, norm_mem_g[l], wkv_b)
        p_mem.append((kv[:, :D_MODEL].reshape(bp, N_MEM, XH, XHD), kv[:, D_MODEL:].reshape(bp, N_MEM, XH, XHD)))
        proj = _norm_matmul(hp, norm_mix_g[l], w_in_re)
        rw_scan, rw_post = _prep_rw_seq(proj, zeros(bp, SUBLANES, RW_W), mu, pv, lora, seg64, bp, tp)
        (m_h, m_s, m_m, o_rw), (st_h, st_s, st_c, st_n, st_m, st_r) = _mixers_chunk(
            proj, rw_scan, (hgrn_lb_logits, pn_h, tril_blk, ones_blk, seg64_b, seg64, conv, gp1, gp2, pd, pn_m, tril_c,
                            sel_c, tril_blk_h, ones_blk_h), l, bp, tp)
        proj3 = proj.reshape(bp, tp, IN_RE)
        diag_blocks = lambda s: jnp.stack(
            [s.reshape(bp, NH, HEAD_DIM, NH, HEAD_DIM)[:, h, :, h, :] for h in range(NH)], axis=1)
        p_states.append((diag_blocks(st_r), proj3[:, -1, RW_OFF:RW_OFF + RW_W], st_h.transpose(0, 1, 3, 2),
                         st_s.transpose(0, 1, 3, 2), proj3[:, tp - 3:, SS_OFF + 256:SS_OFF + 1024],
                         st_c.transpose(0, 1, 3, 2), st_n[:, :NH], st_m[:, 0, :NH]))
        hp = _post_call(_post_seq_kernel, hp, (o_rw, rw_post, m_h, m_s, m_m), (pq, seg64, w_out_b))
        hp = _xattn_seq(hp, norm_x_g[l], wq_b, kv, wo_b, bp, tp)
        hp = _ffn(hp, norm_ff_g[l], w1_b, w2_b, final_norm_g, last)

        proj = _norm_matmul(hs, norm_mix_g[l], w_in_re)
        cb = state_ssm_conv[l]
        prep = _prep_step(proj, state_rwkv_shift[l], cb[:, 0], cb[:, 1], cb[:, 2],
                          (mu, pv, lora, seg64, hgrn_lb_logits, conv, gate_ex), l)
        rw_t, hg_t, ml_t, rw_post, hg_post, ss_scan, ss_post, ml_post = prep
        (o_rw_t, o_hg_t, o_ml_t), s_big, n_n, m_n = _step_lanes(rw_t, hg_t, ml_t, lane_states, s_big, l)
        o_ss, s_ssm = _ssm_step(ss_scan, state_ssm, s_ssm, l)
        conv_new = jnp.concatenate([cb[:, 1:], proj[:, None, SS_OFF + 256:SS_OFF + 1024]], axis=1)
        s_small.append((proj[:, RW_OFF:RW_OFF + RW_W], conv_new, n_n, m_n))
        hs = _post_step(hs, (o_rw_t, o_hg_t, o_ss, o_ml_t, rw_post, hg_post, ss_post, ml_post, pq, seg64, seg128,
                             w_out_b))
        q = _norm_matmul(hs, norm_x_g[l], wq_b)
        att = _xattn_step(q, cache_mem_k, cache_mem_v, l)
        hs = _mm_res(att, wo_b, hs)
        hs = _ffn(hs, norm_ff_g[l], w1_b, w2_b, final_norm_g, last)

    y_prompt = hp.reshape(bp, tp, D_MODEL)
    y_sample = hs.reshape(bs, 1, D_MODEL)
    p_out = [jnp.stack(s) for s in zip(*p_states)]
    s_shift, s_conv, s_n, s_m = [jnp.stack(s) for s in zip(*s_small)]
    s_n, s_m = from_lanes(s_n), from_lanes(s_m[:, :, 0, :])
    s_wkv, s_hgrn, s_c = (from_lanes(a) for a in s_big)
    p_mem_k = jnp.stack([kv[0] for kv in p_mem])
    p_mem_v = jnp.stack([kv[1] for kv in p_mem])
    return (y_prompt, y_sample, *p_out, p_mem_k, p_mem_v, s_wkv, s_shift, s_hgrn, s_ssm, s_conv, s_c, s_n, s_m)
```

```python
import functools

import numpy as np
import jax
import jax.numpy as jnp
from jax import lax
from jax.experimental import pallas as pl
from jax.experimental.pallas import tpu as pltpu

F32 = jnp.float32
BF16 = jnp.bfloat16
HI = lax.Precision.HIGHEST

D_MODEL = 1024
DEPTH = 2
NH = 4
HEAD_DIM = 64
GW = 256
SSM_STATE = 128
N_MEM = 256
XH = 4
XHD = 256
NORM_EPS = 1e-6
RWKV_GN_EPS = 64e-5
LANES = 128
SUBLANES = 8
VMEM_LIMIT = 48 * 1024 * 1024
CHUNK = 128
SUB = 16
HGRN_SUB = 16

RW_OFF, RW_W = 0, 896
HG_OFF = 896
SS_OFF = 1920
G1_OFF = 2944
ML_OFF = 3072
G2_OFF = 3840
MO_OFF = 3968
IN_RE = 4224
GT_DT, GT_I, GT_F = 0, 0, 4
LORA_ROWS = (0, 32, 64)


def _in_weight_kernel(w_ref, o_ref):
    w = w_ref[...]
    lane = lax.broadcasted_iota(jnp.int32, (w.shape[0], LANES), 1)
    gate_tile = lambda c0, n: jnp.where(lane < n, w[:, c0:c0 + LANES], 0.0)
    o_ref[:, 0:G1_OFF] = w[:, 0:G1_OFF].astype(BF16)
    o_ref[:, G1_OFF:ML_OFF] = gate_tile(2944, NH).astype(BF16)
    o_ref[:, ML_OFF:G2_OFF] = w[:, 2948:3716].astype(BF16)
    o_ref[:, G2_OFF:MO_OFF] = gate_tile(3716, 2 * NH).astype(BF16)
    o_ref[:, MO_OFF:IN_RE] = w[:, 3724:3980].astype(BF16)


def _in_weight(w_in, layer):
    k, n = w_in.shape[1:]
    tr = 256
    return pl.pallas_call(
        _in_weight_kernel,
        grid=(k // tr,),
        in_specs=[pl.BlockSpec((None, tr, n), lambda i: (layer, i, 0))],
        out_specs=pl.BlockSpec((tr, IN_RE), lambda i: (i, 0)),
        out_shape=jax.ShapeDtypeStruct((k, IN_RE), BF16),
        compiler_params=_cparams(("parallel",)),
    )(w_in)


def _seg_matrix(width):
    i = np.arange(GW)
    return jnp.asarray((i[:, None] // width == i[None, :] // width).astype(np.float32))


def _gate_expand():
    e = np.zeros((3, LANES, GW), np.float32)
    for s, off in enumerate((GT_DT, GT_I, GT_F)):
        for h in range(NH):
            e[s, off + h, h * HEAD_DIM:(h + 1) * HEAD_DIM] = 1.0
    return jnp.asarray(e)


def _lane_row(*pieces):
    z = jnp.zeros((LANES,), F32)
    for off, val in pieces:
        z = z.at[off:off + val.shape[0]].set(val)
    return z


def _pad_rows(a, n):
    return jnp.pad(a, ((0, n - a.shape[0]), (0, 0)))


def _cparams(sem):
    return pltpu.CompilerParams(dimension_semantics=sem, vmem_limit_bytes=VMEM_LIMIT)


def _softplus(x):
    return jnp.maximum(x, 0.0) + jnp.log(1.0 + jnp.exp(-jnp.abs(x)))


def _sigmoid(x):
    return jax.nn.sigmoid(x)


def _rms(x, g):
    ms = jnp.mean(x * x, axis=-1, keepdims=True)
    return x * lax.rsqrt(ms + NORM_EPS) * g


def _dot(a, b):
    return jnp.dot(a.astype(BF16), b.astype(BF16), preferred_element_type=F32)


def _dot_nt(a, b):
    return lax.dot_general(a.astype(BF16), b.astype(BF16), (((1,), (1,)), ((), ())), preferred_element_type=F32)


def _dot_tn(a, b):
    return lax.dot_general(a.astype(BF16), b.astype(BF16), (((0,), (0,)), ((), ())), preferred_element_type=F32)


def _dot_hi(a, b):
    return jnp.dot(a, b, precision=HI, preferred_element_type=F32)


def _split3(x):
    hi = x.astype(BF16)
    r = x - hi.astype(F32)
    mid = r.astype(BF16)
    return hi, mid, (r - mid.astype(F32)).astype(BF16)


def _dot01(parts, m):
    mb = m.astype(BF16)
    hi, mid, lo = (jnp.dot(p, mb, preferred_element_type=F32) for p in parts)
    return hi + mid + lo


def _norm_matmul_kernel(x_ref, g_ref, w_ref, o_ref, xn_ref):
    @pl.when(pl.program_id(1) == 0)
    def _():
        xn_ref[...] = _rms(x_ref[...], g_ref[...]).astype(BF16)

    o_ref[...] = jnp.dot(xn_ref[...], w_ref[...], preferred_element_type=F32)


def _row_tile(m):
    return min(m, 1024)


def _norm_matmul(x, g, w):
    m, k = x.shape
    n = w.shape[1]
    tm = _row_tile(m)
    tn = next(c for c in (1536, 1408, 1024, 512, n) if n % c == 0)
    return pl.pallas_call(
        _norm_matmul_kernel,
        grid=(m // tm, n // tn),
        in_specs=[pl.BlockSpec((tm, k), lambda i, j: (i, 0)),
                  pl.BlockSpec((1, k), lambda i, j: (0, 0)),
                  pl.BlockSpec((k, tn), lambda i, j: (0, j))],
        out_specs=pl.BlockSpec((tm, tn), lambda i, j: (i, j)),
        out_shape=jax.ShapeDtypeStruct((m, n), F32),
        scratch_shapes=[pltpu.VMEM((tm, k), BF16)],
        compiler_params=_cparams(("parallel", "arbitrary")),
    )(x, g.reshape(1, k), w)


def _mm_res_kernel(a_ref, w_ref, res_ref, o_ref):
    o_ref[...] = res_ref[...] + jnp.dot(a_ref[...].astype(BF16), w_ref[...], preferred_element_type=F32)


def _mm_res(a, w, res):
    m, k = a.shape
    n = w.shape[1]
    tm = min(m, 512)
    return pl.pallas_call(
        _mm_res_kernel,
        grid=(m // tm,),
        in_specs=[pl.BlockSpec((tm, k), lambda i: (i, 0)),
                  pl.BlockSpec((k, n), lambda i: (0, 0)),
                  pl.BlockSpec((tm, n), lambda i: (i, 0))],
        out_specs=pl.BlockSpec((tm, n), lambda i: (i, 0)),
        out_shape=jax.ShapeDtypeStruct((m, n), F32),
        compiler_params=_cparams(("parallel",)),
    )(a, w, res)


def _ffn_kernel(final, x_ref, g_ref, w1_ref, w2_ref, fg_ref, o_ref, xn_ref, acc_ref):
    j = pl.program_id(1)

    @pl.when(j == 0)
    def _():
        xn_ref[...] = _rms(x_ref[...], g_ref[...]).astype(BF16)
        acc_ref[...] = jnp.zeros_like(acc_ref)

    h = jnp.dot(xn_ref[...], w1_ref[...], preferred_element_type=F32)
    h = jnp.square(jnp.maximum(h, 0.0))
    acc_ref[...] += jnp.dot(h.astype(BF16), w2_ref[...], preferred_element_type=F32)

    @pl.when(j == pl.num_programs(1) - 1)
    def _():
        y = x_ref[...] + acc_ref[...]
        o_ref[...] = _rms(y, fg_ref[...]) if final else y


def _ffn(x, g, w1, w2, final_g, final):
    m, k = x.shape
    ff = w1.shape[1]
    tm = _row_tile(m)
    tf = 1024
    return pl.pallas_call(
        functools.partial(_ffn_kernel, final),
        grid=(m // tm, ff // tf),
        in_specs=[pl.BlockSpec((tm, k), lambda i, j: (i, 0)),
                  pl.BlockSpec((1, k), lambda i, j: (0, 0)),
                  pl.BlockSpec((k, tf), lambda i, j: (0, j)),
                  pl.BlockSpec((tf, k), lambda i, j: (j, 0)),
                  pl.BlockSpec((1, k), lambda i, j: (0, 0))],
        out_specs=pl.BlockSpec((tm, k), lambda i, j: (i, 0)),
        out_shape=jax.ShapeDtypeStruct((m, k), F32),
        scratch_shapes=[pltpu.VMEM((tm, k), BF16), pltpu.VMEM((tm, k), F32)],
        compiler_params=_cparams(("parallel", "arbitrary")),
    )(x, g.reshape(1, k), w1, w2, final_g.reshape(1, k))


def _shift_rows(x, tail, j):
    xr = pltpu.roll(x, j, 0)
    tr = pltpu.roll(tail, j, 0)
    rows = lax.broadcasted_iota(jnp.int32, tr.shape, 0)
    first = jnp.where(rows < j, tr, xr[0:SUBLANES])
    return jnp.concatenate([first, xr[SUBLANES:]], axis=0)


def _hgrn_lower_bound(logit_ref, layer):
    lg = logit_ref[...]
    e = jnp.exp(lg - jnp.max(lg, axis=0, keepdims=True))
    prb = e / jnp.sum(e, axis=0, keepdims=True)
    return jnp.sum(prb[0:layer + 1], axis=0, keepdims=True) - prb[0:1]


def _conv_silu(u, c3, c2, c1, conv_ref):
    cw = conv_ref[...]
    conv = cw[4:5] + c3 * cw[0:1] + c2 * cw[1:2] + c1 * cw[2:3] + u * cw[3:4]
    return conv * _sigmoid(conv)


def _prep_rwkv(pr, prev_rw, mu_ref, pv_ref, lora_ref, seg_ref, rw_scan, rw_post, log_decay):
    pv = pv_ref[...]
    row = lambda i: pv[i:i + 1, :]
    seg = seg_ref[...]
    pm = pr + (prev_rw - pr) * mu_ref[...]
    r, k, v, low = pm[:, 0:256], pm[:, 256:512], pm[:, 512:768], pm[:, 768:896]
    w_log = -_softplus(-(row(0) + _dot_hi(jnp.tanh(low), lora_ref[0]))) - 0.5
    a = _sigmoid(row(1) + _dot_hi(low, lora_ref[1]))
    kk = k * row(2)
    kk = kk / jnp.maximum(jnp.sqrt(_dot01(_split3(kk * kk), seg)), 1e-12)
    k2 = k * (1.0 + (a - 1.0) * row(3))
    rw_scan[:, 0:256] = r
    rw_scan[:, 256:512] = k2
    rw_scan[:, 512:768] = v
    rw_scan[:, 768:1024] = -jnp.exp(w_log) if log_decay else jnp.exp(-jnp.exp(w_log))
    rw_scan[:, 1024:1280] = kk
    rw_scan[:, 1280:1536] = kk * a
    rw_post[:, 0:256] = _dot01(_split3(r * k2 * row(4)), seg) * v
    rw_post[:, 256:512] = _dot_hi(_sigmoid(low), lora_ref[2])


def _prep_rw_seq_kernel(p_ref, prev8_ref, tail0_ref, mu_ref, pv_ref, lora_ref, seg_ref, rw_scan, rw_post):
    tail = jnp.where(pl.program_id(1) == 0, tail0_ref[0], prev8_ref[...])
    pr = p_ref[...]
    _prep_rwkv(pr, _shift_rows(pr, tail, 1), mu_ref, pv_ref, lora_ref, seg_ref, rw_scan, rw_post, True)


def _prep_rw_seq(proj, tail0, mu, pv, lora, seg, bsz, t):
    tt = 512
    nt = t // tt
    m = bsz * t
    c2 = lambda b, i: (0, 0)
    return pl.pallas_call(
        _prep_rw_seq_kernel,
        grid=(bsz, nt),
        in_specs=[pl.BlockSpec((tt, RW_W), lambda b, i: (b * nt + i, 0)),
                  pl.BlockSpec((SUBLANES, RW_W), lambda b, i: (jnp.maximum((b * nt + i) * (tt // SUBLANES) - 1, 0), 0)),
                  pl.BlockSpec((1, SUBLANES, RW_W), lambda b, i: (b, 0, 0)),
                  pl.BlockSpec((1, RW_W), c2), pl.BlockSpec((16, GW), c2),
                  pl.BlockSpec((3, LANES, GW), lambda b, i: (0, 0, 0)), pl.BlockSpec((GW, GW), c2)],
        out_specs=[pl.BlockSpec((tt, 1536), lambda b, i: (b * nt + i, 0)),
                   pl.BlockSpec((tt, 512), lambda b, i: (b * nt + i, 0))],
        out_shape=[jax.ShapeDtypeStruct((m, 1536), F32), jax.ShapeDtypeStruct((m, 512), F32)],
        compiler_params=_cparams(("parallel", "parallel")),
    )(proj, proj, tail0, mu, pv, lora, seg)


PREP_SCAN_WIDTHS = (1536, 1024, 1280)
PREP_ROW_WIDTHS = (512, 256, 1024, 512, 256)


def _prep_step_kernel(layer, p_ref, prev_ref, c3_ref, c2_ref, c1_ref, mu_ref, pv_ref, lora_ref, seg_ref, logit_ref,
                      conv_ref, ex_ref, rw_t, hg_t, ml_t, rw_post, hg_post, ss_scan, ss_post, ml_post, rw_scan,
                      hg_scan, ml_scan):
    pv = pv_ref[...]
    row = lambda i: pv[i:i + 1, :]
    _prep_rwkv(p_ref[:, RW_OFF:RW_OFF + RW_W], prev_ref[...], mu_ref, pv_ref, lora_ref, seg_ref, rw_scan, rw_post,
               False)
    g1 = p_ref[:, G1_OFF:G1_OFF + LANES]
    g2 = p_ref[:, G2_OFF:G2_OFF + LANES]
    dt_pre, i_pre, f_pre_m = _dot_hi(g1, ex_ref[0]), _dot_hi(g2, ex_ref[1]), _dot_hi(g2, ex_ref[2])

    lb = _hgrn_lower_bound(logit_ref, layer)
    q_pre = p_ref[:, HG_OFF:HG_OFF + 256]
    f_pre = p_ref[:, HG_OFF + 256:HG_OFF + 512]
    gg = p_ref[:, HG_OFF + 768:HG_OFF + 1024]
    hg_scan[:, 0:256] = q_pre * _sigmoid(q_pre)
    hg_scan[:, 256:512] = lb + (1.0 - lb) * _sigmoid(f_pre)
    hg_scan[:, 512:768] = (1.0 - lb) * _sigmoid(-f_pre)
    hg_scan[:, 768:1024] = p_ref[:, HG_OFF + 512:HG_OFF + 768]
    hg_post[...] = gg * _sigmoid(gg)

    z = p_ref[:, SS_OFF:SS_OFF + 256]
    xbc = _conv_silu(p_ref[:, SS_OFF + 256:SS_OFF + 1024], c3_ref[...], c2_ref[...], c1_ref[...], conv_ref)
    xs = xbc[:, 0:256]
    dt = _softplus(dt_pre + row(5))
    ss_scan[:, 0:256] = xs * dt
    ss_scan[:, 256:512] = jnp.exp(dt * (-jnp.exp(row(6))))
    ss_scan[:, 512:1024] = xbc[:, 256:768]
    ss_post[:, 0:256] = xs * row(7)
    ss_post[:, 256:512] = z * _sigmoid(z)

    ml_scan[:, 0:256] = p_ref[:, ML_OFF:ML_OFF + 256]
    ml_scan[:, 256:512] = p_ref[:, ML_OFF + 256:ML_OFF + 512] * (HEAD_DIM ** -0.5)
    ml_scan[:, 512:768] = p_ref[:, ML_OFF + 512:ML_OFF + 768]
    ml_scan[:, 768:1024] = i_pre + row(8)
    ml_scan[:, 1024:1280] = -_softplus(-(f_pre_m + row(9)))
    ml_post[...] = _sigmoid(p_ref[:, MO_OFF:MO_OFF + 256])
    rw_t[...] = rw_scan[...].T
    hg_t[...] = hg_scan[...].T
    ml_t[...] = ml_scan[...].T


def _prep_step(proj, prev_rw, c3, c2, c1, params, layer):
    m = proj.shape[0]
    full = lambda a: pl.BlockSpec(a.shape, lambda i: (0,) * a.ndim)
    ins = (proj, prev_rw, c3, c2, c1) + tuple(params)
    shapes = [(w, m) for w in PREP_SCAN_WIDTHS] + [(m, w) for w in PREP_ROW_WIDTHS]
    return pl.pallas_call(
        functools.partial(_prep_step_kernel, layer),
        grid=(1,),
        in_specs=[full(a) for a in ins],
        out_specs=[pl.BlockSpec(s, lambda i: (0, 0)) for s in shapes],
        out_shape=[jax.ShapeDtypeStruct(s, F32) for s in shapes],
        scratch_shapes=[pltpu.VMEM((m, w), F32) for w in PREP_SCAN_WIDTHS],
        compiler_params=_cparams(("arbitrary",)),
    )(*ins)


def _ssm_chunk_body(ss_ref, gt_ref, conv_ref, gp_ref, pd_ref, tril_ref, o_ref, ht_ref, tail_ref):
    @pl.when(pl.program_id(1) == 0)
    def _():
        ht_ref[...] = jnp.zeros_like(ht_ref)
        tail_ref[...] = jnp.zeros_like(tail_ref)

    yield
    L = ss_ref.shape[0]
    u = ss_ref[:, 256:1024]
    tail = tail_ref[...]
    c1, c2, c3 = (_shift_rows(u, tail, j) for j in (1, 2, 3))
    tail_ref[...] = u[L - SUBLANES:L]
    xbc = _conv_silu(u, c3, c2, c1, conv_ref)
    z = ss_ref[:, 0:256]
    zs = z * _sigmoid(z)
    dt = _softplus(gt_ref[...] + gp_ref[0:1])
    la = dt * (-jnp.exp(gp_ref[1:2]))
    tril = tril_ref[...]
    mask = tril > 0.5
    cum = _dot_hi(tril, la)
    yield
    cum_t = cum.T
    pd = pd_ref[...]
    hsl = lambda h: slice(h * HEAD_DIM, (h + 1) * HEAD_DIM)
    bgs = [xbc[:, 256 + SSM_STATE * g:256 + SSM_STATE * (g + 1)] for g in range(2)]
    cgs = [xbc[:, 512 + SSM_STATE * g:512 + SSM_STATE * (g + 1)] for g in range(2)]
    scores = [_dot_nt(cgs[g], bgs[g]) for g in range(2)]
    hts = [ht_ref[0, h] for h in range(NH)]
    inter = [_dot(cgs[h // 2], hts[h]) for h in range(NH)]
    bg_t = [bgs[g].T for g in range(2)]
    ccs = [cum[:, GT_DT + h:GT_DT + h + 1] for h in range(NH)]
    dtc = [dt[:, GT_DT + h:GT_DT + h + 1] for h in range(NH)]
    lms = []
    for h in range(NH):
        cr = cum_t[GT_DT + h:GT_DT + h + 1, :]
        lms.append(jnp.where(mask, jnp.exp(jnp.where(mask, ccs[h] - cr, 0.0)), 0.0))
    yield
    ydots = [_dot(scores[h // 2] * lms[h], xbc[:, hsl(h)] * dtc[h]) for h in range(NH)]
    cends = [ccs[h][L - 1:L, :] for h in range(NH)]
    hdots = [_dot(bg_t[h // 2], xbc[:, hsl(h)] * (jnp.exp(cends[h] - ccs[h]) * dtc[h])) for h in range(NH)]
    yield
    ys = []
    for h in range(NH):
        y = ydots[h] + inter[h] * jnp.exp(ccs[h])
        ys.append((y + xbc[:, hsl(h)] * pd[0:1, hsl(h)]) * zs[:, hsl(h)])
        ht_ref[0, h] = jnp.exp(cends[h]) * hts[h] + hdots[h]
    for g in range(2):
        y0, y1 = ys[2 * g], ys[2 * g + 1]
        ms = jnp.sum(y0 * y0, axis=-1, keepdims=True) + jnp.sum(y1 * y1, axis=-1, keepdims=True)
        scale = lax.rsqrt(ms * (1.0 / (2 * HEAD_DIM)) + NORM_EPS)
        for h in (2 * g, 2 * g + 1):
            o_ref[:, hsl(h)] = ys[h] * scale * pd[1:2, hsl(h)]


def _mlstm_chunk_body(qkv_ref, og_ref, gt_ref, gp_ref, pn_ref, tril_ref, o_ref, ct_ref, n_ref, m_ref):
    @pl.when(pl.program_id(1) == 0)
    def _():
        ct_ref[...] = jnp.zeros_like(ct_ref)
        n_ref[...] = jnp.zeros_like(n_ref)
        m_ref[...] = jnp.zeros_like(m_ref)

    yield
    L = qkv_ref.shape[0]
    x = gt_ref[...] + gp_ref[0:1]
    lf = -_softplus(-x)
    tril = tril_ref[...]
    mask = tril > 0.5
    b = _dot_hi(tril, lf)
    yield
    b_t = b.T
    x_t = x.T
    m_row = m_ref[0]
    m_new_row = m_row
    lane = lax.broadcasted_iota(jnp.int32, m_row.shape, 1)
    pn = pn_ref[...]
    hsl = lambda h, base=0: slice(base + h * HEAD_DIM, base + (h + 1) * HEAD_DIM)
    for h in range(NH):
        b_col, b_row = b[:, GT_F + h:GT_F + h + 1], b_t[GT_F + h:GT_F + h + 1, :]
        li_col, li_row = x[:, GT_I + h:GT_I + h + 1], x_t[GT_I + h:GT_I + h + 1, :]
        m_prev = m_row[:, h:h + 1]
        log_inter = b_col + m_prev
        log_intra = jnp.where(mask, b_col - b_row + li_row, -jnp.inf)
        m_t = jnp.maximum(log_inter, jnp.max(log_intra, axis=-1, keepdims=True))
        w_inter = jnp.exp(log_inter - m_t)
        qh = qkv_ref[:, hsl(h)]
        kh = qkv_ref[:, hsl(h, 256)] * (HEAD_DIM ** -0.5)
        vh = qkv_ref[:, hsl(h, 512)]
        ct = ct_ref[0, h]
        qk_raw = _dot_nt(qh, kh)
        q_ct = _dot(qh, ct)
        yield
        qk = qk_raw * jnp.exp(log_intra - m_t)
        n_h = n_ref[0, h:h + 1, :]
        m_end = m_t[L - 1:L, :]
        b_end = b_col[L - 1:L, :]
        g_prev = jnp.exp(b_end + m_prev - m_end)
        kg = kh * jnp.exp(b_end - b_col + li_col - m_end)
        qk_v = _dot(qk, vh)
        kg_v = _dot_tn(kg, vh)
        yield
        num = qk_v + w_inter * q_ct
        den = jnp.sum(qk, axis=-1, keepdims=True) + w_inter * jnp.sum(qh * n_h, axis=-1, keepdims=True)
        hout = num / jnp.maximum(jnp.abs(den), jnp.exp(-m_t))
        ct_ref[0, h] = g_prev * ct + kg_v
        n_ref[0, h:h + 1, :] = g_prev * n_h + jnp.sum(kg, axis=0, keepdims=True)
        m_new_row = jnp.where(lane == h, m_end, m_new_row)
        hn = hout * lax.rsqrt(jnp.mean(hout * hout, axis=-1, keepdims=True) + NORM_EPS)
        o_ref[:, hsl(h)] = hn * pn[0:1, hsl(h)] * _sigmoid(og_ref[:, hsl(h)])
    m_ref[0] = m_new_row


def _hgrn_chunk_body(layer, hg_ref, logit_ref, pn_ref, tril_ref, ones_ref, seg_ref, bd_ref, o_ref, st_ref, q_scr,
                     b_scr, i_scr):
    @pl.when(pl.program_id(1) == 0)
    def _():
        st_ref[...] = jnp.zeros_like(st_ref)

    yield
    lb = _hgrn_lower_bound(logit_ref, layer)
    seg = seg_ref[...]
    bd = bd_ref[...]
    pn = pn_ref[...]
    sub = HGRN_SUB
    srow = lax.broadcasted_iota(jnp.int32, (sub, GW), 0)
    L = hg_ref.shape[0]

    q_pre, f_pre, v, gg = hg_ref[:, 0:256], hg_ref[:, 256:512], hg_ref[:, 512:768], hg_ref[:, 768:1024]
    q = q_pre * _sigmoid(q_pre)
    k = (1.0 - lb) * _sigmoid(-f_pre)
    lf = jnp.log(lb + (1.0 - lb) * _sigmoid(f_pre))
    b = _dot_hi(tril_ref[...], lf)
    b_tot = _dot_hi(ones_ref[...], lf)
    yield
    q_scr[...] = q
    b_scr[...] = b
    quarter = L // 4
    for part in range(4):
        tiles = []
        for t in range(part * quarter, (part + 1) * quarter):
            rows = slice((t // sub) * sub, (t // sub + 1) * sub)
            live = srow <= (t % sub)
            dec = jnp.where(live, jnp.exp(jnp.where(live, b_scr[t:t + 1, :] - b[rows], 0.0)), 0.0)
            tiles.append(dec * q_scr[t:t + 1, :] * k[rows])
        a_b = jnp.dot(jnp.concatenate(tiles, axis=0).astype(BF16), seg, preferred_element_type=F32)
        yield
        for j in range(quarter):
            t = part * quarter + j
            rows = slice((t // sub) * sub, (t // sub + 1) * sub)
            i_scr[t:t + 1, :] = jnp.sum(a_b[j * sub:(j + 1) * sub, :] * v[rows], axis=0, keepdims=True)
    qx = q * jnp.exp(b)
    kx = k * jnp.exp(b_tot - b)
    p_end = jnp.exp(b_tot)
    hsl = lambda h: slice(h * HEAD_DIM, (h + 1) * HEAD_DIM)
    sts = [st_ref[0, h] for h in range(NH)]
    v_t = v.T
    trow = lax.broadcasted_iota(jnp.int32, (L, GW), 0)
    for c in range(L // sub):
        rows = slice(c * sub, (c + 1) * sub)
        km = jnp.where((trow >= c * sub) & (trow < (c + 1) * sub), kx, 0.0)
        upds = [_dot(v_t[hsl(h), :], km[:, hsl(h)]) for h in range(NH)]
        inters = [_dot_nt(qx[rows, hsl(h)], sts[h]) for h in range(NH)]
        yield
        for h in range(NH):
            i_scr[rows, hsl(h)] = inters[h] + i_scr[rows, hsl(h)]
            sts[h] = sts[h] * p_end[c * sub:c * sub + 1, hsl(h)] + upds[h]
    for h in range(NH):
        st_ref[0, h] = sts[h]
    o = i_scr[...]
    gate = gg * _sigmoid(gg)
    for h in range(NH):
        sl = slice(h * HEAD_DIM, (h + 1) * HEAD_DIM)
        oh = o[:, sl]
        o_ref[:, sl] = oh * lax.rsqrt(jnp.mean(oh * oh, axis=-1, keepdims=True) + NORM_EPS) * pn[0:1, sl] * gate[:, sl]


def _rwkv_chunk_body(rw_ref, tril_ref, ones_ref, sel_ref, bd_ref, o_ref, st_ref, kx_scr):
    @pl.when(pl.program_id(1) == 0)
    def _():
        st_ref[...] = jnp.zeros_like(st_ref)

    yield
    L = rw_ref.shape[0]
    ns = L // SUB
    xr = ns * NH * SUB
    bd = bd_ref[...]
    csum = lambda a: jnp.sum(a, axis=0, keepdims=True)
    xi = lax.broadcasted_iota(jnp.int32, (xr, GW), 0)
    xj = lax.broadcasted_iota(jnp.int32, (xr, GW), 1)
    head_rows = (((xi // SUB) & (NH - 1)) == xj // HEAD_DIM).astype(F32)
    ti = lax.broadcasted_iota(jnp.int32, (L, xr), 0)
    tj = lax.broadcasted_iota(jnp.int32, (L, xr), 1)
    same = (tj // (NH * SUB)) == (ti // SUB)
    lt = same & ((tj & (SUB - 1)) < (ti & (SUB - 1)))
    le = same & ((tj & (SUB - 1)) <= (ti & (SUB - 1)))

    def expand(x):
        parts = []
        for c in range(ns):
            parts += [x[c * SUB:(c + 1) * SUB]] * NH
        return jnp.concatenate(parts, axis=0) * head_rows

    r, k, v = rw_ref[:, 0:256], rw_ref[:, 256:512], rw_ref[:, 512:768]
    lw, kk, ka = rw_ref[:, 768:1024], rw_ref[:, 1024:1280], rw_ref[:, 1280:1536]
    cs = _dot_hi(tril_ref[...], lw)
    c_tot = _dot_hi(ones_ref[...], lw)
    yield
    e_inv = jnp.exp(-cs)
    kkx = kk * jnp.exp(cs - lw)
    rx = r * jnp.exp(cs)
    kae, ke = ka * e_inv, k * e_inv
    e_end = jnp.exp(c_tot - cs)
    a_hat, k_hat = ka * e_end, k * e_end
    p_end = jnp.exp(c_tot)
    kx_scr[...] = kkx
    tiles = [kx_scr[t:t + 1, :] * kae[(t // SUB) * SUB:(t // SUB + 1) * SUB] for t in range(L)]
    maa = jnp.dot(jnp.concatenate(tiles, axis=0).astype(BF16), sel_ref[...], preferred_element_type=F32)
    v_x = expand(v)
    kkx_x = expand(kkx)
    mk = _dot_nt(jnp.concatenate([kkx, rx], axis=0), expand(ke))
    ra = _dot_nt(rx, expand(kae))
    yield
    m_ak = jnp.where(lt, mk[0:L], 0.0)
    a_rk = jnp.where(le, mk[L:2 * L], 0.0)
    a_ra = jnp.where(le, ra, 0.0)
    mv_x = expand(_dot(m_ak, v_x))
    rk_v = _dot(a_rk, v_x)
    yield
    crow = lax.broadcasted_iota(jnp.int32, (SUB, NH * SUB), 0)
    ccol = lax.broadcasted_iota(jnp.int32, (SUB, NH * SUB), 1) & (SUB - 1)
    eye_c = (crow == ccol).astype(F32)
    gs, us = [], []
    for c in range(ns):
        t_arr = jnp.zeros((SUB, NH * SUB), F32)
        for t in range(SUB):
            i = c * SUB + t
            row_t = eye_c[t:t + 1, :] - csum(maa[i * SUB:(i + 1) * SUB, :] * t_arr)
            t_arr = jnp.where(crow == t, row_t, t_arr)
        xs = slice(c * NH * SUB, (c + 1) * NH * SUB)
        gs.append(_dot(t_arr, kkx_x[xs]))
        us.append(_dot(t_arr, mv_x[xs]))
        if c % 2 == 1:
            yield
    g_all, u_all = jnp.concatenate(gs, axis=0), jnp.concatenate(us, axis=0)
    ra_g = _dot(a_ra, expand(g_all))
    ra_u = _dot(a_ra, expand(u_all))
    phis = [_dot_tn(g_all[c * SUB:(c + 1) * SUB], a_hat[c * SUB:(c + 1) * SUB]) for c in range(ns)]
    psis = [_dot_tn(jnp.concatenate([v[c * SUB:(c + 1) * SUB], -u_all[c * SUB:(c + 1) * SUB]], axis=0),
                    jnp.concatenate([k_hat[c * SUB:(c + 1) * SUB], a_hat[c * SUB:(c + 1) * SUB]], axis=0))
            for c in range(ns)]
    yield
    omega = rx - ra_g
    o_ind = rk_v - ra_u
    st = st_ref[0]
    for c in range(ns):
        rows = slice(c * SUB, (c + 1) * SUB)
        o_dot = _dot_nt(omega[rows], st)
        st_phi = _dot(st, phis[c])
        yield
        o_ref[rows, :] = o_dot + o_ind[rows]
        st = st * p_end[c * SUB:c * SUB + 1, :] + (psis[c] - st_phi) * bd
    st_ref[0] = st


def _mixers_chunk_kernel(layer, p_ref, rw_ref, logit_ref, pnh_ref, trilb_ref, onesb_ref, segb_ref, bd_ref, conv_ref,
                         gp1_ref, gp2_ref, pd_ref, pnm_ref, trilc_ref, sel_ref, trilh_ref, onesh_ref, ohg_ref, oss_ref,
                         oml_ref, orw_ref, sth_ref, ht_ref, ct_ref, n_ref, m_ref, str_ref, q_scr, b_scr, i_scr, tail_scr,
                         kx_scr):
    cols = lambda off, w: p_ref.at[:, off:off + w]
    bodies = [
        _rwkv_chunk_body(rw_ref, trilb_ref, onesb_ref, sel_ref, bd_ref, orw_ref, str_ref, kx_scr),
        _hgrn_chunk_body(layer, cols(HG_OFF, 1024), logit_ref, pnh_ref, trilh_ref, onesh_ref, segb_ref, bd_ref, ohg_ref,
                         sth_ref, q_scr, b_scr, i_scr),
        _mlstm_chunk_body(cols(ML_OFF, 768), cols(MO_OFF, GW), cols(G2_OFF, LANES), gp2_ref, pnm_ref, trilc_ref,
                          oml_ref, ct_ref, n_ref, m_ref),
        _ssm_chunk_body(cols(SS_OFF, 1024), cols(G1_OFF, LANES), conv_ref, gp1_ref, pd_ref, trilc_ref, oss_ref, ht_ref,
                        tail_scr),
    ]
    while bodies:
        for body in list(bodies):
            try:
                next(body)
            except StopIteration:
                bodies.remove(body)


MIXER_STATE_SHAPES = ((NH, HEAD_DIM, HEAD_DIM), (NH, SSM_STATE, HEAD_DIM), (NH, HEAD_DIM, HEAD_DIM), (SUBLANES, HEAD_DIM),
                      (1, LANES), (GW, GW))


def _mixers_chunk(proj, rw_scan, consts, layer, bsz, t):
    nt = t // CHUNK
    m = bsz * t
    rows = lambda w: pl.BlockSpec((CHUNK, w), lambda b, i: (b * nt + i, 0))
    const = lambda a: pl.BlockSpec(a.shape, lambda b, i: (0,) * a.ndim)
    state = lambda s: pl.BlockSpec((1,) + s, lambda b, i: (b,) + (0,) * len(s))
    big = pltpu.VMEM((CHUNK, GW), F32)
    outs = pl.pallas_call(
        functools.partial(_mixers_chunk_kernel, layer),
        grid=(bsz, nt),
        in_specs=[rows(IN_RE), rows(1536)] + [const(a) for a in consts],
        out_specs=[rows(GW)] * 4 + [state(s) for s in MIXER_STATE_SHAPES],
        out_shape=[jax.ShapeDtypeStruct((m, GW), F32)] * 4
        + [jax.ShapeDtypeStruct((bsz,) + s, F32) for s in MIXER_STATE_SHAPES],
        scratch_shapes=[big, big, big, pltpu.VMEM((SUBLANES, 768), F32), big],
        compiler_params=_cparams(("parallel", "arbitrary")),
    )(proj, rw_scan, *consts)
    return outs[:4], outs[4:]


def _segmean(a, s, n):
    return _dot01(_split3(a), s) * (1.0 / n)


def _rwkv_post(o, rwp_ref, pq, s64):
    d = o - _segmean(o, s64, HEAD_DIM)
    var = _segmean(d * d, s64, HEAD_DIM)
    return (d * lax.rsqrt(var + RWKV_GN_EPS) * pq[0:1] + pq[1:2] + rwp_ref[:, 0:256]) * rwp_ref[:, 256:512]


def _out_proj(x, parts, w_ref):
    acc = x
    for i, part in enumerate(parts):
        acc = acc + jnp.dot(part.astype(BF16), w_ref[i * GW:(i + 1) * GW, :], preferred_element_type=F32)
    return acc


def _post_step_kernel(x_ref, orw_ref, ohg_ref, oss_ref, oml_ref, rwp_ref, hgp_ref, ssp_ref, mlp_ref, pq_ref, seg64_ref,
                      seg128_ref, w_ref, o_ref):
    pq = pq_ref[...]
    s64 = seg64_ref[...]
    m_r = _rwkv_post(orw_ref[...].T, rwp_ref, pq, s64)
    o = ohg_ref[...].T
    m_h = o * lax.rsqrt(_segmean(o * o, s64, HEAD_DIM) + NORM_EPS) * pq[2:3] * hgp_ref[...]
    y = (oss_ref[...] + ssp_ref[:, 0:256]) * ssp_ref[:, 256:512]
    m_s = y * lax.rsqrt(_segmean(y * y, seg128_ref[...], 128) + NORM_EPS) * pq[3:4]
    o = oml_ref[...].T
    m_m = o * lax.rsqrt(_segmean(o * o, s64, HEAD_DIM) + NORM_EPS) * pq[4:5] * mlp_ref[...]
    o_ref[...] = _out_proj(x_ref[...], (m_r, m_h, m_s, m_m), w_ref)


def _post_seq_kernel(x_ref, orw_ref, rwp_ref, mh_ref, ms_ref, mm_ref, pq_ref, seg64_ref, w_ref, o_ref):
    m_r = _rwkv_post(orw_ref[...], rwp_ref, pq_ref[...], seg64_ref[...])
    o_ref[...] = _out_proj(x_ref[...], (m_r, mh_ref[...], ms_ref[...], mm_ref[...]), w_ref)


def _post_step(x, ins):
    full = lambda a: pl.BlockSpec(a.shape, lambda i: (0,) * a.ndim)
    return pl.pallas_call(
        _post_step_kernel,
        grid=(1,),
        in_specs=[full(a) for a in (x,) + tuple(ins)],
        out_specs=full(x),
        out_shape=jax.ShapeDtypeStruct(x.shape, F32),
        compiler_params=_cparams(("arbitrary",)),
    )(x, *ins)


def _post_call(kernel_fn, x, row_ins, consts):
    m = x.shape[0]
    tm = min(m, 512)
    rows = lambda a: pl.BlockSpec((tm, a.shape[1]), lambda i: (i, 0))
    const = lambda a: pl.BlockSpec(a.shape, lambda i: (0, 0))
    return pl.pallas_call(
        kernel_fn,
        grid=(m // tm,),
        in_specs=[rows(a) for a in (x,) + tuple(row_ins)] + [const(a) for a in consts],
        out_specs=rows(x),
        out_shape=jax.ShapeDtypeStruct((m, D_MODEL), F32),
        compiler_params=_cparams(("parallel",)),
    )(x, *row_ins, *consts)


def _xattn_seq_kernel(x_ref, g_ref, wq_ref, k_ref, v_ref, wo_ref, o_ref):
    x = x_ref[...]
    q = jnp.dot(_rms(x, g_ref[...]).astype(BF16), wq_ref[...], preferred_element_type=F32)
    acc = x
    for h in range(XH):
        sl = slice(h * XHD, (h + 1) * XHD)
        s = _dot_nt(q[:, sl], k_ref[:, sl]) * (XHD ** -0.5)
        e = jnp.exp(s - jnp.max(s, axis=-1, keepdims=True))
        pr = e / jnp.sum(e, axis=-1, keepdims=True)
        acc = acc + _dot(_dot(pr, v_ref[:, sl]), wo_ref[sl, :])
    o_ref[...] = acc


def _xattn_seq(x, g, wq, kv, wo, bsz, t):
    tq = 512
    nt = t // tq
    const = lambda a: pl.BlockSpec(a.shape, lambda b, i: (0, 0))
    g2 = g.reshape(1, D_MODEL)
    return pl.pallas_call(
        _xattn_seq_kernel,
        grid=(bsz, nt),
        in_specs=[pl.BlockSpec((tq, D_MODEL), lambda b, i: (b * nt + i, 0)), const(g2), const(wq),
                  pl.BlockSpec((N_MEM, D_MODEL), lambda b, i: (b, 0)),
                  pl.BlockSpec((N_MEM, D_MODEL), lambda b, i: (b, 1)),
                  const(wo)],
        out_specs=pl.BlockSpec((tq, D_MODEL), lambda b, i: (b * nt + i, 0)),
        out_shape=jax.ShapeDtypeStruct((bsz * t, D_MODEL), F32),
        compiler_params=_cparams(("parallel", "parallel")),
    )(x, g2, wq, kv, kv, wo)


def _xattn_step_kernel(bb, q_ref, k_ref, v_ref, o_ref):
    for i in range(bb):
        t = k_ref[i] * q_ref[i]
        s = jnp.sum(t, axis=-1, keepdims=True) * (XHD ** -0.5)
        e = jnp.exp(s - jnp.max(s, axis=0, keepdims=True))
        pr = e / jnp.sum(e, axis=0, keepdims=True)
        o_ref[i] = jnp.sum(pr * v_ref[i], axis=0)


def _xattn_step(q, cache_k, cache_v, layer):
    bsz = q.shape[0]
    bb = 4
    cache_spec = pl.BlockSpec((None, bb, N_MEM, XH, XHD), lambda i: (layer, i, 0, 0, 0))
    return pl.pallas_call(
        functools.partial(_xattn_step_kernel, bb),
        grid=(bsz // bb,),
        in_specs=[pl.BlockSpec((bb, XH, XHD), lambda i: (i, 0, 0)), cache_spec, cache_spec],
        out_specs=pl.BlockSpec((bb, XH, XHD), lambda i: (i, 0, 0)),
        out_shape=jax.ShapeDtypeStruct((bsz, XH, XHD), F32),
        compiler_params=_cparams(("parallel",)),
    )(q.reshape(bsz, XH, XHD), cache_k, cache_v).reshape(bsz, D_MODEL)


def _step_lanes_kernel(rw_ref, hg_ref, ml_ref, wkv_ref, hgs_ref, c_ref, n_ref, m_ref, a0_ref, a1_ref, a2_ref,
                       orw_ref, ohg_ref, oml_ref, wkv_o, hgs_o, c_o, n_o, m_o):
    del a0_ref, a1_ref, a2_ref
    csum = lambda a: jnp.sum(a, axis=0, keepdims=True)
    r, k, w, kk, ka = rw_ref[0], rw_ref[1], rw_ref[3], rw_ref[4], rw_ref[5]
    for i in range(HEAD_DIM):
        s = wkv_ref[i]
        s = s * w - csum(s * kk) * ka + rw_ref[2, i:i + 1, :] * k
        orw_ref[i:i + 1, :] = csum(s * r)
        wkv_o[i] = s
    vh = hg_ref[3]
    acc = jnp.zeros_like(vh)
    for i in range(HEAD_DIM):
        s = hgs_ref[i] * hg_ref[1, i:i + 1, :] + hg_ref[2, i:i + 1, :] * vh
        acc = acc + hg_ref[0, i:i + 1, :] * s
        hgs_o[i] = s
    ohg_ref[...] = acc
    q, k = ml_ref[0], ml_ref[1]
    li, lf = ml_ref[3, 0:1, :], ml_ref[4, 0:1, :]
    m_old = m_ref[...]
    m_new = jnp.maximum(lf + m_old, li)
    gp = jnp.exp(lf + m_old - m_new)
    gs = jnp.exp(li - m_new)
    n = gp * n_ref[...] + gs * k
    scale = 1.0 / jnp.maximum(jnp.abs(csum(n * q)), jnp.exp(-m_new))
    gk = gs * k
    for i in range(HEAD_DIM):
        s = c_ref[i] * gp + ml_ref[2, i:i + 1, :] * gk
        oml_ref[i:i + 1, :] = csum(s * q) * scale
        c_o[i] = s
    n_o[...] = n
    m_o[...] = m_new


def _ssm_step_kernel(xdt_ref, bn_ref, cn_ref, dec_ref, ssm_ref, a_ref, o_ref, ssm_o):
    del a_ref
    ri = lax.broadcasted_iota(jnp.int32, (HEAD_DIM, HEAD_DIM), 0)
    ci = lax.broadcasted_iota(jnp.int32, (HEAD_DIM, HEAD_DIM), 1)
    eye = (ri == ci).astype(F32)
    col = lambda r: jnp.sum(eye * r, axis=-1, keepdims=True)
    row = lambda c: jnp.sum(eye * c, axis=-2, keepdims=True)
    s = ssm_ref[...] * dec_ref[...] + col(xdt_ref[...]) * bn_ref[...]
    o_ref[...] = row(jnp.sum(s * cn_ref[...], axis=-1, keepdims=True))
    ssm_o[...] = s


def _step_lanes(rw_t, hg_t, ml_t, states, acc, layer):
    bsz = rw_t.shape[1]
    fields = lambda a: a.reshape(-1, NH, HEAD_DIM, bsz)
    seq_ins = (fields(rw_t), fields(hg_t), fields(ml_t))
    seq_spec = lambda a: pl.BlockSpec((a.shape[0], None, HEAD_DIM, bsz), lambda h: (0, h, 0, 0))
    st_layer = lambda a: pl.BlockSpec((None, None) + a.shape[2:], lambda h: (layer, h) + (0,) * (a.ndim - 2))
    st_head = lambda a: pl.BlockSpec((None,) + a.shape[2:], lambda h: (h,) + (0,) * (a.ndim - 2))
    o_shape = jax.ShapeDtypeStruct((NH, HEAD_DIM, bsz), F32)
    o_spec = pl.BlockSpec((None, HEAD_DIM, bsz), lambda h: (h, 0, 0))
    big, small = states[:3], states[3:]
    n_in = len(seq_ins) + len(states)
    outs = pl.pallas_call(
        _step_lanes_kernel,
        grid=(NH,),
        in_specs=[seq_spec(a) for a in seq_ins] + [st_layer(a) for a in states]
        + [pl.BlockSpec(memory_space=pl.ANY)] * len(big),
        out_specs=[o_spec] * 3 + [st_layer(a) for a in big] + [st_head(a) for a in small],
        out_shape=[o_shape] * 3 + [jax.ShapeDtypeStruct(a.shape, F32) for a in big]
        + [jax.ShapeDtypeStruct(a.shape[1:], F32) for a in small],
        input_output_aliases={n_in + j: 3 + j for j in range(len(big))},
        compiler_params=_cparams(("parallel",)),
    )(*seq_ins, *states, *acc)
    o_parts = [o.reshape(GW, bsz) for o in outs[:3]]
    return o_parts, outs[3:6], outs[6], outs[7]


def _ssm_step(ss, state_ssm, acc, layer):
    bsz = ss.shape[0]
    bb = SUBLANES
    per_head = lambda a: a.reshape(bsz, NH, HEAD_DIM)[:, :, 0].reshape(bsz, NH, 1, 1)
    groups = lambda a: jnp.repeat(a.reshape(bsz, 2, 1, SSM_STATE), NH // 2, axis=1)
    seq_ins = (ss[:, 0:GW].reshape(bsz, NH, 1, HEAD_DIM), groups(ss[:, 512:768]), groups(ss[:, 768:1024]),
               per_head(ss[:, GW:2 * GW]))
    blk = lambda a: pl.BlockSpec((bb,) + a.shape[1:], lambda i: (i,) + (0,) * (a.ndim - 1))
    st_layer = pl.BlockSpec((None, bb) + state_ssm.shape[2:], lambda i: (layer, i, 0, 0, 0))
    o, ssm_new = pl.pallas_call(
        _ssm_step_kernel,
        grid=(bsz // bb,),
        in_specs=[blk(a) for a in seq_ins] + [st_layer, pl.BlockSpec(memory_space=pl.ANY)],
        out_specs=[pl.BlockSpec((bb, NH, 1, HEAD_DIM), lambda i: (i, 0, 0, 0)), st_layer],
        out_shape=[jax.ShapeDtypeStruct((bsz, NH, 1, HEAD_DIM), F32), jax.ShapeDtypeStruct(state_ssm.shape, F32)],
        input_output_aliases={len(seq_ins) + 1: 1},
        compiler_params=_cparams(("parallel",)),
    )(*seq_ins, state_ssm, acc)
    return o.reshape(bsz, GW), ssm_new


def kernel(x_prompt, x_sample, state_rwkv_wkv, state_rwkv_shift, state_hgrn, state_ssm, state_ssm_conv, state_mlstm_C, state_mlstm_n, state_mlstm_m, cache_mem_k, cache_mem_v, mem_prompt, norm_mix_g, w_in, w_out, rwkv_mu, rwkv_w0, rwkv_w_up, rwkv_a0, rwkv_a_up, rwkv_g_up, rwkv_k_k, rwkv_k_a, rwkv_r_k, rwkv_ln_g, rwkv_ln_b, hgrn_lb_logits, hgrn_norm_g, ssm_conv_w, ssm_conv_b, ssm_dt_bias, ssm_A_log, ssm_D, ssm_norm_g, mlstm_i_bias, mlstm_f_bias, mlstm_norm_g, norm_x_g, norm_mem_g, xattn_wq, xattn_wk, xattn_wv, xattn_wo, norm_ff_g, ff_w1, ff_w2, final_norm_g):
    bp, tp, _ = x_prompt.shape
    bs = x_sample.shape[0]
    seg64, seg128 = _seg_matrix(HEAD_DIM), _seg_matrix(128)
    seg64_b = seg64.astype(BF16)
    gate_ex = _gate_expand()
    tril_c = jnp.asarray(np.tril(np.ones((CHUNK, CHUNK), np.float32)))

    def block_ones(sub):
        blk = np.arange(CHUNK) // sub
        return jnp.asarray((blk[:, None] == blk[None, :]).astype(np.float32))

    ones_blk, ones_blk_h = block_ones(SUB), block_ones(HGRN_SUB)
    tril_blk, tril_blk_h = ones_blk * tril_c, ones_blk_h * tril_c
    sel_c = jnp.asarray((np.arange(GW)[:, None] // HEAD_DIM == np.arange(NH * SUB)[None, :] // SUB), BF16)
    rep = lambda a: jnp.repeat(a, HEAD_DIM)
    zeros = lambda *s: jnp.zeros(s, F32)

    hp = x_prompt.reshape(bp * tp, D_MODEL)
    hs = x_sample.reshape(bs, D_MODEL)
    mem = mem_prompt.reshape(bp * N_MEM, D_MODEL)
    p_states, s_small, p_mem = [], [], []
    to_lanes = lambda a: jnp.moveaxis(a, 1, -1)
    from_lanes = lambda a: jnp.moveaxis(a, -1, 1)
    lane_states = (to_lanes(state_rwkv_wkv), to_lanes(state_hgrn), to_lanes(state_mlstm_C), to_lanes(state_mlstm_n),
                   to_lanes(state_mlstm_m)[:, :, None, :])
    s_big = tuple(jnp.zeros(a.shape, F32) for a in lane_states[:3])
    s_ssm = jnp.zeros(state_ssm.shape, F32)

    for l in range(DEPTH):
        last = l == DEPTH - 1
        w_in_re = _in_weight(w_in, l)
        pv = _pad_rows(jnp.stack([rwkv_w0[l], rwkv_a0[l], rwkv_k_k[l], rwkv_k_a[l], rwkv_r_k[l].reshape(GW),
                                  rep(ssm_dt_bias[l]), rep(ssm_A_log[l]), rep(ssm_D[l]), rep(mlstm_i_bias[l]),
                                  rep(mlstm_f_bias[l])]), 16)
        lora = jnp.stack([jnp.pad(w, ((r0, LANES - r0 - w.shape[0]), (0, 0)))
                          for r0, w in zip(LORA_ROWS, (rwkv_w_up[l], rwkv_a_up[l], rwkv_g_up[l]))])
        conv = _pad_rows(jnp.concatenate([ssm_conv_w[l], ssm_conv_b[l][None]], axis=0), 8)
        mu = rwkv_mu[l].reshape(1, RW_W)
        gp1 = _pad_rows(jnp.stack([_lane_row((GT_DT, ssm_dt_bias[l])), _lane_row((GT_DT, ssm_A_log[l]))]), 8)
        gp2 = _pad_rows(_lane_row((GT_I, mlstm_i_bias[l]), (GT_F, mlstm_f_bias[l]))[None], 8)
        pd = _pad_rows(jnp.stack([rep(ssm_D[l]), ssm_norm_g[l]]), 8)
        pn_h = _pad_rows(hgrn_norm_g[l][None], 8)
        pn_m = _pad_rows(mlstm_norm_g[l][None], 8)
        pq = _pad_rows(jnp.stack([rwkv_ln_g[l], rwkv_ln_b[l], hgrn_norm_g[l], ssm_norm_g[l], mlstm_norm_g[l]]), 8)
        w_out_b = w_out[l].astype(BF16)
        wq_b, wo_b = xattn_wq[l].astype(BF16), xattn_wo[l].astype(BF16)
        wkv_b = jnp.concatenate([xattn_wk[l], xattn_wv[l]], axis=1).astype(BF16)
        w1_b, w2_b = ff_w1[l].astype(BF16), ff_w2[l].astype(BF16)

        kv = _norm_matmul(mem, norm_mem_g[l], wkv_b)
        p_mem.append((kv[:, :D_MODEL].reshape(bp, N_MEM, XH, XHD), kv[:, D_MODEL:].reshape(bp, N_MEM, XH, XHD)))
        proj = _norm_matmul(hp, norm_mix_g[l], w_in_re)
        rw_scan, rw_post = _prep_rw_seq(proj, zeros(bp, SUBLANES, RW_W), mu, pv, lora, seg64, bp, tp)
        (m_h, m_s, m_m, o_rw), (st_h, st_s, st_c, st_n, st_m, st_r) = _mixers_chunk(
            proj, rw_scan, (hgrn_lb_logits, pn_h, tril_blk, ones_blk, seg64_b, seg64, conv, gp1, gp2, pd, pn_m, tril_c,
                            sel_c, tril_blk_h, ones_blk_h), l, bp, tp)
        proj3 = proj.reshape(bp, tp, IN_RE)
        diag_blocks = lambda s: jnp.stack(
            [s.reshape(bp, NH, HEAD_DIM, NH, HEAD_DIM)[:, h, :, h, :] for h in range(NH)], axis=1)
        p_states.append((diag_blocks(st_r), proj3[:, -1, RW_OFF:RW_OFF + RW_W], st_h.transpose(0, 1, 3, 2),
                         st_s.transpose(0, 1, 3, 2), proj3[:, tp - 3:, SS_OFF + 256:SS_OFF + 1024],
                         st_c.transpose(0, 1, 3, 2), st_n[:, :NH], st_m[:, 0, :NH]))
        hp = _post_call(_post_seq_kernel, hp, (o_rw, rw_post, m_h, m_s, m_m), (pq, seg64, w_out_b))
        hp = _xattn_seq(hp, norm_x_g[l], wq_b, kv, wo_b, bp, tp)
        hp = _ffn(hp, norm_ff_g[l], w1_b, w2_b, final_norm_g, last)

        proj = _norm_matmul(hs, norm_mix_g[l], w_in_re)
        cb = state_ssm_conv[l]
        prep = _prep_step(proj, state_rwkv_shift[l], cb[:, 0], cb[:, 1], cb[:, 2],
                          (mu, pv, lora, seg64, hgrn_lb_logits, conv, gate_ex), l)
        rw_t, hg_t, ml_t, rw_post, hg_post, ss_scan, ss_post, ml_post = prep
        (o_rw_t, o_hg_t, o_ml_t), s_big, n_n, m_n = _step_lanes(rw_t, hg_t, ml_t, lane_states, s_big, l)
        o_ss, s_ssm = _ssm_step(ss_scan, state_ssm, s_ssm, l)
        conv_new = jnp.concatenate([cb[:, 1:], proj[:, None, SS_OFF + 256:SS_OFF + 1024]], axis=1)
        s_small.append((proj[:, RW_OFF:RW_OFF + RW_W], conv_new, n_n, m_n))
        hs = _post_step(hs, (o_rw_t, o_hg_t, o_ss, o_ml_t, rw_post, hg_post, ss_post, ml_post, pq, seg64, seg128,
                             w_out_b))
        q = _norm_matmul(hs, norm_x_g[l], wq_b)
        att = _xattn_step(q, cache_mem_k, cache_mem_v, l)
        hs = _mm_res(att, wo_b, hs)
        hs = _ffn(hs, norm_ff_g[l], w1_b, w2_b, final_norm_g, last)

    y_prompt = hp.reshape(bp, tp, D_MODEL)
    y_sample = hs.reshape(bs, 1, D_MODEL)
    p_out = [jnp.stack(s) for s in zip(*p_states)]
    s_shift, s_conv, s_n, s_m = [jnp.stack(s) for s in zip(*s_small)]
    s_n, s_m = from_lanes(s_n), from_lanes(s_m[:, :, 0, :])
    s_wkv, s_hgrn, s_c = (from_lanes(a) for a in s_big)
    p_mem_k = jnp.stack([kv[0] for kv in p_mem])
    p_mem_v = jnp.stack([kv[1] for kv in p_mem])
    return (y_prompt, y_sample, *p_out, p_mem_k, p_mem_v, s_wkv, s_shift, s_hgrn, s_ssm, s_conv, s_c, s_n, s_m)
```

```python
import functools

import numpy as np
import jax
import jax.numpy as jnp
from jax import lax
from jax.experimental import pallas as pl
from jax.experimental.pallas import tpu as pltpu

F32 = jnp.float32
BF16 = jnp.bfloat16
HI = lax.Precision.HIGHEST

D_MODEL = 1024
DEPTH = 2
NH = 4
HEAD_DIM = 64
GW = 256
SSM_STATE = 128
N_MEM = 256
XH = 4
XHD = 256
NORM_EPS = 1e-6
RWKV_GN_EPS = 64e-5
LANES = 128
SUBLANES = 8
VMEM_LIMIT = 48 * 1024 * 1024
CHUNK = 128
SUB = 32
HGRN_SUB = 16

RW_OFF, RW_W = 0, 896
HG_OFF = 896
SS_OFF = 1920
G1_OFF = 2944
ML_OFF = 3072
G2_OFF = 3840
MO_OFF = 3968
IN_RE = 4224
GT_DT, GT_I, GT_F = 0, 0, 4
LORA_ROWS = (0, 32, 64)


def _in_weight_kernel(w_ref, o_ref):
    w = w_ref[...]
    lane = lax.broadcasted_iota(jnp.int32, (w.shape[0], LANES), 1)
    gate_tile = lambda c0, n: jnp.where(lane < n, w[:, c0:c0 + LANES], 0.0)
    o_ref[:, 0:G1_OFF] = w[:, 0:G1_OFF].astype(BF16)
    o_ref[:, G1_OFF:ML_OFF] = gate_tile(2944, NH).astype(BF16)
    o_ref[:, ML_OFF:G2_OFF] = w[:, 2948:3716].astype(BF16)
    o_ref[:, G2_OFF:MO_OFF] = gate_tile(3716, 2 * NH).astype(BF16)
    o_ref[:, MO_OFF:IN_RE] = w[:, 3724:3980].astype(BF16)


def _in_weight(w_in, layer):
    k, n = w_in.shape[1:]
    tr = 256
    return pl.pallas_call(
        _in_weight_kernel,
        grid=(k // tr,),
        in_specs=[pl.BlockSpec((None, tr, n), lambda i: (layer, i, 0))],
        out_specs=pl.BlockSpec((tr, IN_RE), lambda i: (i, 0)),
        out_shape=jax.ShapeDtypeStruct((k, IN_RE), BF16),
        compiler_params=_cparams(("parallel",)),
    )(w_in)


def _seg_matrix(width):
    i = np.arange(GW)
    return jnp.asarray((i[:, None] // width == i[None, :] // width).astype(np.float32))


def _gate_expand():
    e = np.zeros((3, LANES, GW), np.float32)
    for s, off in enumerate((GT_DT, GT_I, GT_F)):
        for h in range(NH):
            e[s, off + h, h * HEAD_DIM:(h + 1) * HEAD_DIM] = 1.0
    return jnp.asarray(e)


def _lane_row(*pieces):
    z = jnp.zeros((LANES,), F32)
    for off, val in pieces:
        z = z.at[off:off + val.shape[0]].set(val)
    return z


def _pad_rows(a, n):
    return jnp.pad(a, ((0, n - a.shape[0]), (0, 0)))


def _cparams(sem):
    return pltpu.CompilerParams(dimension_semantics=sem, vmem_limit_bytes=VMEM_LIMIT)


def _softplus(x):
    return jnp.maximum(x, 0.0) + jnp.log(1.0 + jnp.exp(-jnp.abs(x)))


def _sigmoid(x):
    return jax.nn.sigmoid(x)


def _rms(x, g):
    ms = jnp.mean(x * x, axis=-1, keepdims=True)
    return x * lax.rsqrt(ms + NORM_EPS) * g


def _dot(a, b):
    return jnp.dot(a.astype(BF16), b.astype(BF16), preferred_element_type=F32)


def _dot_nt(a, b):
    return lax.dot_general(a.astype(BF16), b.astype(BF16), (((1,), (1,)), ((), ())), preferred_element_type=F32)


def _dot_tn(a, b):
    return lax.dot_general(a.astype(BF16), b.astype(BF16), (((0,), (0,)), ((), ())), preferred_element_type=F32)


def _dot_hi(a, b):
    return jnp.dot(a, b, precision=HI, preferred_element_type=F32)


def _split3(x):
    hi = x.astype(BF16)
    r = x - hi.astype(F32)
    mid = r.astype(BF16)
    return hi, mid, (r - mid.astype(F32)).astype(BF16)


def _dot01(parts, m):
    mb = m.astype(BF16)
    hi, mid, lo = (jnp.dot(p, mb, preferred_element_type=F32) for p in parts)
    return hi + mid + lo


def _norm_matmul_kernel(x_ref, g_ref, w_ref, o_ref, xn_ref):
    @pl.when(pl.program_id(1) == 0)
    def _():
        xn_ref[...] = _rms(x_ref[...], g_ref[...]).astype(BF16)

    o_ref[...] = jnp.dot(xn_ref[...], w_ref[...], preferred_element_type=F32)


def _row_tile(m):
    return min(m, 1024)


def _norm_matmul(x, g, w):
    m, k = x.shape
    n = w.shape[1]
    tm = _row_tile(m)
    tn = next(c for c in (1536, 1408, 1024, 512, n) if n % c == 0)
    return pl.pallas_call(
        _norm_matmul_kernel,
        grid=(m // tm, n // tn),
        in_specs=[pl.BlockSpec((tm, k), lambda i, j: (i, 0)),
                  pl.BlockSpec((1, k), lambda i, j: (0, 0)),
                  pl.BlockSpec((k, tn), lambda i, j: (0, j))],
        out_specs=pl.BlockSpec((tm, tn), lambda i, j: (i, j)),
        out_shape=jax.ShapeDtypeStruct((m, n), F32),
        scratch_shapes=[pltpu.VMEM((tm, k), BF16)],
        compiler_params=_cparams(("parallel", "arbitrary")),
    )(x, g.reshape(1, k), w)


def _mm_res_kernel(a_ref, w_ref, res_ref, o_ref):
    o_ref[...] = res_ref[...] + jnp.dot(a_ref[...].astype(BF16), w_ref[...], preferred_element_type=F32)


def _mm_res(a, w, res):
    m, k = a.shape
    n = w.shape[1]
    tm = min(m, 512)
    return pl.pallas_call(
        _mm_res_kernel,
        grid=(m // tm,),
        in_specs=[pl.BlockSpec((tm, k), lambda i: (i, 0)),
                  pl.BlockSpec((k, n), lambda i: (0, 0)),
                  pl.BlockSpec((tm, n), lambda i: (i, 0))],
        out_specs=pl.BlockSpec((tm, n), lambda i: (i, 0)),
        out_shape=jax.ShapeDtypeStruct((m, n), F32),
        compiler_params=_cparams(("parallel",)),
    )(a, w, res)


def _ffn_kernel(final, x_ref, g_ref, w1_ref, w2_ref, fg_ref, o_ref, xn_ref, acc_ref):
    j = pl.program_id(1)

    @pl.when(j == 0)
    def _():
        xn_ref[...] = _rms(x_ref[...], g_ref[...]).astype(BF16)
        acc_ref[...] = jnp.zeros_like(acc_ref)

    h = jnp.dot(xn_ref[...], w1_ref[...], preferred_element_type=F32)
    h = jnp.square(jnp.maximum(h, 0.0))
    acc_ref[...] += jnp.dot(h.astype(BF16), w2_ref[...], preferred_element_type=F32)

    @pl.when(j == pl.num_programs(1) - 1)
    def _():
        y = x_ref[...] + acc_ref[...]
        o_ref[...] = _rms(y, fg_ref[...]) if final else y


def _ffn(x, g, w1, w2, final_g, final):
    m, k = x.shape
    ff = w1.shape[1]
    tm = _row_tile(m)
    tf = 1024
    return pl.pallas_call(
        functools.partial(_ffn_kernel, final),
        grid=(m // tm, ff // tf),
        in_specs=[pl.BlockSpec((tm, k), lambda i, j: (i, 0)),
                  pl.BlockSpec((1, k), lambda i, j: (0, 0)),
                  pl.BlockSpec((k, tf), lambda i, j: (0, j)),
                  pl.BlockSpec((tf, k), lambda i, j: (j, 0)),
                  pl.BlockSpec((1, k), lambda i, j: (0, 0))],
        out_specs=pl.BlockSpec((tm, k), lambda i, j: (i, 0)),
        out_shape=jax.ShapeDtypeStruct((m, k), F32),
        scratch_shapes=[pltpu.VMEM((tm, k), BF16), pltpu.VMEM((tm, k), F32)],
        compiler_params=_cparams(("parallel", "arbitrary")),
    )(x, g.reshape(1, k), w1, w2, final_g.reshape(1, k))


def _shift_rows(x, tail, j):
    xr = pltpu.roll(x, j, 0)
    tr = pltpu.roll(tail, j, 0)
    rows = lax.broadcasted_iota(jnp.int32, tr.shape, 0)
    first = jnp.where(rows < j, tr, xr[0:SUBLANES])
    return jnp.concatenate([first, xr[SUBLANES:]], axis=0)


def _hgrn_lower_bound(logit_ref, layer):
    lg = logit_ref[...]
    e = jnp.exp(lg - jnp.max(lg, axis=0, keepdims=True))
    prb = e / jnp.sum(e, axis=0, keepdims=True)
    return jnp.sum(prb[0:layer + 1], axis=0, keepdims=True) - prb[0:1]


def _conv_silu(u, c3, c2, c1, conv_ref):
    cw = conv_ref[...]
    conv = cw[4:5] + c3 * cw[0:1] + c2 * cw[1:2] + c1 * cw[2:3] + u * cw[3:4]
    return conv * _sigmoid(conv)


def _prep_rwkv(pr, prev_rw, mu_ref, pv_ref, lora_ref, seg_ref, rw_scan, rw_post, log_decay):
    pv = pv_ref[...]
    row = lambda i: pv[i:i + 1, :]
    seg = seg_ref[...]
    pm = pr + (prev_rw - pr) * mu_ref[...]
    r, k, v, low = pm[:, 0:256], pm[:, 256:512], pm[:, 512:768], pm[:, 768:896]
    w_log = -_softplus(-(row(0) + _dot_hi(jnp.tanh(low), lora_ref[0]))) - 0.5
    a = _sigmoid(row(1) + _dot_hi(low, lora_ref[1]))
    kk = k * row(2)
    kk = kk / jnp.maximum(jnp.sqrt(_dot01(_split3(kk * kk), seg)), 1e-12)
    k2 = k * (1.0 + (a - 1.0) * row(3))
    rw_scan[:, 0:256] = r
    rw_scan[:, 256:512] = k2
    rw_scan[:, 512:768] = v
    rw_scan[:, 768:1024] = -jnp.exp(w_log) if log_decay else jnp.exp(-jnp.exp(w_log))
    rw_scan[:, 1024:1280] = kk
    rw_scan[:, 1280:1536] = kk * a
    rw_post[:, 0:256] = _dot01(_split3(r * k2 * row(4)), seg) * v
    rw_post[:, 256:512] = _dot_hi(_sigmoid(low), lora_ref[2])


def _prep_rw_seq_kernel(p_ref, prev8_ref, tail0_ref, mu_ref, pv_ref, lora_ref, seg_ref, rw_scan, rw_post):
    tail = jnp.where(pl.program_id(1) == 0, tail0_ref[0], prev8_ref[...])
    pr = p_ref[...]
    _prep_rwkv(pr, _shift_rows(pr, tail, 1), mu_ref, pv_ref, lora_ref, seg_ref, rw_scan, rw_post, True)


def _prep_rw_seq(proj, tail0, mu, pv, lora, seg, bsz, t):
    tt = 512
    nt = t // tt
    m = bsz * t
    c2 = lambda b, i: (0, 0)
    return pl.pallas_call(
        _prep_rw_seq_kernel,
        grid=(bsz, nt),
        in_specs=[pl.BlockSpec((tt, RW_W), lambda b, i: (b * nt + i, 0)),
                  pl.BlockSpec((SUBLANES, RW_W), lambda b, i: (jnp.maximum((b * nt + i) * (tt // SUBLANES) - 1, 0), 0)),
                  pl.BlockSpec((1, SUBLANES, RW_W), lambda b, i: (b, 0, 0)),
                  pl.BlockSpec((1, RW_W), c2), pl.BlockSpec((16, GW), c2),
                  pl.BlockSpec((3, LANES, GW), lambda b, i: (0, 0, 0)), pl.BlockSpec((GW, GW), c2)],
        out_specs=[pl.BlockSpec((tt, 1536), lambda b, i: (b * nt + i, 0)),
                   pl.BlockSpec((tt, 512), lambda b, i: (b * nt + i, 0))],
        out_shape=[jax.ShapeDtypeStruct((m, 1536), F32), jax.ShapeDtypeStruct((m, 512), F32)],
        compiler_params=_cparams(("parallel", "parallel")),
    )(proj, proj, tail0, mu, pv, lora, seg)


PREP_SCAN_WIDTHS = (1536, 1024, 1280)
PREP_ROW_WIDTHS = (512, 256, 1024, 512, 256)


def _prep_step_kernel(layer, p_ref, prev_ref, c3_ref, c2_ref, c1_ref, mu_ref, pv_ref, lora_ref, seg_ref, logit_ref,
                      conv_ref, ex_ref, rw_t, hg_t, ml_t, rw_post, hg_post, ss_scan, ss_post, ml_post, rw_scan,
                      hg_scan, ml_scan):
    pv = pv_ref[...]
    row = lambda i: pv[i:i + 1, :]
    _prep_rwkv(p_ref[:, RW_OFF:RW_OFF + RW_W], prev_ref[...], mu_ref, pv_ref, lora_ref, seg_ref, rw_scan, rw_post,
               False)
    g1 = p_ref[:, G1_OFF:G1_OFF + LANES]
    g2 = p_ref[:, G2_OFF:G2_OFF + LANES]
    dt_pre, i_pre, f_pre_m = _dot_hi(g1, ex_ref[0]), _dot_hi(g2, ex_ref[1]), _dot_hi(g2, ex_ref[2])

    lb = _hgrn_lower_bound(logit_ref, layer)
    q_pre = p_ref[:, HG_OFF:HG_OFF + 256]
    f_pre = p_ref[:, HG_OFF + 256:HG_OFF + 512]
    gg = p_ref[:, HG_OFF + 768:HG_OFF + 1024]
    hg_scan[:, 0:256] = q_pre * _sigmoid(q_pre)
    hg_scan[:, 256:512] = lb + (1.0 - lb) * _sigmoid(f_pre)
    hg_scan[:, 512:768] = (1.0 - lb) * _sigmoid(-f_pre)
    hg_scan[:, 768:1024] = p_ref[:, HG_OFF + 512:HG_OFF + 768]
    hg_post[...] = gg * _sigmoid(gg)

    z = p_ref[:, SS_OFF:SS_OFF + 256]
    xbc = _conv_silu(p_ref[:, SS_OFF + 256:SS_OFF + 1024], c3_ref[...], c2_ref[...], c1_ref[...], conv_ref)
    xs = xbc[:, 0:256]
    dt = _softplus(dt_pre + row(5))
    ss_scan[:, 0:256] = xs * dt
    ss_scan[:, 256:512] = jnp.exp(dt * (-jnp.exp(row(6))))
    ss_scan[:, 512:1024] = xbc[:, 256:768]
    ss_post[:, 0:256] = xs * row(7)
    ss_post[:, 256:512] = z * _sigmoid(z)

    ml_scan[:, 0:256] = p_ref[:, ML_OFF:ML_OFF + 256]
    ml_scan[:, 256:512] = p_ref[:, ML_OFF + 256:ML_OFF + 512] * (HEAD_DIM ** -0.5)
    ml_scan[:, 512:768] = p_ref[:, ML_OFF + 512:ML_OFF + 768]
    ml_scan[:, 768:1024] = i_pre + row(8)
    ml_scan[:, 1024:1280] = -_softplus(-(f_pre_m + row(9)))
    ml_post[...] = _sigmoid(p_ref[:, MO_OFF:MO_OFF + 256])
    rw_t[...] = rw_scan[...].T
    hg_t[...] = hg_scan[...].T
    ml_t[...] = ml_scan[...].T


def _prep_step(proj, prev_rw, c3, c2, c1, params, layer):
    m = proj.shape[0]
    full = lambda a: pl.BlockSpec(a.shape, lambda i: (0,) * a.ndim)
    ins = (proj, prev_rw, c3, c2, c1) + tuple(params)
    shapes = [(w, m) for w in PREP_SCAN_WIDTHS] + [(m, w) for w in PREP_ROW_WIDTHS]
    return pl.pallas_call(
        functools.partial(_prep_step_kernel, layer),
        grid=(1,),
        in_specs=[full(a) for a in ins],
        out_specs=[pl.BlockSpec(s, lambda i: (0, 0)) for s in shapes],
        out_shape=[jax.ShapeDtypeStruct(s, F32) for s in shapes],
        scratch_shapes=[pltpu.VMEM((m, w), F32) for w in PREP_SCAN_WIDTHS],
        compiler_params=_cparams(("arbitrary",)),
    )(*ins)


def _ssm_chunk_body(ss_ref, gt_ref, conv_ref, gp_ref, pd_ref, tril_ref, o_ref, ht_ref, tail_ref):
    @pl.when(pl.program_id(1) == 0)
    def _():
        ht_ref[...] = jnp.zeros_like(ht_ref)
        tail_ref[...] = jnp.zeros_like(tail_ref)

    yield
    L = ss_ref.shape[0]
    u = ss_ref[:, 256:1024]
    tail = tail_ref[...]
    c1, c2, c3 = (_shift_rows(u, tail, j) for j in (1, 2, 3))
    tail_ref[...] = u[L - SUBLANES:L]
    xbc = _conv_silu(u, c3, c2, c1, conv_ref)
    z = ss_ref[:, 0:256]
    zs = z * _sigmoid(z)
    dt = _softplus(gt_ref[...] + gp_ref[0:1])
    la = dt * (-jnp.exp(gp_ref[1:2]))
    tril = tril_ref[...]
    mask = tril > 0.5
    cum = _dot_hi(tril, la)
    yield
    cum_t = cum.T
    pd = pd_ref[...]
    hsl = lambda h: slice(h * HEAD_DIM, (h + 1) * HEAD_DIM)
    bgs = [xbc[:, 256 + SSM_STATE * g:256 + SSM_STATE * (g + 1)] for g in range(2)]
    cgs = [xbc[:, 512 + SSM_STATE * g:512 + SSM_STATE * (g + 1)] for g in range(2)]
    scores = [_dot_nt(cgs[g], bgs[g]) for g in range(2)]
    hts = [ht_ref[0, h] for h in range(NH)]
    inter = [_dot(cgs[h // 2], hts[h]) for h in range(NH)]
    bg_t = [bgs[g].T for g in range(2)]
    ccs = [cum[:, GT_DT + h:GT_DT + h + 1] for h in range(NH)]
    dtc = [dt[:, GT_DT + h:GT_DT + h + 1] for h in range(NH)]
    lms = []
    for h in range(NH):
        cr = cum_t[GT_DT + h:GT_DT + h + 1, :]
        lms.append(jnp.where(mask, jnp.exp(jnp.where(mask, ccs[h] - cr, 0.0)), 0.0))
    yield
    ydots = [_dot(scores[h // 2] * lms[h], xbc[:, hsl(h)] * dtc[h]) for h in range(NH)]
    cends = [ccs[h][L - 1:L, :] for h in range(NH)]
    hdots = [_dot(bg_t[h // 2], xbc[:, hsl(h)] * (jnp.exp(cends[h] - ccs[h]) * dtc[h])) for h in range(NH)]
    yield
    ys = []
    for h in range(NH):
        y = ydots[h] + inter[h] * jnp.exp(ccs[h])
        ys.append((y + xbc[:, hsl(h)] * pd[0:1, hsl(h)]) * zs[:, hsl(h)])
        ht_ref[0, h] = jnp.exp(cends[h]) * hts[h] + hdots[h]
    for g in range(2):
        y0, y1 = ys[2 * g], ys[2 * g + 1]
        ms = jnp.sum(y0 * y0, axis=-1, keepdims=True) + jnp.sum(y1 * y1, axis=-1, keepdims=True)
        scale = lax.rsqrt(ms * (1.0 / (2 * HEAD_DIM)) + NORM_EPS)
        for h in (2 * g, 2 * g + 1):
            o_ref[:, hsl(h)] = ys[h] * scale * pd[1:2, hsl(h)]


def _mlstm_chunk_body(qkv_ref, og_ref, gt_ref, gp_ref, pn_ref, tril_ref, o_ref, ct_ref, n_ref, m_ref):
    @pl.when(pl.program_id(1) == 0)
    def _():
        ct_ref[...] = jnp.zeros_like(ct_ref)
        n_ref[...] = jnp.zeros_like(n_ref)
        m_ref[...] = jnp.zeros_like(m_ref)

    yield
    L = qkv_ref.shape[0]
    x = gt_ref[...] + gp_ref[0:1]
    lf = -_softplus(-x)
    tril = tril_ref[...]
    mask = tril > 0.5
    b = _dot_hi(tril, lf)
    yield
    b_t = b.T
    x_t = x.T
    m_row = m_ref[0]
    m_new_row = m_row
    lane = lax.broadcasted_iota(jnp.int32, m_row.shape, 1)
    pn = pn_ref[...]
    hsl = lambda h, base=0: slice(base + h * HEAD_DIM, base + (h + 1) * HEAD_DIM)
    for h in range(NH):
        b_col, b_row = b[:, GT_F + h:GT_F + h + 1], b_t[GT_F + h:GT_F + h + 1, :]
        li_col, li_row = x[:, GT_I + h:GT_I + h + 1], x_t[GT_I + h:GT_I + h + 1, :]
        m_prev = m_row[:, h:h + 1]
        log_inter = b_col + m_prev
        log_intra = jnp.where(mask, b_col - b_row + li_row, -jnp.inf)
        m_t = jnp.maximum(log_inter, jnp.max(log_intra, axis=-1, keepdims=True))
        w_inter = jnp.exp(log_inter - m_t)
        qh = qkv_ref[:, hsl(h)]
        kh = qkv_ref[:, hsl(h, 256)] * (HEAD_DIM ** -0.5)
        vh = qkv_ref[:, hsl(h, 512)]
        ct = ct_ref[0, h]
        qk_raw = _dot_nt(qh, kh)
        q_ct = _dot(qh, ct)
        yield
        qk = qk_raw * jnp.exp(log_intra - m_t)
        n_h = n_ref[0, h:h + 1, :]
        m_end = m_t[L - 1:L, :]
        b_end = b_col[L - 1:L, :]
        g_prev = jnp.exp(b_end + m_prev - m_end)
        kg = kh * jnp.exp(b_end - b_col + li_col - m_end)
        qk_v = _dot(qk, vh)
        kg_v = _dot_tn(kg, vh)
        yield
        num = qk_v + w_inter * q_ct
        den = jnp.sum(qk, axis=-1, keepdims=True) + w_inter * jnp.sum(qh * n_h, axis=-1, keepdims=True)
        hout = num / jnp.maximum(jnp.abs(den), jnp.exp(-m_t))
        ct_ref[0, h] = g_prev * ct + kg_v
        n_ref[0, h:h + 1, :] = g_prev * n_h + jnp.sum(kg, axis=0, keepdims=True)
        m_new_row = jnp.where(lane == h, m_end, m_new_row)
        hn = hout * lax.rsqrt(jnp.mean(hout * hout, axis=-1, keepdims=True) + NORM_EPS)
        o_ref[:, hsl(h)] = hn * pn[0:1, hsl(h)] * _sigmoid(og_ref[:, hsl(h)])
    m_ref[0] = m_new_row


def _hgrn_chunk_body(layer, hg_ref, logit_ref, pn_ref, tril_ref, ones_ref, seg_ref, bd_ref, o_ref, st_ref, q_scr,
                     b_scr, i_scr):
    @pl.when(pl.program_id(1) == 0)
    def _():
        st_ref[...] = jnp.zeros_like(st_ref)

    yield
    lb = _hgrn_lower_bound(logit_ref, layer)
    seg = seg_ref[...]
    bd = bd_ref[...]
    pn = pn_ref[...]
    sub = HGRN_SUB
    srow = lax.broadcasted_iota(jnp.int32, (sub, GW), 0)
    L = hg_ref.shape[0]

    q_pre, f_pre, v, gg = hg_ref[:, 0:256], hg_ref[:, 256:512], hg_ref[:, 512:768], hg_ref[:, 768:1024]
    q = q_pre * _sigmoid(q_pre)
    k = (1.0 - lb) * _sigmoid(-f_pre)
    lf = jnp.log(lb + (1.0 - lb) * _sigmoid(f_pre))
    b = _dot_hi(tril_ref[...], lf)
    b_tot = _dot_hi(ones_ref[...], lf)
    yield
    q_scr[...] = q
    b_scr[...] = b
    quarter = L // 4
    for part in range(4):
        tiles = []
        for t in range(part * quarter, (part + 1) * quarter):
            rows = slice((t // sub) * sub, (t // sub + 1) * sub)
            live = srow <= (t % sub)
            dec = jnp.where(live, jnp.exp(jnp.where(live, b_scr[t:t + 1, :] - b[rows], 0.0)), 0.0)
            tiles.append(dec * q_scr[t:t + 1, :] * k[rows])
        a_b = jnp.dot(jnp.concatenate(tiles, axis=0).astype(BF16), seg, preferred_element_type=F32)
        yield
        for j in range(quarter):
            t = part * quarter + j
            rows = slice((t // sub) * sub, (t // sub + 1) * sub)
            i_scr[t:t + 1, :] = jnp.sum(a_b[j * sub:(j + 1) * sub, :] * v[rows], axis=0, keepdims=True)
    qx = q * jnp.exp(b)
    kx = k * jnp.exp(b_tot - b)
    p_end = jnp.exp(b_tot)
    hsl = lambda h: slice(h * HEAD_DIM, (h + 1) * HEAD_DIM)
    sts = [st_ref[0, h] for h in range(NH)]
    v_t = v.T
    trow = lax.broadcasted_iota(jnp.int32, (L, GW), 0)
    for c in range(L // sub):
        rows = slice(c * sub, (c + 1) * sub)
        km = jnp.where((trow >= c * sub) & (trow < (c + 1) * sub), kx, 0.0)
        upds = [_dot(v_t[hsl(h), :], km[:, hsl(h)]) for h in range(NH)]
        inters = [_dot_nt(qx[rows, hsl(h)], sts[h]) for h in range(NH)]
        yield
        for h in range(NH):
            i_scr[rows, hsl(h)] = inters[h] + i_scr[rows, hsl(h)]
            sts[h] = sts[h] * p_end[c * sub:c * sub + 1, hsl(h)] + upds[h]
    for h in range(NH):
        st_ref[0, h] = sts[h]
    o = i_scr[...]
    gate = gg * _sigmoid(gg)
    for h in range(NH):
        sl = slice(h * HEAD_DIM, (h + 1) * HEAD_DIM)
        oh = o[:, sl]
        o_ref[:, sl] = oh * lax.rsqrt(jnp.mean(oh * oh, axis=-1, keepdims=True) + NORM_EPS) * pn[0:1, sl] * gate[:, sl]


def _rwkv_chunk_body(rw_ref, tril_ref, ones_ref, sel_ref, bd_ref, o_ref, st_ref, kx_scr):
    @pl.when(pl.program_id(1) == 0)
    def _():
        st_ref[...] = jnp.zeros_like(st_ref)

    yield
    L = rw_ref.shape[0]
    ns = L // SUB
    xr = ns * NH * SUB
    bd = bd_ref[...]
    csum = lambda a: jnp.sum(a, axis=0, keepdims=True)
    xi = lax.broadcasted_iota(jnp.int32, (xr, GW), 0)
    xj = lax.broadcasted_iota(jnp.int32, (xr, GW), 1)
    head_rows = (((xi // SUB) & (NH - 1)) == xj // HEAD_DIM).astype(F32)
    ti = lax.broadcasted_iota(jnp.int32, (L, xr), 0)
    tj = lax.broadcasted_iota(jnp.int32, (L, xr), 1)
    same = (tj // (NH * SUB)) == (ti // SUB)
    lt = same & ((tj & (SUB - 1)) < (ti & (SUB - 1)))
    le = same & ((tj & (SUB - 1)) <= (ti & (SUB - 1)))

    def expand(x):
        parts = []
        for c in range(ns):
            parts += [x[c * SUB:(c + 1) * SUB]] * NH
        return jnp.concatenate(parts, axis=0) * head_rows

    r, k, v = rw_ref[:, 0:256], rw_ref[:, 256:512], rw_ref[:, 512:768]
    lw, kk, ka = rw_ref[:, 768:1024], rw_ref[:, 1024:1280], rw_ref[:, 1280:1536]
    cs = _dot_hi(tril_ref[...], lw)
    c_tot = _dot_hi(ones_ref[...], lw)
    yield
    e_inv = jnp.exp(-cs)
    kkx = kk * jnp.exp(cs - lw)
    rx = r * jnp.exp(cs)
    kae, ke = ka * e_inv, k * e_inv
    e_end = jnp.exp(c_tot - cs)
    a_hat, k_hat = ka * e_end, k * e_end
    p_end = jnp.exp(c_tot)
    kx_scr[...] = kkx
    tiles = [kx_scr[t:t + 1, :] * kae[(t // SUB) * SUB:(t // SUB + 1) * SUB] for t in range(L)]
    maa = jnp.dot(jnp.concatenate(tiles, axis=0).astype(BF16), sel_ref[...], preferred_element_type=F32)
    v_x = expand(v)
    kkx_x = expand(kkx)
    mk = _dot_nt(jnp.concatenate([kkx, rx], axis=0), expand(ke))
    ra = _dot_nt(rx, expand(kae))
    yield
    m_ak = jnp.where(lt, mk[0:L], 0.0)
    a_rk = jnp.where(le, mk[L:2 * L], 0.0)
    a_ra = jnp.where(le, ra, 0.0)
    mv_x = expand(_dot(m_ak, v_x))
    rk_v = _dot(a_rk, v_x)
    yield
    crow = lax.broadcasted_iota(jnp.int32, (SUB, NH * SUB), 0)
    ccol = lax.broadcasted_iota(jnp.int32, (SUB, NH * SUB), 1) & (SUB - 1)
    eye_c = (crow == ccol).astype(F32)
    gs, us = [], []
    for c in range(ns):
        t_arr = jnp.zeros((SUB, NH * SUB), F32)
        for t in range(SUB):
            i = c * SUB + t
            row_t = eye_c[t:t + 1, :] - csum(maa[i * SUB:(i + 1) * SUB, :] * t_arr)
            t_arr = jnp.where(crow == t, row_t, t_arr)
        xs = slice(c * NH * SUB, (c + 1) * NH * SUB)
        gs.append(_dot(t_arr, kkx_x[xs]))
        us.append(_dot(t_arr, mv_x[xs]))
        if c % 2 == 1:
            yield
    g_all, u_all = jnp.concatenate(gs, axis=0), jnp.concatenate(us, axis=0)
    ra_g = _dot(a_ra, expand(g_all))
    ra_u = _dot(a_ra, expand(u_all))
    phis = [_dot_tn(g_all[c * SUB:(c + 1) * SUB], a_hat[c * SUB:(c + 1) * SUB]) for c in range(ns)]
    psis = [_dot_tn(jnp.concatenate([v[c * SUB:(c + 1) * SUB], -u_all[c * SUB:(c + 1) * SUB]], axis=0),
                    jnp.concatenate([k_hat[c * SUB:(c + 1) * SUB], a_hat[c * SUB:(c + 1) * SUB]], axis=0))
            for c in range(ns)]
    yield
    omega = rx - ra_g
    o_ind = rk_v - ra_u
    st = st_ref[0]
    for c in range(ns):
        rows = slice(c * SUB, (c + 1) * SUB)
        o_dot = _dot_nt(omega[rows], st)
        st_phi = _dot(st, phis[c])
        yield
        o_ref[rows, :] = o_dot + o_ind[rows]
        st = st * p_end[c * SUB:c * SUB + 1, :] + (psis[c] - st_phi) * bd
    st_ref[0] = st


def _mixers_chunk_kernel(layer, p_ref, rw_ref, logit_ref, pnh_ref, trilb_ref, onesb_ref, segb_ref, bd_ref, conv_ref,
                         gp1_ref, gp2_ref, pd_ref, pnm_ref, trilc_ref, sel_ref, trilh_ref, onesh_ref, ohg_ref, oss_ref,
                         oml_ref, orw_ref, sth_ref, ht_ref, ct_ref, n_ref, m_ref, str_ref, q_scr, b_scr, i_scr, tail_scr,
                         kx_scr):
    cols = lambda off, w: p_ref.at[:, off:off + w]
    bodies = [
        _rwkv_chunk_body(rw_ref, trilb_ref, onesb_ref, sel_ref, bd_ref, orw_ref, str_ref, kx_scr),
        _hgrn_chunk_body(layer, cols(HG_OFF, 1024), logit_ref, pnh_ref, trilh_ref, onesh_ref, segb_ref, bd_ref, ohg_ref,
                         sth_ref, q_scr, b_scr, i_scr),
        _mlstm_chunk_body(cols(ML_OFF, 768), cols(MO_OFF, GW), cols(G2_OFF, LANES), gp2_ref, pnm_ref, trilc_ref,
                          oml_ref, ct_ref, n_ref, m_ref),
        _ssm_chunk_body(cols(SS_OFF, 1024), cols(G1_OFF, LANES), conv_ref, gp1_ref, pd_ref, trilc_ref, oss_ref, ht_ref,
                        tail_scr),
    ]
    while bodies:
        for body in list(bodies):
            try:
                next(body)
            except StopIteration:
                bodies.remove(body)


MIXER_STATE_SHAPES = ((NH, HEAD_DIM, HEAD_DIM), (NH, SSM_STATE, HEAD_DIM), (NH, HEAD_DIM, HEAD_DIM), (SUBLANES, HEAD_DIM),
                      (1, LANES), (GW, GW))


def _mixers_chunk(proj, rw_scan, consts, layer, bsz, t):
    nt = t // CHUNK
    m = bsz * t
    rows = lambda w: pl.BlockSpec((CHUNK, w), lambda b, i: (b * nt + i, 0))
    const = lambda a: pl.BlockSpec(a.shape, lambda b, i: (0,) * a.ndim)
    state = lambda s: pl.BlockSpec((1,) + s, lambda b, i: (b,) + (0,) * len(s))
    big = pltpu.VMEM((CHUNK, GW), F32)
    outs = pl.pallas_call(
        functools.partial(_mixers_chunk_kernel, layer),
        grid=(bsz, nt),
        in_specs=[rows(IN_RE), rows(1536)] + [const(a) for a in consts],
        out_specs=[rows(GW)] * 4 + [state(s) for s in MIXER_STATE_SHAPES],
        out_shape=[jax.ShapeDtypeStruct((m, GW), F32)] * 4
        + [jax.ShapeDtypeStruct((bsz,) + s, F32) for s in MIXER_STATE_SHAPES],
        scratch_shapes=[big, big, big, pltpu.VMEM((SUBLANES, 768), F32), big],
        compiler_params=_cparams(("parallel", "arbitrary")),
    )(proj, rw_scan, *consts)
    return outs[:4], outs[4:]


def _segmean(a, s, n):
    return _dot01(_split3(a), s) * (1.0 / n)


def _rwkv_post(o, rwp_ref, pq, s64):
    d = o - _segmean(o, s64, HEAD_DIM)
    var = _segmean(d * d, s64, HEAD_DIM)
    return (d * lax.rsqrt(var + RWKV_GN_EPS) * pq[0:1] + pq[1:2] + rwp_ref[:, 0:256]) * rwp_ref[:, 256:512]


def _out_proj(x, parts, w_ref):
    acc = x
    for i, part in enumerate(parts):
        acc = acc + jnp.dot(part.astype(BF16), w_ref[i * GW:(i + 1) * GW, :], preferred_element_type=F32)
    return acc


def _post_step_kernel(x_ref, orw_ref, ohg_ref, oss_ref, oml_ref, rwp_ref, hgp_ref, ssp_ref, mlp_ref, pq_ref, seg64_ref,
                      seg128_ref, w_ref, o_ref):
    pq = pq_ref[...]
    s64 = seg64_ref[...]
    m_r = _rwkv_post(orw_ref[...].T, rwp_ref, pq, s64)
    o = ohg_ref[...].T
    m_h = o * lax.rsqrt(_segmean(o * o, s64, HEAD_DIM) + NORM_EPS) * pq[2:3] * hgp_ref[...]
    y = (oss_ref[...] + ssp_ref[:, 0:256]) * ssp_ref[:, 256:512]
    m_s = y * lax.rsqrt(_segmean(y * y, seg128_ref[...], 128) + NORM_EPS) * pq[3:4]
    o = oml_ref[...].T
    m_m = o * lax.rsqrt(_segmean(o * o, s64, HEAD_DIM) + NORM_EPS) * pq[4:5] * mlp_ref[...]
    o_ref[...] = _out_proj(x_ref[...], (m_r, m_h, m_s, m_m), w_ref)


def _post_seq_kernel(x_ref, orw_ref, rwp_ref, mh_ref, ms_ref, mm_ref, pq_ref, seg64_ref, w_ref, o_ref):
    m_r = _rwkv_post(orw_ref[...], rwp_ref, pq_ref[...], seg64_ref[...])
    o_ref[...] = _out_proj(x_ref[...], (m_r, mh_ref[...], ms_ref[...], mm_ref[...]), w_ref)


def _post_step(x, ins):
    full = lambda a: pl.BlockSpec(a.shape, lambda i: (0,) * a.ndim)
    return pl.pallas_call(
        _post_step_kernel,
        grid=(1,),
        in_specs=[full(a) for a in (x,) + tuple(ins)],
        out_specs=full(x),
        out_shape=jax.ShapeDtypeStruct(x.shape, F32),
        compiler_params=_cparams(("arbitrary",)),
    )(x, *ins)


def _post_call(kernel_fn, x, row_ins, consts):
    m = x.shape[0]
    tm = min(m, 512)
    rows = lambda a: pl.BlockSpec((tm, a.shape[1]), lambda i: (i, 0))
    const = lambda a: pl.BlockSpec(a.shape, lambda i: (0, 0))
    return pl.pallas_call(
        kernel_fn,
        grid=(m // tm,),
        in_specs=[rows(a) for a in (x,) + tuple(row_ins)] + [const(a) for a in consts],
        out_specs=rows(x),
        out_shape=jax.ShapeDtypeStruct((m, D_MODEL), F32),
        compiler_params=_cparams(("parallel",)),
    )(x, *row_ins, *consts)


def _xattn_seq_kernel(x_ref, g_ref, wq_ref, k_ref, v_ref, wo_ref, o_ref):
    x = x_ref[...]
    q = jnp.dot(_rms(x, g_ref[...]).astype(BF16), wq_ref[...], preferred_element_type=F32)
    acc = x
    for h in range(XH):
        sl = slice(h * XHD, (h + 1) * XHD)
        s = _dot_nt(q[:, sl], k_ref[:, sl]) * (XHD ** -0.5)
        e = jnp.exp(s - jnp.max(s, axis=-1, keepdims=True))
        pr = e / jnp.sum(e, axis=-1, keepdims=True)
        acc = acc + _dot(_dot(pr, v_ref[:, sl]), wo_ref[sl, :])
    o_ref[...] = acc


def _xattn_seq(x, g, wq, kv, wo, bsz, t):
    tq = 512
    nt = t // tq
    const = lambda a: pl.BlockSpec(a.shape, lambda b, i: (0, 0))
    g2 = g.reshape(1, D_MODEL)
    return pl.pallas_call(
        _xattn_seq_kernel,
        grid=(bsz, nt),
        in_specs=[pl.BlockSpec((tq, D_MODEL), lambda b, i: (b * nt + i, 0)), const(g2), const(wq),
                  pl.BlockSpec((N_MEM, D_MODEL), lambda b, i: (b, 0)),
                  pl.BlockSpec((N_MEM, D_MODEL), lambda b, i: (b, 1)),
                  const(wo)],
        out_specs=pl.BlockSpec((tq, D_MODEL), lambda b, i: (b * nt + i, 0)),
        out_shape=jax.ShapeDtypeStruct((bsz * t, D_MODEL), F32),
        compiler_params=_cparams(("parallel", "parallel")),
    )(x, g2, wq, kv, kv, wo)


def _xattn_step_kernel(bb, q_ref, k_ref, v_ref, o_ref):
    for i in range(bb):
        t = k_ref[i] * q_ref[i]
        s = jnp.sum(t, axis=-1, keepdims=True) * (XHD ** -0.5)
        e = jnp.exp(s - jnp.max(s, axis=0, keepdims=True))
        pr = e / jnp.sum(e, axis=0, keepdims=True)
        o_ref[i] = jnp.sum(pr * v_ref[i], axis=0)


def _xattn_step(q, cache_k, cache_v, layer):
    bsz = q.shape[0]
    bb = 4
    cache_spec = pl.BlockSpec((None, bb, N_MEM, XH, XHD), lambda i: (layer, i, 0, 0, 0))
    return pl.pallas_call(
        functools.partial(_xattn_step_kernel, bb),
        grid=(bsz // bb,),
        in_specs=[pl.BlockSpec((bb, XH, XHD), lambda i: (i, 0, 0)), cache_spec, cache_spec],
        out_specs=pl.BlockSpec((bb, XH, XHD), lambda i: (i, 0, 0)),
        out_shape=jax.ShapeDtypeStruct((bsz, XH, XHD), F32),
        compiler_params=_cparams(("parallel",)),
    )(q.reshape(bsz, XH, XHD), cache_k, cache_v).reshape(bsz, D_MODEL)


def _step_lanes_kernel(rw_ref, hg_ref, ml_ref, wkv_ref, hgs_ref, c_ref, n_ref, m_ref, a0_ref, a1_ref, a2_ref,
                       orw_ref, ohg_ref, oml_ref, wkv_o, hgs_o, c_o, n_o, m_o):
    del a0_ref, a1_ref, a2_ref
    csum = lambda a: jnp.sum(a, axis=0, keepdims=True)
    r, k, w, kk, ka = rw_ref[0], rw_ref[1], rw_ref[3], rw_ref[4], rw_ref[5]
    for i in range(HEAD_DIM):
        s = wkv_ref[i]
        s = s * w - csum(s * kk) * ka + rw_ref[2, i:i + 1, :] * k
        orw_ref[i:i + 1, :] = csum(s * r)
        wkv_o[i] = s
    vh = hg_ref[3]
    acc = jnp.zeros_like(vh)
    for i in range(HEAD_DIM):
        s = hgs_ref[i] * hg_ref[1, i:i + 1, :] + hg_ref[2, i:i + 1, :] * vh
        acc = acc + hg_ref[0, i:i + 1, :] * s
        hgs_o[i] = s
    ohg_ref[...] = acc
    q, k = ml_ref[0], ml_ref[1]
    li, lf = ml_ref[3, 0:1, :], ml_ref[4, 0:1, :]
    m_old = m_ref[...]
    m_new = jnp.maximum(lf + m_old, li)
    gp = jnp.exp(lf + m_old - m_new)
    gs = jnp.exp(li - m_new)
    n = gp * n_ref[...] + gs * k
    scale = 1.0 / jnp.maximum(jnp.abs(csum(n * q)), jnp.exp(-m_new))
    gk = gs * k
    for i in range(HEAD_DIM):
        s = c_ref[i] * gp + ml_ref[2, i:i + 1, :] * gk
        oml_ref[i:i + 1, :] = csum(s * q) * scale
        c_o[i] = s
    n_o[...] = n
    m_o[...] = m_new


def _ssm_step_kernel(xdt_ref, bn_ref, cn_ref, dec_ref, ssm_ref, a_ref, o_ref, ssm_o):
    del a_ref
    ri = lax.broadcasted_iota(jnp.int32, (HEAD_DIM, HEAD_DIM), 0)
    ci = lax.broadcasted_iota(jnp.int32, (HEAD_DIM, HEAD_DIM), 1)
    eye = (ri == ci).astype(F32)
    col = lambda r: jnp.sum(eye * r, axis=-1, keepdims=True)
    row = lambda c: jnp.sum(eye * c, axis=-2, keepdims=True)
    s = ssm_ref[...] * dec_ref[...] + col(xdt_ref[...]) * bn_ref[...]
    o_ref[...] = row(jnp.sum(s * cn_ref[...], axis=-1, keepdims=True))
    ssm_o[...] = s


def _step_lanes(rw_t, hg_t, ml_t, states, acc, layer):
    bsz = rw_t.shape[1]
    fields = lambda a: a.reshape(-1, NH, HEAD_DIM, bsz)
    seq_ins = (fields(rw_t), fields(hg_t), fields(ml_t))
    seq_spec = lambda a: pl.BlockSpec((a.shape[0], None, HEAD_DIM, bsz), lambda h: (0, h, 0, 0))
    st_layer = lambda a: pl.BlockSpec((None, None) + a.shape[2:], lambda h: (layer, h) + (0,) * (a.ndim - 2))
    st_head = lambda a: pl.BlockSpec((None,) + a.shape[2:], lambda h: (h,) + (0,) * (a.ndim - 2))
    o_shape = jax.ShapeDtypeStruct((NH, HEAD_DIM, bsz), F32)
    o_spec = pl.BlockSpec((None, HEAD_DIM, bsz), lambda h: (h, 0, 0))
    big, small = states[:3], states[3:]
    n_in = len(seq_ins) + len(states)
    outs = pl.pallas_call(
        _step_lanes_kernel,
        grid=(NH,),
        in_specs=[seq_spec(a) for a in seq_ins] + [st_layer(a) for a in states]
        + [pl.BlockSpec(memory_space=pl.ANY)] * len(big),
        out_specs=[o_spec] * 3 + [st_layer(a) for a in big] + [st_head(a) for a in small],
        out_shape=[o_shape] * 3 + [jax.ShapeDtypeStruct(a.shape, F32) for a in big]
        + [jax.ShapeDtypeStruct(a.shape[1:], F32) for a in small],
        input_output_aliases={n_in + j: 3 + j for j in range(len(big))},
        compiler_params=_cparams(("parallel",)),
    )(*seq_ins, *states, *acc)
    o_parts = [o.reshape(GW, bsz) for o in outs[:3]]
    return o_parts, outs[3:6], outs[6], outs[7]


def _ssm_step(ss, state_ssm, acc, layer):
    bsz = ss.shape[0]
    bb = SUBLANES
    per_head = lambda a: a.reshape(bsz, NH, HEAD_DIM)[:, :, 0].reshape(bsz, NH, 1, 1)
    groups = lambda a: jnp.repeat(a.reshape(bsz, 2, 1, SSM_STATE), NH // 2, axis=1)
    seq_ins = (ss[:, 0:GW].reshape(bsz, NH, 1, HEAD_DIM), groups(ss[:, 512:768]), groups(ss[:, 768:1024]),
               per_head(ss[:, GW:2 * GW]))
    blk = lambda a: pl.BlockSpec((bb,) + a.shape[1:], lambda i: (i,) + (0,) * (a.ndim - 1))
    st_layer = pl.BlockSpec((None, bb) + state_ssm.shape[2:], lambda i: (layer, i, 0, 0, 0))
    o, ssm_new = pl.pallas_call(
        _ssm_step_kernel,
        grid=(bsz // bb,),
        in_specs=[blk(a) for a in seq_ins] + [st_layer, pl.BlockSpec(memory_space=pl.ANY)],
        out_specs=[pl.BlockSpec((bb, NH, 1, HEAD_DIM), lambda i: (i, 0, 0, 0)), st_layer],
        out_shape=[jax.ShapeDtypeStruct((bsz, NH, 1, HEAD_DIM), F32), jax.ShapeDtypeStruct(state_ssm.shape, F32)],
        input_output_aliases={len(seq_ins) + 1: 1},
        compiler_params=_cparams(("parallel",)),
    )(*seq_ins, state_ssm, acc)
    return o.reshape(bsz, GW), ssm_new


def kernel(x_prompt, x_sample, state_rwkv_wkv, state_rwkv_shift, state_hgrn, state_ssm, state_ssm_conv, state_mlstm_C, state_mlstm_n, state_mlstm_m, cache_mem_k, cache_mem_v, mem_prompt, norm_mix_g, w_in, w_out, rwkv_mu, rwkv_w0, rwkv_w_up, rwkv_a0, rwkv_a_up, rwkv_g_up, rwkv_k_k, rwkv_k_a, rwkv_r_k, rwkv_ln_g, rwkv_ln_b, hgrn_lb_logits, hgrn_norm_g, ssm_conv_w, ssm_conv_b, ssm_dt_bias, ssm_A_log, ssm_D, ssm_norm_g, mlstm_i_bias, mlstm_f_bias, mlstm_norm_g, norm_x_g, norm_mem_g, xattn_wq, xattn_wk, xattn_wv, xattn_wo, norm_ff_g, ff_w1, ff_w2, final_norm_g):
    bp, tp, _ = x_prompt.shape
    bs = x_sample.shape[0]
    seg64, seg128 = _seg_matrix(HEAD_DIM), _seg_matrix(128)
    seg64_b = seg64.astype(BF16)
    gate_ex = _gate_expand()
    tril_c = jnp.asarray(np.tril(np.ones((CHUNK, CHUNK), np.float32)))

    def block_ones(sub):
        blk = np.arange(CHUNK) // sub
        return jnp.asarray((blk[:, None] == blk[None, :]).astype(np.float32))

    ones_blk, ones_blk_h = block_ones(SUB), block_ones(HGRN_SUB)
    tril_blk, tril_blk_h = ones_blk * tril_c, ones_blk_h * tril_c
    sel_c = jnp.asarray((np.arange(GW)[:, None] // HEAD_DIM == np.arange(NH * SUB)[None, :] // SUB), BF16)
    rep = lambda a: jnp.repeat(a, HEAD_DIM)
    zeros = lambda *s: jnp.zeros(s, F32)

    hp = x_prompt.reshape(bp * tp, D_MODEL)
    hs = x_sample.reshape(bs, D_MODEL)
    mem = mem_prompt.reshape(bp * N_MEM, D_MODEL)
    p_states, s_small, p_mem = [], [], []
    to_lanes = lambda a: jnp.moveaxis(a, 1, -1)
    from_lanes = lambda a: jnp.moveaxis(a, -1, 1)
    lane_states = (to_lanes(state_rwkv_wkv), to_lanes(state_hgrn), to_lanes(state_mlstm_C), to_lanes(state_mlstm_n),
                   to_lanes(state_mlstm_m)[:, :, None, :])
    s_big = tuple(jnp.zeros(a.shape, F32) for a in lane_states[:3])
    s_ssm = jnp.zeros(state_ssm.shape, F32)

    for l in range(DEPTH):
        last = l == DEPTH - 1
        w_in_re = _in_weight(w_in, l)
        pv = _pad_rows(jnp.stack([rwkv_w0[l], rwkv_a0[l], rwkv_k_k[l], rwkv_k_a[l], rwkv_r_k[l].reshape(GW),
                                  rep(ssm_dt_bias[l]), rep(ssm_A_log[l]), rep(ssm_D[l]), rep(mlstm_i_bias[l]),
                                  rep(mlstm_f_bias[l])]), 16)
        lora = jnp.stack([jnp.pad(w, ((r0, LANES - r0 - w.shape[0]), (0, 0)))
                          for r0, w in zip(LORA_ROWS, (rwkv_w_up[l], rwkv_a_up[l], rwkv_g_up[l]))])
        conv = _pad_rows(jnp.concatenate([ssm_conv_w[l], ssm_conv_b[l][None]], axis=0), 8)
        mu = rwkv_mu[l].reshape(1, RW_W)
        gp1 = _pad_rows(jnp.stack([_lane_row((GT_DT, ssm_dt_bias[l])), _lane_row((GT_DT, ssm_A_log[l]))]), 8)
        gp2 = _pad_rows(_lane_row((GT_I, mlstm_i_bias[l]), (GT_F, mlstm_f_bias[l]))[None], 8)
        pd = _pad_rows(jnp.stack([rep(ssm_D[l]), ssm_norm_g[l]]), 8)
        pn_h = _pad_rows(hgrn_norm_g[l][None], 8)
        pn_m = _pad_rows(mlstm_norm_g[l][None], 8)
        pq = _pad_rows(jnp.stack([rwkv_ln_g[l], rwkv_ln_b[l], hgrn_norm_g[l], ssm_norm_g[l], mlstm_norm_g[l]]), 8)
        w_out_b = w_out[l].astype(BF16)
        wq_b, wo_b = xattn_wq[l].astype(BF16), xattn_wo[l].astype(BF16)
        wkv_b = jnp.concatenate([xattn_wk[l], xattn_wv[l]], axis=1).astype(BF16)
        w1_b, w2_b = ff_w1[l].astype(BF16), ff_w2[l].astype(BF16)

        kv = _norm_matmul(mem, norm_mem_g[l], wkv_b)
        p_mem.append((kv[:, :D_MODEL].reshape(bp, N_MEM, XH, XHD), kv[:, D_MODEL:].reshape(bp, N_MEM, XH, XHD)))
        proj = _norm_matmul(hp, norm_mix_g[l], w_in_re)
        rw_scan, rw_post = _prep_rw_seq(proj, zeros(bp, SUBLANES, RW_W), mu, pv, lora, seg64, bp, tp)
        (m_h, m_s, m_m, o_rw), (st_h, st_s, st_c, st_n, st_m, st_r) = _mixers_chunk(
            proj, rw_scan, (hgrn_lb_logits, pn_h, tril_blk, ones_blk, seg64_b, seg64, conv, gp1, gp2, pd, pn_m, tril_c,
                            sel_c, tril_blk_h, ones_blk_h), l, bp, tp)
        proj3 = proj.reshape(bp, tp, IN_RE)
        diag_blocks = lambda s: jnp.stack(
            [s.reshape(bp, NH, HEAD_DIM, NH, HEAD_DIM)[:, h, :, h, :] for h in range(NH)], axis=1)
        p_states.append((diag_blocks(st_r), proj3[:, -1, RW_OFF:RW_OFF + RW_W], st_h.transpose(0, 1, 3, 2),
                         st_s.transpose(0, 1, 3, 2), proj3[:, tp - 3:, SS_OFF + 256:SS_OFF + 1024],
                         st_c.transpose(0, 1, 3, 2), st_n[:, :NH], st_m[:, 0, :NH]))
        hp = _post_call(_post_seq_kernel, hp, (o_rw, rw_post, m_h, m_s, m_m), (pq, seg64, w_out_b))
        hp = _xattn_seq(hp, norm_x_g[l], wq_b, kv, wo_b, bp, tp)
        hp = _ffn(hp, norm_ff_g[l], w1_b, w2_b, final_norm_g, last)

        proj = _norm_matmul(hs, norm_mix_g[l], w_in_re)
        cb = state_ssm_conv[l]
        prep = _prep_step(proj, state_rwkv_shift[l], cb[:, 0], cb[:, 1], cb[:, 2],
                          (mu, pv, lora, seg64, hgrn_lb_logits, conv, gate_ex), l)
        rw_t, hg_t, ml_t, rw_post, hg_post, ss_scan, ss_post, ml_post = prep
        (o_rw_t, o_hg_t, o_ml_t), s_big, n_n, m_n = _step_lanes(rw_t, hg_t, ml_t, lane_states, s_big, l)
        o_ss, s_ssm = _ssm_step(ss_scan, state_ssm, s_ssm, l)
        conv_new = jnp.concatenate([cb[:, 1:], proj[:, None, SS_OFF + 256:SS_OFF + 1024]], axis=1)
        s_small.append((proj[:, RW_OFF:RW_OFF + RW_W], conv_new, n_n, m_n))
        hs = _post_step(hs, (o_rw_t, o_hg_t, o_ss, o_ml_t, rw_post, hg_post, ss_post, ml_post, pq, seg64, seg128,
                             w_out_b))
        q = _norm_matmul(hs, norm_x_g[l], wq_b)
        att = _xattn_step(q, cache_mem_k, cache_mem_v, l)
        hs = _mm_res(att, wo_b, hs)
        hs = _ffn(hs, norm_ff_g[l], w1_b, w2_b, final_norm_g, last)

    y_prompt = hp.reshape(bp, tp, D_MODEL)
    y_sample = hs.reshape(bs, 1, D_MODEL)
    p_out = [jnp.stack(s) for s in zip(*p_states)]
    s_shift, s_conv, s_n, s_m = [jnp.stack(s) for s in zip(*s_small)]
    s_n, s_m = from_lanes(s_n), from_lanes(s_m[:, :, 0, :])
    s_wkv, s_hgrn, s_c = (from_lanes(a) for a in s_big)
    p_mem_k = jnp.stack([kv[0] for kv in p_mem])
    p_mem_v = jnp.stack([kv[1] for kv in p_mem])
    return (y_prompt, y_sample, *p_out, p_mem_k, p_mem_v, s_wkv, s_shift, s_hgrn, s_ssm, s_conv, s_c, s_n, s_m)
```

```python
import functools

import numpy as np
import jax
import jax.numpy as jnp
from jax import lax
from jax.experimental import pallas as pl
from jax.experimental.pallas import tpu as pltpu

F32 = jnp.float32
BF16 = jnp.bfloat16
HI = lax.Precision.HIGHEST

D_MODEL = 1024
DEPTH = 2
NH = 4
HEAD_DIM = 64
GW = 256
SSM_STATE = 128
N_MEM = 256
XH = 4
XHD = 256
NORM_EPS = 1e-6
RWKV_GN_EPS = 64e-5
LANES = 128
SUBLANES = 8
VMEM_LIMIT = 48 * 1024 * 1024
CHUNK = 128
SUB = 32
HGRN_SUB = 16

RW_OFF, RW_W = 0, 896
HG_OFF = 896
SS_OFF = 1920
G1_OFF = 2944
ML_OFF = 3072
G2_OFF = 3840
MO_OFF = 3968
IN_RE = 4224
GT_DT, GT_I, GT_F = 0, 0, 4
LORA_ROWS = (0, 32, 64)


def _in_weight_kernel(w_ref, o_ref):
    w = w_ref[...]
    lane = lax.broadcasted_iota(jnp.int32, (w.shape[0], LANES), 1)
    gate_tile = lambda c0, n: jnp.where(lane < n, w[:, c0:c0 + LANES], 0.0)
    o_ref[:, 0:G1_OFF] = w[:, 0:G1_OFF].astype(BF16)
    o_ref[:, G1_OFF:ML_OFF] = gate_tile(2944, NH).astype(BF16)
    o_ref[:, ML_OFF:G2_OFF] = w[:, 2948:3716].astype(BF16)
    o_ref[:, G2_OFF:MO_OFF] = gate_tile(3716, 2 * NH).astype(BF16)
    o_ref[:, MO_OFF:IN_RE] = w[:, 3724:3980].astype(BF16)


def _in_weight(w_in, layer):
    k, n = w_in.shape[1:]
    tr = 256
    return pl.pallas_call(
        _in_weight_kernel,
        grid=(k // tr,),
        in_specs=[pl.BlockSpec((None, tr, n), lambda i: (layer, i, 0))],
        out_specs=pl.BlockSpec((tr, IN_RE), lambda i: (i, 0)),
        out_shape=jax.ShapeDtypeStruct((k, IN_RE), BF16),
        compiler_params=_cparams(("parallel",)),
    )(w_in)


def _seg_matrix(width):
    i = np.arange(GW)
    return jnp.asarray((i[:, None] // width == i[None, :] // width).astype(np.float32))


def _gate_expand():
    e = np.zeros((3, LANES, GW), np.float32)
    for s, off in enumerate((GT_DT, GT_I, GT_F)):
        for h in range(NH):
            e[s, off + h, h * HEAD_DIM:(h + 1) * HEAD_DIM] = 1.0
    return jnp.asarray(e)


def _lane_row(*pieces):
    z = jnp.zeros((LANES,), F32)
    for off, val in pieces:
        z = z.at[off:off + val.shape[0]].set(val)
    return z


def _pad_rows(a, n):
    return jnp.pad(a, ((0, n - a.shape[0]), (0, 0)))


def _cparams(sem):
    return pltpu.CompilerParams(dimension_semantics=sem, vmem_limit_bytes=VMEM_LIMIT)


def _softplus(x):
    return jnp.maximum(x, 0.0) + jnp.log(1.0 + jnp.exp(-jnp.abs(x)))


def _sigmoid(x):
    return jax.nn.sigmoid(x)


def _rms(x, g):
    ms = jnp.mean(x * x, axis=-1, keepdims=True)
    return x * lax.rsqrt(ms + NORM_EPS) * g


def _dot(a, b):
    return jnp.dot(a.astype(BF16), b.astype(BF16), preferred_element_type=F32)


def _dot_nt(a, b):
    return lax.dot_general(a.astype(BF16), b.astype(BF16), (((1,), (1,)), ((), ())), preferred_element_type=F32)


def _dot_tn(a, b):
    return lax.dot_general(a.astype(BF16), b.astype(BF16), (((0,), (0,)), ((), ())), preferred_element_type=F32)


def _dot_hi(a, b):
    return jnp.dot(a, b, precision=HI, preferred_element_type=F32)


def _split3(x):
    hi = x.astype(BF16)
    r = x - hi.astype(F32)
    mid = r.astype(BF16)
    return hi, mid, (r - mid.astype(F32)).astype(BF16)


def _dot01(parts, m):
    mb = m.astype(BF16)
    hi, mid, lo = (jnp.dot(p, mb, preferred_element_type=F32) for p in parts)
    return hi + mid + lo


def _norm_matmul_kernel(x_ref, g_ref, w_ref, o_ref, xn_ref):
    @pl.when(pl.program_id(1) == 0)
    def _():
        xn_ref[...] = _rms(x_ref[...], g_ref[...]).astype(BF16)

    o_ref[...] = jnp.dot(xn_ref[...], w_ref[...], preferred_element_type=F32)


def _row_tile(m):
    return min(m, 1024)


def _norm_matmul(x, g, w):
    m, k = x.shape
    n = w.shape[1]
    tm = _row_tile(m)
    tn = next(c for c in (1536, 1408, 1024, 512, n) if n % c == 0)
    return pl.pallas_call(
        _norm_matmul_kernel,
        grid=(m // tm, n // tn),
        in_specs=[pl.BlockSpec((tm, k), lambda i, j: (i, 0)),
                  pl.BlockSpec((1, k), lambda i, j: (0, 0)),
                  pl.BlockSpec((k, tn), lambda i, j: (0, j))],
        out_specs=pl.BlockSpec((tm, tn), lambda i, j: (i, j)),
        out_shape=jax.ShapeDtypeStruct((m, n), F32),
        scratch_shapes=[pltpu.VMEM((tm, k), BF16)],
        compiler_params=_cparams(("parallel", "arbitrary")),
    )(x, g.reshape(1, k), w)


def _mm_res_kernel(a_ref, w_ref, res_ref, o_ref):
    o_ref[...] = res_ref[...] + jnp.dot(a_ref[...].astype(BF16), w_ref[...], preferred_element_type=F32)


def _mm_res(a, w, res):
    m, k = a.shape
    n = w.shape[1]
    tm = min(m, 512)
    return pl.pallas_call(
        _mm_res_kernel,
        grid=(m // tm,),
        in_specs=[pl.BlockSpec((tm, k), lambda i: (i, 0)),
                  pl.BlockSpec((k, n), lambda i: (0, 0)),
                  pl.BlockSpec((tm, n), lambda i: (i, 0))],
        out_specs=pl.BlockSpec((tm, n), lambda i: (i, 0)),
        out_shape=jax.ShapeDtypeStruct((m, n), F32),
        compiler_params=_cparams(("parallel",)),
    )(a, w, res)


def _ffn_kernel(final, x_ref, g_ref, w1_ref, w2_ref, fg_ref, o_ref, xn_ref, acc_ref):
    j = pl.program_id(1)

    @pl.when(j == 0)
    def _():
        xn_ref[...] = _rms(x_ref[...], g_ref[...]).astype(BF16)
        acc_ref[...] = jnp.zeros_like(acc_ref)

    h = jnp.dot(xn_ref[...], w1_ref[...], preferred_element_type=F32)
    h = jnp.square(jnp.maximum(h, 0.0))
    acc_ref[...] += jnp.dot(h.astype(BF16), w2_ref[...], preferred_element_type=F32)

    @pl.when(j == pl.num_programs(1) - 1)
    def _():
        y = x_ref[...] + acc_ref[...]
        o_ref[...] = _rms(y, fg_ref[...]) if final else y


def _ffn(x, g, w1, w2, final_g, final):
    m, k = x.shape
    ff = w1.shape[1]
    tm = _row_tile(m)
    tf = 1024
    return pl.pallas_call(
        functools.partial(_ffn_kernel, final),
        grid=(m // tm, ff // tf),
        in_specs=[pl.BlockSpec((tm, k), lambda i, j: (i, 0)),
                  pl.BlockSpec((1, k), lambda i, j: (0, 0)),
                  pl.BlockSpec((k, tf), lambda i, j: (0, j)),
                  pl.BlockSpec((tf, k), lambda i, j: (j, 0)),
                  pl.BlockSpec((1, k), lambda i, j: (0, 0))],
        out_specs=pl.BlockSpec((tm, k), lambda i, j: (i, 0)),
        out_shape=jax.ShapeDtypeStruct((m, k), F32),
        scratch_shapes=[pltpu.VMEM((tm, k), BF16), pltpu.VMEM((tm, k), F32)],
        compiler_params=_cparams(("parallel", "arbitrary")),
    )(x, g.reshape(1, k), w1, w2, final_g.reshape(1, k))


def _shift_rows(x, tail, j):
    xr = pltpu.roll(x, j, 0)
    tr = pltpu.roll(tail, j, 0)
    rows = lax.broadcasted_iota(jnp.int32, tr.shape, 0)
    first = jnp.where(rows < j, tr, xr[0:SUBLANES])
    return jnp.concatenate([first, xr[SUBLANES:]], axis=0)


def _hgrn_lower_bound(logit_ref, layer):
    lg = logit_ref[...]
    e = jnp.exp(lg - jnp.max(lg, axis=0, keepdims=True))
    prb = e / jnp.sum(e, axis=0, keepdims=True)
    return jnp.sum(prb[0:layer + 1], axis=0, keepdims=True) - prb[0:1]


def _conv_silu(u, c3, c2, c1, conv_ref):
    cw = conv_ref[...]
    conv = cw[4:5] + c3 * cw[0:1] + c2 * cw[1:2] + c1 * cw[2:3] + u * cw[3:4]
    return conv * _sigmoid(conv)


def _prep_rwkv(pr, prev_rw, mu_ref, pv_ref, lora_ref, seg_ref, rw_scan, rw_post, log_decay):
    pv = pv_ref[...]
    row = lambda i: pv[i:i + 1, :]
    seg = seg_ref[...]
    pm = pr + (prev_rw - pr) * mu_ref[...]
    r, k, v, low = pm[:, 0:256], pm[:, 256:512], pm[:, 512:768], pm[:, 768:896]
    w_log = -_softplus(-(row(0) + _dot_hi(jnp.tanh(low), lora_ref[0]))) - 0.5
    a = _sigmoid(row(1) + _dot_hi(low, lora_ref[1]))
    kk = k * row(2)
    kk = kk / jnp.maximum(jnp.sqrt(_dot01(_split3(kk * kk), seg)), 1e-12)
    k2 = k * (1.0 + (a - 1.0) * row(3))
    rw_scan[:, 0:256] = r
    rw_scan[:, 256:512] = k2
    rw_scan[:, 512:768] = v
    rw_scan[:, 768:1024] = -jnp.exp(w_log) if log_decay else jnp.exp(-jnp.exp(w_log))
    rw_scan[:, 1024:1280] = kk
    rw_scan[:, 1280:1536] = kk * a
    rw_post[:, 0:256] = _dot01(_split3(r * k2 * row(4)), seg) * v
    rw_post[:, 256:512] = _dot_hi(_sigmoid(low), lora_ref[2])


def _prep_rw_seq_kernel(p_ref, prev8_ref, tail0_ref, mu_ref, pv_ref, lora_ref, seg_ref, rw_scan, rw_post):
    tail = jnp.where(pl.program_id(1) == 0, tail0_ref[0], prev8_ref[...])
    pr = p_ref[...]
    _prep_rwkv(pr, _shift_rows(pr, tail, 1), mu_ref, pv_ref, lora_ref, seg_ref, rw_scan, rw_post, True)


def _prep_rw_seq(proj, tail0, mu, pv, lora, seg, bsz, t):
    tt = 512
    nt = t // tt
    m = bsz * t
    c2 = lambda b, i: (0, 0)
    return pl.pallas_call(
        _prep_rw_seq_kernel,
        grid=(bsz, nt),
        in_specs=[pl.BlockSpec((tt, RW_W), lambda b, i: (b * nt + i, 0)),
                  pl.BlockSpec((SUBLANES, RW_W), lambda b, i: (jnp.maximum((b * nt + i) * (tt // SUBLANES) - 1, 0), 0)),
                  pl.BlockSpec((1, SUBLANES, RW_W), lambda b, i: (b, 0, 0)),
                  pl.BlockSpec((1, RW_W), c2), pl.BlockSpec((16, GW), c2),
                  pl.BlockSpec((3, LANES, GW), lambda b, i: (0, 0, 0)), pl.BlockSpec((GW, GW), c2)],
        out_specs=[pl.BlockSpec((tt, 1536), lambda b, i: (b * nt + i, 0)),
                   pl.BlockSpec((tt, 512), lambda b, i: (b * nt + i, 0))],
        out_shape=[jax.ShapeDtypeStruct((m, 1536), F32), jax.ShapeDtypeStruct((m, 512), F32)],
        compiler_params=_cparams(("parallel", "parallel")),
    )(proj, proj, tail0, mu, pv, lora, seg)


PREP_SCAN_WIDTHS = (1536, 1024, 1280)
PREP_ROW_WIDTHS = (512, 256, 1024, 512, 256)


def _prep_step_kernel(layer, p_ref, prev_ref, c3_ref, c2_ref, c1_ref, mu_ref, pv_ref, lora_ref, seg_ref, logit_ref,
                      conv_ref, ex_ref, rw_t, hg_t, ml_t, rw_post, hg_post, ss_scan, ss_post, ml_post, rw_scan,
                      hg_scan, ml_scan):
    pv = pv_ref[...]
    row = lambda i: pv[i:i + 1, :]
    _prep_rwkv(p_ref[:, RW_OFF:RW_OFF + RW_W], prev_ref[...], mu_ref, pv_ref, lora_ref, seg_ref, rw_scan, rw_post,
               False)
    g1 = p_ref[:, G1_OFF:G1_OFF + LANES]
    g2 = p_ref[:, G2_OFF:G2_OFF + LANES]
    dt_pre, i_pre, f_pre_m = _dot_hi(g1, ex_ref[0]), _dot_hi(g2, ex_ref[1]), _dot_hi(g2, ex_ref[2])

    lb = _hgrn_lower_bound(logit_ref, layer)
    q_pre = p_ref[:, HG_OFF:HG_OFF + 256]
    f_pre = p_ref[:, HG_OFF + 256:HG_OFF + 512]
    gg = p_ref[:, HG_OFF + 768:HG_OFF + 1024]
    hg_scan[:, 0:256] = q_pre * _sigmoid(q_pre)
    hg_scan[:, 256:512] = lb + (1.0 - lb) * _sigmoid(f_pre)
    hg_scan[:, 512:768] = (1.0 - lb) * _sigmoid(-f_pre)
    hg_scan[:, 768:1024] = p_ref[:, HG_OFF + 512:HG_OFF + 768]
    hg_post[...] = gg * _sigmoid(gg)

    z = p_ref[:, SS_OFF:SS_OFF + 256]
    xbc = _conv_silu(p_ref[:, SS_OFF + 256:SS_OFF + 1024], c3_ref[...], c2_ref[...], c1_ref[...], conv_ref)
    xs = xbc[:, 0:256]
    dt = _softplus(dt_pre + row(5))
    ss_scan[:, 0:256] = xs * dt
    ss_scan[:, 256:512] = jnp.exp(dt * (-jnp.exp(row(6))))
    ss_scan[:, 512:1024] = xbc[:, 256:768]
    ss_post[:, 0:256] = xs * row(7)
    ss_post[:, 256:512] = z * _sigmoid(z)

    ml_scan[:, 0:256] = p_ref[:, ML_OFF:ML_OFF + 256]
    ml_scan[:, 256:512] = p_ref[:, ML_OFF + 256:ML_OFF + 512] * (HEAD_DIM ** -0.5)
    ml_scan[:, 512:768] = p_ref[:, ML_OFF + 512:ML_OFF + 768]
    ml_scan[:, 768:1024] = i_pre + row(8)
    ml_scan[:, 1024:1280] = -_softplus(-(f_pre_m + row(9)))
    ml_post[...] = _sigmoid(p_ref[:, MO_OFF:MO_OFF + 256])
    rw_t[...] = rw_scan[...].T
    hg_t[...] = hg_scan[...].T
    ml_t[...] = ml_scan[...].T


def _prep_step(proj, prev_rw, c3, c2, c1, params, layer):
    m = proj.shape[0]
    full = lambda a: pl.BlockSpec(a.shape, lambda i: (0,) * a.ndim)
    ins = (proj, prev_rw, c3, c2, c1) + tuple(params)
    shapes = [(w, m) for w in PREP_SCAN_WIDTHS] + [(m, w) for w in PREP_ROW_WIDTHS]
    return pl.pallas_call(
        functools.partial(_prep_step_kernel, layer),
        grid=(1,),
        in_specs=[full(a) for a in ins],
        out_specs=[pl.BlockSpec(s, lambda i: (0, 0)) for s in shapes],
        out_shape=[jax.ShapeDtypeStruct(s, F32) for s in shapes],
        scratch_shapes=[pltpu.VMEM((m, w), F32) for w in PREP_SCAN_WIDTHS],
        compiler_params=_cparams(("arbitrary",)),
    )(*ins)


def _ssm_chunk_body(ss_ref, gt_ref, conv_ref, gp_ref, pd_ref, tril_ref, o_ref, ht_ref, tail_ref):
    @pl.when(pl.program_id(1) == 0)
    def _():
        ht_ref[...] = jnp.zeros_like(ht_ref)
        tail_ref[...] = jnp.zeros_like(tail_ref)

    yield
    L = ss_ref.shape[0]
    u = ss_ref[:, 256:1024]
    tail = tail_ref[...]
    c1, c2, c3 = (_shift_rows(u, tail, j) for j in (1, 2, 3))
    tail_ref[...] = u[L - SUBLANES:L]
    xbc = _conv_silu(u, c3, c2, c1, conv_ref)
    z = ss_ref[:, 0:256]
    zs = z * _sigmoid(z)
    dt = _softplus(gt_ref[...] + gp_ref[0:1])
    la = dt * (-jnp.exp(gp_ref[1:2]))
    tril = tril_ref[...]
    mask = tril > 0.5
    cum = _dot_hi(tril, la)
    yield
    cum_t = cum.T
    pd = pd_ref[...]
    hsl = lambda h: slice(h * HEAD_DIM, (h + 1) * HEAD_DIM)
    bgs = [xbc[:, 256 + SSM_STATE * g:256 + SSM_STATE * (g + 1)] for g in range(2)]
    cgs = [xbc[:, 512 + SSM_STATE * g:512 + SSM_STATE * (g + 1)] for g in range(2)]
    scores = [_dot_nt(cgs[g], bgs[g]) for g in range(2)]
    hts = [ht_ref[0, h] for h in range(NH)]
    inter = [_dot(cgs[h // 2], hts[h]) for h in range(NH)]
    bg_t = [bgs[g].T for g in range(2)]
    ccs = [cum[:, GT_DT + h:GT_DT + h + 1] for h in range(NH)]
    dtc = [dt[:, GT_DT + h:GT_DT + h + 1] for h in range(NH)]
    lms = []
    for h in range(NH):
        cr = cum_t[GT_DT + h:GT_DT + h + 1, :]
        lms.append(jnp.where(mask, jnp.exp(jnp.where(mask, ccs[h] - cr, 0.0)), 0.0))
    yield
    ydots = [_dot(scores[h // 2] * lms[h], xbc[:, hsl(h)] * dtc[h]) for h in range(NH)]
    cends = [ccs[h][L - 1:L, :] for h in range(NH)]
    hdots = [_dot(bg_t[h // 2], xbc[:, hsl(h)] * (jnp.exp(cends[h] - ccs[h]) * dtc[h])) for h in range(NH)]
    yield
    ys = []
    for h in range(NH):
        y = ydots[h] + inter[h] * jnp.exp(ccs[h])
        ys.append((y + xbc[:, hsl(h)] * pd[0:1, hsl(h)]) * zs[:, hsl(h)])
        ht_ref[0, h] = jnp.exp(cends[h]) * hts[h] + hdots[h]
    for g in range(2):
        y0, y1 = ys[2 * g], ys[2 * g + 1]
        ms = jnp.sum(y0 * y0, axis=-1, keepdims=True) + jnp.sum(y1 * y1, axis=-1, keepdims=True)
        scale = lax.rsqrt(ms * (1.0 / (2 * HEAD_DIM)) + NORM_EPS)
        for h in (2 * g, 2 * g + 1):
            o_ref[:, hsl(h)] = ys[h] * scale * pd[1:2, hsl(h)]


def _mlstm_chunk_body(qkv_ref, og_ref, gt_ref, gp_ref, pn_ref, tril_ref, o_ref, ct_ref, n_ref, m_ref):
    @pl.when(pl.program_id(1) == 0)
    def _():
        ct_ref[...] = jnp.zeros_like(ct_ref)
        n_ref[...] = jnp.zeros_like(n_ref)
        m_ref[...] = jnp.zeros_like(m_ref)

    yield
    L = qkv_ref.shape[0]
    x = gt_ref[...] + gp_ref[0:1]
    lf = -_softplus(-x)
    tril = tril_ref[...]
    mask = tril > 0.5
    b = _dot_hi(tril, lf)
    yield
    b_t = b.T
    x_t = x.T
    m_row = m_ref[0]
    m_new_row = m_row
    lane = lax.broadcasted_iota(jnp.int32, m_row.shape, 1)
    pn = pn_ref[...]
    hsl = lambda h, base=0: slice(base + h * HEAD_DIM, base + (h + 1) * HEAD_DIM)
    for h in range(NH):
        b_col, b_row = b[:, GT_F + h:GT_F + h + 1], b_t[GT_F + h:GT_F + h + 1, :]
        li_col, li_row = x[:, GT_I + h:GT_I + h + 1], x_t[GT_I + h:GT_I + h + 1, :]
        m_prev = m_row[:, h:h + 1]
        log_inter = b_col + m_prev
        log_intra = jnp.where(mask, b_col - b_row + li_row, -jnp.inf)
        m_t = jnp.maximum(log_inter, jnp.max(log_intra, axis=-1, keepdims=True))
        w_inter = jnp.exp(log_inter - m_t)
        qh = qkv_ref[:, hsl(h)]
        kh = qkv_ref[:, hsl(h, 256)] * (HEAD_DIM ** -0.5)
        vh = qkv_ref[:, hsl(h, 512)]
        ct = ct_ref[0, h]
        qk_raw = _dot_nt(qh, kh)
        q_ct = _dot(qh, ct)
        yield
        qk = qk_raw * jnp.exp(log_intra - m_t)
        n_h = n_ref[0, h:h + 1, :]
        m_end = m_t[L - 1:L, :]
        b_end = b_col[L - 1:L, :]
        g_prev = jnp.exp(b_end + m_prev - m_end)
        kg = kh * jnp.exp(b_end - b_col + li_col - m_end)
        qk_v = _dot(qk, vh)
        kg_v = _dot_tn(kg, vh)
        yield
        num = qk_v + w_inter * q_ct
        den = jnp.sum(qk, axis=-1, keepdims=True) + w_inter * jnp.sum(qh * n_h, axis=-1, keepdims=True)
        hout = num / jnp.maximum(jnp.abs(den), jnp.exp(-m_t))
        ct_ref[0, h] = g_prev * ct + kg_v
        n_ref[0, h:h + 1, :] = g_prev * n_h + jnp.sum(kg, axis=0, keepdims=True)
        m_new_row = jnp.where(lane == h, m_end, m_new_row)
        hn = hout * lax.rsqrt(jnp.mean(hout * hout, axis=-1, keepdims=True) + NORM_EPS)
        o_ref[:, hsl(h)] = hn * pn[0:1, hsl(h)] * _sigmoid(og_ref[:, hsl(h)])
    m_ref[0] = m_new_row


def _hgrn_chunk_body(layer, hg_ref, logit_ref, pn_ref, tril_ref, ones_ref, seg_ref, bd_ref, o_ref, st_ref, q_scr,
                     b_scr, i_scr):
    @pl.when(pl.program_id(1) == 0)
    def _():
        st_ref[...] = jnp.zeros_like(st_ref)

    yield
    lb = _hgrn_lower_bound(logit_ref, layer)
    seg = seg_ref[...]
    bd = bd_ref[...]
    pn = pn_ref[...]
    sub = HGRN_SUB
    srow = lax.broadcasted_iota(jnp.int32, (sub, GW), 0)
    L = hg_ref.shape[0]

    q_pre, f_pre, v, gg = hg_ref[:, 0:256], hg_ref[:, 256:512], hg_ref[:, 512:768], hg_ref[:, 768:1024]
    q = q_pre * _sigmoid(q_pre)
    k = (1.0 - lb) * _sigmoid(-f_pre)
    lf = jnp.log(lb + (1.0 - lb) * _sigmoid(f_pre))
    b = _dot_hi(tril_ref[...], lf)
    b_tot = _dot_hi(ones_ref[...], lf)
    yield
    q_scr[...] = q
    b_scr[...] = b
    quarter = L // 4
    for part in range(4):
        tiles = []
        for t in range(part * quarter, (part + 1) * quarter):
            rows = slice((t // sub) * sub, (t // sub + 1) * sub)
            live = srow <= (t % sub)
            dec = jnp.where(live, jnp.exp(jnp.where(live, b_scr[t:t + 1, :] - b[rows], 0.0)), 0.0)
            tiles.append(dec * q_scr[t:t + 1, :] * k[rows])
        a_b = jnp.dot(jnp.concatenate(tiles, axis=0).astype(BF16), seg, preferred_element_type=F32)
        yield
        for j in range(quarter):
            t = part * quarter + j
            rows = slice((t // sub) * sub, (t // sub + 1) * sub)
            i_scr[t:t + 1, :] = jnp.sum(a_b[j * sub:(j + 1) * sub, :] * v[rows], axis=0, keepdims=True)
    qx = q * jnp.exp(b)
    kx = k * jnp.exp(b_tot - b)
    p_end = jnp.exp(b_tot)
    hsl = lambda h: slice(h * HEAD_DIM, (h + 1) * HEAD_DIM)
    sts = [st_ref[0, h] for h in range(NH)]
    v_t = v.T
    trow = lax.broadcasted_iota(jnp.int32, (L, GW), 0)
    for c in range(L // sub):
        rows = slice(c * sub, (c + 1) * sub)
        km = jnp.where((trow >= c * sub) & (trow < (c + 1) * sub), kx, 0.0)
        upds = [_dot(v_t[hsl(h), :], km[:, hsl(h)]) for h in range(NH)]
        inters = [_dot_nt(qx[rows, hsl(h)], sts[h]) for h in range(NH)]
        yield
        for h in range(NH):
            i_scr[rows, hsl(h)] = inters[h] + i_scr[rows, hsl(h)]
            sts[h] = sts[h] * p_end[c * sub:c * sub + 1, hsl(h)] + upds[h]
    for h in range(NH):
        st_ref[0, h] = sts[h]
    o = i_scr[...]
    gate = gg * _sigmoid(gg)
    for h in range(NH):
        sl = slice(h * HEAD_DIM, (h + 1) * HEAD_DIM)
        oh = o[:, sl]
        o_ref[:, sl] = oh * lax.rsqrt(jnp.mean(oh * oh, axis=-1, keepdims=True) + NORM_EPS) * pn[0:1, sl] * gate[:, sl]


def _rwkv_chunk_body(rw_ref, tril_ref, ones_ref, sel_ref, bd_ref, o_ref, st_ref, kx_scr):
    @pl.when(pl.program_id(1) == 0)
    def _():
        st_ref[...] = jnp.zeros_like(st_ref)

    yield
    L = rw_ref.shape[0]
    ns = L // SUB
    xr = ns * NH * SUB
    bd = bd_ref[...]
    csum = lambda a: jnp.sum(a, axis=0, keepdims=True)
    xi = lax.broadcasted_iota(jnp.int32, (xr, GW), 0)
    xj = lax.broadcasted_iota(jnp.int32, (xr, GW), 1)
    head_rows = (((xi // SUB) & (NH - 1)) == xj // HEAD_DIM).astype(F32)
    ti = lax.broadcasted_iota(jnp.int32, (L, xr), 0)
    tj = lax.broadcasted_iota(jnp.int32, (L, xr), 1)
    same = (tj // (NH * SUB)) == (ti // SUB)
    lt = same & ((tj & (SUB - 1)) < (ti & (SUB - 1)))
    le = same & ((tj & (SUB - 1)) <= (ti & (SUB - 1)))

    def expand(x):
        parts = []
        for c in range(ns):
            parts += [x[c * SUB:(c + 1) * SUB]] * NH
        return jnp.concatenate(parts, axis=0) * head_rows

    r, k, v = rw_ref[:, 0:256], rw_ref[:, 256:512], rw_ref[:, 512:768]
    lw, kk, ka = rw_ref[:, 768:1024], rw_ref[:, 1024:1280], rw_ref[:, 1280:1536]
    cs = _dot_hi(tril_ref[...], lw)
    c_tot = _dot_hi(ones_ref[...], lw)
    yield
    e_inv = jnp.exp(-cs)
    kkx = kk * jnp.exp(cs - lw)
    rx = r * jnp.exp(cs)
    kae, ke = ka * e_inv, k * e_inv
    e_end = jnp.exp(c_tot - cs)
    a_hat, k_hat = ka * e_end, k * e_end
    p_end = jnp.exp(c_tot)
    kx_scr[...] = kkx
    tiles = [kx_scr[t:t + 1, :] * kae[(t // SUB) * SUB:(t // SUB + 1) * SUB] for t in range(L)]
    maa = jnp.dot(jnp.concatenate(tiles, axis=0).astype(BF16), sel_ref[...], preferred_element_type=F32)
    v_x = expand(v)
    kkx_x = expand(kkx)
    mk = _dot_nt(jnp.concatenate([kkx, rx], axis=0), expand(ke))
    ra = _dot_nt(rx, expand(kae))
    yield
    m_ak = jnp.where(lt, mk[0:L], 0.0)
    a_rk = jnp.where(le, mk[L:2 * L], 0.0)
    a_ra = jnp.where(le, ra, 0.0)
    mv_x = expand(_dot(m_ak, v_x))
    rk_v = _dot(a_rk, v_x)
    yield
    crow = lax.broadcasted_iota(jnp.int32, (SUB, NH * SUB), 0)
    ccol = lax.broadcasted_iota(jnp.int32, (SUB, NH * SUB), 1) & (SUB - 1)
    eye_c = (crow == ccol).astype(F32)
    gs, us = [], []
    for c in range(ns):
        t_arr = jnp.zeros((SUB, NH * SUB), F32)
        for t in range(SUB):
            i = c * SUB + t
            row_t = eye_c[t:t + 1, :] - csum(maa[i * SUB:(i + 1) * SUB, :] * t_arr)
            t_arr = jnp.where(crow == t, row_t, t_arr)
        xs = slice(c * NH * SUB, (c + 1) * NH * SUB)
        gs.append(_dot(t_arr, kkx_x[xs]))
        us.append(_dot(t_arr, mv_x[xs]))
        if c % 2 == 1:
            yield
    g_all, u_all = jnp.concatenate(gs, axis=0), jnp.concatenate(us, axis=0)
    ra_g = _dot(a_ra, expand(g_all))
    ra_u = _dot(a_ra, expand(u_all))
    phis = [_dot_tn(g_all[c * SUB:(c + 1) * SUB], a_hat[c * SUB:(c + 1) * SUB]) for c in range(ns)]
    psis = [_dot_tn(jnp.concatenate([v[c * SUB:(c + 1) * SUB], -u_all[c * SUB:(c + 1) * SUB]], axis=0),
                    jnp.concatenate([k_hat[c * SUB:(c + 1) * SUB], a_hat[c * SUB:(c + 1) * SUB]], axis=0))
            for c in range(ns)]
    yield
    omega = rx - ra_g
    o_ind = rk_v - ra_u
    st = st_ref[0]
    for c in range(ns):
        rows = slice(c * SUB, (c + 1) * SUB)
        o_dot = _dot_nt(omega[rows], st)
        st_phi = _dot(st, phis[c])
        yield
        o_ref[rows, :] = o_dot + o_ind[rows]
        st = st * p_end[c * SUB:c * SUB + 1, :] + (psis[c] - st_phi) * bd
    st_ref[0] = st


def _mixers_chunk_kernel(layer, p_ref, rw_ref, logit_ref, pnh_ref, trilb_ref, onesb_ref, segb_ref, bd_ref, conv_ref,
                         gp1_ref, gp2_ref, pd_ref, pnm_ref, trilc_ref, sel_ref, trilh_ref, onesh_ref, ohg_ref, oss_ref,
                         oml_ref, orw_ref, sth_ref, ht_ref, ct_ref, n_ref, m_ref, str_ref, q_scr, b_scr, i_scr, tail_scr,
                         kx_scr):
    cols = lambda off, w: p_ref.at[:, off:off + w]
    bodies = [
        _rwkv_chunk_body(rw_ref, trilb_ref, onesb_ref, sel_ref, bd_ref, orw_ref, str_ref, kx_scr),
        _hgrn_chunk_body(layer, cols(HG_OFF, 1024), logit_ref, pnh_ref, trilh_ref, onesh_ref, segb_ref, bd_ref, ohg_ref,
                         sth_ref, q_scr, b_scr, i_scr),
        _mlstm_chunk_body(cols(ML_OFF, 768), cols(MO_OFF, GW), cols(G2_OFF, LANES), gp2_ref, pnm_ref, trilc_ref,
                          oml_ref, ct_ref, n_ref, m_ref),
        _ssm_chunk_body(cols(SS_OFF, 1024), cols(G1_OFF, LANES), conv_ref, gp1_ref, pd_ref, trilc_ref, oss_ref, ht_ref,
                        tail_scr),
    ]
    while bodies:
        for body in list(bodies):
            try:
                next(body)
            except StopIteration:
                bodies.remove(body)


MIXER_STATE_SHAPES = ((NH, HEAD_DIM, HEAD_DIM), (NH, SSM_STATE, HEAD_DIM), (NH, HEAD_DIM, HEAD_DIM), (SUBLANES, HEAD_DIM),
                      (1, LANES), (GW, GW))


def _mixers_chunk(proj, rw_scan, consts, layer, bsz, t):
    nt = t // CHUNK
    m = bsz * t
    rows = lambda w: pl.BlockSpec((CHUNK, w), lambda b, i: (b * nt + i, 0))
    const = lambda a: pl.BlockSpec(a.shape, lambda b, i: (0,) * a.ndim)
    state = lambda s: pl.BlockSpec((1,) + s, lambda b, i: (b,) + (0,) * len(s))
    big = pltpu.VMEM((CHUNK, GW), F32)
    outs = pl.pallas_call(
        functools.partial(_mixers_chunk_kernel, layer),
        grid=(bsz, nt),
        in_specs=[rows(IN_RE), rows(1536)] + [const(a) for a in consts],
        out_specs=[rows(GW)] * 4 + [state(s) for s in MIXER_STATE_SHAPES],
        out_shape=[jax.ShapeDtypeStruct((m, GW), F32)] * 4
        + [jax.ShapeDtypeStruct((bsz,) + s, F32) for s in MIXER_STATE_SHAPES],
        scratch_shapes=[big, big, big, pltpu.VMEM((SUBLANES, 768), F32), big],
        compiler_params=_cparams(("parallel", "arbitrary")),
    )(proj, rw_scan, *consts)
    return outs[:4], outs[4:]


def _segmean(a, s, n):
    return _dot01(_split3(a), s) * (1.0 / n)


def _rwkv_post(o, rwp_ref, pq, s64):
    d = o - _segmean(o, s64, HEAD_DIM)
    var = _segmean(d * d, s64, HEAD_DIM)
    return (d * lax.rsqrt(var + RWKV_GN_EPS) * pq[0:1] + pq[1:2] + rwp_ref[:, 0:256]) * rwp_ref[:, 256:512]


def _out_proj(x, parts, w_ref):
    acc = x
    for i, part in enumerate(parts):
        acc = acc + jnp.dot(part.astype(BF16), w_ref[i * GW:(i + 1) * GW, :], preferred_element_type=F32)
    return acc


def _post_step_kernel(x_ref, orw_ref, ohg_ref, oss_ref, oml_ref, rwp_ref, hgp_ref, ssp_ref, mlp_ref, pq_ref, seg64_ref,
                      seg128_ref, w_ref, o_ref):
    pq = pq_ref[...]
    s64 = seg64_ref[...]
    m_r = _rwkv_post(orw_ref[...].T, rwp_ref, pq, s64)
    o = ohg_ref[...].T
    m_h = o * lax.rsqrt(_segmean(o * o, s64, HEAD_DIM) + NORM_EPS) * pq[2:3] * hgp_ref[...]
    y = (oss_ref[...] + ssp_ref[:, 0:256]) * ssp_ref[:, 256:512]
    m_s = y * lax.rsqrt(_segmean(y * y, seg128_ref[...], 128) + NORM_EPS) * pq[3:4]
    o = oml_ref[...].T
    m_m = o * lax.rsqrt(_segmean(o * o, s64, HEAD_DIM) + NORM_EPS) * pq[4:5] * mlp_ref[...]
    o_ref[...] = _out_proj(x_ref[...], (m_r, m_h, m_s, m_m), w_ref)


def _post_seq_kernel(x_ref, orw_ref, rwp_ref, mh_ref, ms_ref, mm_ref, pq_ref, seg64_ref, w_ref, o_ref):
    m_r = _rwkv_post(orw_ref[...], rwp_ref, pq_ref[...], seg64_ref[...])
    o_ref[...] = _out_proj(x_ref[...], (m_r, mh_ref[...], ms_ref[...], mm_ref[...]), w_ref)


def _post_step(x, ins):
    full = lambda a: pl.BlockSpec(a.shape, lambda i: (0,) * a.ndim)
    return pl.pallas_call(
        _post_step_kernel,
        grid=(1,),
        in_specs=[full(a) for a in (x,) + tuple(ins)],
        out_specs=full(x),
        out_shape=jax.ShapeDtypeStruct(x.shape, F32),
        compiler_params=_cparams(("arbitrary",)),
    )(x, *ins)


def _post_call(kernel_fn, x, row_ins, consts):
    m = x.shape[0]
    tm = min(m, 512)
    rows = lambda a: pl.BlockSpec((tm, a.shape[1]), lambda i: (i, 0))
    const = lambda a: pl.BlockSpec(a.shape, lambda i: (0, 0))
    return pl.pallas_call(
        kernel_fn,
        grid=(m // tm,),
        in_specs=[rows(a) for a in (x,) + tuple(row_ins)] + [const(a) for a in consts],
        out_specs=rows(x),
        out_shape=jax.ShapeDtypeStruct((m, D_MODEL), F32),
        compiler_params=_cparams(("parallel",)),
    )(x, *row_ins, *consts)


def _xattn_seq_kernel(x_ref, g_ref, wq_ref, k_ref, v_ref, wo_ref, o_ref):
    x = x_ref[...]
    q = jnp.dot(_rms(x, g_ref[...]).astype(BF16), wq_ref[...], preferred_element_type=F32)
    acc = x
    for h in range(XH):
        sl = slice(h * XHD, (h + 1) * XHD)
        s = _dot_nt(q[:, sl], k_ref[:, sl]) * (XHD ** -0.5)
        e = jnp.exp(s - jnp.max(s, axis=-1, keepdims=True))
        pr = e / jnp.sum(e, axis=-1, keepdims=True)
        acc = acc + _dot(_dot(pr, v_ref[:, sl]), wo_ref[sl, :])
    o_ref[...] = acc


def _xattn_seq(x, g, wq, kv, wo, bsz, t):
    tq = 512
    nt = t // tq
    const = lambda a: pl.BlockSpec(a.shape, lambda b, i: (0, 0))
    g2 = g.reshape(1, D_MODEL)
    return pl.pallas_call(
        _xattn_seq_kernel,
        grid=(bsz, nt),
        in_specs=[pl.BlockSpec((tq, D_MODEL), lambda b, i: (b * nt + i, 0)), const(g2), const(wq),
                  pl.BlockSpec((N_MEM, D_MODEL), lambda b, i: (b, 0)),
                  pl.BlockSpec((N_MEM, D_MODEL), lambda b, i: (b, 1)),
                  const(wo)],
        out_specs=pl.BlockSpec((tq, D_MODEL), lambda b, i: (b * nt + i, 0)),
        out_shape=jax.ShapeDtypeStruct((bsz * t, D_MODEL), F32),
        compiler_params=_cparams(("parallel", "parallel")),
    )(x, g2, wq, kv, kv, wo)


def _xattn_step_kernel(bb, q_ref, k0_ref, k1_ref, v0_ref, v1_ref, o_ref):
    for i in range(bb):
        q = q_ref[i]
        s0 = jnp.sum(k0_ref[i] * q, axis=-1, keepdims=True) * (XHD ** -0.5)
        s1 = jnp.sum(k1_ref[i] * q, axis=-1, keepdims=True) * (XHD ** -0.5)
        m = jnp.maximum(jnp.max(s0, axis=0, keepdims=True), jnp.max(s1, axis=0, keepdims=True))
        e0, e1 = jnp.exp(s0 - m), jnp.exp(s1 - m)
        z = jnp.sum(e0, axis=0, keepdims=True) + jnp.sum(e1, axis=0, keepdims=True)
        o_ref[i] = jnp.sum((e0 / z) * v0_ref[i], axis=0) + jnp.sum((e1 / z) * v1_ref[i], axis=0)


def _xattn_step(q, cache_k, cache_v, layer):
    bsz = q.shape[0]
    bb = 4
    half = lambda j: pl.BlockSpec((None, bb, N_MEM // 2, XH, XHD), lambda i: (layer, i, j, 0, 0))
    return pl.pallas_call(
        functools.partial(_xattn_step_kernel, bb),
        grid=(bsz // bb,),
        in_specs=[pl.BlockSpec((bb, XH, XHD), lambda i: (i, 0, 0)), half(0), half(1), half(0), half(1)],
        out_specs=pl.BlockSpec((bb, XH, XHD), lambda i: (i, 0, 0)),
        out_shape=jax.ShapeDtypeStruct((bsz, XH, XHD), F32),
        compiler_params=_cparams(("parallel",)),
    )(q.reshape(bsz, XH, XHD), cache_k, cache_k, cache_v, cache_v).reshape(bsz, D_MODEL)


def _step_lanes_kernel(rw_ref, hg_ref, ml_ref, wkv_ref, hgs_ref, c_ref, n_ref, m_ref, a0_ref, a1_ref, a2_ref,
                       orw_ref, ohg_ref, oml_ref, wkv_o, hgs_o, c_o, n_o, m_o):
    del a0_ref, a1_ref, a2_ref
    csum = lambda a: jnp.sum(a, axis=0, keepdims=True)
    r, k, w, kk, ka = rw_ref[0], rw_ref[1], rw_ref[3], rw_ref[4], rw_ref[5]
    for i in range(HEAD_DIM):
        s = wkv_ref[i]
        s = s * w - csum(s * kk) * ka + rw_ref[2, i:i + 1, :] * k
        orw_ref[i:i + 1, :] = csum(s * r)
        wkv_o[i] = s
    vh = hg_ref[3]
    acc = jnp.zeros_like(vh)
    for i in range(HEAD_DIM):
        s = hgs_ref[i] * hg_ref[1, i:i + 1, :] + hg_ref[2, i:i + 1, :] * vh
        acc = acc + hg_ref[0, i:i + 1, :] * s
        hgs_o[i] = s
    ohg_ref[...] = acc
    q, k = ml_ref[0], ml_ref[1]
    li, lf = ml_ref[3, 0:1, :], ml_ref[4, 0:1, :]
    m_old = m_ref[...]
    m_new = jnp.maximum(lf + m_old, li)
    gp = jnp.exp(lf + m_old - m_new)
    gs = jnp.exp(li - m_new)
    n = gp * n_ref[...] + gs * k
    scale = 1.0 / jnp.maximum(jnp.abs(csum(n * q)), jnp.exp(-m_new))
    gk = gs * k
    for i in range(HEAD_DIM):
        s = c_ref[i] * gp + ml_ref[2, i:i + 1, :] * gk
        oml_ref[i:i + 1, :] = csum(s * q) * scale
        c_o[i] = s
    n_o[...] = n
    m_o[...] = m_new


def _ssm_step_kernel(xdt_ref, bn_ref, cn_ref, dec_ref, ssm_ref, a_ref, o_ref, ssm_o):
    del a_ref
    ri = lax.broadcasted_iota(jnp.int32, (HEAD_DIM, HEAD_DIM), 0)
    ci = lax.broadcasted_iota(jnp.int32, (HEAD_DIM, HEAD_DIM), 1)
    eye = (ri == ci).astype(F32)
    col = lambda r: jnp.sum(eye * r, axis=-1, keepdims=True)
    row = lambda c: jnp.sum(eye * c, axis=-2, keepdims=True)
    s = ssm_ref[...] * dec_ref[...] + col(xdt_ref[...]) * bn_ref[...]
    o_ref[...] = row(jnp.sum(s * cn_ref[...], axis=-1, keepdims=True))
    ssm_o[...] = s


def _step_lanes(rw_t, hg_t, ml_t, states, acc, layer):
    bsz = rw_t.shape[1]
    fields = lambda a: a.reshape(-1, NH, HEAD_DIM, bsz)
    seq_ins = (fields(rw_t), fields(hg_t), fields(ml_t))
    seq_spec = lambda a: pl.BlockSpec((a.shape[0], None, HEAD_DIM, bsz), lambda h: (0, h, 0, 0))
    st_layer = lambda a: pl.BlockSpec((None, None) + a.shape[2:], lambda h: (layer, h) + (0,) * (a.ndim - 2))
    st_head = lambda a: pl.BlockSpec((None,) + a.shape[2:], lambda h: (h,) + (0,) * (a.ndim - 2))
    o_shape = jax.ShapeDtypeStruct((NH, HEAD_DIM, bsz), F32)
    o_spec = pl.BlockSpec((None, HEAD_DIM, bsz), lambda h: (h, 0, 0))
    big, small = states[:3], states[3:]
    n_in = len(seq_ins) + len(states)
    outs = pl.pallas_call(
        _step_lanes_kernel,
        grid=(NH,),
        in_specs=[seq_spec(a) for a in seq_ins] + [st_layer(a) for a in states]
        + [pl.BlockSpec(memory_space=pl.ANY)] * len(big),
        out_specs=[o_spec] * 3 + [st_layer(a) for a in big] + [st_head(a) for a in small],
        out_shape=[o_shape] * 3 + [jax.ShapeDtypeStruct(a.shape, F32) for a in big]
        + [jax.ShapeDtypeStruct(a.shape[1:], F32) for a in small],
        input_output_aliases={n_in + j: 3 + j for j in range(len(big))},
        compiler_params=_cparams(("parallel",)),
    )(*seq_ins, *states, *acc)
    o_parts = [o.reshape(GW, bsz) for o in outs[:3]]
    return o_parts, outs[3:6], outs[6], outs[7]


def _ssm_step(ss, state_ssm, acc, layer):
    bsz = ss.shape[0]
    bb = SUBLANES
    per_head = lambda a: a.reshape(bsz, NH, HEAD_DIM)[:, :, 0].reshape(bsz, NH, 1, 1)
    groups = lambda a: jnp.repeat(a.reshape(bsz, 2, 1, SSM_STATE), NH // 2, axis=1)
    seq_ins = (ss[:, 0:GW].reshape(bsz, NH, 1, HEAD_DIM), groups(ss[:, 512:768]), groups(ss[:, 768:1024]),
               per_head(ss[:, GW:2 * GW]))
    blk = lambda a: pl.BlockSpec((bb,) + a.shape[1:], lambda i: (i,) + (0,) * (a.ndim - 1))
    st_layer = pl.BlockSpec((None, bb) + state_ssm.shape[2:], lambda i: (layer, i, 0, 0, 0))
    o, ssm_new = pl.pallas_call(
        _ssm_step_kernel,
        grid=(bsz // bb,),
        in_specs=[blk(a) for a in seq_ins] + [st_layer, pl.BlockSpec(memory_space=pl.ANY)],
        out_specs=[pl.BlockSpec((bb, NH, 1, HEAD_DIM), lambda i: (i, 0, 0, 0)), st_layer],
        out_shape=[jax.ShapeDtypeStruct((bsz, NH, 1, HEAD_DIM), F32), jax.ShapeDtypeStruct(state_ssm.shape, F32)],
        input_output_aliases={len(seq_ins) + 1: 1},
        compiler_params=_cparams(("parallel",)),
    )(*seq_ins, state_ssm, acc)
    return o.reshape(bsz, GW), ssm_new


def kernel(x_prompt, x_sample, state_rwkv_wkv, state_rwkv_shift, state_hgrn, state_ssm, state_ssm_conv, state_mlstm_C, state_mlstm_n, state_mlstm_m, cache_mem_k, cache_mem_v, mem_prompt, norm_mix_g, w_in, w_out, rwkv_mu, rwkv_w0, rwkv_w_up, rwkv_a0, rwkv_a_up, rwkv_g_up, rwkv_k_k, rwkv_k_a, rwkv_r_k, rwkv_ln_g, rwkv_ln_b, hgrn_lb_logits, hgrn_norm_g, ssm_conv_w, ssm_conv_b, ssm_dt_bias, ssm_A_log, ssm_D, ssm_norm_g, mlstm_i_bias, mlstm_f_bias, mlstm_norm_g, norm_x_g, norm_mem_g, xattn_wq, xattn_wk, xattn_wv, xattn_wo, norm_ff_g, ff_w1, ff_w2, final_norm_g):
    bp, tp, _ = x_prompt.shape
    bs = x_sample.shape[0]
    seg64, seg128 = _seg_matrix(HEAD_DIM), _seg_matrix(128)
    seg64_b = seg64.astype(BF16)
    gate_ex = _gate_expand()
    tril_c = jnp.asarray(np.tril(np.ones((CHUNK, CHUNK), np.float32)))

    def block_ones(sub):
        blk = np.arange(CHUNK) // sub
        return jnp.asarray((blk[:, None] == blk[None, :]).astype(np.float32))

    ones_blk, ones_blk_h = block_ones(SUB), block_ones(HGRN_SUB)
    tril_blk, tril_blk_h = ones_blk * tril_c, ones_blk_h * tril_c
    sel_c = jnp.asarray((np.arange(GW)[:, None] // HEAD_DIM == np.arange(NH * SUB)[None, :] // SUB), BF16)
    rep = lambda a: jnp.repeat(a, HEAD_DIM)
    zeros = lambda *s: jnp.zeros(s, F32)

    hp = x_prompt.reshape(bp * tp, D_MODEL)
    hs = x_sample.reshape(bs, D_MODEL)
    mem = mem_prompt.reshape(bp * N_MEM, D_MODEL)
    p_states, s_small, p_mem = [], [], []
    to_lanes = lambda a: jnp.moveaxis(a, 1, -1)
    from_lanes = lambda a: jnp.moveaxis(a, -1, 1)
    lane_states = (to_lanes(state_rwkv_wkv), to_lanes(state_hgrn), to_lanes(state_mlstm_C), to_lanes(state_mlstm_n),
                   to_lanes(state_mlstm_m)[:, :, None, :])
    s_big = tuple(jnp.zeros(a.shape, F32) for a in lane_states[:3])
    s_ssm = jnp.zeros(state_ssm.shape, F32)

    for l in range(DEPTH):
        last = l == DEPTH - 1
        w_in_re = _in_weight(w_in, l)
        pv = _pad_rows(jnp.stack([rwkv_w0[l], rwkv_a0[l], rwkv_k_k[l], rwkv_k_a[l], rwkv_r_k[l].reshape(GW),
                                  rep(ssm_dt_bias[l]), rep(ssm_A_log[l]), rep(ssm_D[l]), rep(mlstm_i_bias[l]),
                                  rep(mlstm_f_bias[l])]), 16)
        lora = jnp.stack([jnp.pad(w, ((r0, LANES - r0 - w.shape[0]), (0, 0)))
                          for r0, w in zip(LORA_ROWS, (rwkv_w_up[l], rwkv_a_up[l], rwkv_g_up[l]))])
        conv = _pad_rows(jnp.concatenate([ssm_conv_w[l], ssm_conv_b[l][None]], axis=0), 8)
        mu = rwkv_mu[l].reshape(1, RW_W)
        gp1 = _pad_rows(jnp.stack([_lane_row((GT_DT, ssm_dt_bias[l])), _lane_row((GT_DT, ssm_A_log[l]))]), 8)
        gp2 = _pad_rows(_lane_row((GT_I, mlstm_i_bias[l]), (GT_F, mlstm_f_bias[l]))[None], 8)
        pd = _pad_rows(jnp.stack([rep(ssm_D[l]), ssm_norm_g[l]]), 8)
        pn_h = _pad_rows(hgrn_norm_g[l][None], 8)
        pn_m = _pad_rows(mlstm_norm_g[l][None], 8)
        pq = _pad_rows(jnp.stack([rwkv_ln_g[l], rwkv_ln_b[l], hgrn_norm_g[l], ssm_norm_g[l], mlstm_norm_g[l]]), 8)
        w_out_b = w_out[l].astype(BF16)
        wq_b, wo_b = xattn_wq[l].astype(BF16), xattn_wo[l].astype(BF16)
        wkv_b = jnp.concatenate([xattn_wk[l], xattn_wv[l]], axis=1).astype(BF16)
        w1_b, w2_b = ff_w1[l].astype(BF16), ff_w2[l].astype(BF16)

        kv = _norm_matmul(mem, norm_mem_g[l], wkv_b)
        p_mem.append((kv[:, :D_MODEL].reshape(bp, N_MEM, XH, XHD), kv[:, D_MODEL:].reshape(bp, N_MEM, XH, XHD)))
        proj = _norm_matmul(hp, norm_mix_g[l], w_in_re)
        rw_scan, rw_post = _prep_rw_seq(proj, zeros(bp, SUBLANES, RW_W), mu, pv, lora, seg64, bp, tp)
        (m_h, m_s, m_m, o_rw), (st_h, st_s, st_c, st_n, st_m, st_r) = _mixers_chunk(
            proj, rw_scan, (hgrn_lb_logits, pn_h, tril_blk, ones_blk, seg64_b, seg64, conv, gp1, gp2, pd, pn_m, tril_c,
                            sel_c, tril_blk_h, ones_blk_h), l, bp, tp)
        proj3 = proj.reshape(bp, tp, IN_RE)
        diag_blocks = lambda s: jnp.stack(
            [s.reshape(bp, NH, HEAD_DIM, NH, HEAD_DIM)[:, h, :, h, :] for h in range(NH)], axis=1)
        p_states.append((diag_blocks(st_r), proj3[:, -1, RW_OFF:RW_OFF + RW_W], st_h.transpose(0, 1, 3, 2),
                         st_s.transpose(0, 1, 3, 2), proj3[:, tp - 3:, SS_OFF + 256:SS_OFF + 1024],
                         st_c.transpose(0, 1, 3, 2), st_n[:, :NH], st_m[:, 0, :NH]))
        hp = _post_call(_post_seq_kernel, hp, (o_rw, rw_post, m_h, m_s, m_m), (pq, seg64, w_out_b))
        hp = _xattn_seq(hp, norm_x_g[l], wq_b, kv, wo_b, bp, tp)
        hp = _ffn(hp, norm_ff_g[l], w1_b, w2_b, final_norm_g, last)

        proj = _norm_matmul(hs, norm_mix_g[l], w_in_re)
        cb = state_ssm_conv[l]
        prep = _prep_step(proj, state_rwkv_shift[l], cb[:, 0], cb[:, 1], cb[:, 2],
                          (mu, pv, lora, seg64, hgrn_lb_logits, conv, gate_ex), l)
        rw_t, hg_t, ml_t, rw_post, hg_post, ss_scan, ss_post, ml_post = prep
        (o_rw_t, o_hg_t, o_ml_t), s_big, n_n, m_n = _step_lanes(rw_t, hg_t, ml_t, lane_states, s_big, l)
        o_ss, s_ssm = _ssm_step(ss_scan, state_ssm, s_ssm, l)
        conv_new = jnp.concatenate([cb[:, 1:], proj[:, None, SS_OFF + 256:SS_OFF + 1024]], axis=1)
        s_small.append((proj[:, RW_OFF:RW_OFF + RW_W], conv_new, n_n, m_n))
        hs = _post_step(hs, (o_rw_t, o_hg_t, o_ss, o_ml_t, rw_post, hg_post, ss_post, ml_post, pq, seg64, seg128,
                             w_out_b))
        q = _norm_matmul(hs, norm_x_g[l], wq_b)
        att = _xattn_step(q, cache_mem_k, cache_mem_v, l)
        hs = _mm_res(att, wo_b, hs)
        hs = _ffn(hs, norm_ff_g[l], w1_b, w2_b, final_norm_g, last)

    y_prompt = hp.reshape(bp, tp, D_MODEL)
    y_sample = hs.reshape(bs, 1, D_MODEL)
    p_out = [jnp.stack(s) for s in zip(*p_states)]
    s_shift, s_conv, s_n, s_m = [jnp.stack(s) for s in zip(*s_small)]
    s_n, s_m = from_lanes(s_n), from_lanes(s_m[:, :, 0, :])
    s_wkv, s_hgrn, s_c = (from_lanes(a) for a in s_big)
    p_mem_k = jnp.stack([kv[0] for kv in p_mem])
    p_mem_v = jnp.stack([kv[1] for kv in p_mem])
    return (y_prompt, y_sample, *p_out, p_mem_k, p_mem_v, s_wkv, s_shift, s_hgrn, s_ssm, s_conv, s_c, s_n, s_m)
```
